```python
import math
import jax, jax.numpy as jnp
from jax import lax
import numpy as np

D_MODEL = 1024
BATCH = 16
SEQ = 2048
DEPTH = 1

N_HEADS_ATTN = 8
HEAD_DIM = 64
V_DIM = 2 * HEAD_DIM
ATTN_WIDTH = N_HEADS_ATTN * V_DIM
QK_WIDTH = N_HEADS_ATTN * 2 * HEAD_DIM
Q_BLOCK = 128
REL_BUCKETS = 32
REL_MAX_DIST = 128
LRU_WIDTH = D_MODEL
LRU_BLOCKS = 8
LRU_BLOCK = LRU_WIDTH // LRU_BLOCKS
CONV_WIDTH = 4
LRU_C = 8.0
N_EXPERTS = 16
EC_CAPACITY_FACTOR = 2
D_FF_EXPERT = 2 * D_MODEL
EPS = 1e-6

OFF_Q = 0
OFF_K = OFF_Q + QK_WIDTH
OFF_V = OFF_K + QK_WIDTH
OFF_LRU_X = OFF_V + ATTN_WIDTH
OFF_LRU_Y = OFF_LRU_X + LRU_WIDTH
OFF_GATE_A = OFF_LRU_Y + LRU_WIDTH
OFF_GATE_R = OFF_GATE_A + D_MODEL
D_IN = OFF_GATE_R + D_MODEL

kernel_name = "hybrid_diffattn_rglru_ecmoe_encoder"


def _rmsnorm(x, g):
    xf = x.astype(jnp.float32)
    y = xf * lax.rsqrt(jnp.mean(xf * xf, axis=-1, keepdims=True) + EPS)
    return (y * g.astype(jnp.float32)).astype(x.dtype)


def _lambda_init(layer_idx):
    return 0.8 - 0.6 * math.exp(-0.3 * layer_idx)


def _rel_bucket(rel):
    half = REL_BUCKETS // 2
    max_exact = half // 2
    ret = jnp.where(rel > 0, half, 0)
    n = jnp.abs(rel)
    nf = jnp.maximum(n, max_exact).astype(jnp.float32)
    large = max_exact + (jnp.log(nf / max_exact) / math.log(REL_MAX_DIST / max_exact)
                         * (half - max_exact)).astype(jnp.int32)
    large = jnp.minimum(large, half - 1)
    return ret + jnp.where(n < max_exact, n, large)


def _diff_attention(q, k, v, lam, rel_bias):
    B, S = q.shape[0], q.shape[1]
    nq = S // Q_BLOCK
    qb = q.reshape(B, nq, Q_BLOCK, N_HEADS_ATTN, 2, HEAD_DIM).transpose(1, 0, 3, 4, 2, 5)
    kt = k.transpose(0, 2, 3, 1, 4)
    vt = v.transpose(0, 2, 1, 3)
    k_pos = jnp.arange(S, dtype=jnp.int32)
    scale = HEAD_DIM ** -0.5

    def block(args):
        q_blk, start = args
        q_pos = start + jnp.arange(Q_BLOCK, dtype=jnp.int32)
        bias = rel_bias[_rel_bucket(k_pos[None, :] - q_pos[:, None])]
        bias = bias.transpose(2, 0, 1).astype(jnp.float32)
        logits = jnp.einsum("bhmqd,bhmkd->bhmqk", q_blk, kt).astype(jnp.float32) * scale
        p = jax.nn.softmax(logits + bias[None, :, None], axis=-1)
        w = (p[:, :, 0] - lam * p[:, :, 1]).astype(v.dtype)
        return jnp.einsum("bhqk,bhkv->bhqv", w, vt)

    starts = jnp.arange(nq, dtype=jnp.int32) * Q_BLOCK
    o = lax.map(block, (qb, starts))
    return o.transpose(1, 0, 3, 2, 4).reshape(B, S, N_HEADS_ATTN, V_DIM)


def _scan_combine(c1, c2):
    a1, b1 = c1
    a2, b2 = c2
    return a1 * a2, a2 * b1 + b2


def _rg_lru(xf, w_r, b_r, w_i, b_i, lam_param, reverse):
    B, S, W = xf.shape
    xb = xf.reshape(B, S, LRU_BLOCKS, LRU_BLOCK)
    r = jax.nn.sigmoid(jnp.einsum("bsnc,ncd->bsnd", xb, w_r.astype(jnp.float32)).reshape(B, S, W)
                       + b_r.astype(jnp.float32))
    i = jax.nn.sigmoid(jnp.einsum("bsnc,ncd->bsnd", xb, w_i.astype(jnp.float32)).reshape(B, S, W)
                       + b_i.astype(jnp.float32))
    log_a = -LRU_C * r * jax.nn.softplus(-lam_param.astype(jnp.float32))
    a = jnp.exp(log_a)
    mult = jnp.sqrt(jnp.maximum(-jnp.expm1(2.0 * log_a), 0.0))
    _, h = lax.associative_scan(_scan_combine, (a, mult * i * xf), axis=1, reverse=reverse)
    return h


def _ec_moe(xn, w_router, w_gate, w_up, w_down):
    B, S, D = xn.shape
    cap = EC_CAPACITY_FACTOR * S // N_EXPERTS
    aff = jax.nn.softmax(jnp.einsum("bsd,de->bse", xn.astype(jnp.float32),
                                    w_router.astype(jnp.float32)), axis=-1)
    vals, idx = lax.top_k(aff.transpose(0, 2, 1), cap)
    xg = jax.vmap(lambda xs, ix: xs[ix])(xn, idx)
    hg = jax.nn.silu(jnp.einsum("becd,edf->becf", xg, w_gate)) * jnp.einsum("becd,edf->becf", xg, w_up)
    ye = jnp.einsum("becf,efd->becd", hg, w_down) * vals[..., None].astype(xn.dtype)
    return jax.vmap(lambda ix, yb: jnp.zeros((S, D), yb.dtype).at[ix.reshape(-1)].add(yb.reshape(-1, D)))(idx, ye)


def setup_inputs(seed: int = 0) -> dict:
    key = jax.random.key(seed)
    ks = jax.random.split(key, 32)
    f32 = jnp.float32
    nrm = lambda k, shape, s: jax.random.normal(k, shape, f32) * s
    u = jax.random.uniform(ks[20], (DEPTH, 2, LRU_WIDTH), f32, 0.9, 0.999)
    return {
        "x": jax.random.normal(ks[0], (BATCH, SEQ, D_MODEL), f32),
        "g_mix": 1.0 + nrm(ks[1], (DEPTH, D_MODEL), 0.05),
        "w_in": nrm(ks[2], (DEPTH, D_MODEL, D_IN), D_MODEL ** -0.5),
        "g_q": 1.0 + nrm(ks[3], (DEPTH, HEAD_DIM), 0.05),
        "g_k": 1.0 + nrm(ks[4], (DEPTH, HEAD_DIM), 0.05),
        "lam_q1": nrm(ks[5], (DEPTH, HEAD_DIM), 0.1),
        "lam_k1": nrm(ks[6], (DEPTH, HEAD_DIM), 0.1),
        "lam_q2": nrm(ks[7], (DEPTH, HEAD_DIM), 0.1),
        "lam_k2": nrm(ks[8], (DEPTH, HEAD_DIM), 0.1),
        "g_subln": 1.0 + nrm(ks[9], (DEPTH, V_DIM), 0.05),
        "rel_bias": nrm(ks[10], (REL_BUCKETS, N_HEADS_ATTN), 0.5),
        "conv_w": nrm(ks[11], (DEPTH, CONV_WIDTH, LRU_WIDTH), CONV_WIDTH ** -0.5),
        "conv_b": nrm(ks[12], (DEPTH, LRU_WIDTH), 0.01),
        "gate_r_w": nrm(ks[13], (DEPTH, 2, LRU_BLOCKS, LRU_BLOCK, LRU_BLOCK), LRU_BLOCK ** -0.5),
        "gate_r_b": nrm(ks[14], (DEPTH, 2, LRU_WIDTH), 0.01),
        "gate_i_w": nrm(ks[15], (DEPTH, 2, LRU_BLOCKS, LRU_BLOCK, LRU_BLOCK), LRU_BLOCK ** -0.5),
        "gate_i_b": nrm(ks[16], (DEPTH, 2, LRU_WIDTH), 0.01),
        "lru_lambda": jnp.log(u) - jnp.log1p(-u),
        "w_proj_attn": nrm(ks[17], (DEPTH, ATTN_WIDTH, D_MODEL), ATTN_WIDTH ** -0.5),
        "w_proj_lru": nrm(ks[18], (DEPTH, LRU_WIDTH, D_MODEL), LRU_WIDTH ** -0.5),
        "w_out": nrm(ks[19], (DEPTH, D_MODEL, D_MODEL), D_MODEL ** -0.5),
        "g_ffn": 1.0 + nrm(ks[21], (DEPTH, D_MODEL), 0.05),
        "w_router": nrm(ks[22], (DEPTH, D_MODEL, N_EXPERTS), D_MODEL ** -0.5),
        "w_gate_e": nrm(ks[23], (DEPTH, N_EXPERTS, D_MODEL, D_FF_EXPERT), D_MODEL ** -0.5),
        "w_up_e": nrm(ks[24], (DEPTH, N_EXPERTS, D_MODEL, D_FF_EXPERT), D_MODEL ** -0.5),
        "w_down_e": nrm(ks[25], (DEPTH, N_EXPERTS, D_FF_EXPERT, D_MODEL), D_FF_EXPERT ** -0.5),
    }


def reference(x, g_mix, w_in, g_q, g_k, lam_q1, lam_k1, lam_q2, lam_k2, g_subln, rel_bias,
              conv_w, conv_b, gate_r_w, gate_r_b, gate_i_w, gate_i_b, lru_lambda,
              w_proj_attn, w_proj_lru, w_out, g_ffn, w_router, w_gate_e, w_up_e, w_down_e):
    B, S, D = x.shape
    h = x
    for layer in range(DEPTH):
        lam_init = _lambda_init(layer)
        xn = _rmsnorm(h, g_mix[layer])
        proj = jnp.einsum("bsd,dn->bsn", xn, w_in[layer])
        q = proj[..., OFF_Q:OFF_K].reshape(B, S, N_HEADS_ATTN, 2, HEAD_DIM)
        k = proj[..., OFF_K:OFF_V].reshape(B, S, N_HEADS_ATTN, 2, HEAD_DIM)
        v = proj[..., OFF_V:OFF_LRU_X].reshape(B, S, N_HEADS_ATTN, V_DIM)
        x_lru = proj[..., OFF_LRU_X:OFF_LRU_Y]
        y_lru = proj[..., OFF_LRU_Y:OFF_GATE_A]
        gate_a = proj[..., OFF_GATE_A:OFF_GATE_R]
        gate_r = proj[..., OFF_GATE_R:D_IN]

        q = _rmsnorm(q, g_q[layer])
        k = _rmsnorm(k, g_k[layer])
        lam = (jnp.exp(jnp.sum(lam_q1[layer].astype(jnp.float32) * lam_k1[layer].astype(jnp.float32)))
               - jnp.exp(jnp.sum(lam_q2[layer].astype(jnp.float32) * lam_k2[layer].astype(jnp.float32)))
               + lam_init)
        o = _diff_attention(q, k, v, lam, rel_bias)
        o = _rmsnorm(o, g_subln[layer]) * (1.0 - lam_init)
        branch_a = jnp.einsum("bsa,ad->bsd", o.reshape(B, S, ATTN_WIDTH), w_proj_attn[layer])

        xc = lax.conv_general_dilated(
            x_lru, conv_w[layer][:, None, :], window_strides=(1,),
            padding=[(CONV_WIDTH // 2, CONV_WIDTH - 1 - CONV_WIDTH // 2)],
            dimension_numbers=("NWC", "WIO", "NWC"), feature_group_count=LRU_WIDTH) + conv_b[layer]
        xcf = xc.astype(jnp.float32)
        h_fwd = _rg_lru(xcf, gate_r_w[layer, 0], gate_r_b[layer, 0], gate_i_w[layer, 0],
                        gate_i_b[layer, 0], lru_lambda[layer, 0], reverse=False)
        h_bwd = _rg_lru(xcf, gate_r_w[layer, 1], gate_r_b[layer, 1], gate_i_w[layer, 1],
                        gate_i_b[layer, 1], lru_lambda[layer, 1], reverse=True)
        lru_out = ((h_fwd + h_bwd) * jax.nn.gelu(y_lru.astype(jnp.float32))).astype(x.dtype)
        branch_r = jnp.einsum("bsw,wd->bsd", lru_out, w_proj_lru[layer])

        mixed = jax.nn.sigmoid(gate_a) * branch_a + jax.nn.sigmoid(gate_r) * branch_r
        h = h + jnp.einsum("bsd,de->bse", mixed, w_out[layer])

        hn = _rmsnorm(h, g_ffn[layer])
        h = h + _ec_moe(hn, w_router[layer], w_gate_e[layer], w_up_e[layer], w_down_e[layer])
    return h
```

```python
import functools
import math

import jax
import jax.numpy as jnp
import numpy as np
from jax import lax
from jax.experimental import pallas as pl
from jax.experimental.pallas import tpu as pltpu

F32 = jnp.float32
BF16 = jnp.bfloat16
EPS = 1e-6
LANES = 128
SUBLANES = 8
VMEM_LIMIT = 56 * 1024 * 1024
LOG2E = 1.4426950408889634
NEG_BIG = -1e30
REL_MAX_DIST = 128
LRU_C = 8.0
EC_CAPACITY_FACTOR = 2


def _params(*sem):
    return pltpu.CompilerParams(dimension_semantics=sem, vmem_limit_bytes=VMEM_LIMIT)


def _inproj_body(x_ref, g_ref, w_ref, o_ref, xn_ref):
    @pl.when(pl.program_id(1) == 0)
    def _():
        x = x_ref[...]
        ms = jnp.mean(x * x, axis=-1, keepdims=True)
        xn_ref[...] = (x * lax.rsqrt(ms + EPS) * g_ref[...]).astype(BF16)

    o_ref[...] = jnp.dot(xn_ref[...], w_ref[...], preferred_element_type=F32).astype(o_ref.dtype)


def _inproj(x2, g, w_bf):
    T, D = x2.shape
    N = w_bf.shape[1]
    tm = min(1024, T)
    tn = min(1024, N)
    return pl.pallas_call(
        _inproj_body,
        grid=(T // tm, N // tn),
        in_specs=[pl.BlockSpec((tm, D), lambda i, j: (i, 0)),
                  pl.BlockSpec((1, D), lambda i, j: (0, 0)),
                  pl.BlockSpec((D, tn), lambda i, j: (0, j))],
        out_specs=pl.BlockSpec((tm, tn), lambda i, j: (i, j)),
        out_shape=jax.ShapeDtypeStruct((T, N), BF16),
        scratch_shapes=[pltpu.VMEM((tm, D), BF16)],
        compiler_params=_params("parallel", "arbitrary"),
        name="in_proj",
    )(x2, g, w_bf)


def _bucket_table(S, n_buckets):
    rel = np.arange(2 * S) - S
    half = n_buckets // 2
    max_exact = half // 2
    ret = np.where(rel > 0, half, 0)
    n = np.abs(rel)
    nf = np.maximum(n, max_exact).astype(np.float64)
    large = max_exact + (np.log(nf / max_exact) / math.log(REL_MAX_DIST / max_exact)
                         * (half - max_exact)).astype(np.int32)
    large = np.minimum(large, half - 1)
    return (ret + np.where(n < max_exact, n, large)).astype(np.int32).reshape(1, 2 * S)


def _halves_rmsnorm(x, hd):
    lane = lax.broadcasted_iota(jnp.int32, x.shape, 1)
    lo = lane < hd
    x2 = x * x
    s_lo = jnp.sum(jnp.where(lo, x2, 0.0), axis=-1, keepdims=True)
    s_hi = jnp.sum(jnp.where(lo, 0.0, x2), axis=-1, keepdims=True)
    return x * lax.rsqrt(jnp.where(lo, s_lo, s_hi) * (1.0 / hd) + EPS), lo


def _attn_body(bucket_ref, relb_ref, q_ref, k_ref, v_ref, gq_ref, gk_ref, lamp_ref, gs_ref, o_ref,
               e_ref, kn_ref, va_ref, m_ref, acc_ref, *, S, tq, kc, hd, vd, lam_init, n_buckets):
    h = pl.program_id(0)
    b = pl.program_id(1)
    i = pl.program_id(2)

    @pl.when((b == 0) & (i == 0))
    def _():
        bk = bucket_ref[...]
        tab = jnp.zeros((1, 2 * S), F32)
        for n in range(n_buckets):
            tab = jnp.where(bk == n, relb_ref[h, n], tab)
        tab = jnp.broadcast_to(tab * LOG2E, (tq, 2 * S))
        e_ref[...] = pltpu.roll(tab, 0, 1, stride=1, stride_axis=0)

    @pl.when(i == 0)
    def _():
        kn, _ = _halves_rmsnorm(k_ref[...].astype(F32), hd)
        kn_ref[...] = (kn * gk_ref[...]).astype(BF16)
        va_ref[:, :vd] = v_ref[...]
        va_ref[:, vd:] = jnp.ones((S, vd), BF16)

    qn, lo = _halves_rmsnorm(q_ref[...].astype(F32), hd)
    qn = qn * (gq_ref[...] * (hd ** -0.5 * LOG2E))
    q_maps = (jnp.where(lo, qn, 0.0).astype(BF16), jnp.where(lo, 0.0, qn).astype(BF16))

    m_ref[...] = jnp.full(m_ref.shape, NEG_BIG, F32)
    acc_ref[...] = jnp.zeros(acc_ref.shape, F32)

    def chunk(c, carry):
        k0 = pl.multiple_of(c * kc, kc)
        e0 = pl.multiple_of(S + c * kc - i * tq, LANES)
        bias = e_ref[:, pl.ds(e0, kc)]
        kc_n = kn_ref[pl.ds(k0, kc), :]
        va = va_ref[pl.ds(k0, kc), :]
        for mi in range(2):
            s = lax.dot_general(q_maps[mi], kc_n, (((1,), (1,)), ((), ())),
                                preferred_element_type=F32) + bias
            m_old = m_ref[mi]
            m_new = jnp.maximum(m_old, jnp.max(s, axis=-1, keepdims=True))
            alpha = jnp.exp2(m_old - m_new)
            p = jnp.exp2(s - m_new).astype(BF16)
            acc_ref[mi] = alpha * acc_ref[mi] + jnp.dot(p, va, preferred_element_type=F32)
            m_ref[mi] = m_new
        return carry

    lax.fori_loop(0, S // kc, chunk, 0)

    lp = lamp_ref[...]
    lam = (jnp.exp(jnp.sum(lp[0:1] * lp[1:2], axis=-1, keepdims=True))
           - jnp.exp(jnp.sum(lp[2:3] * lp[3:4], axis=-1, keepdims=True)) + lam_init)
    a1 = acc_ref[0]
    a2 = acc_ref[1]
    o = a1[:, :vd] / a1[:, vd:vd + 1] - lam * (a2[:, :vd] / a2[:, vd:vd + 1])
    o = o * lax.rsqrt(jnp.mean(o * o, axis=-1, keepdims=True) + EPS)
    o_ref[...] = (o * (gs_ref[...] * (1.0 - lam_init))).astype(o_ref.dtype)


def _attention(proj, rel_bias, g_q, g_k, lam_params, g_subln, *, B, S, H, hd, vd, off_k, off_v, lam_init):
    T = B * S
    tq = min(512, S)
    kc = min(512, S)
    nq = S // tq
    n_buckets = rel_bias.shape[0]
    hw = 2 * hd
    bucket = jnp.asarray(_bucket_table(S, n_buckets))
    body = functools.partial(_attn_body, S=S, tq=tq, kc=kc, hd=hd, vd=vd, lam_init=lam_init,
                             n_buckets=n_buckets)
    kblk = off_k // hw
    vblk = off_v // vd
    return pl.pallas_call(
        body,
        grid=(H, B, nq),
        in_specs=[pl.BlockSpec((1, 2 * S), lambda h, b, i: (0, 0)),
                  pl.BlockSpec(memory_space=pltpu.SMEM),
                  pl.BlockSpec((tq, hw), lambda h, b, i: (b * nq + i, h)),
                  pl.BlockSpec((S, hw), lambda h, b, i: (b, kblk + h)),
                  pl.BlockSpec((S, vd), lambda h, b, i: (b, vblk + h)),
                  pl.BlockSpec((1, hw), lambda h, b, i: (0, 0)),
                  pl.BlockSpec((1, hw), lambda h, b, i: (0, 0)),
                  pl.BlockSpec((4, hd), lambda h, b, i: (0, 0)),
                  pl.BlockSpec((1, vd), lambda h, b, i: (0, 0))],
        out_specs=pl.BlockSpec((tq, vd), lambda h, b, i: (b * nq + i, h)),
        out_shape=jax.ShapeDtypeStruct((T, H * vd), BF16),
        scratch_shapes=[pltpu.VMEM((tq, 2 * S), F32),
                        pltpu.VMEM((S, hw), BF16),
                        pltpu.VMEM((S, 2 * vd), BF16),
                        pltpu.VMEM((2, tq, 1), F32),
                        pltpu.VMEM((2, tq, 2 * vd), F32)],
        compiler_params=_params("arbitrary", "arbitrary", "arbitrary"),
        name="diff_attention",
    )(bucket, rel_bias.T, proj, proj, proj, jnp.tile(g_q, 2)[None], jnp.tile(g_k, 2)[None],
      lam_params, g_subln[None])


def _lru_body(x_ref, y_ref, cw_ref, cb_ref, w_ref, bias_ref, lam_ref, o_ref,
              a_ref, b_ref, hf_ref, hb_ref, *, S, W):
    x = x_ref[...].astype(F32)
    row = lax.broadcasted_iota(jnp.int32, (S, W), 0)
    cw = cw_ref[...]
    xc = (cw[0:1] * jnp.where(row >= 2, pltpu.roll(x, 2, 0), 0.0)
          + cw[1:2] * jnp.where(row >= 1, pltpu.roll(x, 1, 0), 0.0)
          + cw[2:3] * x
          + cw[3:4] * jnp.where(row < S - 1, pltpu.roll(x, S - 1, 0), 0.0)
          + cb_ref[...])
    gates = jnp.dot(xc.astype(BF16), w_ref[0], preferred_element_type=F32) + bias_ref[0]
    lam = lam_ref[...]
    for d in range(2):
        r = jax.nn.sigmoid(gates[:, (2 * d) * W:(2 * d + 1) * W])
        ig = jax.nn.sigmoid(gates[:, (2 * d + 1) * W:(2 * d + 2) * W])
        nl = -lam[d:d + 1]
        softplus = jnp.maximum(nl, 0.0) + jnp.log1p(jnp.exp(-jnp.abs(nl)))
        a = jnp.exp((-LRU_C * softplus) * r)
        a_ref[d] = a
        b_ref[d] = jnp.sqrt(jnp.maximum(1.0 - a * a, 0.0)) * ig * xc

    nv = S // SUBLANES
    ri = lax.broadcasted_iota(jnp.int32, (SUBLANES, W), 0)

    def step(j, carry):
        hf, hb = carry
        rf = pl.multiple_of(j * SUBLANES, SUBLANES)
        a = a_ref[0, pl.ds(rf, SUBLANES), :]
        bb = b_ref[0, pl.ds(rf, SUBLANES), :]
        for d in (1, 2, 4):
            keep = ri >= d
            bb = bb + a * jnp.where(keep, pltpu.roll(bb, d, 0), 0.0)
            a = a * jnp.where(keep, pltpu.roll(a, d, 0), 1.0)
        hfull = a * hf + bb
        hf_ref[pl.ds(rf, SUBLANES), :] = hfull
        hf = jnp.broadcast_to(hfull[SUBLANES - 1:SUBLANES, :], (SUBLANES, W))
        rb = pl.multiple_of((nv - 1 - j) * SUBLANES, SUBLANES)
        a = a_ref[1, pl.ds(rb, SUBLANES), :]
        bb = b_ref[1, pl.ds(rb, SUBLANES), :]
        for d in (1, 2, 4):
            keep = ri < SUBLANES - d
            bb = bb + a * jnp.where(keep, pltpu.roll(bb, SUBLANES - d, 0), 0.0)
            a = a * jnp.where(keep, pltpu.roll(a, SUBLANES - d, 0), 1.0)
        hfull = a * hb + bb
        hb_ref[pl.ds(rb, SUBLANES), :] = hfull
        hb = jnp.broadcast_to(hfull[0:1, :], (SUBLANES, W))
        return hf, hb

    zero = jnp.zeros((SUBLANES, W), F32)
    lax.fori_loop(0, nv, step, (zero, zero), unroll=4)

    y = y_ref[...].astype(F32)
    o_ref[...] = ((hf_ref[...] + hb_ref[...]) * jax.nn.gelu(y)).astype(o_ref.dtype)


def _rg_lru(proj, conv_w, conv_b, w_cat, b_cat, lru_lambda, *, B, S, off_x, off_y):
    T = B * S
    NB, W, _ = w_cat.shape
    xblk = off_x // W
    yblk = off_y // W
    body = functools.partial(_lru_body, S=S, W=W)
    return pl.pallas_call(
        body,
        grid=(B, NB),
        in_specs=[pl.BlockSpec((S, W), lambda b, n: (b, xblk + n)),
                  pl.BlockSpec((S, W), lambda b, n: (b, yblk + n)),
                  pl.BlockSpec((conv_w.shape[0], W), lambda b, n: (0, n)),
                  pl.BlockSpec((1, W), lambda b, n: (0, n)),
                  pl.BlockSpec((1, W, 4 * W), lambda b, n: (n, 0, 0)),
                  pl.BlockSpec((1, 1, 4 * W), lambda b, n: (n, 0, 0)),
                  pl.BlockSpec((2, W), lambda b, n: (0, n))],
        out_specs=pl.BlockSpec((S, W), lambda b, n: (b, n)),
        out_shape=jax.ShapeDtypeStruct((T, NB * W), BF16),
        scratch_shapes=[pltpu.VMEM((2, S, W), F32), pltpu.VMEM((2, S, W), F32),
                        pltpu.VMEM((S, W), F32), pltpu.VMEM((S, W), F32)],
        compiler_params=_params("parallel", "parallel"),
        name="rg_lru",
    )(proj, proj, conv_w, conv_b[None], w_cat, b_cat, lru_lambda)


def _merge_body(o_ref, r_ref, ga_ref, gr_ref, x_ref, wpa_ref, wpl_ref, wo_ref, g_ref, wr_ref,
                h_ref, hn_ref, aff_ref, *, E):
    ba = jnp.dot(o_ref[...], wpa_ref[...], preferred_element_type=F32)
    br = jnp.dot(r_ref[...], wpl_ref[...], preferred_element_type=F32)
    mixed = (jax.nn.sigmoid(ga_ref[...].astype(F32)) * ba
             + jax.nn.sigmoid(gr_ref[...].astype(F32)) * br)
    h = x_ref[...] + jnp.dot(mixed.astype(BF16), wo_ref[...], preferred_element_type=F32)
    h_ref[...] = h
    hn = h * lax.rsqrt(jnp.mean(h * h, axis=-1, keepdims=True) + EPS) * g_ref[...]
    hn_hi = hn.astype(BF16)
    hn_ref[...] = hn_hi
    hn_lo = (hn - hn_hi.astype(F32)).astype(BF16)
    lg = (jnp.dot(hn_hi, wr_ref[...], preferred_element_type=F32)
          + jnp.dot(hn_lo, wr_ref[...], preferred_element_type=F32))
    logits = lg[:, :E] + lg[:, E:2 * E]
    logits = logits - jnp.max(logits, axis=-1, keepdims=True)
    ex = jnp.exp(logits)
    aff_ref[...] = ex / jnp.sum(ex, axis=-1, keepdims=True)


def _merge(o_attn, lru_out, proj, x2, wpa, wpl, wo, g_ffn, wr2, *, off_ga, off_gr, E):
    T, D = x2.shape
    tm = min(512, T)
    gab = off_ga // D
    grb = off_gr // D
    row = lambda i: (i, 0)
    const = lambda i: (0, 0)
    return pl.pallas_call(
        functools.partial(_merge_body, E=E),
        grid=(T // tm,),
        in_specs=[pl.BlockSpec((tm, D), row), pl.BlockSpec((tm, D), row),
                  pl.BlockSpec((tm, D), lambda i: (i, gab)), pl.BlockSpec((tm, D), lambda i: (i, grb)),
                  pl.BlockSpec((tm, D), row),
                  pl.BlockSpec((D, D), const), pl.BlockSpec((D, D), const), pl.BlockSpec((D, D), const),
                  pl.BlockSpec((1, D), const), pl.BlockSpec((D, 2 * E), const)],
        out_specs=[pl.BlockSpec((tm, D), row), pl.BlockSpec((tm, D), row), pl.BlockSpec((tm, E), row)],
        out_shape=[jax.ShapeDtypeStruct((T, D), F32), jax.ShapeDtypeStruct((T, D), BF16),
                   jax.ShapeDtypeStruct((T, E), F32)],
        compiler_params=_params("parallel"),
        name="merge_router",
    )(o_attn, lru_out, proj, proj, x2, wpa, wpl, wo, g_ffn, wr2)


def _select_body(aff_ref, pos_ref, *, E, S, C):
    bits = pltpu.bitcast(aff_ref[0], jnp.int32)
    t = jnp.zeros((E, 1), jnp.int32)
    for bit in range(30, -1, -1):
        cand = t | (1 << bit)
        cnt = jnp.sum(jnp.where(bits >= cand, 1.0, 0.0), axis=-1, keepdims=True)
        t = jnp.where(cnt >= C, cand, t)
    gt = bits > t
    eq = bits == t
    need = C - jnp.sum(jnp.where(gt, 1.0, 0.0), axis=-1, keepdims=True).astype(jnp.int32)
    packed = jnp.where(gt, 1 << 16, 0) + jnp.where(eq, 1, 0)
    lane = lax.broadcasted_iota(jnp.int32, (E, S), 1)
    incl = packed
    d = 1
    while d < S:
        incl = incl + jnp.where(lane >= d, pltpu.roll(incl, d, 1), 0)
        d *= 2
    excl = incl - packed
    n_gt = excl >> 16
    n_eq = excl & 0xFFFF
    sel = gt | (eq & (n_eq < need))
    pos_ref[0] = jnp.where(sel, n_gt + jnp.minimum(n_eq, need), -1)


def _select(aff_t, C):
    B, E, S = aff_t.shape
    return pl.pallas_call(
        functools.partial(_select_body, E=E, S=S, C=C),
        grid=(B,),
        in_specs=[pl.BlockSpec((1, E, S), lambda b: (b, 0, 0))],
        out_specs=pl.BlockSpec((1, E, S), lambda b: (b, 0, 0)),
        out_shape=jax.ShapeDtypeStruct((B, E, S), jnp.int32),
        compiler_params=_params("parallel"),
        name="topc_select",
    )(aff_t)


def _ffn_body(pos_ref, hn_ref, wg_ref, wu_ref, wd_ref, y_ref, *, C, S, fc):
    slot = lax.broadcasted_iota(jnp.int32, (C, S), 0)
    onehot = jnp.where(pos_ref[0, 0] == slot, 1.0, 0.0).astype(BF16)
    xg = jnp.dot(onehot, hn_ref[0], preferred_element_type=F32).astype(BF16)
    F = wg_ref.shape[2]
    y = jnp.zeros(y_ref.shape[2:], F32)
    for f0 in range(0, F, fc):
        g = jnp.dot(xg, wg_ref[0, :, f0:f0 + fc], preferred_element_type=F32)
        u = jnp.dot(xg, wu_ref[0, :, f0:f0 + fc], preferred_element_type=F32)
        hid = (g * jax.nn.sigmoid(g) * u).astype(BF16)
        y = y + jnp.dot(hid, wd_ref[0, f0:f0 + fc, :], preferred_element_type=F32)
    y_ref[0, 0] = y.astype(y_ref.dtype)


def _expert_ffn(pos_rows, hn3, wg, wu, wd, C):
    B, S, D = hn3.shape
    E, _, F = wg.shape
    fc = min(512, F)
    return pl.pallas_call(
        functools.partial(_ffn_body, C=C, S=S, fc=fc),
        grid=(E, B),
        in_specs=[pl.BlockSpec((1, 1, 1, S), lambda e, b: (b, e, 0, 0)),
                  pl.BlockSpec((1, S, D), lambda e, b: (b, 0, 0)),
                  pl.BlockSpec((1, D, F), lambda e, b: (e, 0, 0)),
                  pl.BlockSpec((1, D, F), lambda e, b: (e, 0, 0)),
                  pl.BlockSpec((1, F, D), lambda e, b: (e, 0, 0))],
        out_specs=pl.BlockSpec((1, 1, C, D), lambda e, b: (b, e, 0, 0)),
        out_shape=jax.ShapeDtypeStruct((B, E, C, D), BF16),
        compiler_params=_params("arbitrary", "arbitrary"),
        name="expert_ffn",
    )(pos_rows, hn3, wg, wu, wd)


def _combine_body(pos_ref, aff_ref, y_ref, h_ref, o_ref, *, E, C):
    pc = pos_ref[0]
    ac = aff_ref[0]
    ts = pc.shape[0]
    slot = lax.broadcasted_iota(jnp.int32, (ts, C), 1)
    pieces = [jnp.where(pc[:, e:e + 1] == slot, ac[:, e:e + 1], 0.0).astype(BF16) for e in range(E)]
    scatter = jnp.concatenate(pieces, axis=1)
    y = y_ref[0].reshape(E * C, y_ref.shape[3])
    o_ref[0] = h_ref[0] + jnp.dot(scatter, y, preferred_element_type=F32)


def _combine(pos_cols, aff_cols, y, h3):
    B, S, D = h3.shape
    _, E, C, _ = y.shape
    ts = min(512, S)
    return pl.pallas_call(
        functools.partial(_combine_body, E=E, C=C),
        grid=(B, S // ts),
        in_specs=[pl.BlockSpec((1, ts, E), lambda b, i: (b, i, 0)),
                  pl.BlockSpec((1, ts, E), lambda b, i: (b, i, 0)),
                  pl.BlockSpec((1, E, C, D), lambda b, i: (b, 0, 0, 0)),
                  pl.BlockSpec((1, ts, D), lambda b, i: (b, i, 0))],
        out_specs=pl.BlockSpec((1, ts, D), lambda b, i: (b, i, 0)),
        out_shape=jax.ShapeDtypeStruct((B, S, D), F32),
        compiler_params=_params("parallel", "arbitrary"),
        name="moe_combine",
    )(pos_cols, aff_cols, y, h3)


def kernel(x, g_mix, w_in, g_q, g_k, lam_q1, lam_k1, lam_q2, lam_k2, g_subln, rel_bias, conv_w, conv_b,
           gate_r_w, gate_r_b, gate_i_w, gate_i_b, lru_lambda, w_proj_attn, w_proj_lru, w_out, g_ffn,
           w_router, w_gate_e, w_up_e, w_down_e):
    B, S, D = x.shape
    depth = w_in.shape[0]
    H = rel_bias.shape[1]
    hd = g_q.shape[-1]
    vd = g_subln.shape[-1]
    qk_w = H * 2 * hd
    attn_w = H * vd
    lru_w = conv_w.shape[-1]
    NB, LB = gate_r_w.shape[2], gate_r_w.shape[3]
    E = w_router.shape[-1]
    C = EC_CAPACITY_FACTOR * S // E
    off_k = qk_w
    off_v = off_k + qk_w
    off_x = off_v + attn_w
    off_y = off_x + lru_w
    off_ga = off_y + lru_w
    off_gr = off_ga + D

    h2 = x.reshape(B * S, D)
    for layer in range(depth):
        lam_init = 0.8 - 0.6 * math.exp(-0.3 * layer)
        proj = _inproj(h2, g_mix[layer][None], w_in[layer].astype(BF16))

        lam_params = jnp.stack([lam_q1[layer], lam_k1[layer], lam_q2[layer], lam_k2[layer]])
        o_attn = _attention(proj, rel_bias, g_q[layer], g_k[layer], lam_params, g_subln[layer],
                            B=B, S=S, H=H, hd=hd, vd=vd, off_k=off_k, off_v=off_v, lam_init=lam_init)

        w_cat = jnp.concatenate([gate_r_w[layer, 0], gate_i_w[layer, 0],
                                 gate_r_w[layer, 1], gate_i_w[layer, 1]], axis=-1).astype(BF16)
        b_cat = jnp.stack([gate_r_b[layer, 0].reshape(NB, LB), gate_i_b[layer, 0].reshape(NB, LB),
                           gate_r_b[layer, 1].reshape(NB, LB), gate_i_b[layer, 1].reshape(NB, LB)],
                          axis=1).reshape(NB, 1, 4 * LB)
        lru_out = _rg_lru(proj, conv_w[layer], conv_b[layer], w_cat, b_cat, lru_lambda[layer],
                          B=B, S=S, off_x=off_x, off_y=off_y)

        wr = w_router[layer]
        wr_hi = wr.astype(BF16)
        wr2 = jnp.concatenate([wr_hi, (wr - wr_hi.astype(F32)).astype(BF16)], axis=1)
        h2, hn, aff = _merge(o_attn, lru_out, proj, h2, w_proj_attn[layer].astype(BF16),
                             w_proj_lru[layer].astype(BF16), w_out[layer].astype(BF16),
                             g_ffn[layer][None], wr2, off_ga=off_ga, off_gr=off_gr, E=E)

        aff_cols = aff.reshape(B, S, E)
        pos_rows = _select(aff_cols.transpose(0, 2, 1), C)
        y = _expert_ffn(pos_rows.reshape(B, E, 1, S), hn.reshape(B, S, D), w_gate_e[layer].astype(BF16),
                        w_up_e[layer].astype(BF16), w_down_e[layer].astype(BF16), C)
        h3 = _combine(pos_rows.transpose(0, 2, 1), aff_cols, y, h2.reshape(B, S, D))
        h2 = h3.reshape(B * S, D)
    return h2.reshape(B, S, D)
```

```python
import functools
import math

import jax
import jax.numpy as jnp
import numpy as np
from jax import lax
from jax.experimental import pallas as pl
from jax.experimental.pallas import tpu as pltpu

F32 = jnp.float32
BF16 = jnp.bfloat16
EPS = 1e-6
LANES = 128
SUBLANES = 8
VMEM_LIMIT = 56 * 1024 * 1024
LOG2E = 1.4426950408889634
NEG_BIG = -1e30
REL_MAX_DIST = 128
LRU_C = 8.0
EC_CAPACITY_FACTOR = 2


def _params(*sem):
    return pltpu.CompilerParams(dimension_semantics=sem, vmem_limit_bytes=VMEM_LIMIT)


def _inproj_body(x_ref, g_ref, w_ref, o_ref, xn_ref):
    @pl.when(pl.program_id(1) == 0)
    def _():
        x = x_ref[...]
        ms = jnp.mean(x * x, axis=-1, keepdims=True)
        xn_ref[...] = (x * lax.rsqrt(ms + EPS) * g_ref[...]).astype(BF16)

    o_ref[...] = jnp.dot(xn_ref[...], w_ref[...], preferred_element_type=F32).astype(o_ref.dtype)


def _inproj(x2, g, w_bf):
    T, D = x2.shape
    N = w_bf.shape[1]
    tm = min(1024, T)
    tn = min(1024, N)
    return pl.pallas_call(
        _inproj_body,
        grid=(T // tm, N // tn),
        in_specs=[pl.BlockSpec((tm, D), lambda i, j: (i, 0)),
                  pl.BlockSpec((1, D), lambda i, j: (0, 0)),
                  pl.BlockSpec((D, tn), lambda i, j: (0, j))],
        out_specs=pl.BlockSpec((tm, tn), lambda i, j: (i, j)),
        out_shape=jax.ShapeDtypeStruct((T, N), BF16),
        scratch_shapes=[pltpu.VMEM((tm, D), BF16)],
        compiler_params=_params("parallel", "arbitrary"),
        name="in_proj",
    )(x2, g, w_bf)


def _bucket_table(S, n_buckets):
    rel = np.arange(2 * S) - S
    half = n_buckets // 2
    max_exact = half // 2
    ret = np.where(rel > 0, half, 0)
    n = np.abs(rel)
    nf = np.maximum(n, max_exact).astype(np.float64)
    large = max_exact + (np.log(nf / max_exact) / math.log(REL_MAX_DIST / max_exact)
                         * (half - max_exact)).astype(np.int32)
    large = np.minimum(large, half - 1)
    return (ret + np.where(n < max_exact, n, large)).astype(np.int32).reshape(1, 2 * S)


def _halves_sumsq(x, lo):
    x2 = x * x
    return (jnp.sum(jnp.where(lo, x2, 0.0), axis=-1, keepdims=True),
            jnp.sum(jnp.where(lo, 0.0, x2), axis=-1, keepdims=True))


def _halves_rmsnorm(x, hd):
    lane = lax.broadcasted_iota(jnp.int32, x.shape, 1)
    lo = lane < hd
    s_lo, s_hi = _halves_sumsq(x, lo)
    return x * lax.rsqrt(jnp.where(lo, s_lo, s_hi) * (1.0 / hd) + EPS), lo, lane


_ST_BMAX, _ST_BRANGE, _ST_KMAX1, _ST_KMAX2 = 0, 1, 2, 3
NORM_SLACK = 1.01
SAFE_LOG2_SPAN = 100.0


def _attn_body(bucket_ref, relb_ref, q_ref, k_ref, v_ref, gq_ref, gk_ref, lamp_ref, gs_ref, o_ref,
               e_ref, st_ref, k1_ref, k2_ref, va_ref, m_ref, acc_ref,
               *, S, tq, kc, hd, vd, lam_init, n_buckets):
    h = pl.program_id(0)
    b = pl.program_id(1)
    i = pl.program_id(2)

    def put_stat(row, val):
        st_ref[row:row + 1, :] = jnp.broadcast_to(val, (1, LANES))

    @pl.when((b == 0) & (i == 0))
    def _():
        bk = bucket_ref[...]
        tab = jnp.zeros((1, 2 * S), F32)
        for n in range(n_buckets):
            tab = jnp.where(bk == n, relb_ref[h, n], tab)
        tab = tab * LOG2E
        bmax = jnp.max(tab, axis=-1, keepdims=True)
        put_stat(_ST_BMAX, bmax)
        put_stat(_ST_BRANGE, bmax - jnp.min(tab, axis=-1, keepdims=True))
        e_ref[...] = pltpu.roll(jnp.broadcast_to(tab, (tq, 2 * S)), 0, 1, stride=1, stride_axis=0)

    @pl.when(i == 0)
    def _():
        kn, lo, lane = _halves_rmsnorm(k_ref[...].astype(F32), hd)
        kn = kn * gk_ref[...]
        n_lo, n_hi = _halves_sumsq(kn, lo)
        put_stat(_ST_KMAX1, jnp.sqrt(jnp.max(n_lo, axis=0, keepdims=True)))
        put_stat(_ST_KMAX2, jnp.sqrt(jnp.max(n_hi, axis=0, keepdims=True)))
        k1_ref[...] = jnp.where(lo, kn, jnp.where(lane == hd, 1.0, 0.0)).astype(BF16)
        k2_ref[...] = jnp.where(lo, jnp.where(lane == 0, 1.0, 0.0), kn).astype(BF16)
        va_ref[:, :vd] = v_ref[...]
        va_ref[:, vd:] = jnp.ones((S, vd), BF16)

    qn, lo, lane = _halves_rmsnorm(q_ref[...].astype(F32), hd)
    qn = qn * (gq_ref[...] * (hd ** -0.5 * LOG2E))
    u_lo, u_hi = _halves_sumsq(qn, lo)
    bmax = st_ref[_ST_BMAX:_ST_BMAX + 1, 0:1]
    r1 = NORM_SLACK * jnp.sqrt(u_lo) * st_ref[_ST_KMAX1:_ST_KMAX1 + 1, 0:1]
    r2 = NORM_SLACK * jnp.sqrt(u_hi) * st_ref[_ST_KMAX2:_ST_KMAX2 + 1, 0:1]
    q_maps = (jnp.where(lo, qn, jnp.where(lane == hd, -(r1 + bmax), 0.0)).astype(BF16),
              jnp.where(lo, jnp.where(lane == 0, -(r2 + bmax), 0.0), qn).astype(BF16))
    k_refs = (k1_ref, k2_ref)
    span = (2.0 * jnp.max(jnp.maximum(r1, r2), axis=0, keepdims=True)
            + st_ref[_ST_BRANGE:_ST_BRANGE + 1, 0:1])
    fast = span[0, 0] <= SAFE_LOG2_SPAN

    acc_ref[...] = jnp.zeros(acc_ref.shape, F32)

    def logits(c, mi):
        k0 = pl.multiple_of(c * kc, kc)
        e0 = pl.multiple_of(S + c * kc - i * tq, LANES)
        s = lax.dot_general(q_maps[mi], k_refs[mi][pl.ds(k0, kc), :], (((1,), (1,)), ((), ())),
                            preferred_element_type=F32)
        return s + e_ref[:, pl.ds(e0, kc)], va_ref[pl.ds(k0, kc), :]

    @pl.when(fast)
    def _():
        def chunk(c, carry):
            for mi in range(2):
                s, va = logits(c, mi)
                acc_ref[mi] += jnp.dot(jnp.exp2(s).astype(BF16), va, preferred_element_type=F32)
            return carry

        lax.fori_loop(0, S // kc, chunk, 0)

    @pl.when(jnp.logical_not(fast))
    def _():
        m_ref[...] = jnp.full(m_ref.shape, NEG_BIG, F32)

        def chunk(c, carry):
            for mi in range(2):
                s, va = logits(c, mi)
                m_old = m_ref[mi]
                m_new = jnp.maximum(m_old, jnp.max(s, axis=-1, keepdims=True))
                p = jnp.exp2(s - m_new).astype(BF16)
                acc_ref[mi] = (jnp.exp2(m_old - m_new) * acc_ref[mi]
                               + jnp.dot(p, va, preferred_element_type=F32))
                m_ref[mi] = m_new
            return carry

        lax.fori_loop(0, S // kc, chunk, 0)

    lp = lamp_ref[...]
    lam = (jnp.exp(jnp.sum(lp[0:1] * lp[1:2], axis=-1, keepdims=True))
           - jnp.exp(jnp.sum(lp[2:3] * lp[3:4], axis=-1, keepdims=True)) + lam_init)
    a1 = acc_ref[0]
    a2 = acc_ref[1]
    o = a1[:, :vd] / a1[:, vd:] - lam * (a2[:, :vd] / a2[:, vd:])
    o = o * lax.rsqrt(jnp.mean(o * o, axis=-1, keepdims=True) + EPS)
    o_ref[...] = (o * (gs_ref[...] * (1.0 - lam_init))).astype(o_ref.dtype)


def _attention(proj, rel_bias, g_q, g_k, lam_params, g_subln, *, B, S, H, hd, vd, off_k, off_v, lam_init):
    T = B * S
    tq = min(512, S)
    kc = min(512, S)
    nq = S // tq
    n_buckets = rel_bias.shape[0]
    hw = 2 * hd
    bucket = jnp.asarray(_bucket_table(S, n_buckets))
    body = functools.partial(_attn_body, S=S, tq=tq, kc=kc, hd=hd, vd=vd, lam_init=lam_init,
                             n_buckets=n_buckets)
    kblk = off_k // hw
    vblk = off_v // vd
    return pl.pallas_call(
        body,
        grid=(H, B, nq),
        in_specs=[pl.BlockSpec((1, 2 * S), lambda h, b, i: (0, 0)),
                  pl.BlockSpec(memory_space=pltpu.SMEM),
                  pl.BlockSpec((tq, hw), lambda h, b, i: (b * nq + i, h)),
                  pl.BlockSpec((S, hw), lambda h, b, i: (b, kblk + h)),
                  pl.BlockSpec((S, vd), lambda h, b, i: (b, vblk + h)),
                  pl.BlockSpec((1, hw), lambda h, b, i: (0, 0)),
                  pl.BlockSpec((1, hw), lambda h, b, i: (0, 0)),
                  pl.BlockSpec((4, hd), lambda h, b, i: (0, 0)),
                  pl.BlockSpec((1, vd), lambda h, b, i: (0, 0))],
        out_specs=pl.BlockSpec((tq, vd), lambda h, b, i: (b * nq + i, h)),
        out_shape=jax.ShapeDtypeStruct((T, H * vd), BF16),
        scratch_shapes=[pltpu.VMEM((tq, 2 * S), F32),
                        pltpu.VMEM((SUBLANES, LANES), F32),
                        pltpu.VMEM((S, hw), BF16),
                        pltpu.VMEM((S, hw), BF16),
                        pltpu.VMEM((S, 2 * vd), BF16),
                        pltpu.VMEM((2, tq, 1), F32),
                        pltpu.VMEM((2, tq, 2 * vd), F32)],
        compiler_params=_params("arbitrary", "arbitrary", "arbitrary"),
        name="diff_attention",
    )(bucket, rel_bias.T, proj, proj, proj, jnp.tile(g_q, 2)[None], jnp.tile(g_k, 2)[None],
      lam_params, g_subln[None])


def _lru_body(x_ref, y_ref, cw_ref, cb_ref, w_ref, bias_ref, lam_ref, o_ref,
              a_ref, b_ref, hf_ref, hb_ref, *, S, W):
    x = x_ref[...].astype(F32)
    row = lax.broadcasted_iota(jnp.int32, (S, W), 0)
    cw = cw_ref[...]
    xc = (cw[0:1] * jnp.where(row >= 2, pltpu.roll(x, 2, 0), 0.0)
          + cw[1:2] * jnp.where(row >= 1, pltpu.roll(x, 1, 0), 0.0)
          + cw[2:3] * x
          + cw[3:4] * jnp.where(row < S - 1, pltpu.roll(x, S - 1, 0), 0.0)
          + cb_ref[...])
    gates = jnp.dot(xc.astype(BF16), w_ref[0], preferred_element_type=F32) + bias_ref[0]
    lam = lam_ref[...]
    for d in range(2):
        r = jax.nn.sigmoid(gates[:, (2 * d) * W:(2 * d + 1) * W])
        ig = jax.nn.sigmoid(gates[:, (2 * d + 1) * W:(2 * d + 2) * W])
        nl = -lam[d:d + 1]
        softplus = jnp.maximum(nl, 0.0) + jnp.log1p(jnp.exp(-jnp.abs(nl)))
        a = jnp.exp((-LRU_C * softplus) * r)
        a_ref[d] = a
        b_ref[d] = jnp.sqrt(jnp.maximum(1.0 - a * a, 0.0)) * ig * xc

    nv = S // SUBLANES
    ri = lax.broadcasted_iota(jnp.int32, (SUBLANES, W), 0)

    def step(j, carry):
        hf, hb = carry
        rf = pl.multiple_of(j * SUBLANES, SUBLANES)
        a = a_ref[0, pl.ds(rf, SUBLANES), :]
        bb = b_ref[0, pl.ds(rf, SUBLANES), :]
        for d in (1, 2, 4):
            keep = ri >= d
            bb = bb + a * jnp.where(keep, pltpu.roll(bb, d, 0), 0.0)
            a = a * jnp.where(keep, pltpu.roll(a, d, 0), 1.0)
        hfull = a * hf + bb
        hf_ref[pl.ds(rf, SUBLANES), :] = hfull
        hf = jnp.broadcast_to(hfull[SUBLANES - 1:SUBLANES, :], (SUBLANES, W))
        rb = pl.multiple_of((nv - 1 - j) * SUBLANES, SUBLANES)
        a = a_ref[1, pl.ds(rb, SUBLANES), :]
        bb = b_ref[1, pl.ds(rb, SUBLANES), :]
        for d in (1, 2, 4):
            keep = ri < SUBLANES - d
            bb = bb + a * jnp.where(keep, pltpu.roll(bb, SUBLANES - d, 0), 0.0)
            a = a * jnp.where(keep, pltpu.roll(a, SUBLANES - d, 0), 1.0)
        hfull = a * hb + bb
        hb_ref[pl.ds(rb, SUBLANES), :] = hfull
        hb = jnp.broadcast_to(hfull[0:1, :], (SUBLANES, W))
        return hf, hb

    zero = jnp.zeros((SUBLANES, W), F32)
    lax.fori_loop(0, nv, step, (zero, zero), unroll=4)

    y = y_ref[...].astype(F32)
    o_ref[...] = ((hf_ref[...] + hb_ref[...]) * jax.nn.gelu(y)).astype(o_ref.dtype)


def _rg_lru(proj, conv_w, conv_b, w_cat, b_cat, lru_lambda, *, B, S, off_x, off_y):
    T = B * S
    NB, W, _ = w_cat.shape
    xblk = off_x // W
    yblk = off_y // W
    body = functools.partial(_lru_body, S=S, W=W)
    return pl.pallas_call(
        body,
        grid=(B, NB),
        in_specs=[pl.BlockSpec((S, W), lambda b, n: (b, xblk + n)),
                  pl.BlockSpec((S, W), lambda b, n: (b, yblk + n)),
                  pl.BlockSpec((conv_w.shape[0], W), lambda b, n: (0, n)),
                  pl.BlockSpec((1, W), lambda b, n: (0, n)),
                  pl.BlockSpec((1, W, 4 * W), lambda b, n: (n, 0, 0)),
                  pl.BlockSpec((1, 1, 4 * W), lambda b, n: (n, 0, 0)),
                  pl.BlockSpec((2, W), lambda b, n: (0, n))],
        out_specs=pl.BlockSpec((S, W), lambda b, n: (b, n)),
        out_shape=jax.ShapeDtypeStruct((T, NB * W), BF16),
        scratch_shapes=[pltpu.VMEM((2, S, W), F32), pltpu.VMEM((2, S, W), F32),
                        pltpu.VMEM((S, W), F32), pltpu.VMEM((S, W), F32)],
        compiler_params=_params("parallel", "parallel"),
        name="rg_lru",
    )(proj, proj, conv_w, conv_b[None], w_cat, b_cat, lru_lambda)


def _merge_body(o_ref, r_ref, ga_ref, gr_ref, x_ref, wpa_ref, wpl_ref, wo_ref, g_ref, wr_ref,
                h_ref, hn_ref, aff_ref, *, E):
    ba = jnp.dot(o_ref[...], wpa_ref[...], preferred_element_type=F32)
    br = jnp.dot(r_ref[...], wpl_ref[...], preferred_element_type=F32)
    mixed = (jax.nn.sigmoid(ga_ref[...].astype(F32)) * ba
             + jax.nn.sigmoid(gr_ref[...].astype(F32)) * br)
    h = x_ref[...] + jnp.dot(mixed.astype(BF16), wo_ref[...], preferred_element_type=F32)
    h_ref[...] = h
    hn = h * lax.rsqrt(jnp.mean(h * h, axis=-1, keepdims=True) + EPS) * g_ref[...]
    hn_hi = hn.astype(BF16)
    hn_ref[...] = hn_hi
    hn_lo = (hn - hn_hi.astype(F32)).astype(BF16)
    lg = (jnp.dot(hn_hi, wr_ref[...], preferred_element_type=F32)
          + jnp.dot(hn_lo, wr_ref[...], preferred_element_type=F32))
    logits = lg[:, :E] + lg[:, E:2 * E]
    logits = logits - jnp.max(logits, axis=-1, keepdims=True)
    ex = jnp.exp(logits)
    aff_ref[...] = ex / jnp.sum(ex, axis=-1, keepdims=True)


def _merge(o_attn, lru_out, proj, x2, wpa, wpl, wo, g_ffn, wr2, *, off_ga, off_gr, E):
    T, D = x2.shape
    tm = min(512, T)
    gab = off_ga // D
    grb = off_gr // D
    row = lambda i: (i, 0)
    const = lambda i: (0, 0)
    return pl.pallas_call(
        functools.partial(_merge_body, E=E),
        grid=(T // tm,),
        in_specs=[pl.BlockSpec((tm, D), row), pl.BlockSpec((tm, D), row),
                  pl.BlockSpec((tm, D), lambda i: (i, gab)), pl.BlockSpec((tm, D), lambda i: (i, grb)),
                  pl.BlockSpec((tm, D), row),
                  pl.BlockSpec((D, D), const), pl.BlockSpec((D, D), const), pl.BlockSpec((D, D), const),
                  pl.BlockSpec((1, D), const), pl.BlockSpec((D, 2 * E), const)],
        out_specs=[pl.BlockSpec((tm, D), row), pl.BlockSpec((tm, D), row), pl.BlockSpec((tm, E), row)],
        out_shape=[jax.ShapeDtypeStruct((T, D), F32), jax.ShapeDtypeStruct((T, D), BF16),
                   jax.ShapeDtypeStruct((T, E), F32)],
        compiler_params=_params("parallel"),
        name="merge_router",
    )(o_attn, lru_out, proj, proj, x2, wpa, wpl, wo, g_ffn, wr2)


def _select_body(aff_ref, pos_ref, *, E, S, C):
    bits = pltpu.bitcast(aff_ref[0], jnp.int32)
    t = jnp.zeros((E, 1), jnp.int32)
    for bit in range(30, -1, -1):
        cand = t | (1 << bit)
        cnt = jnp.sum(jnp.where(bits >= cand, 1.0, 0.0), axis=-1, keepdims=True)
        t = jnp.where(cnt >= C, cand, t)
    gt = bits > t
    eq = bits == t
    need = C - jnp.sum(jnp.where(gt, 1.0, 0.0), axis=-1, keepdims=True).astype(jnp.int32)
    packed = jnp.where(gt, 1 << 16, 0) + jnp.where(eq, 1, 0)
    lane = lax.broadcasted_iota(jnp.int32, (E, S), 1)
    incl = packed
    d = 1
    while d < S:
        incl = incl + jnp.where(lane >= d, pltpu.roll(incl, d, 1), 0)
        d *= 2
    excl = incl - packed
    n_gt = excl >> 16
    n_eq = excl & 0xFFFF
    sel = gt | (eq & (n_eq < need))
    pos_ref[0] = jnp.where(sel, n_gt + jnp.minimum(n_eq, need), -1)


def _select(aff_t, C):
    B, E, S = aff_t.shape
    return pl.pallas_call(
        functools.partial(_select_body, E=E, S=S, C=C),
        grid=(B,),
        in_specs=[pl.BlockSpec((1, E, S), lambda b: (b, 0, 0))],
        out_specs=pl.BlockSpec((1, E, S), lambda b: (b, 0, 0)),
        out_shape=jax.ShapeDtypeStruct((B, E, S), jnp.int32),
        compiler_params=_params("parallel"),
        name="topc_select",
    )(aff_t)


def _ffn_body(pos_ref, hn_ref, wg_ref, wu_ref, wd_ref, y_ref, *, C, S, fc):
    slot = lax.broadcasted_iota(jnp.int32, (C, S), 0)
    onehot = jnp.where(pos_ref[0, 0] == slot, 1.0, 0.0).astype(BF16)
    xg = jnp.dot(onehot, hn_ref[0], preferred_element_type=F32).astype(BF16)
    F = wg_ref.shape[2]
    y = jnp.zeros(y_ref.shape[2:], F32)
    for f0 in range(0, F, fc):
        g = jnp.dot(xg, wg_ref[0, :, f0:f0 + fc], preferred_element_type=F32)
        u = jnp.dot(xg, wu_ref[0, :, f0:f0 + fc], preferred_element_type=F32)
        hid = (g * jax.nn.sigmoid(g) * u).astype(BF16)
        y = y + jnp.dot(hid, wd_ref[0, f0:f0 + fc, :], preferred_element_type=F32)
    y_ref[0, 0] = y.astype(y_ref.dtype)


def _expert_ffn(pos_rows, hn3, wg, wu, wd, C):
    B, S, D = hn3.shape
    E, _, F = wg.shape
    fc = min(512, F)
    return pl.pallas_call(
        functools.partial(_ffn_body, C=C, S=S, fc=fc),
        grid=(E, B),
        in_specs=[pl.BlockSpec((1, 1, 1, S), lambda e, b: (b, e, 0, 0)),
                  pl.BlockSpec((1, S, D), lambda e, b: (b, 0, 0)),
                  pl.BlockSpec((1, D, F), lambda e, b: (e, 0, 0)),
                  pl.BlockSpec((1, D, F), lambda e, b: (e, 0, 0)),
                  pl.BlockSpec((1, F, D), lambda e, b: (e, 0, 0))],
        out_specs=pl.BlockSpec((1, 1, C, D), lambda e, b: (b, e, 0, 0)),
        out_shape=jax.ShapeDtypeStruct((B, E, C, D), BF16),
        compiler_params=_params("arbitrary", "arbitrary"),
        name="expert_ffn",
    )(pos_rows, hn3, wg, wu, wd)


def _combine_body(pos_ref, aff_ref, y_ref, h_ref, o_ref, *, E, C):
    pc = pos_ref[0]
    ac = aff_ref[0]
    ts = pc.shape[0]
    slot = lax.broadcasted_iota(jnp.int32, (ts, C), 1)
    pieces = [jnp.where(pc[:, e:e + 1] == slot, ac[:, e:e + 1], 0.0).astype(BF16) for e in range(E)]
    scatter = jnp.concatenate(pieces, axis=1)
    y = y_ref[0].reshape(E * C, y_ref.shape[3])
    o_ref[0] = h_ref[0] + jnp.dot(scatter, y, preferred_element_type=F32)


def _combine(pos_cols, aff_cols, y, h3):
    B, S, D = h3.shape
    _, E, C, _ = y.shape
    ts = min(512, S)
    return pl.pallas_call(
        functools.partial(_combine_body, E=E, C=C),
        grid=(B, S // ts),
        in_specs=[pl.BlockSpec((1, ts, E), lambda b, i: (b, i, 0)),
                  pl.BlockSpec((1, ts, E), lambda b, i: (b, i, 0)),
                  pl.BlockSpec((1, E, C, D), lambda b, i: (b, 0, 0, 0)),
                  pl.BlockSpec((1, ts, D), lambda b, i: (b, i, 0))],
        out_specs=pl.BlockSpec((1, ts, D), lambda b, i: (b, i, 0)),
        out_shape=jax.ShapeDtypeStruct((B, S, D), F32),
        compiler_params=_params("parallel", "arbitrary"),
        name="moe_combine",
    )(pos_cols, aff_cols, y, h3)


def kernel(x, g_mix, w_in, g_q, g_k, lam_q1, lam_k1, lam_q2, lam_k2, g_subln, rel_bias, conv_w, conv_b,
           gate_r_w, gate_r_b, gate_i_w, gate_i_b, lru_lambda, w_proj_attn, w_proj_lru, w_out, g_ffn,
           w_router, w_gate_e, w_up_e, w_down_e):
    B, S, D = x.shape
    depth = w_in.shape[0]
    H = rel_bias.shape[1]
    hd = g_q.shape[-1]
    vd = g_subln.shape[-1]
    qk_w = H * 2 * hd
    attn_w = H * vd
    lru_w = conv_w.shape[-1]
    NB, LB = gate_r_w.shape[2], gate_r_w.shape[3]
    E = w_router.shape[-1]
    C = EC_CAPACITY_FACTOR * S // E
    off_k = qk_w
    off_v = off_k + qk_w
    off_x = off_v + attn_w
    off_y = off_x + lru_w
    off_ga = off_y + lru_w
    off_gr = off_ga + D

    h2 = x.reshape(B * S, D)
    for layer in range(depth):
        lam_init = 0.8 - 0.6 * math.exp(-0.3 * layer)
        proj = _inproj(h2, g_mix[layer][None], w_in[layer].astype(BF16))

        lam_params = jnp.stack([lam_q1[layer], lam_k1[layer], lam_q2[layer], lam_k2[layer]])
        o_attn = _attention(proj, rel_bias, g_q[layer], g_k[layer], lam_params, g_subln[layer],
                            B=B, S=S, H=H, hd=hd, vd=vd, off_k=off_k, off_v=off_v, lam_init=lam_init)

        w_cat = jnp.concatenate([gate_r_w[layer, 0], gate_i_w[layer, 0],
                                 gate_r_w[layer, 1], gate_i_w[layer, 1]], axis=-1).astype(BF16)
        b_cat = jnp.stack([gate_r_b[layer, 0].reshape(NB, LB), gate_i_b[layer, 0].reshape(NB, LB),
                           gate_r_b[layer, 1].reshape(NB, LB), gate_i_b[layer, 1].reshape(NB, LB)],
                          axis=1).reshape(NB, 1, 4 * LB)
        lru_out = _rg_lru(proj, conv_w[layer], conv_b[layer], w_cat, b_cat, lru_lambda[layer],
                          B=B, S=S, off_x=off_x, off_y=off_y)

        wr = w_router[layer]
        wr_hi = wr.astype(BF16)
        wr2 = jnp.concatenate([wr_hi, (wr - wr_hi.astype(F32)).astype(BF16)], axis=1)
        h2, hn, aff = _merge(o_attn, lru_out, proj, h2, w_proj_attn[layer].astype(BF16),
                             w_proj_lru[layer].astype(BF16), w_out[layer].astype(BF16),
                             g_ffn[layer][None], wr2, off_ga=off_ga, off_gr=off_gr, E=E)

        aff_cols = aff.reshape(B, S, E)
        pos_rows = _select(aff_cols.transpose(0, 2, 1), C)
        y = _expert_ffn(pos_rows.reshape(B, E, 1, S), hn.reshape(B, S, D), w_gate_e[layer].astype(BF16),
                        w_up_e[layer].astype(BF16), w_down_e[layer].astype(BF16), C)
        h3 = _combine(pos_rows.transpose(0, 2, 1), aff_cols, y, h2.reshape(B, S, D))
        h2 = h3.reshape(B * S, D)
    return h2.reshape(B, S, D)
```

```python
import functools
import math

import jax
import jax.numpy as jnp
import numpy as np
from jax import lax
from jax.experimental import pallas as pl
from jax.experimental.pallas import tpu as pltpu

F32 = jnp.float32
BF16 = jnp.bfloat16
EPS = 1e-6
LANES = 128
SUBLANES = 8
VMEM_LIMIT = 56 * 1024 * 1024
LOG2E = 1.4426950408889634
NEG_BIG = -1e30
REL_MAX_DIST = 128
LRU_C = 8.0
EC_CAPACITY_FACTOR = 2


def _params(*sem):
    return pltpu.CompilerParams(dimension_semantics=sem, vmem_limit_bytes=VMEM_LIMIT)


def _inproj_body(x_ref, g_ref, w_ref, o_ref, xn_ref):
    @pl.when(pl.program_id(1) == 0)
    def _():
        x = x_ref[...]
        ms = jnp.mean(x * x, axis=-1, keepdims=True)
        xn_ref[...] = (x * lax.rsqrt(ms + EPS) * g_ref[...]).astype(BF16)

    o_ref[...] = jnp.dot(xn_ref[...], w_ref[...], preferred_element_type=F32).astype(o_ref.dtype)


def _inproj(x2, g, w_bf):
    T, D = x2.shape
    N = w_bf.shape[1]
    tm = min(1024, T)
    tn = min(1024, N)
    return pl.pallas_call(
        _inproj_body,
        grid=(T // tm, N // tn),
        in_specs=[pl.BlockSpec((tm, D), lambda i, j: (i, 0)),
                  pl.BlockSpec((1, D), lambda i, j: (0, 0)),
                  pl.BlockSpec((D, tn), lambda i, j: (0, j))],
        out_specs=pl.BlockSpec((tm, tn), lambda i, j: (i, j)),
        out_shape=jax.ShapeDtypeStruct((T, N), BF16),
        scratch_shapes=[pltpu.VMEM((tm, D), BF16)],
        compiler_params=_params("parallel", "arbitrary"),
        name="in_proj",
    )(x2, g, w_bf)


def _bucket_table(S, n_buckets):
    rel = np.arange(2 * S) - S
    half = n_buckets // 2
    max_exact = half // 2
    ret = np.where(rel > 0, half, 0)
    n = np.abs(rel)
    nf = np.maximum(n, max_exact).astype(np.float64)
    large = max_exact + (np.log(nf / max_exact) / math.log(REL_MAX_DIST / max_exact)
                         * (half - max_exact)).astype(np.int32)
    large = np.minimum(large, half - 1)
    return (ret + np.where(n < max_exact, n, large)).astype(np.int32).reshape(1, 2 * S)


def _seg_sumsq(x, seg_ones):
    x2 = x * x
    hi = x2.astype(BF16)
    lo = (x2 - hi.astype(F32)).astype(BF16)
    return (jnp.dot(hi, seg_ones, preferred_element_type=F32)
            + jnp.dot(lo, seg_ones, preferred_element_type=F32))


NORM_SLACK = 1.01
SAFE_LOG2_SPAN = 100.0


def _attn_body(bucket_ref, relb_ref, q_ref, k_ref, v_ref, gq_ref, gk_ref, lamp_ref, gs_ref, segq_ref,
               segv_ref, o_ref, e_ref, fast_ref, kn_ref, va_ref, m_ref, acc_ref,
               *, S, tq, kc, hd, vd, lam_init, n_buckets):
    h = pl.program_id(0)
    b = pl.program_id(1)
    i = pl.program_id(2)
    q_scale = hd ** -0.5 * LOG2E

    @pl.when((b == 0) & (i == 0))
    def _():
        bk = bucket_ref[...]
        tab = jnp.zeros((1, 2 * S), F32)
        for n in range(n_buckets):
            tab = jnp.where(bk == n, relb_ref[h, n], tab)
        tab = tab * LOG2E
        bmax = jnp.max(tab, axis=-1, keepdims=True)
        bmin = jnp.min(tab, axis=-1, keepdims=True)
        bound = (NORM_SLACK * hd * q_scale) * (jnp.max(jnp.abs(gq_ref[...]), axis=-1, keepdims=True)
                                               * jnp.max(jnp.abs(gk_ref[...]), axis=-1, keepdims=True))
        span = 2.0 * bound + (bmax - bmin)
        fast_ref[0] = (span[0, 0] <= SAFE_LOG2_SPAN).astype(jnp.int32)
        tab = jnp.broadcast_to(tab - (bound + bmax), (tq, 2 * S))
        e_ref[...] = pltpu.roll(tab, 0, 1, stride=1, stride_axis=0)

    @pl.when(i == 0)
    def _():
        k = k_ref[...].astype(F32)
        ms = _seg_sumsq(k, segq_ref[...]) * (1.0 / hd)
        kn_ref[...] = (k * lax.rsqrt(ms + EPS) * gk_ref[...]).astype(BF16)
        va_ref[:, :vd] = v_ref[...]
        va_ref[:, vd:] = jnp.ones((S, vd), BF16)

    q = q_ref[...].astype(F32)
    ms = _seg_sumsq(q, segq_ref[...]) * (1.0 / hd)
    qn = q * lax.rsqrt(ms + EPS) * (gq_ref[...] * q_scale)
    lo = lax.broadcasted_iota(jnp.int32, qn.shape, 1) < hd
    q_maps = (jnp.where(lo, qn, 0.0).astype(BF16), jnp.where(lo, 0.0, qn).astype(BF16))
    nt_dims = (((1,), (1,)), ((), ()))

    @pl.when(fast_ref[0] == 1)
    def _():
        bias = e_ref[:, pl.ds(pl.multiple_of(S - i * tq, LANES), S)]
        for mi in range(2):
            s = lax.dot_general(q_maps[mi], kn_ref[...], nt_dims, preferred_element_type=F32)
            acc_ref[mi] = jnp.dot(jnp.exp2(s + bias).astype(BF16), va_ref[...],
                                  preferred_element_type=F32)

    @pl.when(fast_ref[0] != 1)
    def _():
        m_ref[...] = jnp.full(m_ref.shape, NEG_BIG, F32)
        acc_ref[...] = jnp.zeros(acc_ref.shape, F32)

        def chunk(c, carry):
            k0 = pl.multiple_of(c * kc, kc)
            bias = e_ref[:, pl.ds(pl.multiple_of(S + c * kc - i * tq, LANES), kc)]
            for mi in range(2):
                s = lax.dot_general(q_maps[mi], kn_ref[pl.ds(k0, kc), :], nt_dims,
                                    preferred_element_type=F32) + bias
                m_old = m_ref[mi]
                m_new = jnp.maximum(m_old, jnp.max(s, axis=-1, keepdims=True))
                p = jnp.exp2(s - m_new).astype(BF16)
                acc_ref[mi] = (jnp.exp2(m_old - m_new) * acc_ref[mi]
                               + jnp.dot(p, va_ref[pl.ds(k0, kc), :], preferred_element_type=F32))
                m_ref[mi] = m_new
            return carry

        lax.fori_loop(0, S // kc, chunk, 0)

    lp = lamp_ref[...]
    lam = (jnp.exp(jnp.sum(lp[0:1] * lp[1:2], axis=-1, keepdims=True))
           - jnp.exp(jnp.sum(lp[2:3] * lp[3:4], axis=-1, keepdims=True)) + lam_init)
    a1 = acc_ref[0]
    a2 = acc_ref[1]
    o = a1[:, :vd] / a1[:, vd:] - lam * (a2[:, :vd] / a2[:, vd:])
    o = o * lax.rsqrt(_seg_sumsq(o, segv_ref[...]) * (1.0 / vd) + EPS)
    o_ref[...] = (o * (gs_ref[...] * (1.0 - lam_init))).astype(o_ref.dtype)


def _attention(proj, rel_bias, g_q, g_k, lam_params, g_subln, *, B, S, H, hd, vd, off_k, off_v, lam_init):
    T = B * S
    tq = min(512, S)
    kc = min(512, S)
    nq = S // tq
    n_buckets = rel_bias.shape[0]
    hw = 2 * hd
    bucket = jnp.asarray(_bucket_table(S, n_buckets))
    body = functools.partial(_attn_body, S=S, tq=tq, kc=kc, hd=hd, vd=vd, lam_init=lam_init,
                             n_buckets=n_buckets)
    kblk = off_k // hw
    vblk = off_v // vd
    seg_q = np.kron(np.eye(2, dtype=np.float32), np.ones((hd, hd), np.float32))
    return pl.pallas_call(
        body,
        grid=(H, B, nq),
        in_specs=[pl.BlockSpec((1, 2 * S), lambda h, b, i: (0, 0)),
                  pl.BlockSpec(memory_space=pltpu.SMEM),
                  pl.BlockSpec((tq, hw), lambda h, b, i: (b * nq + i, h)),
                  pl.BlockSpec((S, hw), lambda h, b, i: (b, kblk + h)),
                  pl.BlockSpec((S, vd), lambda h, b, i: (b, vblk + h)),
                  pl.BlockSpec((1, hw), lambda h, b, i: (0, 0)),
                  pl.BlockSpec((1, hw), lambda h, b, i: (0, 0)),
                  pl.BlockSpec((4, hd), lambda h, b, i: (0, 0)),
                  pl.BlockSpec((1, vd), lambda h, b, i: (0, 0)),
                  pl.BlockSpec((hw, hw), lambda h, b, i: (0, 0)),
                  pl.BlockSpec((vd, vd), lambda h, b, i: (0, 0))],
        out_specs=pl.BlockSpec((tq, vd), lambda h, b, i: (b * nq + i, h)),
        out_shape=jax.ShapeDtypeStruct((T, H * vd), BF16),
        scratch_shapes=[pltpu.VMEM((tq, 2 * S), F32),
                        pltpu.SMEM((1,), jnp.int32),
                        pltpu.VMEM((S, hw), BF16),
                        pltpu.VMEM((S, 2 * vd), BF16),
                        pltpu.VMEM((2, tq, 1), F32),
                        pltpu.VMEM((2, tq, 2 * vd), F32)],
        compiler_params=_params("arbitrary", "arbitrary", "arbitrary"),
        name="diff_attention",
    )(bucket, rel_bias.T, proj, proj, proj, jnp.tile(g_q, 2)[None], jnp.tile(g_k, 2)[None],
      lam_params, g_subln[None], jnp.asarray(seg_q, BF16), jnp.ones((vd, vd), BF16))


def _lru_body(x_ref, y_ref, cw_ref, cb_ref, w_ref, bias_ref, lam_ref, o_ref,
              a_ref, b_ref, hf_ref, hb_ref, *, S, W):
    x = x_ref[...].astype(F32)
    row = lax.broadcasted_iota(jnp.int32, (S, W), 0)
    cw = cw_ref[...]
    xc = (cw[0:1] * jnp.where(row >= 2, pltpu.roll(x, 2, 0), 0.0)
          + cw[1:2] * jnp.where(row >= 1, pltpu.roll(x, 1, 0), 0.0)
          + cw[2:3] * x
          + cw[3:4] * jnp.where(row < S - 1, pltpu.roll(x, S - 1, 0), 0.0)
          + cb_ref[...])
    gates = jnp.dot(xc.astype(BF16), w_ref[0], preferred_element_type=F32) + bias_ref[0]
    lam = lam_ref[...]
    for d in range(2):
        r = jax.nn.sigmoid(gates[:, (2 * d) * W:(2 * d + 1) * W])
        ig = jax.nn.sigmoid(gates[:, (2 * d + 1) * W:(2 * d + 2) * W])
        nl = -lam[d:d + 1]
        softplus = jnp.maximum(nl, 0.0) + jnp.log1p(jnp.exp(-jnp.abs(nl)))
        a = jnp.exp((-LRU_C * softplus) * r)
        a_ref[d] = a
        b_ref[d] = jnp.sqrt(jnp.maximum(1.0 - a * a, 0.0)) * ig * xc

    nv = S // SUBLANES
    ri = lax.broadcasted_iota(jnp.int32, (SUBLANES, W), 0)

    def step(j, carry):
        hf, hb = carry
        rf = pl.multiple_of(j * SUBLANES, SUBLANES)
        a = a_ref[0, pl.ds(rf, SUBLANES), :]
        bb = b_ref[0, pl.ds(rf, SUBLANES), :]
        for d in (1, 2, 4):
            keep = ri >= d
            bb = bb + a * jnp.where(keep, pltpu.roll(bb, d, 0), 0.0)
            a = a * jnp.where(keep, pltpu.roll(a, d, 0), 1.0)
        hfull = a * hf + bb
        hf_ref[pl.ds(rf, SUBLANES), :] = hfull
        hf = jnp.broadcast_to(hfull[SUBLANES - 1:SUBLANES, :], (SUBLANES, W))
        rb = pl.multiple_of((nv - 1 - j) * SUBLANES, SUBLANES)
        a = a_ref[1, pl.ds(rb, SUBLANES), :]
        bb = b_ref[1, pl.ds(rb, SUBLANES), :]
        for d in (1, 2, 4):
            keep = ri < SUBLANES - d
            bb = bb + a * jnp.where(keep, pltpu.roll(bb, SUBLANES - d, 0), 0.0)
            a = a * jnp.where(keep, pltpu.roll(a, SUBLANES - d, 0), 1.0)
        hfull = a * hb + bb
        hb_ref[pl.ds(rb, SUBLANES), :] = hfull
        hb = jnp.broadcast_to(hfull[0:1, :], (SUBLANES, W))
        return hf, hb

    zero = jnp.zeros((SUBLANES, W), F32)
    lax.fori_loop(0, nv, step, (zero, zero), unroll=4)

    y = y_ref[...].astype(F32)
    o_ref[...] = ((hf_ref[...] + hb_ref[...]) * jax.nn.gelu(y)).astype(o_ref.dtype)


def _rg_lru(proj, conv_w, conv_b, w_cat, b_cat, lru_lambda, *, B, S, off_x, off_y):
    T = B * S
    NB, W, _ = w_cat.shape
    xblk = off_x // W
    yblk = off_y // W
    body = functools.partial(_lru_body, S=S, W=W)
    return pl.pallas_call(
        body,
        grid=(B, NB),
        in_specs=[pl.BlockSpec((S, W), lambda b, n: (b, xblk + n)),
                  pl.BlockSpec((S, W), lambda b, n: (b, yblk + n)),
                  pl.BlockSpec((conv_w.shape[0], W), lambda b, n: (0, n)),
                  pl.BlockSpec((1, W), lambda b, n: (0, n)),
                  pl.BlockSpec((1, W, 4 * W), lambda b, n: (n, 0, 0)),
                  pl.BlockSpec((1, 1, 4 * W), lambda b, n: (n, 0, 0)),
                  pl.BlockSpec((2, W), lambda b, n: (0, n))],
        out_specs=pl.BlockSpec((S, W), lambda b, n: (b, n)),
        out_shape=jax.ShapeDtypeStruct((T, NB * W), BF16),
        scratch_shapes=[pltpu.VMEM((2, S, W), F32), pltpu.VMEM((2, S, W), F32),
                        pltpu.VMEM((S, W), F32), pltpu.VMEM((S, W), F32)],
        compiler_params=_params("parallel", "parallel"),
        name="rg_lru",
    )(proj, proj, conv_w, conv_b[None], w_cat, b_cat, lru_lambda)


def _merge_body(o_ref, r_ref, ga_ref, gr_ref, x_ref, wpa_ref, wpl_ref, wo_ref, g_ref, wr_ref,
                h_ref, hn_ref, aff_ref, *, E):
    ba = jnp.dot(o_ref[...], wpa_ref[...], preferred_element_type=F32)
    br = jnp.dot(r_ref[...], wpl_ref[...], preferred_element_type=F32)
    mixed = (jax.nn.sigmoid(ga_ref[...].astype(F32)) * ba
             + jax.nn.sigmoid(gr_ref[...].astype(F32)) * br)
    h = x_ref[...] + jnp.dot(mixed.astype(BF16), wo_ref[...], preferred_element_type=F32)
    h_ref[...] = h
    hn = h * lax.rsqrt(jnp.mean(h * h, axis=-1, keepdims=True) + EPS) * g_ref[...]
    hn_hi = hn.astype(BF16)
    hn_ref[...] = hn_hi
    hn_lo = (hn - hn_hi.astype(F32)).astype(BF16)
    lg = (jnp.dot(hn_hi, wr_ref[...], preferred_element_type=F32)
          + jnp.dot(hn_lo, wr_ref[...], preferred_element_type=F32))
    logits = lg[:, :E] + lg[:, E:2 * E]
    logits = logits - jnp.max(logits, axis=-1, keepdims=True)
    ex = jnp.exp(logits)
    aff_ref[...] = ex / jnp.sum(ex, axis=-1, keepdims=True)


def _merge(o_attn, lru_out, proj, x2, wpa, wpl, wo, g_ffn, wr2, *, off_ga, off_gr, E):
    T, D = x2.shape
    tm = min(512, T)
    gab = off_ga // D
    grb = off_gr // D
    row = lambda i: (i, 0)
    const = lambda i: (0, 0)
    return pl.pallas_call(
        functools.partial(_merge_body, E=E),
        grid=(T // tm,),
        in_specs=[pl.BlockSpec((tm, D), row), pl.BlockSpec((tm, D), row),
                  pl.BlockSpec((tm, D), lambda i: (i, gab)), pl.BlockSpec((tm, D), lambda i: (i, grb)),
                  pl.BlockSpec((tm, D), row),
                  pl.BlockSpec((D, D), const), pl.BlockSpec((D, D), const), pl.BlockSpec((D, D), const),
                  pl.BlockSpec((1, D), const), pl.BlockSpec((D, 2 * E), const)],
        out_specs=[pl.BlockSpec((tm, D), row), pl.BlockSpec((tm, D), row), pl.BlockSpec((tm, E), row)],
        out_shape=[jax.ShapeDtypeStruct((T, D), F32), jax.ShapeDtypeStruct((T, D), BF16),
                   jax.ShapeDtypeStruct((T, E), F32)],
        compiler_params=_params("parallel"),
        name="merge_router",
    )(o_attn, lru_out, proj, proj, x2, wpa, wpl, wo, g_ffn, wr2)


def _select_body(aff_ref, pos_ref, *, E, S, C):
    bits = pltpu.bitcast(aff_ref[0], jnp.int32)
    t = jnp.zeros((E, 1), jnp.int32)
    for bit in range(30, -1, -1):
        cand = t | (1 << bit)
        cnt = jnp.sum(jnp.where(bits >= cand, 1.0, 0.0), axis=-1, keepdims=True)
        t = jnp.where(cnt >= C, cand, t)
    gt = bits > t
    eq = bits == t
    need = C - jnp.sum(jnp.where(gt, 1.0, 0.0), axis=-1, keepdims=True).astype(jnp.int32)
    packed = jnp.where(gt, 1 << 16, 0) + jnp.where(eq, 1, 0)
    lane = lax.broadcasted_iota(jnp.int32, (E, S), 1)
    incl = packed
    d = 1
    while d < S:
        incl = incl + jnp.where(lane >= d, pltpu.roll(incl, d, 1), 0)
        d *= 2
    excl = incl - packed
    n_gt = excl >> 16
    n_eq = excl & 0xFFFF
    sel = gt | (eq & (n_eq < need))
    pos_ref[0] = jnp.where(sel, n_gt + jnp.minimum(n_eq, need), -1)


def _select(aff_t, C):
    B, E, S = aff_t.shape
    return pl.pallas_call(
        functools.partial(_select_body, E=E, S=S, C=C),
        grid=(B,),
        in_specs=[pl.BlockSpec((1, E, S), lambda b: (b, 0, 0))],
        out_specs=pl.BlockSpec((1, E, S), lambda b: (b, 0, 0)),
        out_shape=jax.ShapeDtypeStruct((B, E, S), jnp.int32),
        compiler_params=_params("parallel"),
        name="topc_select",
    )(aff_t)


def _ffn_body(pos_ref, hn_ref, wg_ref, wu_ref, wd_ref, y_ref, *, C, S, fc):
    slot = lax.broadcasted_iota(jnp.int32, (C, S), 0)
    onehot = jnp.where(pos_ref[0, 0] == slot, 1.0, 0.0).astype(BF16)
    xg = jnp.dot(onehot, hn_ref[0], preferred_element_type=F32).astype(BF16)
    F = wg_ref.shape[2]
    y = jnp.zeros(y_ref.shape[2:], F32)
    for f0 in range(0, F, fc):
        g = jnp.dot(xg, wg_ref[0, :, f0:f0 + fc], preferred_element_type=F32)
        u = jnp.dot(xg, wu_ref[0, :, f0:f0 + fc], preferred_element_type=F32)
        hid = (g * jax.nn.sigmoid(g) * u).astype(BF16)
        y = y + jnp.dot(hid, wd_ref[0, f0:f0 + fc, :], preferred_element_type=F32)
    y_ref[0, 0] = y.astype(y_ref.dtype)


def _expert_ffn(pos_rows, hn3, wg, wu, wd, C):
    B, S, D = hn3.shape
    E, _, F = wg.shape
    fc = min(512, F)
    return pl.pallas_call(
        functools.partial(_ffn_body, C=C, S=S, fc=fc),
        grid=(E, B),
        in_specs=[pl.BlockSpec((1, 1, 1, S), lambda e, b: (b, e, 0, 0)),
                  pl.BlockSpec((1, S, D), lambda e, b: (b, 0, 0)),
                  pl.BlockSpec((1, D, F), lambda e, b: (e, 0, 0)),
                  pl.BlockSpec((1, D, F), lambda e, b: (e, 0, 0)),
                  pl.BlockSpec((1, F, D), lambda e, b: (e, 0, 0))],
        out_specs=pl.BlockSpec((1, 1, C, D), lambda e, b: (b, e, 0, 0)),
        out_shape=jax.ShapeDtypeStruct((B, E, C, D), BF16),
        compiler_params=_params("arbitrary", "arbitrary"),
        name="expert_ffn",
    )(pos_rows, hn3, wg, wu, wd)


def _combine_body(pos_ref, aff_ref, y_ref, h_ref, o_ref, *, E, C):
    pc = pos_ref[0]
    ac = aff_ref[0]
    ts = pc.shape[0]
    slot = lax.broadcasted_iota(jnp.int32, (ts, C), 1)
    pieces = [jnp.where(pc[:, e:e + 1] == slot, ac[:, e:e + 1], 0.0).astype(BF16) for e in range(E)]
    scatter = jnp.concatenate(pieces, axis=1)
    y = y_ref[0].reshape(E * C, y_ref.shape[3])
    o_ref[0] = h_ref[0] + jnp.dot(scatter, y, preferred_element_type=F32)


def _combine(pos_cols, aff_cols, y, h3):
    B, S, D = h3.shape
    _, E, C, _ = y.shape
    ts = min(512, S)
    return pl.pallas_call(
        functools.partial(_combine_body, E=E, C=C),
        grid=(B, S // ts),
        in_specs=[pl.BlockSpec((1, ts, E), lambda b, i: (b, i, 0)),
                  pl.BlockSpec((1, ts, E), lambda b, i: (b, i, 0)),
                  pl.BlockSpec((1, E, C, D), lambda b, i: (b, 0, 0, 0)),
                  pl.BlockSpec((1, ts, D), lambda b, i: (b, i, 0))],
        out_specs=pl.BlockSpec((1, ts, D), lambda b, i: (b, i, 0)),
        out_shape=jax.ShapeDtypeStruct((B, S, D), F32),
        compiler_params=_params("parallel", "arbitrary"),
        name="moe_combine",
    )(pos_cols, aff_cols, y, h3)


def kernel(x, g_mix, w_in, g_q, g_k, lam_q1, lam_k1, lam_q2, lam_k2, g_subln, rel_bias, conv_w, conv_b,
           gate_r_w, gate_r_b, gate_i_w, gate_i_b, lru_lambda, w_proj_attn, w_proj_lru, w_out, g_ffn,
           w_router, w_gate_e, w_up_e, w_down_e):
    B, S, D = x.shape
    depth = w_in.shape[0]
    H = rel_bias.shape[1]
    hd = g_q.shape[-1]
    vd = g_subln.shape[-1]
    qk_w = H * 2 * hd
    attn_w = H * vd
    lru_w = conv_w.shape[-1]
    NB, LB = gate_r_w.shape[2], gate_r_w.shape[3]
    E = w_router.shape[-1]
    C = EC_CAPACITY_FACTOR * S // E
    off_k = qk_w
    off_v = off_k + qk_w
    off_x = off_v + attn_w
    off_y = off_x + lru_w
    off_ga = off_y + lru_w
    off_gr = off_ga + D

    h2 = x.reshape(B * S, D)
    for layer in range(depth):
        lam_init = 0.8 - 0.6 * math.exp(-0.3 * layer)
        proj = _inproj(h2, g_mix[layer][None], w_in[layer].astype(BF16))

        lam_params = jnp.stack([lam_q1[layer], lam_k1[layer], lam_q2[layer], lam_k2[layer]])
        o_attn = _attention(proj, rel_bias, g_q[layer], g_k[layer], lam_params, g_subln[layer],
                            B=B, S=S, H=H, hd=hd, vd=vd, off_k=off_k, off_v=off_v, lam_init=lam_init)

        w_cat = jnp.concatenate([gate_r_w[layer, 0], gate_i_w[layer, 0],
                                 gate_r_w[layer, 1], gate_i_w[layer, 1]], axis=-1).astype(BF16)
        b_cat = jnp.stack([gate_r_b[layer, 0].reshape(NB, LB), gate_i_b[layer, 0].reshape(NB, LB),
                           gate_r_b[layer, 1].reshape(NB, LB), gate_i_b[layer, 1].reshape(NB, LB)],
                          axis=1).reshape(NB, 1, 4 * LB)
        lru_out = _rg_lru(proj, conv_w[layer], conv_b[layer], w_cat, b_cat, lru_lambda[layer],
                          B=B, S=S, off_x=off_x, off_y=off_y)

        wr = w_router[layer]
        wr_hi = wr.astype(BF16)
        wr2 = jnp.concatenate([wr_hi, (wr - wr_hi.astype(F32)).astype(BF16)], axis=1)
        h2, hn, aff = _merge(o_attn, lru_out, proj, h2, w_proj_attn[layer].astype(BF16),
                             w_proj_lru[layer].astype(BF16), w_out[layer].astype(BF16),
                             g_ffn[layer][None], wr2, off_ga=off_ga, off_gr=off_gr, E=E)

        aff_cols = aff.reshape(B, S, E)
        pos_rows = _select(aff_cols.transpose(0, 2, 1), C)
        y = _expert_ffn(pos_rows.reshape(B, E, 1, S), hn.reshape(B, S, D), w_gate_e[layer].astype(BF16),
                        w_up_e[layer].astype(BF16), w_down_e[layer].astype(BF16), C)
        h3 = _combine(pos_rows.transpose(0, 2, 1), aff_cols, y, h2.reshape(B, S, D))
        h2 = h3.reshape(B * S, D)
    return h2.reshape(B, S, D)
```

```python
import functools
import math

import jax
import jax.numpy as jnp
import numpy as np
from jax import lax
from jax.experimental import pallas as pl
from jax.experimental.pallas import tpu as pltpu

F32 = jnp.float32
BF16 = jnp.bfloat16
EPS = 1e-6
LANES = 128
SUBLANES = 8
VMEM_LIMIT = 56 * 1024 * 1024
LOG2E = 1.4426950408889634
NEG_BIG = -1e30
REL_MAX_DIST = 128
LRU_C = 8.0
EC_CAPACITY_FACTOR = 2


def _params(*sem):
    return pltpu.CompilerParams(dimension_semantics=sem, vmem_limit_bytes=VMEM_LIMIT)


def _inproj_body(x_ref, g_ref, w_ref, o_ref, xn_ref):
    @pl.when(pl.program_id(1) == 0)
    def _():
        x = x_ref[...]
        ms = jnp.mean(x * x, axis=-1, keepdims=True)
        xn_ref[...] = (x * lax.rsqrt(ms + EPS) * g_ref[...]).astype(BF16)

    o_ref[...] = jnp.dot(xn_ref[...], w_ref[...], preferred_element_type=F32).astype(o_ref.dtype)


def _inproj(x2, g, w_bf):
    T, D = x2.shape
    N = w_bf.shape[1]
    tm = min(1024, T)
    tn = min(1024, N)
    return pl.pallas_call(
        _inproj_body,
        grid=(T // tm, N // tn),
        in_specs=[pl.BlockSpec((tm, D), lambda i, j: (i, 0)),
                  pl.BlockSpec((1, D), lambda i, j: (0, 0)),
                  pl.BlockSpec((D, tn), lambda i, j: (0, j))],
        out_specs=pl.BlockSpec((tm, tn), lambda i, j: (i, j)),
        out_shape=jax.ShapeDtypeStruct((T, N), BF16),
        scratch_shapes=[pltpu.VMEM((tm, D), BF16)],
        compiler_params=_params("parallel", "arbitrary"),
        name="in_proj",
    )(x2, g, w_bf)


def _bucket_table(S, n_buckets):
    rel = np.arange(2 * S) - S
    half = n_buckets // 2
    max_exact = half // 2
    ret = np.where(rel > 0, half, 0)
    n = np.abs(rel)
    nf = np.maximum(n, max_exact).astype(np.float64)
    large = max_exact + (np.log(nf / max_exact) / math.log(REL_MAX_DIST / max_exact)
                         * (half - max_exact)).astype(np.int32)
    large = np.minimum(large, half - 1)
    return (ret + np.where(n < max_exact, n, large)).astype(np.int32).reshape(1, 2 * S)


def _seg_sumsq(x, seg_ones):
    x2 = x * x
    hi = x2.astype(BF16)
    lo = (x2 - hi.astype(F32)).astype(BF16)
    return (jnp.dot(hi, seg_ones, preferred_element_type=F32)
            + jnp.dot(lo, seg_ones, preferred_element_type=F32))


NORM_SLACK = 1.01
SAFE_LOG2_SPAN = 100.0


def _attn_body(bucket_ref, relb_ref, q_ref, k_ref, v_ref, gq_ref, gk_ref, lamp_ref, gs_ref, segq_ref,
               segv_ref, o_ref, e_ref, fast_ref, kn_ref, va_ref, m_ref, acc_ref,
               *, S, tq, kc, hd, vd, lam_init, n_buckets):
    h = pl.program_id(0)
    b = pl.program_id(1)
    i = pl.program_id(2)
    q_scale = hd ** -0.5 * LOG2E

    @pl.when((b == 0) & (i == 0))
    def _():
        bk = bucket_ref[...]
        tab = jnp.zeros((1, 2 * S), F32)
        for n in range(n_buckets):
            tab = jnp.where(bk == n, relb_ref[h, n], tab)
        tab = tab * LOG2E
        bmax = jnp.max(tab, axis=-1, keepdims=True)
        bmin = jnp.min(tab, axis=-1, keepdims=True)
        bound = (NORM_SLACK * hd * q_scale) * (jnp.max(jnp.abs(gq_ref[...]), axis=-1, keepdims=True)
                                               * jnp.max(jnp.abs(gk_ref[...]), axis=-1, keepdims=True))
        span = 2.0 * bound + (bmax - bmin)
        fast_ref[0] = (span[0, 0] <= SAFE_LOG2_SPAN).astype(jnp.int32)
        tab = jnp.broadcast_to(tab - (bound + bmax), (tq, 2 * S))
        e_ref[...] = pltpu.roll(tab, 0, 1, stride=1, stride_axis=0)

    @pl.when(i == 0)
    def _():
        k = k_ref[...].astype(F32)
        ms = _seg_sumsq(k, segq_ref[...]) * (1.0 / hd)
        kn_ref[...] = (k * lax.rsqrt(ms + EPS) * gk_ref[...]).astype(BF16)
        va_ref[:, :vd] = v_ref[...]
        va_ref[:, vd:] = jnp.ones((S, vd), BF16)

    q = q_ref[...].astype(F32)
    ms = _seg_sumsq(q, segq_ref[...]) * (1.0 / hd)
    qn = q * lax.rsqrt(ms + EPS) * (gq_ref[...] * q_scale)
    lo = lax.broadcasted_iota(jnp.int32, qn.shape, 1) < hd
    q_maps = (jnp.where(lo, qn, 0.0).astype(BF16), jnp.where(lo, 0.0, qn).astype(BF16))
    nt_dims = (((1,), (1,)), ((), ()))

    @pl.when(fast_ref[0] == 1)
    def _():
        bias = e_ref[:, pl.ds(pl.multiple_of(S - i * tq, LANES), S)]
        for mi in range(2):
            s = lax.dot_general(q_maps[mi], kn_ref[...], nt_dims, preferred_element_type=F32)
            acc_ref[mi] = jnp.dot(jnp.exp2(s + bias).astype(BF16), va_ref[...],
                                  preferred_element_type=F32)

    @pl.when(fast_ref[0] != 1)
    def _():
        m_ref[...] = jnp.full(m_ref.shape, NEG_BIG, F32)
        acc_ref[...] = jnp.zeros(acc_ref.shape, F32)

        def chunk(c, carry):
            k0 = pl.multiple_of(c * kc, kc)
            bias = e_ref[:, pl.ds(pl.multiple_of(S + c * kc - i * tq, LANES), kc)]
            for mi in range(2):
                s = lax.dot_general(q_maps[mi], kn_ref[pl.ds(k0, kc), :], nt_dims,
                                    preferred_element_type=F32) + bias
                m_old = m_ref[mi]
                m_new = jnp.maximum(m_old, jnp.max(s, axis=-1, keepdims=True))
                p = jnp.exp2(s - m_new).astype(BF16)
                acc_ref[mi] = (jnp.exp2(m_old - m_new) * acc_ref[mi]
                               + jnp.dot(p, va_ref[pl.ds(k0, kc), :], preferred_element_type=F32))
                m_ref[mi] = m_new
            return carry

        lax.fori_loop(0, S // kc, chunk, 0)

    lp = lamp_ref[...]
    lam = (jnp.exp(jnp.sum(lp[0:1] * lp[1:2], axis=-1, keepdims=True))
           - jnp.exp(jnp.sum(lp[2:3] * lp[3:4], axis=-1, keepdims=True)) + lam_init)
    a1 = acc_ref[0]
    a2 = acc_ref[1]
    o = a1[:, :vd] / a1[:, vd:] - lam * (a2[:, :vd] / a2[:, vd:])
    o = o * lax.rsqrt(_seg_sumsq(o, segv_ref[...]) * (1.0 / vd) + EPS)
    o_ref[...] = (o * (gs_ref[...] * (1.0 - lam_init))).astype(o_ref.dtype)


def _attention(proj, rel_bias, g_q, g_k, lam_params, g_subln, *, B, S, H, hd, vd, off_k, off_v, lam_init):
    T = B * S
    tq = min(512, S)
    kc = min(512, S)
    nq = S // tq
    n_buckets = rel_bias.shape[0]
    hw = 2 * hd
    bucket = jnp.asarray(_bucket_table(S, n_buckets))
    body = functools.partial(_attn_body, S=S, tq=tq, kc=kc, hd=hd, vd=vd, lam_init=lam_init,
                             n_buckets=n_buckets)
    kblk = off_k // hw
    vblk = off_v // vd
    seg_q = np.kron(np.eye(2, dtype=np.float32), np.ones((hd, hd), np.float32))
    return pl.pallas_call(
        body,
        grid=(H, B, nq),
        in_specs=[pl.BlockSpec((1, 2 * S), lambda h, b, i: (0, 0)),
                  pl.BlockSpec(memory_space=pltpu.SMEM),
                  pl.BlockSpec((tq, hw), lambda h, b, i: (b * nq + i, h)),
                  pl.BlockSpec((S, hw), lambda h, b, i: (b, kblk + h)),
                  pl.BlockSpec((S, vd), lambda h, b, i: (b, vblk + h)),
                  pl.BlockSpec((1, hw), lambda h, b, i: (0, 0)),
                  pl.BlockSpec((1, hw), lambda h, b, i: (0, 0)),
                  pl.BlockSpec((4, hd), lambda h, b, i: (0, 0)),
                  pl.BlockSpec((1, vd), lambda h, b, i: (0, 0)),
                  pl.BlockSpec((hw, hw), lambda h, b, i: (0, 0)),
                  pl.BlockSpec((vd, vd), lambda h, b, i: (0, 0))],
        out_specs=pl.BlockSpec((tq, vd), lambda h, b, i: (b * nq + i, h)),
        out_shape=jax.ShapeDtypeStruct((T, H * vd), BF16),
        scratch_shapes=[pltpu.VMEM((tq, 2 * S), F32),
                        pltpu.SMEM((1,), jnp.int32),
                        pltpu.VMEM((S, hw), BF16),
                        pltpu.VMEM((S, 2 * vd), BF16),
                        pltpu.VMEM((2, tq, 1), F32),
                        pltpu.VMEM((2, tq, 2 * vd), F32)],
        compiler_params=_params("arbitrary", "arbitrary", "arbitrary"),
        name="diff_attention",
    )(bucket, rel_bias.T, proj, proj, proj, jnp.tile(g_q, 2)[None], jnp.tile(g_k, 2)[None],
      lam_params, g_subln[None], jnp.asarray(seg_q, BF16), jnp.ones((vd, vd), BF16))


LRU_SEGMENTS = SUBLANES
LRU_PITCH_PAD = 8


def _sublane_scan(a, bb, reverse):
    ri = lax.broadcasted_iota(jnp.int32, a.shape, 0)
    for d in (1, 2, 4):
        if reverse:
            keep, sh = ri < SUBLANES - d, SUBLANES - d
        else:
            keep, sh = ri >= d, d
        bb = bb + a * jnp.where(keep, pltpu.roll(bb, sh, 0), 0.0)
        a = a * jnp.where(keep, pltpu.roll(a, sh, 0), 1.0)
    return bb


def _lru_body(x_ref, y_ref, cw_ref, cb_ref, w_ref, bias_ref, lam_ref, o_ref,
              af_ref, bf_ref, ab_ref, bb_ref, hf_ref, pf_ref, hb_ref, pb_ref, hs_ref, *, S, W):
    L = S // LRU_SEGMENTS
    pitch = L + LRU_PITCH_PAD
    x = x_ref[...].astype(F32)
    row = lax.broadcasted_iota(jnp.int32, (S, W), 0)
    cw = cw_ref[...]
    xc = (cw[0:1] * jnp.where(row >= 2, pltpu.roll(x, 2, 0), 0.0)
          + cw[1:2] * jnp.where(row >= 1, pltpu.roll(x, 1, 0), 0.0)
          + cw[2:3] * x
          + cw[3:4] * jnp.where(row < S - 1, pltpu.roll(x, S - 1, 0), 0.0)
          + cb_ref[...])
    gates = jnp.dot(xc.astype(BF16), w_ref[0], preferred_element_type=F32) + bias_ref[0]
    lam = lam_ref[...]
    for d, (a_ref, b_ref) in enumerate(((af_ref, bf_ref), (ab_ref, bb_ref))):
        r = jax.nn.sigmoid(gates[:, (2 * d) * W:(2 * d + 1) * W])
        ig = jax.nn.sigmoid(gates[:, (2 * d + 1) * W:(2 * d + 2) * W])
        nl = -lam[d:d + 1]
        softplus = jnp.maximum(nl, 0.0) + jnp.log1p(jnp.exp(-jnp.abs(nl)))
        a = jnp.exp((-LRU_C * softplus) * r)
        y2 = jnp.maximum(1.0 - a * a, 0.0)
        bvals = jnp.where(y2 > 0.0, y2 * lax.rsqrt(y2), 0.0) * ig * xc
        for sg in range(LRU_SEGMENTS):
            a_ref[sg * pitch:sg * pitch + L, :] = a[sg * L:(sg + 1) * L]
            b_ref[sg * pitch:sg * pitch + L, :] = bvals[sg * L:(sg + 1) * L]

    def seg_rows(ref, j):
        return ref[pl.ds(j, LRU_SEGMENTS, stride=pitch), :]

    def step(j, carry):
        hf, pf, hb, pb = carry
        a = seg_rows(af_ref, j)
        hf = a * hf + seg_rows(bf_ref, j)
        pf = a * pf
        hf_ref[j] = hf
        pf_ref[j] = pf
        jb = L - 1 - j
        a = seg_rows(ab_ref, jb)
        hb = a * hb + seg_rows(bb_ref, jb)
        pb = a * pb
        hb_ref[jb] = hb
        pb_ref[jb] = pb
        return hf, pf, hb, pb

    zero = jnp.zeros((LRU_SEGMENTS, W), F32)
    one = jnp.ones((LRU_SEGMENTS, W), F32)
    hf, pf, hb, pb = lax.fori_loop(0, L, step, (zero, one, zero, one), unroll=8)

    si = lax.broadcasted_iota(jnp.int32, (LRU_SEGMENTS, W), 0)
    cf = jnp.where(si >= 1, pltpu.roll(_sublane_scan(pf, hf, False), 1, 0), 0.0)
    cb = jnp.where(si < LRU_SEGMENTS - 1,
                   pltpu.roll(_sublane_scan(pb, hb, True), LRU_SEGMENTS - 1, 0), 0.0)

    def fix(j, carry):
        hs_ref[pl.ds(j, LRU_SEGMENTS, stride=pitch), :] = (
            hf_ref[j] + pf_ref[j] * cf + hb_ref[j] + pb_ref[j] * cb)
        return carry

    lax.fori_loop(0, L, fix, 0, unroll=8)

    for sg in range(LRU_SEGMENTS):
        y = y_ref[sg * L:(sg + 1) * L, :].astype(F32)
        o_ref[sg * L:(sg + 1) * L, :] = (hs_ref[sg * pitch:sg * pitch + L, :]
                                         * jax.nn.gelu(y)).astype(o_ref.dtype)


def _rg_lru(proj, conv_w, conv_b, w_cat, b_cat, lru_lambda, *, B, S, off_x, off_y):
    T = B * S
    NB, W, _ = w_cat.shape
    xblk = off_x // W
    yblk = off_y // W
    body = functools.partial(_lru_body, S=S, W=W)
    padded = LRU_SEGMENTS * (S // LRU_SEGMENTS + LRU_PITCH_PAD)
    return pl.pallas_call(
        body,
        grid=(B, NB),
        in_specs=[pl.BlockSpec((S, W), lambda b, n: (b, xblk + n)),
                  pl.BlockSpec((S, W), lambda b, n: (b, yblk + n)),
                  pl.BlockSpec((conv_w.shape[0], W), lambda b, n: (0, n)),
                  pl.BlockSpec((1, W), lambda b, n: (0, n)),
                  pl.BlockSpec((1, W, 4 * W), lambda b, n: (n, 0, 0)),
                  pl.BlockSpec((1, 1, 4 * W), lambda b, n: (n, 0, 0)),
                  pl.BlockSpec((2, W), lambda b, n: (0, n))],
        out_specs=pl.BlockSpec((S, W), lambda b, n: (b, n)),
        out_shape=jax.ShapeDtypeStruct((T, NB * W), BF16),
        scratch_shapes=([pltpu.VMEM((padded, W), F32)] * 4
                        + [pltpu.VMEM((S // LRU_SEGMENTS, LRU_SEGMENTS, W), F32)] * 4
                        + [pltpu.VMEM((padded, W), F32)]),
        compiler_params=_params("parallel", "parallel"),
        name="rg_lru",
    )(proj, proj, conv_w, conv_b[None], w_cat, b_cat, lru_lambda)


def _merge_body(o_ref, r_ref, ga_ref, gr_ref, x_ref, wpa_ref, wpl_ref, wo_ref, g_ref, wr_ref,
                h_ref, hn_ref, aff_ref, *, E):
    ba = jnp.dot(o_ref[...], wpa_ref[...], preferred_element_type=F32)
    br = jnp.dot(r_ref[...], wpl_ref[...], preferred_element_type=F32)
    mixed = (jax.nn.sigmoid(ga_ref[...].astype(F32)) * ba
             + jax.nn.sigmoid(gr_ref[...].astype(F32)) * br)
    h = x_ref[...] + jnp.dot(mixed.astype(BF16), wo_ref[...], preferred_element_type=F32)
    h_ref[...] = h
    hn = h * lax.rsqrt(jnp.mean(h * h, axis=-1, keepdims=True) + EPS) * g_ref[...]
    hn_hi = hn.astype(BF16)
    hn_ref[...] = hn_hi
    hn_lo = (hn - hn_hi.astype(F32)).astype(BF16)
    lg = (jnp.dot(hn_hi, wr_ref[...], preferred_element_type=F32)
          + jnp.dot(hn_lo, wr_ref[...], preferred_element_type=F32))
    logits = lg[:, :E] + lg[:, E:2 * E]
    logits = logits - jnp.max(logits, axis=-1, keepdims=True)
    ex = jnp.exp(logits)
    aff_ref[...] = ex / jnp.sum(ex, axis=-1, keepdims=True)


def _merge(o_attn, lru_out, proj, x2, wpa, wpl, wo, g_ffn, wr2, *, off_ga, off_gr, E):
    T, D = x2.shape
    tm = min(512, T)
    gab = off_ga // D
    grb = off_gr // D
    row = lambda i: (i, 0)
    const = lambda i: (0, 0)
    return pl.pallas_call(
        functools.partial(_merge_body, E=E),
        grid=(T // tm,),
        in_specs=[pl.BlockSpec((tm, D), row), pl.BlockSpec((tm, D), row),
                  pl.BlockSpec((tm, D), lambda i: (i, gab)), pl.BlockSpec((tm, D), lambda i: (i, grb)),
                  pl.BlockSpec((tm, D), row),
                  pl.BlockSpec((D, D), const), pl.BlockSpec((D, D), const), pl.BlockSpec((D, D), const),
                  pl.BlockSpec((1, D), const), pl.BlockSpec((D, 2 * E), const)],
        out_specs=[pl.BlockSpec((tm, D), row), pl.BlockSpec((tm, D), row), pl.BlockSpec((tm, E), row)],
        out_shape=[jax.ShapeDtypeStruct((T, D), F32), jax.ShapeDtypeStruct((T, D), BF16),
                   jax.ShapeDtypeStruct((T, E), F32)],
        compiler_params=_params("parallel"),
        name="merge_router",
    )(o_attn, lru_out, proj, proj, x2, wpa, wpl, wo, g_ffn, wr2)


def _select_body(aff_ref, pos_ref, *, E, S, C):
    bits = pltpu.bitcast(aff_ref[0], jnp.int32)
    t = jnp.zeros((E, 1), jnp.int32)
    for bit in range(30, -1, -1):
        cand = t | (1 << bit)
        cnt = jnp.sum(jnp.where(bits >= cand, 1.0, 0.0), axis=-1, keepdims=True)
        t = jnp.where(cnt >= C, cand, t)
    gt = bits > t
    eq = bits == t
    need = C - jnp.sum(jnp.where(gt, 1.0, 0.0), axis=-1, keepdims=True).astype(jnp.int32)
    packed = jnp.where(gt, 1 << 16, 0) + jnp.where(eq, 1, 0)
    lane = lax.broadcasted_iota(jnp.int32, (E, S), 1)
    incl = packed
    d = 1
    while d < S:
        incl = incl + jnp.where(lane >= d, pltpu.roll(incl, d, 1), 0)
        d *= 2
    excl = incl - packed
    n_gt = excl >> 16
    n_eq = excl & 0xFFFF
    sel = gt | (eq & (n_eq < need))
    pos_ref[0] = jnp.where(sel, n_gt + jnp.minimum(n_eq, need), -1)


def _select(aff_t, C):
    B, E, S = aff_t.shape
    return pl.pallas_call(
        functools.partial(_select_body, E=E, S=S, C=C),
        grid=(B,),
        in_specs=[pl.BlockSpec((1, E, S), lambda b: (b, 0, 0))],
        out_specs=pl.BlockSpec((1, E, S), lambda b: (b, 0, 0)),
        out_shape=jax.ShapeDtypeStruct((B, E, S), jnp.int32),
        compiler_params=_params("parallel"),
        name="topc_select",
    )(aff_t)


def _ffn_body(pos_ref, hn_ref, wg_ref, wu_ref, wd_ref, y_ref, *, C, S, fc):
    slot = lax.broadcasted_iota(jnp.int32, (C, S), 0)
    onehot = jnp.where(pos_ref[0, 0] == slot, 1.0, 0.0).astype(BF16)
    xg = jnp.dot(onehot, hn_ref[0], preferred_element_type=F32).astype(BF16)
    F = wg_ref.shape[2]
    y = jnp.zeros(y_ref.shape[2:], F32)
    for f0 in range(0, F, fc):
        g = jnp.dot(xg, wg_ref[0, :, f0:f0 + fc], preferred_element_type=F32)
        u = jnp.dot(xg, wu_ref[0, :, f0:f0 + fc], preferred_element_type=F32)
        hid = (g * jax.nn.sigmoid(g) * u).astype(BF16)
        y = y + jnp.dot(hid, wd_ref[0, f0:f0 + fc, :], preferred_element_type=F32)
    y_ref[0, 0] = y.astype(y_ref.dtype)


def _expert_ffn(pos_rows, hn3, wg, wu, wd, C):
    B, S, D = hn3.shape
    E, _, F = wg.shape
    fc = min(512, F)
    return pl.pallas_call(
        functools.partial(_ffn_body, C=C, S=S, fc=fc),
        grid=(E, B),
        in_specs=[pl.BlockSpec((1, 1, 1, S), lambda e, b: (b, e, 0, 0)),
                  pl.BlockSpec((1, S, D), lambda e, b: (b, 0, 0)),
                  pl.BlockSpec((1, D, F), lambda e, b: (e, 0, 0)),
                  pl.BlockSpec((1, D, F), lambda e, b: (e, 0, 0)),
                  pl.BlockSpec((1, F, D), lambda e, b: (e, 0, 0))],
        out_specs=pl.BlockSpec((1, 1, C, D), lambda e, b: (b, e, 0, 0)),
        out_shape=jax.ShapeDtypeStruct((B, E, C, D), BF16),
        compiler_params=_params("arbitrary", "arbitrary"),
        name="expert_ffn",
    )(pos_rows, hn3, wg, wu, wd)


def _combine_body(pos_ref, aff_ref, y_ref, h_ref, o_ref, *, E, C):
    pc = pos_ref[0]
    ac = aff_ref[0]
    ts = pc.shape[0]
    slot = lax.broadcasted_iota(jnp.int32, (ts, C), 1)
    pieces = [jnp.where(pc[:, e:e + 1] == slot, ac[:, e:e + 1], 0.0).astype(BF16) for e in range(E)]
    scatter = jnp.concatenate(pieces, axis=1)
    y = y_ref[0].reshape(E * C, y_ref.shape[3])
    o_ref[0] = h_ref[0] + jnp.dot(scatter, y, preferred_element_type=F32)


def _combine(pos_cols, aff_cols, y, h3):
    B, S, D = h3.shape
    _, E, C, _ = y.shape
    ts = min(512, S)
    return pl.pallas_call(
        functools.partial(_combine_body, E=E, C=C),
        grid=(B, S // ts),
        in_specs=[pl.BlockSpec((1, ts, E), lambda b, i: (b, i, 0)),
                  pl.BlockSpec((1, ts, E), lambda b, i: (b, i, 0)),
                  pl.BlockSpec((1, E, C, D), lambda b, i: (b, 0, 0, 0)),
                  pl.BlockSpec((1, ts, D), lambda b, i: (b, i, 0))],
        out_specs=pl.BlockSpec((1, ts, D), lambda b, i: (b, i, 0)),
        out_shape=jax.ShapeDtypeStruct((B, S, D), F32),
        compiler_params=_params("parallel", "arbitrary"),
        name="moe_combine",
    )(pos_cols, aff_cols, y, h3)


def kernel(x, g_mix, w_in, g_q, g_k, lam_q1, lam_k1, lam_q2, lam_k2, g_subln, rel_bias, conv_w, conv_b,
           gate_r_w, gate_r_b, gate_i_w, gate_i_b, lru_lambda, w_proj_attn, w_proj_lru, w_out, g_ffn,
           w_router, w_gate_e, w_up_e, w_down_e):
    B, S, D = x.shape
    depth = w_in.shape[0]
    H = rel_bias.shape[1]
    hd = g_q.shape[-1]
    vd = g_subln.shape[-1]
    qk_w = H * 2 * hd
    attn_w = H * vd
    lru_w = conv_w.shape[-1]
    NB, LB = gate_r_w.shape[2], gate_r_w.shape[3]
    E = w_router.shape[-1]
    C = EC_CAPACITY_FACTOR * S // E
    off_k = qk_w
    off_v = off_k + qk_w
    off_x = off_v + attn_w
    off_y = off_x + lru_w
    off_ga = off_y + lru_w
    off_gr = off_ga + D

    h2 = x.reshape(B * S, D)
    for layer in range(depth):
        lam_init = 0.8 - 0.6 * math.exp(-0.3 * layer)
        proj = _inproj(h2, g_mix[layer][None], w_in[layer].astype(BF16))

        lam_params = jnp.stack([lam_q1[layer], lam_k1[layer], lam_q2[layer], lam_k2[layer]])
        o_attn = _attention(proj, rel_bias, g_q[layer], g_k[layer], lam_params, g_subln[layer],
                            B=B, S=S, H=H, hd=hd, vd=vd, off_k=off_k, off_v=off_v, lam_init=lam_init)

        w_cat = jnp.concatenate([gate_r_w[layer, 0], gate_i_w[layer, 0],
                                 gate_r_w[layer, 1], gate_i_w[layer, 1]], axis=-1).astype(BF16)
        b_cat = jnp.stack([gate_r_b[layer, 0].reshape(NB, LB), gate_i_b[layer, 0].reshape(NB, LB),
                           gate_r_b[layer, 1].reshape(NB, LB), gate_i_b[layer, 1].reshape(NB, LB)],
                          axis=1).reshape(NB, 1, 4 * LB)
        lru_out = _rg_lru(proj, conv_w[layer], conv_b[layer], w_cat, b_cat, lru_lambda[layer],
                          B=B, S=S, off_x=off_x, off_y=off_y)

        wr = w_router[layer]
        wr_hi = wr.astype(BF16)
        wr2 = jnp.concatenate([wr_hi, (wr - wr_hi.astype(F32)).astype(BF16)], axis=1)
        h2, hn, aff = _merge(o_attn, lru_out, proj, h2, w_proj_attn[layer].astype(BF16),
                             w_proj_lru[layer].astype(BF16), w_out[layer].astype(BF16),
                             g_ffn[layer][None], wr2, off_ga=off_ga, off_gr=off_gr, E=E)

        aff_cols = aff.reshape(B, S, E)
        pos_rows = _select(aff_cols.transpose(0, 2, 1), C)
        y = _expert_ffn(pos_rows.reshape(B, E, 1, S), hn.reshape(B, S, D), w_gate_e[layer].astype(BF16),
                        w_up_e[layer].astype(BF16), w_down_e[layer].astype(BF16), C)
        h3 = _combine(pos_rows.transpose(0, 2, 1), aff_cols, y, h2.reshape(B, S, D))
        h2 = h3.reshape(B * S, D)
    return h2.reshape(B, S, D)
```

```python
import functools
import math

import jax
import jax.numpy as jnp
import numpy as np
from jax import lax
from jax.experimental import pallas as pl
from jax.experimental.pallas import tpu as pltpu

F32 = jnp.float32
BF16 = jnp.bfloat16
EPS = 1e-6
LANES = 128
SUBLANES = 8
VMEM_LIMIT = 56 * 1024 * 1024
LOG2E = 1.4426950408889634
NEG_BIG = -1e30
REL_MAX_DIST = 128
LRU_C = 8.0
EC_CAPACITY_FACTOR = 2


def _params(*sem):
    return pltpu.CompilerParams(dimension_semantics=sem, vmem_limit_bytes=VMEM_LIMIT)


def _inproj_body(x_ref, g_ref, w_ref, o_ref, xn_ref):
    @pl.when(pl.program_id(1) == 0)
    def _():
        x = x_ref[...]
        ms = jnp.mean(x * x, axis=-1, keepdims=True)
        xn_ref[...] = (x * lax.rsqrt(ms + EPS) * g_ref[...]).astype(BF16)

    o_ref[...] = jnp.dot(xn_ref[...], w_ref[...], preferred_element_type=F32).astype(o_ref.dtype)


def _inproj(x2, g, w_bf):
    T, D = x2.shape
    N = w_bf.shape[1]
    tm = min(1024, T)
    tn = min(1024, N)
    return pl.pallas_call(
        _inproj_body,
        grid=(T // tm, N // tn),
        in_specs=[pl.BlockSpec((tm, D), lambda i, j: (i, 0)),
                  pl.BlockSpec((1, D), lambda i, j: (0, 0)),
                  pl.BlockSpec((D, tn), lambda i, j: (0, j))],
        out_specs=pl.BlockSpec((tm, tn), lambda i, j: (i, j)),
        out_shape=jax.ShapeDtypeStruct((T, N), BF16),
        scratch_shapes=[pltpu.VMEM((tm, D), BF16)],
        compiler_params=_params("parallel", "arbitrary"),
        name="in_proj",
    )(x2, g, w_bf)


def _bucket_table(S, n_buckets):
    rel = np.arange(2 * S) - S
    half = n_buckets // 2
    max_exact = half // 2
    ret = np.where(rel > 0, half, 0)
    n = np.abs(rel)
    nf = np.maximum(n, max_exact).astype(np.float64)
    large = max_exact + (np.log(nf / max_exact) / math.log(REL_MAX_DIST / max_exact)
                         * (half - max_exact)).astype(np.int32)
    large = np.minimum(large, half - 1)
    return (ret + np.where(n < max_exact, n, large)).astype(np.int32).reshape(1, 2 * S)


def _seg_sumsq(x, seg_ones):
    x2 = x * x
    hi = x2.astype(BF16)
    lo = (x2 - hi.astype(F32)).astype(BF16)
    return (jnp.dot(hi, seg_ones, preferred_element_type=F32)
            + jnp.dot(lo, seg_ones, preferred_element_type=F32))


NORM_SLACK = 1.01
SAFE_LOG2_SPAN = 100.0


def _attn_body(bucket_ref, relb_ref, q_ref, k_ref, v_ref, gq_ref, gk_ref, lamp_ref, gs_ref, segq_ref,
               segv_ref, o_ref, e_ref, fast_ref, kn_ref, va_ref, m_ref, acc_ref,
               *, S, tq, n_sub, kc, hd, vd, lam_init, n_buckets):
    h = pl.program_id(0)
    b = pl.program_id(1)
    i = pl.program_id(2)
    q_scale = hd ** -0.5 * LOG2E

    @pl.when((b == 0) & (i == 0))
    def _():
        bk = bucket_ref[...]
        tab = jnp.zeros((1, 2 * S), F32)
        for n in range(n_buckets):
            tab = jnp.where(bk == n, relb_ref[h, n], tab)
        tab = tab * LOG2E
        bmax = jnp.max(tab, axis=-1, keepdims=True)
        bmin = jnp.min(tab, axis=-1, keepdims=True)
        bound = (NORM_SLACK * hd * q_scale) * (jnp.max(jnp.abs(gq_ref[...]), axis=-1, keepdims=True)
                                               * jnp.max(jnp.abs(gk_ref[...]), axis=-1, keepdims=True))
        span = 2.0 * bound + (bmax - bmin)
        fast_ref[0] = (span[0, 0] <= SAFE_LOG2_SPAN).astype(jnp.int32)
        tab = jnp.broadcast_to(tab - (bound + bmax), (tq, 2 * S))
        e_ref[...] = pltpu.roll(tab, 0, 1, stride=1, stride_axis=0)

    @pl.when(i == 0)
    def _():
        k = k_ref[...].astype(F32)
        ms = _seg_sumsq(k, segq_ref[...]) * (1.0 / hd)
        kn_ref[...] = (k * lax.rsqrt(ms + EPS) * gk_ref[...]).astype(BF16)
        va_ref[:, :vd] = v_ref[...]
        va_ref[:, vd:] = jnp.ones((S, vd), BF16)

    nt_dims = (((1,), (1,)), ((), ()))
    lp = lamp_ref[...]
    lam = (jnp.exp(jnp.sum(lp[0:1] * lp[1:2], axis=-1, keepdims=True))
           - jnp.exp(jnp.sum(lp[2:3] * lp[3:4], axis=-1, keepdims=True)) + lam_init)

    def q_maps_of(t):
        q = q_ref[t * tq:(t + 1) * tq, :].astype(F32)
        ms = _seg_sumsq(q, segq_ref[...]) * (1.0 / hd)
        qn = q * lax.rsqrt(ms + EPS) * (gq_ref[...] * q_scale)
        lo = lax.broadcasted_iota(jnp.int32, qn.shape, 1) < hd
        return jnp.where(lo, qn, 0.0).astype(BF16), jnp.where(lo, 0.0, qn).astype(BF16)

    def e_start(t, k0):
        return pl.multiple_of(S + k0 - (i * n_sub + t) * tq, LANES)

    def finish(t, a1, a2):
        o = a1[:, :vd] / a1[:, vd:] - lam * (a2[:, :vd] / a2[:, vd:])
        o = o * lax.rsqrt(_seg_sumsq(o, segv_ref[...]) * (1.0 / vd) + EPS)
        o_ref[t * tq:(t + 1) * tq, :] = (o * (gs_ref[...] * (1.0 - lam_init))).astype(o_ref.dtype)

    @pl.when(fast_ref[0] == 1)
    def _():
        for t in range(n_sub):
            q_maps = q_maps_of(t)
            bias = e_ref[:, pl.ds(e_start(t, 0), S)]
            accs = []
            for mi in range(2):
                s = lax.dot_general(q_maps[mi], kn_ref[...], nt_dims, preferred_element_type=F32)
                accs.append(jnp.dot(jnp.exp2(s + bias).astype(BF16), va_ref[...],
                                    preferred_element_type=F32))
            finish(t, *accs)

    @pl.when(fast_ref[0] != 1)
    def _():
        for t in range(n_sub):
            q_maps = q_maps_of(t)
            m_ref[...] = jnp.full(m_ref.shape, NEG_BIG, F32)
            acc_ref[...] = jnp.zeros(acc_ref.shape, F32)

            def chunk(c, carry):
                k0 = pl.multiple_of(c * kc, kc)
                bias = e_ref[:, pl.ds(e_start(t, k0), kc)]
                for mi in range(2):
                    s = lax.dot_general(q_maps[mi], kn_ref[pl.ds(k0, kc), :], nt_dims,
                                        preferred_element_type=F32) + bias
                    m_old = m_ref[mi]
                    m_new = jnp.maximum(m_old, jnp.max(s, axis=-1, keepdims=True))
                    p = jnp.exp2(s - m_new).astype(BF16)
                    acc_ref[mi] = (jnp.exp2(m_old - m_new) * acc_ref[mi]
                                   + jnp.dot(p, va_ref[pl.ds(k0, kc), :], preferred_element_type=F32))
                    m_ref[mi] = m_new
                return carry

            lax.fori_loop(0, S // kc, chunk, 0)
            finish(t, acc_ref[0], acc_ref[1])


def _attention(proj, rel_bias, g_q, g_k, lam_params, g_subln, *, B, S, H, hd, vd, off_k, off_v, lam_init):
    T = B * S
    tq = min(512, S)
    kc = min(512, S)
    n_sub = math.gcd(4, S // tq)
    tqb = n_sub * tq
    nq = S // tqb
    n_buckets = rel_bias.shape[0]
    hw = 2 * hd
    bucket = jnp.asarray(_bucket_table(S, n_buckets))
    body = functools.partial(_attn_body, S=S, tq=tq, n_sub=n_sub, kc=kc, hd=hd, vd=vd, lam_init=lam_init,
                             n_buckets=n_buckets)
    kblk = off_k // hw
    vblk = off_v // vd
    seg_q = np.kron(np.eye(2, dtype=np.float32), np.ones((hd, hd), np.float32))
    return pl.pallas_call(
        body,
        grid=(H, B, nq),
        in_specs=[pl.BlockSpec((1, 2 * S), lambda h, b, i: (0, 0)),
                  pl.BlockSpec(memory_space=pltpu.SMEM),
                  pl.BlockSpec((tqb, hw), lambda h, b, i: (b * nq + i, h)),
                  pl.BlockSpec((S, hw), lambda h, b, i: (b, kblk + h)),
                  pl.BlockSpec((S, vd), lambda h, b, i: (b, vblk + h)),
                  pl.BlockSpec((1, hw), lambda h, b, i: (0, 0)),
                  pl.BlockSpec((1, hw), lambda h, b, i: (0, 0)),
                  pl.BlockSpec((4, hd), lambda h, b, i: (0, 0)),
                  pl.BlockSpec((1, vd), lambda h, b, i: (0, 0)),
                  pl.BlockSpec((hw, hw), lambda h, b, i: (0, 0)),
                  pl.BlockSpec((vd, vd), lambda h, b, i: (0, 0))],
        out_specs=pl.BlockSpec((tqb, vd), lambda h, b, i: (b * nq + i, h)),
        out_shape=jax.ShapeDtypeStruct((T, H * vd), BF16),
        scratch_shapes=[pltpu.VMEM((tq, 2 * S), F32),
                        pltpu.SMEM((1,), jnp.int32),
                        pltpu.VMEM((S, hw), BF16),
                        pltpu.VMEM((S, 2 * vd), BF16),
                        pltpu.VMEM((2, tq, 1), F32),
                        pltpu.VMEM((2, tq, 2 * vd), F32)],
        compiler_params=_params("arbitrary", "arbitrary", "arbitrary"),
        name="diff_attention",
    )(bucket, rel_bias.T, proj, proj, proj, jnp.tile(g_q, 2)[None], jnp.tile(g_k, 2)[None],
      lam_params, g_subln[None], jnp.asarray(seg_q, BF16), jnp.ones((vd, vd), BF16))


LRU_SEGMENTS = SUBLANES
LRU_PITCH_PAD = 8


def _sublane_scan(a, bb, reverse):
    ri = lax.broadcasted_iota(jnp.int32, a.shape, 0)
    for d in (1, 2, 4):
        if reverse:
            keep, sh = ri < SUBLANES - d, SUBLANES - d
        else:
            keep, sh = ri >= d, d
        bb = bb + a * jnp.where(keep, pltpu.roll(bb, sh, 0), 0.0)
        a = a * jnp.where(keep, pltpu.roll(a, sh, 0), 1.0)
    return bb


def _lru_body(x_ref, y_ref, cw_ref, cb_ref, w_ref, bias_ref, lam_ref, o_ref,
              af_ref, bf_ref, ab_ref, bb_ref, hf_ref, pf_ref, hb_ref, pb_ref, hs_ref, *, S, W):
    L = S // LRU_SEGMENTS
    pitch = L + LRU_PITCH_PAD
    x = x_ref[...].astype(F32)
    row = lax.broadcasted_iota(jnp.int32, (S, W), 0)
    cw = cw_ref[...]
    xc = (cw[0:1] * jnp.where(row >= 2, pltpu.roll(x, 2, 0), 0.0)
          + cw[1:2] * jnp.where(row >= 1, pltpu.roll(x, 1, 0), 0.0)
          + cw[2:3] * x
          + cw[3:4] * jnp.where(row < S - 1, pltpu.roll(x, S - 1, 0), 0.0)
          + cb_ref[...])
    gates = jnp.dot(xc.astype(BF16), w_ref[0], preferred_element_type=F32) + bias_ref[0]
    lam = lam_ref[...]
    for d, (a_ref, b_ref) in enumerate(((af_ref, bf_ref), (ab_ref, bb_ref))):
        r = jax.nn.sigmoid(gates[:, (2 * d) * W:(2 * d + 1) * W])
        ig = jax.nn.sigmoid(gates[:, (2 * d + 1) * W:(2 * d + 2) * W])
        nl = -lam[d:d + 1]
        softplus = jnp.maximum(nl, 0.0) + jnp.log1p(jnp.exp(-jnp.abs(nl)))
        a = jnp.exp((-LRU_C * softplus) * r)
        y2 = jnp.maximum(1.0 - a * a, 0.0)
        bvals = jnp.where(y2 > 0.0, y2 * lax.rsqrt(y2), 0.0) * ig * xc
        for sg in range(LRU_SEGMENTS):
            a_ref[sg * pitch:sg * pitch + L, :] = a[sg * L:(sg + 1) * L]
            b_ref[sg * pitch:sg * pitch + L, :] = bvals[sg * L:(sg + 1) * L]

    def seg_rows(ref, j):
        return ref[pl.ds(j, LRU_SEGMENTS, stride=pitch), :]

    def step(j, carry):
        hf, pf, hb, pb = carry
        a = seg_rows(af_ref, j)
        hf = a * hf + seg_rows(bf_ref, j)
        pf = a * pf
        hf_ref[j] = hf
        pf_ref[j] = pf
        jb = L - 1 - j
        a = seg_rows(ab_ref, jb)
        hb = a * hb + seg_rows(bb_ref, jb)
        pb = a * pb
        hb_ref[jb] = hb
        pb_ref[jb] = pb
        return hf, pf, hb, pb

    zero = jnp.zeros((LRU_SEGMENTS, W), F32)
    one = jnp.ones((LRU_SEGMENTS, W), F32)
    hf, pf, hb, pb = lax.fori_loop(0, L, step, (zero, one, zero, one), unroll=8)

    si = lax.broadcasted_iota(jnp.int32, (LRU_SEGMENTS, W), 0)
    cf = jnp.where(si >= 1, pltpu.roll(_sublane_scan(pf, hf, False), 1, 0), 0.0)
    cb = jnp.where(si < LRU_SEGMENTS - 1,
                   pltpu.roll(_sublane_scan(pb, hb, True), LRU_SEGMENTS - 1, 0), 0.0)

    def fix(j, carry):
        hs_ref[pl.ds(j, LRU_SEGMENTS, stride=pitch), :] = (
            hf_ref[j] + pf_ref[j] * cf + hb_ref[j] + pb_ref[j] * cb)
        return carry

    lax.fori_loop(0, L, fix, 0, unroll=8)

    for sg in range(LRU_SEGMENTS):
        y = y_ref[sg * L:(sg + 1) * L, :].astype(F32)
        o_ref[sg * L:(sg + 1) * L, :] = (hs_ref[sg * pitch:sg * pitch + L, :]
                                         * jax.nn.gelu(y)).astype(o_ref.dtype)


def _rg_lru(proj, conv_w, conv_b, w_cat, b_cat, lru_lambda, *, B, S, off_x, off_y):
    T = B * S
    NB, W, _ = w_cat.shape
    xblk = off_x // W
    yblk = off_y // W
    body = functools.partial(_lru_body, S=S, W=W)
    padded = LRU_SEGMENTS * (S // LRU_SEGMENTS + LRU_PITCH_PAD)
    return pl.pallas_call(
        body,
        grid=(B, NB),
        in_specs=[pl.BlockSpec((S, W), lambda b, n: (b, xblk + n)),
                  pl.BlockSpec((S, W), lambda b, n: (b, yblk + n)),
                  pl.BlockSpec((conv_w.shape[0], W), lambda b, n: (0, n)),
                  pl.BlockSpec((1, W), lambda b, n: (0, n)),
                  pl.BlockSpec((1, W, 4 * W), lambda b, n: (n, 0, 0)),
                  pl.BlockSpec((1, 1, 4 * W), lambda b, n: (n, 0, 0)),
                  pl.BlockSpec((2, W), lambda b, n: (0, n))],
        out_specs=pl.BlockSpec((S, W), lambda b, n: (b, n)),
        out_shape=jax.ShapeDtypeStruct((T, NB * W), BF16),
        scratch_shapes=([pltpu.VMEM((padded, W), F32)] * 4
                        + [pltpu.VMEM((S // LRU_SEGMENTS, LRU_SEGMENTS, W), F32)] * 4
                        + [pltpu.VMEM((padded, W), F32)]),
        compiler_params=_params("parallel", "parallel"),
        name="rg_lru",
    )(proj, proj, conv_w, conv_b[None], w_cat, b_cat, lru_lambda)


def _merge_body(o_ref, r_ref, ga_ref, gr_ref, x_ref, wpa_ref, wpl_ref, wo_ref, g_ref, wr_ref,
                h_ref, hn_ref, aff_ref, *, E):
    ba = jnp.dot(o_ref[...], wpa_ref[...], preferred_element_type=F32)
    br = jnp.dot(r_ref[...], wpl_ref[...], preferred_element_type=F32)
    mixed = (jax.nn.sigmoid(ga_ref[...].astype(F32)) * ba
             + jax.nn.sigmoid(gr_ref[...].astype(F32)) * br)
    h = x_ref[...] + jnp.dot(mixed.astype(BF16), wo_ref[...], preferred_element_type=F32)
    h_ref[...] = h
    hn = h * lax.rsqrt(jnp.mean(h * h, axis=-1, keepdims=True) + EPS) * g_ref[...]
    hn_hi = hn.astype(BF16)
    hn_ref[...] = hn_hi
    hn_lo = (hn - hn_hi.astype(F32)).astype(BF16)
    lg = (jnp.dot(hn_hi, wr_ref[...], preferred_element_type=F32)
          + jnp.dot(hn_lo, wr_ref[...], preferred_element_type=F32))
    logits = lg[:, :E] + lg[:, E:2 * E]
    logits = logits - jnp.max(logits, axis=-1, keepdims=True)
    ex = jnp.exp(logits)
    aff_ref[...] = ex / jnp.sum(ex, axis=-1, keepdims=True)


def _merge(o_attn, lru_out, proj, x2, wpa, wpl, wo, g_ffn, wr2, *, off_ga, off_gr, E):
    T, D = x2.shape
    tm = min(512, T)
    gab = off_ga // D
    grb = off_gr // D
    row = lambda i: (i, 0)
    const = lambda i: (0, 0)
    return pl.pallas_call(
        functools.partial(_merge_body, E=E),
        grid=(T // tm,),
        in_specs=[pl.BlockSpec((tm, D), row), pl.BlockSpec((tm, D), row),
                  pl.BlockSpec((tm, D), lambda i: (i, gab)), pl.BlockSpec((tm, D), lambda i: (i, grb)),
                  pl.BlockSpec((tm, D), row),
                  pl.BlockSpec((D, D), const), pl.BlockSpec((D, D), const), pl.BlockSpec((D, D), const),
                  pl.BlockSpec((1, D), const), pl.BlockSpec((D, 2 * E), const)],
        out_specs=[pl.BlockSpec((tm, D), row), pl.BlockSpec((tm, D), row), pl.BlockSpec((tm, E), row)],
        out_shape=[jax.ShapeDtypeStruct((T, D), F32), jax.ShapeDtypeStruct((T, D), BF16),
                   jax.ShapeDtypeStruct((T, E), F32)],
        compiler_params=_params("parallel"),
        name="merge_router",
    )(o_attn, lru_out, proj, proj, x2, wpa, wpl, wo, g_ffn, wr2)


def _select_body(aff_ref, pos_ref, *, E, S, C):
    bits = pltpu.bitcast(aff_ref[0], jnp.int32)
    t = jnp.zeros((E, 1), jnp.int32)
    for bit in range(30, -1, -1):
        cand = t | (1 << bit)
        cnt = jnp.sum(jnp.where(bits >= cand, 1.0, 0.0), axis=-1, keepdims=True)
        t = jnp.where(cnt >= C, cand, t)
    gt = bits > t
    eq = bits == t
    need = C - jnp.sum(jnp.where(gt, 1.0, 0.0), axis=-1, keepdims=True).astype(jnp.int32)
    packed = jnp.where(gt, 1 << 16, 0) + jnp.where(eq, 1, 0)
    lane = lax.broadcasted_iota(jnp.int32, (E, S), 1)
    incl = packed
    d = 1
    while d < S:
        incl = incl + jnp.where(lane >= d, pltpu.roll(incl, d, 1), 0)
        d *= 2
    excl = incl - packed
    n_gt = excl >> 16
    n_eq = excl & 0xFFFF
    sel = gt | (eq & (n_eq < need))
    pos_ref[0] = jnp.where(sel, n_gt + jnp.minimum(n_eq, need), -1)


def _select(aff_t, C):
    B, E, S = aff_t.shape
    return pl.pallas_call(
        functools.partial(_select_body, E=E, S=S, C=C),
        grid=(B,),
        in_specs=[pl.BlockSpec((1, E, S), lambda b: (b, 0, 0))],
        out_specs=pl.BlockSpec((1, E, S), lambda b: (b, 0, 0)),
        out_shape=jax.ShapeDtypeStruct((B, E, S), jnp.int32),
        compiler_params=_params("parallel"),
        name="topc_select",
    )(aff_t)


def _ffn_body(pos_ref, hn_ref, wg_ref, wu_ref, wd_ref, y_ref, *, C, S, fc):
    slot = lax.broadcasted_iota(jnp.int32, (C, S), 0)
    onehot = jnp.where(pos_ref[0, 0] == slot, 1.0, 0.0).astype(BF16)
    xg = jnp.dot(onehot, hn_ref[0], preferred_element_type=F32).astype(BF16)
    F = wg_ref.shape[2]
    y = jnp.zeros(y_ref.shape[2:], F32)
    for f0 in range(0, F, fc):
        g = jnp.dot(xg, wg_ref[0, :, f0:f0 + fc], preferred_element_type=F32)
        u = jnp.dot(xg, wu_ref[0, :, f0:f0 + fc], preferred_element_type=F32)
        hid = (g * jax.nn.sigmoid(g) * u).astype(BF16)
        y = y + jnp.dot(hid, wd_ref[0, f0:f0 + fc, :], preferred_element_type=F32)
    y_ref[0, 0] = y.astype(y_ref.dtype)


def _expert_ffn(pos_rows, hn3, wg, wu, wd, C):
    B, S, D = hn3.shape
    E, _, F = wg.shape
    fc = min(512, F)
    return pl.pallas_call(
        functools.partial(_ffn_body, C=C, S=S, fc=fc),
        grid=(E, B),
        in_specs=[pl.BlockSpec((1, 1, 1, S), lambda e, b: (b, e, 0, 0)),
                  pl.BlockSpec((1, S, D), lambda e, b: (b, 0, 0)),
                  pl.BlockSpec((1, D, F), lambda e, b: (e, 0, 0)),
                  pl.BlockSpec((1, D, F), lambda e, b: (e, 0, 0)),
                  pl.BlockSpec((1, F, D), lambda e, b: (e, 0, 0))],
        out_specs=pl.BlockSpec((1, 1, C, D), lambda e, b: (b, e, 0, 0)),
        out_shape=jax.ShapeDtypeStruct((B, E, C, D), BF16),
        compiler_params=_params("arbitrary", "arbitrary"),
        name="expert_ffn",
    )(pos_rows, hn3, wg, wu, wd)


def _combine_body(pos_ref, aff_ref, y_ref, h_ref, o_ref, *, E, C):
    pc = pos_ref[0]
    ac = aff_ref[0]
    ts = pc.shape[0]
    slot = lax.broadcasted_iota(jnp.int32, (ts, C), 1)
    pieces = [jnp.where(pc[:, e:e + 1] == slot, ac[:, e:e + 1], 0.0).astype(BF16) for e in range(E)]
    scatter = jnp.concatenate(pieces, axis=1)
    y = y_ref[0].reshape(E * C, y_ref.shape[3])
    o_ref[0] = h_ref[0] + jnp.dot(scatter, y, preferred_element_type=F32)


def _combine(pos_cols, aff_cols, y, h3):
    B, S, D = h3.shape
    _, E, C, _ = y.shape
    ts = min(512, S)
    return pl.pallas_call(
        functools.partial(_combine_body, E=E, C=C),
        grid=(B, S // ts),
        in_specs=[pl.BlockSpec((1, ts, E), lambda b, i: (b, i, 0)),
                  pl.BlockSpec((1, ts, E), lambda b, i: (b, i, 0)),
                  pl.BlockSpec((1, E, C, D), lambda b, i: (b, 0, 0, 0)),
                  pl.BlockSpec((1, ts, D), lambda b, i: (b, i, 0))],
        out_specs=pl.BlockSpec((1, ts, D), lambda b, i: (b, i, 0)),
        out_shape=jax.ShapeDtypeStruct((B, S, D), F32),
        compiler_params=_params("parallel", "arbitrary"),
        name="moe_combine",
    )(pos_cols, aff_cols, y, h3)


def kernel(x, g_mix, w_in, g_q, g_k, lam_q1, lam_k1, lam_q2, lam_k2, g_subln, rel_bias, conv_w, conv_b,
           gate_r_w, gate_r_b, gate_i_w, gate_i_b, lru_lambda, w_proj_attn, w_proj_lru, w_out, g_ffn,
           w_router, w_gate_e, w_up_e, w_down_e):
    B, S, D = x.shape
    depth = w_in.shape[0]
    H = rel_bias.shape[1]
    hd = g_q.shape[-1]
    vd = g_subln.shape[-1]
    qk_w = H * 2 * hd
    attn_w = H * vd
    lru_w = conv_w.shape[-1]
    NB, LB = gate_r_w.shape[2], gate_r_w.shape[3]
    E = w_router.shape[-1]
    C = EC_CAPACITY_FACTOR * S // E
    off_k = qk_w
    off_v = off_k + qk_w
    off_x = off_v + attn_w
    off_y = off_x + lru_w
    off_ga = off_y + lru_w
    off_gr = off_ga + D

    h2 = x.reshape(B * S, D)
    for layer in range(depth):
        lam_init = 0.8 - 0.6 * math.exp(-0.3 * layer)
        proj = _inproj(h2, g_mix[layer][None], w_in[layer].astype(BF16))

        lam_params = jnp.stack([lam_q1[layer], lam_k1[layer], lam_q2[layer], lam_k2[layer]])
        o_attn = _attention(proj, rel_bias, g_q[layer], g_k[layer], lam_params, g_subln[layer],
                            B=B, S=S, H=H, hd=hd, vd=vd, off_k=off_k, off_v=off_v, lam_init=lam_init)

        w_cat = jnp.concatenate([gate_r_w[layer, 0], gate_i_w[layer, 0],
                                 gate_r_w[layer, 1], gate_i_w[layer, 1]], axis=-1).astype(BF16)
        b_cat = jnp.stack([gate_r_b[layer, 0].reshape(NB, LB), gate_i_b[layer, 0].reshape(NB, LB),
                           gate_r_b[layer, 1].reshape(NB, LB), gate_i_b[layer, 1].reshape(NB, LB)],
                          axis=1).reshape(NB, 1, 4 * LB)
        lru_out = _rg_lru(proj, conv_w[layer], conv_b[layer], w_cat, b_cat, lru_lambda[layer],
                          B=B, S=S, off_x=off_x, off_y=off_y)

        wr = w_router[layer]
        wr_hi = wr.astype(BF16)
        wr2 = jnp.concatenate([wr_hi, (wr - wr_hi.astype(F32)).astype(BF16)], axis=1)
        h2, hn, aff = _merge(o_attn, lru_out, proj, h2, w_proj_attn[layer].astype(BF16),
                             w_proj_lru[layer].astype(BF16), w_out[layer].astype(BF16),
                             g_ffn[layer][None], wr2, off_ga=off_ga, off_gr=off_gr, E=E)

        aff_cols = aff.reshape(B, S, E)
        pos_rows = _select(aff_cols.transpose(0, 2, 1), C)
        y = _expert_ffn(pos_rows.reshape(B, E, 1, S), hn.reshape(B, S, D), w_gate_e[layer].astype(BF16),
                        w_up_e[layer].astype(BF16), w_down_e[layer].astype(BF16), C)
        h3 = _combine(pos_rows.transpose(0, 2, 1), aff_cols, y, h2.reshape(B, S, D))
        h2 = h3.reshape(B * S, D)
    return h2.reshape(B, S, D)
```

```python
import functools
import math

import jax
import jax.numpy as jnp
import numpy as np
from jax import lax
from jax.experimental import pallas as pl
from jax.experimental.pallas import tpu as pltpu

F32 = jnp.float32
BF16 = jnp.bfloat16
EPS = 1e-6
LANES = 128
SUBLANES = 8
VMEM_LIMIT = 56 * 1024 * 1024
LOG2E = 1.4426950408889634
NEG_BIG = -1e30
REL_MAX_DIST = 128
LRU_C = 8.0
EC_CAPACITY_FACTOR = 2


def _params(*sem):
    return pltpu.CompilerParams(dimension_semantics=sem, vmem_limit_bytes=VMEM_LIMIT)


def _inproj_body(x_ref, g_ref, w_ref, o_ref, xn_ref):
    @pl.when(pl.program_id(1) == 0)
    def _():
        x = x_ref[...]
        ms = jnp.mean(x * x, axis=-1, keepdims=True)
        xn_ref[...] = (x * lax.rsqrt(ms + EPS) * g_ref[...]).astype(BF16)

    o_ref[...] = jnp.dot(xn_ref[...], w_ref[...], preferred_element_type=F32).astype(o_ref.dtype)


def _inproj(x2, g, w_bf):
    T, D = x2.shape
    N = w_bf.shape[1]
    tm = min(1024, T)
    tn = min(1024, N)
    return pl.pallas_call(
        _inproj_body,
        grid=(T // tm, N // tn),
        in_specs=[pl.BlockSpec((tm, D), lambda i, j: (i, 0)),
                  pl.BlockSpec((1, D), lambda i, j: (0, 0)),
                  pl.BlockSpec((D, tn), lambda i, j: (0, j))],
        out_specs=pl.BlockSpec((tm, tn), lambda i, j: (i, j)),
        out_shape=jax.ShapeDtypeStruct((T, N), BF16),
        scratch_shapes=[pltpu.VMEM((tm, D), BF16)],
        compiler_params=_params("parallel", "arbitrary"),
        name="in_proj",
    )(x2, g, w_bf)


def _bucket_table(S, n_buckets):
    rel = np.arange(2 * S) - S
    half = n_buckets // 2
    max_exact = half // 2
    ret = np.where(rel > 0, half, 0)
    n = np.abs(rel)
    nf = np.maximum(n, max_exact).astype(np.float64)
    large = max_exact + (np.log(nf / max_exact) / math.log(REL_MAX_DIST / max_exact)
                         * (half - max_exact)).astype(np.int32)
    large = np.minimum(large, half - 1)
    return (ret + np.where(n < max_exact, n, large)).astype(np.int32).reshape(1, 2 * S)


def _seg_sumsq(x, seg_ones):
    return jnp.dot((x * x).astype(BF16), seg_ones, preferred_element_type=F32)


NORM_SLACK = 1.01
SAFE_LOG2_SPAN = 100.0


def _attn_body(bucket_ref, relb_ref, q_ref, k_ref, v_ref, gq_ref, gk_ref, lamp_ref, gs_ref, segq_ref,
               segv_ref, o_ref, e_ref, fast_ref, kn_ref, va_ref, m_ref, acc_ref,
               *, S, tq, n_sub, kc, hd, vd, lam_init, n_buckets):
    h = pl.program_id(0)
    b = pl.program_id(1)
    i = pl.program_id(2)
    q_scale = hd ** -0.5 * LOG2E

    @pl.when((b == 0) & (i == 0))
    def _():
        bk = bucket_ref[...]
        tab = jnp.zeros((1, 2 * S), F32)
        for n in range(n_buckets):
            tab = jnp.where(bk == n, relb_ref[h, n], tab)
        tab = tab * LOG2E
        bmax = jnp.max(tab, axis=-1, keepdims=True)
        bmin = jnp.min(tab, axis=-1, keepdims=True)
        bound = (NORM_SLACK * hd * q_scale) * (jnp.max(jnp.abs(gq_ref[...]), axis=-1, keepdims=True)
                                               * jnp.max(jnp.abs(gk_ref[...]), axis=-1, keepdims=True))
        span = 2.0 * bound + (bmax - bmin)
        fast_ref[0] = (span[0, 0] <= SAFE_LOG2_SPAN).astype(jnp.int32)
        tab = jnp.broadcast_to(tab - (bound + bmax), (tq, 2 * S))
        e_ref[...] = pltpu.roll(tab, 0, 1, stride=1, stride_axis=0)

    @pl.when(i == 0)
    def _():
        k = k_ref[...].astype(F32)
        ms = _seg_sumsq(k, segq_ref[...]) * (1.0 / hd)
        kn_ref[...] = (k * lax.rsqrt(ms + EPS) * gk_ref[...]).astype(BF16)
        va_ref[:, :vd] = v_ref[...]
        va_ref[:, vd:] = jnp.ones((S, vd), BF16)

    nt_dims = (((1,), (1,)), ((), ()))
    lp = lamp_ref[...]
    lam = (jnp.exp(jnp.sum(lp[0:1] * lp[1:2], axis=-1, keepdims=True))
           - jnp.exp(jnp.sum(lp[2:3] * lp[3:4], axis=-1, keepdims=True)) + lam_init)

    def q_maps_of(t):
        q = q_ref[t * tq:(t + 1) * tq, :].astype(F32)
        ms = _seg_sumsq(q, segq_ref[...]) * (1.0 / hd)
        qn = q * lax.rsqrt(ms + EPS) * (gq_ref[...] * q_scale)
        lo = lax.broadcasted_iota(jnp.int32, qn.shape, 1) < hd
        return jnp.where(lo, qn, 0.0).astype(BF16), jnp.where(lo, 0.0, qn).astype(BF16)

    def e_start(t, k0):
        return pl.multiple_of(S + k0 - (i * n_sub + t) * tq, LANES)

    def finish(t, a1, a2):
        o = a1[:, :vd] / a1[:, vd:] - lam * (a2[:, :vd] / a2[:, vd:])
        o = o * lax.rsqrt(_seg_sumsq(o, segv_ref[...]) * (1.0 / vd) + EPS)
        o_ref[t * tq:(t + 1) * tq, :] = (o * (gs_ref[...] * (1.0 - lam_init))).astype(o_ref.dtype)

    @pl.when(fast_ref[0] == 1)
    def _():
        for t in range(n_sub):
            q_maps = q_maps_of(t)
            bias = e_ref[:, pl.ds(e_start(t, 0), S)]
            accs = []
            for mi in range(2):
                s = lax.dot_general(q_maps[mi], kn_ref[...], nt_dims, preferred_element_type=F32)
                accs.append(jnp.dot(jnp.exp2(s + bias).astype(BF16), va_ref[...],
                                    preferred_element_type=F32))
            finish(t, *accs)

    @pl.when(fast_ref[0] != 1)
    def _():
        for t in range(n_sub):
            q_maps = q_maps_of(t)
            m_ref[...] = jnp.full(m_ref.shape, NEG_BIG, F32)
            acc_ref[...] = jnp.zeros(acc_ref.shape, F32)

            def chunk(c, carry):
                k0 = pl.multiple_of(c * kc, kc)
                bias = e_ref[:, pl.ds(e_start(t, k0), kc)]
                for mi in range(2):
                    s = lax.dot_general(q_maps[mi], kn_ref[pl.ds(k0, kc), :], nt_dims,
                                        preferred_element_type=F32) + bias
                    m_old = m_ref[mi]
                    m_new = jnp.maximum(m_old, jnp.max(s, axis=-1, keepdims=True))
                    p = jnp.exp2(s - m_new).astype(BF16)
                    acc_ref[mi] = (jnp.exp2(m_old - m_new) * acc_ref[mi]
                                   + jnp.dot(p, va_ref[pl.ds(k0, kc), :], preferred_element_type=F32))
                    m_ref[mi] = m_new
                return carry

            lax.fori_loop(0, S // kc, chunk, 0)
            finish(t, acc_ref[0], acc_ref[1])


def _attention(proj, rel_bias, g_q, g_k, lam_params, g_subln, *, B, S, H, hd, vd, off_k, off_v, lam_init):
    T = B * S
    tq = min(512, S)
    kc = min(512, S)
    n_sub = math.gcd(4, S // tq)
    tqb = n_sub * tq
    nq = S // tqb
    n_buckets = rel_bias.shape[0]
    hw = 2 * hd
    bucket = jnp.asarray(_bucket_table(S, n_buckets))
    body = functools.partial(_attn_body, S=S, tq=tq, n_sub=n_sub, kc=kc, hd=hd, vd=vd, lam_init=lam_init,
                             n_buckets=n_buckets)
    kblk = off_k // hw
    vblk = off_v // vd
    seg_q = np.kron(np.eye(2, dtype=np.float32), np.ones((hd, hd), np.float32))
    return pl.pallas_call(
        body,
        grid=(H, B, nq),
        in_specs=[pl.BlockSpec((1, 2 * S), lambda h, b, i: (0, 0)),
                  pl.BlockSpec(memory_space=pltpu.SMEM),
                  pl.BlockSpec((tqb, hw), lambda h, b, i: (b * nq + i, h)),
                  pl.BlockSpec((S, hw), lambda h, b, i: (b, kblk + h)),
                  pl.BlockSpec((S, vd), lambda h, b, i: (b, vblk + h)),
                  pl.BlockSpec((1, hw), lambda h, b, i: (0, 0)),
                  pl.BlockSpec((1, hw), lambda h, b, i: (0, 0)),
                  pl.BlockSpec((4, hd), lambda h, b, i: (0, 0)),
                  pl.BlockSpec((1, vd), lambda h, b, i: (0, 0)),
                  pl.BlockSpec((hw, hw), lambda h, b, i: (0, 0)),
                  pl.BlockSpec((vd, vd), lambda h, b, i: (0, 0))],
        out_specs=pl.BlockSpec((tqb, vd), lambda h, b, i: (b * nq + i, h)),
        out_shape=jax.ShapeDtypeStruct((T, H * vd), BF16),
        scratch_shapes=[pltpu.VMEM((tq, 2 * S), F32),
                        pltpu.SMEM((1,), jnp.int32),
                        pltpu.VMEM((S, hw), BF16),
                        pltpu.VMEM((S, 2 * vd), BF16),
                        pltpu.VMEM((2, tq, 1), F32),
                        pltpu.VMEM((2, tq, 2 * vd), F32)],
        compiler_params=_params("arbitrary", "arbitrary", "arbitrary"),
        name="diff_attention",
    )(bucket, rel_bias.T, proj, proj, proj, jnp.tile(g_q, 2)[None], jnp.tile(g_k, 2)[None],
      lam_params, g_subln[None], jnp.asarray(seg_q, BF16), jnp.ones((vd, vd), BF16))


LRU_SEGMENTS = SUBLANES
LRU_PITCH_PAD = 8


def _sublane_scan(a, bb, reverse):
    ri = lax.broadcasted_iota(jnp.int32, a.shape, 0)
    for d in (1, 2, 4):
        if reverse:
            keep, sh = ri < SUBLANES - d, SUBLANES - d
        else:
            keep, sh = ri >= d, d
        bb = bb + a * jnp.where(keep, pltpu.roll(bb, sh, 0), 0.0)
        a = a * jnp.where(keep, pltpu.roll(a, sh, 0), 1.0)
    return bb


def _lru_body(x_ref, y_ref, cw_ref, cb_ref, w_ref, bias_ref, lam_ref, o_ref,
              af_ref, bf_ref, ab_ref, bb_ref, hf_ref, pf_ref, hb_ref, pb_ref, hs_ref, *, S, W):
    L = S // LRU_SEGMENTS
    pitch = L + LRU_PITCH_PAD
    x = x_ref[...].astype(F32)
    row = lax.broadcasted_iota(jnp.int32, (S, W), 0)
    cw = cw_ref[...]
    xc = (cw[0:1] * jnp.where(row >= 2, pltpu.roll(x, 2, 0), 0.0)
          + cw[1:2] * jnp.where(row >= 1, pltpu.roll(x, 1, 0), 0.0)
          + cw[2:3] * x
          + cw[3:4] * jnp.where(row < S - 1, pltpu.roll(x, S - 1, 0), 0.0)
          + cb_ref[...])
    gates = jnp.dot(xc.astype(BF16), w_ref[0], preferred_element_type=F32) + bias_ref[0]
    lam = lam_ref[...]
    for d, (a_ref, b_ref) in enumerate(((af_ref, bf_ref), (ab_ref, bb_ref))):
        r = jax.nn.sigmoid(gates[:, (2 * d) * W:(2 * d + 1) * W])
        ig = jax.nn.sigmoid(gates[:, (2 * d + 1) * W:(2 * d + 2) * W])
        nl = -lam[d:d + 1]
        softplus = jnp.maximum(nl, 0.0) + jnp.log1p(jnp.exp(-jnp.abs(nl)))
        a = jnp.exp2((-LRU_C * LOG2E * softplus) * r)
        y2 = 1.0 - a * a
        bvals = jnp.where(y2 > 0.0, y2 * lax.rsqrt(y2), 0.0) * ig * xc
        for sg in range(LRU_SEGMENTS):
            a_ref[sg * pitch:sg * pitch + L, :] = a[sg * L:(sg + 1) * L]
            b_ref[sg * pitch:sg * pitch + L, :] = bvals[sg * L:(sg + 1) * L]

    def seg_rows(ref, j):
        return ref[pl.ds(j, LRU_SEGMENTS, stride=pitch), :]

    def step(j, carry):
        hf, pf, hb, pb = carry
        a = seg_rows(af_ref, j)
        hf = a * hf + seg_rows(bf_ref, j)
        pf = a * pf
        hf_ref[j] = hf
        pf_ref[j] = pf
        jb = L - 1 - j
        a = seg_rows(ab_ref, jb)
        hb = a * hb + seg_rows(bb_ref, jb)
        pb = a * pb
        hb_ref[jb] = hb
        pb_ref[jb] = pb
        return hf, pf, hb, pb

    zero = jnp.zeros((LRU_SEGMENTS, W), F32)
    one = jnp.ones((LRU_SEGMENTS, W), F32)
    hf, pf, hb, pb = lax.fori_loop(0, L, step, (zero, one, zero, one), unroll=8)

    si = lax.broadcasted_iota(jnp.int32, (LRU_SEGMENTS, W), 0)
    cf = jnp.where(si >= 1, pltpu.roll(_sublane_scan(pf, hf, False), 1, 0), 0.0)
    cb = jnp.where(si < LRU_SEGMENTS - 1,
                   pltpu.roll(_sublane_scan(pb, hb, True), LRU_SEGMENTS - 1, 0), 0.0)

    def fix(j, carry):
        hs_ref[pl.ds(j, LRU_SEGMENTS, stride=pitch), :] = (
            hf_ref[j] + pf_ref[j] * cf + hb_ref[j] + pb_ref[j] * cb)
        return carry

    lax.fori_loop(0, L, fix, 0, unroll=8)

    for sg in range(LRU_SEGMENTS):
        y = y_ref[sg * L:(sg + 1) * L, :].astype(F32)
        o_ref[sg * L:(sg + 1) * L, :] = (hs_ref[sg * pitch:sg * pitch + L, :]
                                         * jax.nn.gelu(y)).astype(o_ref.dtype)


def _rg_lru(proj, conv_w, conv_b, w_cat, b_cat, lru_lambda, *, B, S, off_x, off_y):
    T = B * S
    NB, W, _ = w_cat.shape
    xblk = off_x // W
    yblk = off_y // W
    body = functools.partial(_lru_body, S=S, W=W)
    padded = LRU_SEGMENTS * (S // LRU_SEGMENTS + LRU_PITCH_PAD)
    return pl.pallas_call(
        body,
        grid=(B, NB),
        in_specs=[pl.BlockSpec((S, W), lambda b, n: (b, xblk + n)),
                  pl.BlockSpec((S, W), lambda b, n: (b, yblk + n)),
                  pl.BlockSpec((conv_w.shape[0], W), lambda b, n: (0, n)),
                  pl.BlockSpec((1, W), lambda b, n: (0, n)),
                  pl.BlockSpec((1, W, 4 * W), lambda b, n: (n, 0, 0)),
                  pl.BlockSpec((1, 1, 4 * W), lambda b, n: (n, 0, 0)),
                  pl.BlockSpec((2, W), lambda b, n: (0, n))],
        out_specs=pl.BlockSpec((S, W), lambda b, n: (b, n)),
        out_shape=jax.ShapeDtypeStruct((T, NB * W), BF16),
        scratch_shapes=([pltpu.VMEM((padded, W), F32)] * 4
                        + [pltpu.VMEM((S // LRU_SEGMENTS, LRU_SEGMENTS, W), F32)] * 4
                        + [pltpu.VMEM((padded, W), F32)]),
        compiler_params=_params("parallel", "parallel"),
        name="rg_lru",
    )(proj, proj, conv_w, conv_b[None], w_cat, b_cat, lru_lambda)


def _merge_body(o_ref, r_ref, ga_ref, gr_ref, x_ref, wpa_ref, wpl_ref, wo_ref, g_ref, wr_ref,
                h_ref, hn_ref, aff_ref, *, E, n_sub):
    ts = o_ref.shape[0] // n_sub
    for t in range(n_sub):
        rows = slice(t * ts, (t + 1) * ts)
        ba = jnp.dot(o_ref[rows, :], wpa_ref[...], preferred_element_type=F32)
        br = jnp.dot(r_ref[rows, :], wpl_ref[...], preferred_element_type=F32)
        mixed = (jax.nn.sigmoid(ga_ref[rows, :].astype(F32)) * ba
                 + jax.nn.sigmoid(gr_ref[rows, :].astype(F32)) * br)
        h = x_ref[rows, :] + jnp.dot(mixed.astype(BF16), wo_ref[...], preferred_element_type=F32)
        h_ref[rows, :] = h
        hn = h * lax.rsqrt(jnp.mean(h * h, axis=-1, keepdims=True) + EPS) * g_ref[...]
        hn_hi = hn.astype(BF16)
        hn_ref[rows, :] = hn_hi
        hn_lo = (hn - hn_hi.astype(F32)).astype(BF16)
        lg = (jnp.dot(hn_hi, wr_ref[...], preferred_element_type=F32)
              + jnp.dot(hn_lo, wr_ref[...], preferred_element_type=F32))
        logits = lg[:, :E] + lg[:, E:2 * E]
        logits = logits - jnp.max(logits, axis=-1, keepdims=True)
        ex = jnp.exp(logits)
        aff_ref[rows, :] = ex / jnp.sum(ex, axis=-1, keepdims=True)


def _merge(o_attn, lru_out, proj, x2, wpa, wpl, wo, g_ffn, wr2, *, off_ga, off_gr, E):
    T, D = x2.shape
    tm = min(1024, T)
    n_sub = 2 if tm % 1024 == 0 else 1
    resident = dict(pipeline_mode=pl.Buffered(1))
    gab = off_ga // D
    grb = off_gr // D
    row = lambda i: (i, 0)
    const = lambda i: (0, 0)
    return pl.pallas_call(
        functools.partial(_merge_body, E=E, n_sub=n_sub),
        grid=(T // tm,),
        in_specs=[pl.BlockSpec((tm, D), row), pl.BlockSpec((tm, D), row),
                  pl.BlockSpec((tm, D), lambda i: (i, gab)), pl.BlockSpec((tm, D), lambda i: (i, grb)),
                  pl.BlockSpec((tm, D), row),
                  pl.BlockSpec((D, D), const, **resident), pl.BlockSpec((D, D), const, **resident),
                  pl.BlockSpec((D, D), const, **resident),
                  pl.BlockSpec((1, D), const), pl.BlockSpec((D, 2 * E), const)],
        out_specs=[pl.BlockSpec((tm, D), row), pl.BlockSpec((tm, D), row), pl.BlockSpec((tm, E), row)],
        out_shape=[jax.ShapeDtypeStruct((T, D), F32), jax.ShapeDtypeStruct((T, D), BF16),
                   jax.ShapeDtypeStruct((T, E), F32)],
        compiler_params=_params("parallel"),
        name="merge_router",
    )(o_attn, lru_out, proj, proj, x2, wpa, wpl, wo, g_ffn, wr2)


def _select_body(aff_ref, pos_ref, *, E, S, C):
    bits = pltpu.bitcast(aff_ref[0], jnp.int32)
    t = jnp.zeros((E, 1), jnp.int32)
    for bit in range(30, -1, -1):
        cand = t | (1 << bit)
        cnt = jnp.sum(jnp.where(bits >= cand, 1.0, 0.0), axis=-1, keepdims=True)
        t = jnp.where(cnt >= C, cand, t)
    gt = bits > t
    eq = bits == t
    need = C - jnp.sum(jnp.where(gt, 1.0, 0.0), axis=-1, keepdims=True).astype(jnp.int32)
    packed = jnp.where(gt, 1 << 16, 0) + jnp.where(eq, 1, 0)
    lane = lax.broadcasted_iota(jnp.int32, (E, S), 1)
    incl = packed
    d = 1
    while d < S:
        incl = incl + jnp.where(lane >= d, pltpu.roll(incl, d, 1), 0)
        d *= 2
    excl = incl - packed
    n_gt = excl >> 16
    n_eq = excl & 0xFFFF
    sel = gt | (eq & (n_eq < need))
    pos_ref[0] = jnp.where(sel, n_gt + jnp.minimum(n_eq, need), -1)


def _select(aff_t, C):
    B, E, S = aff_t.shape
    return pl.pallas_call(
        functools.partial(_select_body, E=E, S=S, C=C),
        grid=(B,),
        in_specs=[pl.BlockSpec((1, E, S), lambda b: (b, 0, 0))],
        out_specs=pl.BlockSpec((1, E, S), lambda b: (b, 0, 0)),
        out_shape=jax.ShapeDtypeStruct((B, E, S), jnp.int32),
        compiler_params=_params("parallel"),
        name="topc_select",
    )(aff_t)


def _ffn_body(pos_ref, hn_ref, wg_ref, wu_ref, wd_ref, y_ref, *, C, S, fc):
    slot = lax.broadcasted_iota(jnp.int32, (C, S), 0)
    onehot = jnp.where(pos_ref[0, 0] == slot, 1.0, 0.0).astype(BF16)
    xg = jnp.dot(onehot, hn_ref[0], preferred_element_type=F32).astype(BF16)
    F = wg_ref.shape[2]
    y = jnp.zeros(y_ref.shape[2:], F32)
    for f0 in range(0, F, fc):
        g = jnp.dot(xg, wg_ref[0, :, f0:f0 + fc], preferred_element_type=F32)
        u = jnp.dot(xg, wu_ref[0, :, f0:f0 + fc], preferred_element_type=F32)
        hid = (g * jax.nn.sigmoid(g) * u).astype(BF16)
        y = y + jnp.dot(hid, wd_ref[0, f0:f0 + fc, :], preferred_element_type=F32)
    y_ref[0, 0] = y.astype(y_ref.dtype)


def _expert_ffn(pos_rows, hn3, wg, wu, wd, C):
    B, S, D = hn3.shape
    E, _, F = wg.shape
    fc = min(512, F)
    return pl.pallas_call(
        functools.partial(_ffn_body, C=C, S=S, fc=fc),
        grid=(E, B),
        in_specs=[pl.BlockSpec((1, 1, 1, S), lambda e, b: (b, e, 0, 0)),
                  pl.BlockSpec((1, S, D), lambda e, b: (b, 0, 0)),
                  pl.BlockSpec((1, D, F), lambda e, b: (e, 0, 0)),
                  pl.BlockSpec((1, D, F), lambda e, b: (e, 0, 0)),
                  pl.BlockSpec((1, F, D), lambda e, b: (e, 0, 0))],
        out_specs=pl.BlockSpec((1, 1, C, D), lambda e, b: (b, e, 0, 0)),
        out_shape=jax.ShapeDtypeStruct((B, E, C, D), BF16),
        compiler_params=_params("arbitrary", "arbitrary"),
        name="expert_ffn",
    )(pos_rows, hn3, wg, wu, wd)


def _combine_body(pos_ref, aff_ref, y_ref, h_ref, o_ref, *, E, C):
    pc = pos_ref[0]
    ac = aff_ref[0]
    ts = pc.shape[0]
    slot = lax.broadcasted_iota(jnp.int32, (ts, C), 1)
    pieces = [jnp.where(pc[:, e:e + 1] == slot, ac[:, e:e + 1], 0.0).astype(BF16) for e in range(E)]
    scatter = jnp.concatenate(pieces, axis=1)
    y = y_ref[0].reshape(E * C, y_ref.shape[3])
    o_ref[0] = h_ref[0] + jnp.dot(scatter, y, preferred_element_type=F32)


def _combine(pos_cols, aff_cols, y, h3):
    B, S, D = h3.shape
    _, E, C, _ = y.shape
    ts = min(512, S)
    return pl.pallas_call(
        functools.partial(_combine_body, E=E, C=C),
        grid=(B, S // ts),
        in_specs=[pl.BlockSpec((1, ts, E), lambda b, i: (b, i, 0)),
                  pl.BlockSpec((1, ts, E), lambda b, i: (b, i, 0)),
                  pl.BlockSpec((1, E, C, D), lambda b, i: (b, 0, 0, 0)),
                  pl.BlockSpec((1, ts, D), lambda b, i: (b, i, 0))],
        out_specs=pl.BlockSpec((1, ts, D), lambda b, i: (b, i, 0)),
        out_shape=jax.ShapeDtypeStruct((B, S, D), F32),
        compiler_params=_params("parallel", "arbitrary"),
        name="moe_combine",
    )(pos_cols, aff_cols, y, h3)


def kernel(x, g_mix, w_in, g_q, g_k, lam_q1, lam_k1, lam_q2, lam_k2, g_subln, rel_bias, conv_w, conv_b,
           gate_r_w, gate_r_b, gate_i_w, gate_i_b, lru_lambda, w_proj_attn, w_proj_lru, w_out, g_ffn,
           w_router, w_gate_e, w_up_e, w_down_e):
    B, S, D = x.shape
    depth = w_in.shape[0]
    H = rel_bias.shape[1]
    hd = g_q.shape[-1]
    vd = g_subln.shape[-1]
    qk_w = H * 2 * hd
    attn_w = H * vd
    lru_w = conv_w.shape[-1]
    NB, LB = gate_r_w.shape[2], gate_r_w.shape[3]
    E = w_router.shape[-1]
    C = EC_CAPACITY_FACTOR * S // E
    off_k = qk_w
    off_v = off_k + qk_w
    off_x = off_v + attn_w
    off_y = off_x + lru_w
    off_ga = off_y + lru_w
    off_gr = off_ga + D

    h2 = x.reshape(B * S, D)
    for layer in range(depth):
        lam_init = 0.8 - 0.6 * math.exp(-0.3 * layer)
        proj = _inproj(h2, g_mix[layer][None], w_in[layer].astype(BF16))

        lam_params = jnp.stack([lam_q1[layer], lam_k1[layer], lam_q2[layer], lam_k2[layer]])
        o_attn = _attention(proj, rel_bias, g_q[layer], g_k[layer], lam_params, g_subln[layer],
                            B=B, S=S, H=H, hd=hd, vd=vd, off_k=off_k, off_v=off_v, lam_init=lam_init)

        w_cat = jnp.concatenate([gate_r_w[layer, 0], gate_i_w[layer, 0],
                                 gate_r_w[layer, 1], gate_i_w[layer, 1]], axis=-1).astype(BF16)
        b_cat = jnp.stack([gate_r_b[layer, 0].reshape(NB, LB), gate_i_b[layer, 0].reshape(NB, LB),
                           gate_r_b[layer, 1].reshape(NB, LB), gate_i_b[layer, 1].reshape(NB, LB)],
                          axis=1).reshape(NB, 1, 4 * LB)
        lru_out = _rg_lru(proj, conv_w[layer], conv_b[layer], w_cat, b_cat, lru_lambda[layer],
                          B=B, S=S, off_x=off_x, off_y=off_y)

        wr = w_router[layer]
        wr_hi = wr.astype(BF16)
        wr2 = jnp.concatenate([wr_hi, (wr - wr_hi.astype(F32)).astype(BF16)], axis=1)
        h2, hn, aff = _merge(o_attn, lru_out, proj, h2, w_proj_attn[layer].astype(BF16),
                             w_proj_lru[layer].astype(BF16), w_out[layer].astype(BF16),
                             g_ffn[layer][None], wr2, off_ga=off_ga, off_gr=off_gr, E=E)

        aff_cols = aff.reshape(B, S, E)
        pos_rows = _select(aff_cols.transpose(0, 2, 1), C)
        y = _expert_ffn(pos_rows.reshape(B, E, 1, S), hn.reshape(B, S, D), w_gate_e[layer].astype(BF16),
                        w_up_e[layer].astype(BF16), w_down_e[layer].astype(BF16), C)
        h3 = _combine(pos_rows.transpose(0, 2, 1), aff_cols, y, h2.reshape(B, S, D))
        h2 = h3.reshape(B * S, D)
    return h2.reshape(B, S, D)
```

```python
import functools
import math

import jax
import jax.numpy as jnp
import numpy as np
from jax import lax
from jax.experimental import pallas as pl
from jax.experimental.pallas import tpu as pltpu

F32 = jnp.float32
BF16 = jnp.bfloat16
EPS = 1e-6
LANES = 128
SUBLANES = 8
VMEM_LIMIT = 56 * 1024 * 1024
LOG2E = 1.4426950408889634
NEG_BIG = -1e30
REL_MAX_DIST = 128
LRU_C = 8.0
EC_CAPACITY_FACTOR = 2


def _params(*sem):
    return pltpu.CompilerParams(dimension_semantics=sem, vmem_limit_bytes=VMEM_LIMIT)


def _inproj_body(x_ref, g_ref, w_ref, o_ref, xn_ref):
    @pl.when(pl.program_id(1) == 0)
    def _():
        x = x_ref[...]
        ms = jnp.mean(x * x, axis=-1, keepdims=True)
        xn_ref[...] = (x * lax.rsqrt(ms + EPS) * g_ref[...]).astype(BF16)

    o_ref[...] = jnp.dot(xn_ref[...], w_ref[...], preferred_element_type=F32).astype(o_ref.dtype)


def _inproj(x2, g, w_bf):
    T, D = x2.shape
    N = w_bf.shape[1]
    tm = min(1024, T)
    tn = min(1024, N)
    return pl.pallas_call(
        _inproj_body,
        grid=(T // tm, N // tn),
        in_specs=[pl.BlockSpec((tm, D), lambda i, j: (i, 0)),
                  pl.BlockSpec((1, D), lambda i, j: (0, 0)),
                  pl.BlockSpec((D, tn), lambda i, j: (0, j))],
        out_specs=pl.BlockSpec((tm, tn), lambda i, j: (i, j)),
        out_shape=jax.ShapeDtypeStruct((T, N), BF16),
        scratch_shapes=[pltpu.VMEM((tm, D), BF16)],
        compiler_params=_params("parallel", "arbitrary"),
        name="in_proj",
    )(x2, g, w_bf)


def _bucket_table(S, n_buckets):
    rel = np.arange(2 * S) - S
    half = n_buckets // 2
    max_exact = half // 2
    ret = np.where(rel > 0, half, 0)
    n = np.abs(rel)
    nf = np.maximum(n, max_exact).astype(np.float64)
    large = max_exact + (np.log(nf / max_exact) / math.log(REL_MAX_DIST / max_exact)
                         * (half - max_exact)).astype(np.int32)
    large = np.minimum(large, half - 1)
    return (ret + np.where(n < max_exact, n, large)).astype(np.int32).reshape(1, 2 * S)


def _seg_sumsq(x, seg_ones):
    x2 = x * x
    hi = x2.astype(BF16)
    lo = (x2 - hi.astype(F32)).astype(BF16)
    return (jnp.dot(hi, seg_ones, preferred_element_type=F32)
            + jnp.dot(lo, seg_ones, preferred_element_type=F32))


NORM_SLACK = 1.01
SAFE_LOG2_SPAN = 100.0


def _attn_body(bucket_ref, relb_ref, q_ref, k_ref, v_ref, gq_ref, gk_ref, lamp_ref, gs_ref, segq_ref,
               segv_ref, o_ref, e_ref, fast_ref, kn_ref, va_ref, m_ref, acc_ref,
               *, S, tq, n_sub, kc, hd, vd, lam_init, n_buckets):
    h = pl.program_id(0)
    b = pl.program_id(1)
    i = pl.program_id(2)
    q_scale = hd ** -0.5 * LOG2E

    @pl.when((b == 0) & (i == 0))
    def _():
        bk = bucket_ref[...]
        tab = jnp.zeros((1, 2 * S), F32)
        for n in range(n_buckets):
            tab = jnp.where(bk == n, relb_ref[h, n], tab)
        tab = tab * LOG2E
        bmax = jnp.max(tab, axis=-1, keepdims=True)
        bmin = jnp.min(tab, axis=-1, keepdims=True)
        bound = (NORM_SLACK * hd * q_scale) * (jnp.max(jnp.abs(gq_ref[...]), axis=-1, keepdims=True)
                                               * jnp.max(jnp.abs(gk_ref[...]), axis=-1, keepdims=True))
        span = 2.0 * bound + (bmax - bmin)
        fast_ref[0] = (span[0, 0] <= SAFE_LOG2_SPAN).astype(jnp.int32)
        tab = jnp.broadcast_to(tab - (bound + bmax), (tq, 2 * S))
        e_ref[...] = pltpu.roll(tab, 0, 1, stride=1, stride_axis=0)

    @pl.when(i == 0)
    def _():
        k = k_ref[...].astype(F32)
        ms = _seg_sumsq(k, segq_ref[...]) * (1.0 / hd)
        kn_ref[...] = (k * lax.rsqrt(ms + EPS) * gk_ref[...]).astype(BF16)
        va_ref[:, :vd] = v_ref[...]
        va_ref[:, vd:] = jnp.ones((S, vd), BF16)

    nt_dims = (((1,), (1,)), ((), ()))
    lp = lamp_ref[...]
    lam = (jnp.exp(jnp.sum(lp[0:1] * lp[1:2], axis=-1, keepdims=True))
           - jnp.exp(jnp.sum(lp[2:3] * lp[3:4], axis=-1, keepdims=True)) + lam_init)

    def q_maps_of(t):
        q = q_ref[t * tq:(t + 1) * tq, :].astype(F32)
        ms = _seg_sumsq(q, segq_ref[...]) * (1.0 / hd)
        qn = q * lax.rsqrt(ms + EPS) * (gq_ref[...] * q_scale)
        lo = lax.broadcasted_iota(jnp.int32, qn.shape, 1) < hd
        return jnp.where(lo, qn, 0.0).astype(BF16), jnp.where(lo, 0.0, qn).astype(BF16)

    def e_start(t, k0):
        return pl.multiple_of(S + k0 - (i * n_sub + t) * tq, LANES)

    def finish(t, a1, a2):
        o = a1[:, :vd] / a1[:, vd:] - lam * (a2[:, :vd] / a2[:, vd:])
        o = o * lax.rsqrt(_seg_sumsq(o, segv_ref[...]) * (1.0 / vd) + EPS)
        o_ref[t * tq:(t + 1) * tq, :] = (o * (gs_ref[...] * (1.0 - lam_init))).astype(o_ref.dtype)

    @pl.when(fast_ref[0] == 1)
    def _():
        for t in range(n_sub):
            q_maps = q_maps_of(t)
            bias = e_ref[:, pl.ds(e_start(t, 0), S)]
            accs = []
            for mi in range(2):
                s = lax.dot_general(q_maps[mi], kn_ref[...], nt_dims, preferred_element_type=F32)
                accs.append(jnp.dot(jnp.exp2(s + bias).astype(BF16), va_ref[...],
                                    preferred_element_type=F32))
            finish(t, *accs)

    @pl.when(fast_ref[0] != 1)
    def _():
        for t in range(n_sub):
            q_maps = q_maps_of(t)
            m_ref[...] = jnp.full(m_ref.shape, NEG_BIG, F32)
            acc_ref[...] = jnp.zeros(acc_ref.shape, F32)

            def chunk(c, carry):
                k0 = pl.multiple_of(c * kc, kc)
                bias = e_ref[:, pl.ds(e_start(t, k0), kc)]
                for mi in range(2):
                    s = lax.dot_general(q_maps[mi], kn_ref[pl.ds(k0, kc), :], nt_dims,
                                        preferred_element_type=F32) + bias
                    m_old = m_ref[mi]
                    m_new = jnp.maximum(m_old, jnp.max(s, axis=-1, keepdims=True))
                    p = jnp.exp2(s - m_new).astype(BF16)
                    acc_ref[mi] = (jnp.exp2(m_old - m_new) * acc_ref[mi]
                                   + jnp.dot(p, va_ref[pl.ds(k0, kc), :], preferred_element_type=F32))
                    m_ref[mi] = m_new
                return carry

            lax.fori_loop(0, S // kc, chunk, 0)
            finish(t, acc_ref[0], acc_ref[1])


def _attention(proj, rel_bias, g_q, g_k, lam_params, g_subln, *, B, S, H, hd, vd, off_k, off_v, lam_init):
    T = B * S
    tq = min(512, S)
    kc = min(512, S)
    n_sub = math.gcd(4, S // tq)
    tqb = n_sub * tq
    nq = S // tqb
    n_buckets = rel_bias.shape[0]
    hw = 2 * hd
    bucket = jnp.asarray(_bucket_table(S, n_buckets))
    body = functools.partial(_attn_body, S=S, tq=tq, n_sub=n_sub, kc=kc, hd=hd, vd=vd, lam_init=lam_init,
                             n_buckets=n_buckets)
    kblk = off_k // hw
    vblk = off_v // vd
    seg_q = np.kron(np.eye(2, dtype=np.float32), np.ones((hd, hd), np.float32))
    return pl.pallas_call(
        body,
        grid=(H, B, nq),
        in_specs=[pl.BlockSpec((1, 2 * S), lambda h, b, i: (0, 0)),
                  pl.BlockSpec(memory_space=pltpu.SMEM),
                  pl.BlockSpec((tqb, hw), lambda h, b, i: (b * nq + i, h)),
                  pl.BlockSpec((S, hw), lambda h, b, i: (b, kblk + h)),
                  pl.BlockSpec((S, vd), lambda h, b, i: (b, vblk + h)),
                  pl.BlockSpec((1, hw), lambda h, b, i: (0, 0)),
                  pl.BlockSpec((1, hw), lambda h, b, i: (0, 0)),
                  pl.BlockSpec((4, hd), lambda h, b, i: (0, 0)),
                  pl.BlockSpec((1, vd), lambda h, b, i: (0, 0)),
                  pl.BlockSpec((hw, hw), lambda h, b, i: (0, 0)),
                  pl.BlockSpec((vd, vd), lambda h, b, i: (0, 0))],
        out_specs=pl.BlockSpec((tqb, vd), lambda h, b, i: (b * nq + i, h)),
        out_shape=jax.ShapeDtypeStruct((T, H * vd), BF16),
        scratch_shapes=[pltpu.VMEM((tq, 2 * S), F32),
                        pltpu.SMEM((1,), jnp.int32),
                        pltpu.VMEM((S, hw), BF16),
                        pltpu.VMEM((S, 2 * vd), BF16),
                        pltpu.VMEM((2, tq, 1), F32),
                        pltpu.VMEM((2, tq, 2 * vd), F32)],
        compiler_params=_params("arbitrary", "arbitrary", "arbitrary"),
        name="diff_attention",
    )(bucket, rel_bias.T, proj, proj, proj, jnp.tile(g_q, 2)[None], jnp.tile(g_k, 2)[None],
      lam_params, g_subln[None], jnp.asarray(seg_q, BF16), jnp.ones((vd, vd), BF16))


LRU_SEGMENTS = SUBLANES
LRU_PITCH_PAD = 8


def _sublane_scan(a, bb, reverse):
    ri = lax.broadcasted_iota(jnp.int32, a.shape, 0)
    for d in (1, 2, 4):
        if reverse:
            keep, sh = ri < SUBLANES - d, SUBLANES - d
        else:
            keep, sh = ri >= d, d
        bb = bb + a * jnp.where(keep, pltpu.roll(bb, sh, 0), 0.0)
        a = a * jnp.where(keep, pltpu.roll(a, sh, 0), 1.0)
    return bb


def _lru_body(x_ref, y_ref, cw_ref, cb_ref, w_ref, bias_ref, lam_ref, o_ref,
              af_ref, bf_ref, ab_ref, bb_ref, hf_ref, pf_ref, hb_ref, pb_ref, hs_ref, *, S, W):
    L = S // LRU_SEGMENTS
    pitch = L + LRU_PITCH_PAD
    x = x_ref[...].astype(F32)
    row = lax.broadcasted_iota(jnp.int32, (S, W), 0)
    cw = cw_ref[...]
    xc = (cw[0:1] * jnp.where(row >= 2, pltpu.roll(x, 2, 0), 0.0)
          + cw[1:2] * jnp.where(row >= 1, pltpu.roll(x, 1, 0), 0.0)
          + cw[2:3] * x
          + cw[3:4] * jnp.where(row < S - 1, pltpu.roll(x, S - 1, 0), 0.0)
          + cb_ref[...])
    gates = jnp.dot(xc.astype(BF16), w_ref[0], preferred_element_type=F32) + bias_ref[0]
    lam = lam_ref[...]
    for d, (a_ref, b_ref) in enumerate(((af_ref, bf_ref), (ab_ref, bb_ref))):
        r = jax.nn.sigmoid(gates[:, (2 * d) * W:(2 * d + 1) * W])
        ig = jax.nn.sigmoid(gates[:, (2 * d + 1) * W:(2 * d + 2) * W])
        nl = -lam[d:d + 1]
        softplus = jnp.maximum(nl, 0.0) + jnp.log1p(jnp.exp(-jnp.abs(nl)))
        a = jnp.exp2((-LRU_C * LOG2E * softplus) * r)
        y2 = 1.0 - a * a
        bvals = jnp.where(y2 > 0.0, y2 * lax.rsqrt(y2), 0.0) * ig * xc
        for sg in range(LRU_SEGMENTS):
            a_ref[sg * pitch:sg * pitch + L, :] = a[sg * L:(sg + 1) * L]
            b_ref[sg * pitch:sg * pitch + L, :] = bvals[sg * L:(sg + 1) * L]

    def seg_rows(ref, j):
        return ref[pl.ds(j, LRU_SEGMENTS, stride=pitch), :]

    def step(j, carry):
        hf, pf, hb, pb = carry
        a = seg_rows(af_ref, j)
        hf = a * hf + seg_rows(bf_ref, j)
        pf = a * pf
        hf_ref[j] = hf
        pf_ref[j] = pf
        jb = L - 1 - j
        a = seg_rows(ab_ref, jb)
        hb = a * hb + seg_rows(bb_ref, jb)
        pb = a * pb
        hb_ref[jb] = hb
        pb_ref[jb] = pb
        return hf, pf, hb, pb

    zero = jnp.zeros((LRU_SEGMENTS, W), F32)
    one = jnp.ones((LRU_SEGMENTS, W), F32)
    hf, pf, hb, pb = lax.fori_loop(0, L, step, (zero, one, zero, one), unroll=8)

    si = lax.broadcasted_iota(jnp.int32, (LRU_SEGMENTS, W), 0)
    cf = jnp.where(si >= 1, pltpu.roll(_sublane_scan(pf, hf, False), 1, 0), 0.0)
    cb = jnp.where(si < LRU_SEGMENTS - 1,
                   pltpu.roll(_sublane_scan(pb, hb, True), LRU_SEGMENTS - 1, 0), 0.0)

    def fix(j, carry):
        hs_ref[pl.ds(j, LRU_SEGMENTS, stride=pitch), :] = (
            hf_ref[j] + pf_ref[j] * cf + hb_ref[j] + pb_ref[j] * cb)
        return carry

    lax.fori_loop(0, L, fix, 0, unroll=8)

    for sg in range(LRU_SEGMENTS):
        y = y_ref[sg * L:(sg + 1) * L, :].astype(F32)
        o_ref[sg * L:(sg + 1) * L, :] = (hs_ref[sg * pitch:sg * pitch + L, :]
                                         * jax.nn.gelu(y)).astype(o_ref.dtype)


def _rg_lru(proj, conv_w, conv_b, w_cat, b_cat, lru_lambda, *, B, S, off_x, off_y):
    T = B * S
    NB, W, _ = w_cat.shape
    xblk = off_x // W
    yblk = off_y // W
    body = functools.partial(_lru_body, S=S, W=W)
    padded = LRU_SEGMENTS * (S // LRU_SEGMENTS + LRU_PITCH_PAD)
    return pl.pallas_call(
        body,
        grid=(B, NB),
        in_specs=[pl.BlockSpec((S, W), lambda b, n: (b, xblk + n)),
                  pl.BlockSpec((S, W), lambda b, n: (b, yblk + n)),
                  pl.BlockSpec((conv_w.shape[0], W), lambda b, n: (0, n)),
                  pl.BlockSpec((1, W), lambda b, n: (0, n)),
                  pl.BlockSpec((1, W, 4 * W), lambda b, n: (n, 0, 0)),
                  pl.BlockSpec((1, 1, 4 * W), lambda b, n: (n, 0, 0)),
                  pl.BlockSpec((2, W), lambda b, n: (0, n))],
        out_specs=pl.BlockSpec((S, W), lambda b, n: (b, n)),
        out_shape=jax.ShapeDtypeStruct((T, NB * W), BF16),
        scratch_shapes=([pltpu.VMEM((padded, W), F32)] * 4
                        + [pltpu.VMEM((S // LRU_SEGMENTS, LRU_SEGMENTS, W), F32)] * 4
                        + [pltpu.VMEM((padded, W), F32)]),
        compiler_params=_params("parallel", "parallel"),
        name="rg_lru",
    )(proj, proj, conv_w, conv_b[None], w_cat, b_cat, lru_lambda)


def _merge_body(o_ref, r_ref, ga_ref, gr_ref, x_ref, wpa_ref, wpl_ref, wo_ref, g_ref, wr_ref,
                h_ref, hn_ref, aff_ref, *, E, n_sub):
    ts = o_ref.shape[0] // n_sub
    for t in range(n_sub):
        rows = slice(t * ts, (t + 1) * ts)
        ba = jnp.dot(o_ref[rows, :], wpa_ref[...], preferred_element_type=F32)
        br = jnp.dot(r_ref[rows, :], wpl_ref[...], preferred_element_type=F32)
        mixed = (jax.nn.sigmoid(ga_ref[rows, :].astype(F32)) * ba
                 + jax.nn.sigmoid(gr_ref[rows, :].astype(F32)) * br)
        h = x_ref[rows, :] + jnp.dot(mixed.astype(BF16), wo_ref[...], preferred_element_type=F32)
        h_ref[rows, :] = h
        hn = h * lax.rsqrt(jnp.mean(h * h, axis=-1, keepdims=True) + EPS) * g_ref[...]
        hn_hi = hn.astype(BF16)
        hn_ref[rows, :] = hn_hi
        hn_lo = (hn - hn_hi.astype(F32)).astype(BF16)
        lg = (jnp.dot(hn_hi, wr_ref[...], preferred_element_type=F32)
              + jnp.dot(hn_lo, wr_ref[...], preferred_element_type=F32))
        logits = lg[:, :E] + lg[:, E:2 * E]
        logits = logits - jnp.max(logits, axis=-1, keepdims=True)
        ex = jnp.exp(logits)
        aff_ref[rows, :] = ex / jnp.sum(ex, axis=-1, keepdims=True)


def _merge(o_attn, lru_out, proj, x2, wpa, wpl, wo, g_ffn, wr2, *, off_ga, off_gr, E):
    T, D = x2.shape
    tm = min(1024, T)
    n_sub = 2 if tm % 1024 == 0 else 1
    resident = dict(pipeline_mode=pl.Buffered(1))
    gab = off_ga // D
    grb = off_gr // D
    row = lambda i: (i, 0)
    const = lambda i: (0, 0)
    return pl.pallas_call(
        functools.partial(_merge_body, E=E, n_sub=n_sub),
        grid=(T // tm,),
        in_specs=[pl.BlockSpec((tm, D), row), pl.BlockSpec((tm, D), row),
                  pl.BlockSpec((tm, D), lambda i: (i, gab)), pl.BlockSpec((tm, D), lambda i: (i, grb)),
                  pl.BlockSpec((tm, D), row),
                  pl.BlockSpec((D, D), const, **resident), pl.BlockSpec((D, D), const, **resident),
                  pl.BlockSpec((D, D), const, **resident),
                  pl.BlockSpec((1, D), const), pl.BlockSpec((D, 2 * E), const)],
        out_specs=[pl.BlockSpec((tm, D), row), pl.BlockSpec((tm, D), row), pl.BlockSpec((tm, E), row)],
        out_shape=[jax.ShapeDtypeStruct((T, D), F32), jax.ShapeDtypeStruct((T, D), BF16),
                   jax.ShapeDtypeStruct((T, E), F32)],
        compiler_params=_params("parallel"),
        name="merge_router",
    )(o_attn, lru_out, proj, proj, x2, wpa, wpl, wo, g_ffn, wr2)


def _select_body(aff_ref, pos_ref, *, E, S, C):
    bits = pltpu.bitcast(aff_ref[0], jnp.int32)
    t = jnp.zeros((E, 1), jnp.int32)
    for bit in range(30, -1, -1):
        cand = t | (1 << bit)
        cnt = jnp.sum(jnp.where(bits >= cand, 1.0, 0.0), axis=-1, keepdims=True)
        t = jnp.where(cnt >= C, cand, t)
    gt = bits > t
    eq = bits == t
    need = C - jnp.sum(jnp.where(gt, 1.0, 0.0), axis=-1, keepdims=True).astype(jnp.int32)
    packed = jnp.where(gt, 1 << 16, 0) + jnp.where(eq, 1, 0)
    lane = lax.broadcasted_iota(jnp.int32, (E, S), 1)
    incl = packed
    d = 1
    while d < S:
        incl = incl + jnp.where(lane >= d, pltpu.roll(incl, d, 1), 0)
        d *= 2
    excl = incl - packed
    n_gt = excl >> 16
    n_eq = excl & 0xFFFF
    sel = gt | (eq & (n_eq < need))
    pos_ref[0] = jnp.where(sel, n_gt + jnp.minimum(n_eq, need), -1)


def _select(aff_t, C):
    B, E, S = aff_t.shape
    return pl.pallas_call(
        functools.partial(_select_body, E=E, S=S, C=C),
        grid=(B,),
        in_specs=[pl.BlockSpec((1, E, S), lambda b: (b, 0, 0))],
        out_specs=pl.BlockSpec((1, E, S), lambda b: (b, 0, 0)),
        out_shape=jax.ShapeDtypeStruct((B, E, S), jnp.int32),
        compiler_params=_params("parallel"),
        name="topc_select",
    )(aff_t)


def _ffn_body(pos_ref, hn_ref, wg_hbm, wu_hbm, wd_hbm, y_ref,
              wg_s, wu_s, wd_s, stg_g, stg_u, stg_d, sems, *, C, S, fc):
    e = pl.program_id(0)
    b = pl.program_id(1)
    n_experts = pl.num_programs(0)
    rows_in = stg_g.shape[0]
    rows_dn = stg_d.shape[0]
    n_chunks = wg_s.shape[1] // rows_in

    def chunk_copies(ee, k):
        r_in = pl.multiple_of(k * rows_in, rows_in)
        r_dn = pl.multiple_of(k * rows_dn, rows_dn)
        return (pltpu.make_async_copy(wg_hbm.at[ee, pl.ds(r_in, rows_in), :], stg_g, sems.at[0]),
                pltpu.make_async_copy(wu_hbm.at[ee, pl.ds(r_in, rows_in), :], stg_u, sems.at[1]),
                pltpu.make_async_copy(wd_hbm.at[ee, pl.ds(r_dn, rows_dn), :], stg_d, sems.at[2]))

    def cast_chunk(slot_, k):
        r_in = pl.multiple_of(k * rows_in, rows_in)
        r_dn = pl.multiple_of(k * rows_dn, rows_dn)
        wg_s[slot_, pl.ds(r_in, rows_in), :] = stg_g[...].astype(BF16)
        wu_s[slot_, pl.ds(r_in, rows_in), :] = stg_u[...].astype(BF16)
        wd_s[slot_, pl.ds(r_dn, rows_dn), :] = stg_d[...].astype(BF16)

    @pl.when((e == 0) & (b == 0))
    def _():
        def stage(k, carry):
            for cp in chunk_copies(0, k):
                cp.start()
            for cp in chunk_copies(0, k):
                cp.wait()
            cast_chunk(0, k)
            return carry

        lax.fori_loop(0, n_chunks, stage, 0)

    has_next = e + 1 < n_experts

    @pl.when(has_next)
    def _():
        for cp in chunk_copies(e + 1, b):
            cp.start()

    cur = e % 2
    slot = lax.broadcasted_iota(jnp.int32, (C, S), 0)
    onehot = jnp.where(pos_ref[0, 0] == slot, 1.0, 0.0).astype(BF16)
    xg = jnp.dot(onehot, hn_ref[0], preferred_element_type=F32).astype(BF16)
    F = wg_s.shape[2]
    y = jnp.zeros(y_ref.shape[2:], F32)
    for f0 in range(0, F, fc):
        g = jnp.dot(xg, wg_s[cur, :, f0:f0 + fc], preferred_element_type=F32)
        u = jnp.dot(xg, wu_s[cur, :, f0:f0 + fc], preferred_element_type=F32)
        hid = (g * jax.nn.sigmoid(g) * u).astype(BF16)
        y = y + jnp.dot(hid, wd_s[cur, f0:f0 + fc, :], preferred_element_type=F32)
    y_ref[0, 0] = y.astype(y_ref.dtype)

    @pl.when(has_next)
    def _():
        for cp in chunk_copies(e + 1, b):
            cp.wait()
        cast_chunk(1 - cur, b)


def _expert_ffn(pos_rows, hn3, wg, wu, wd, C):
    B, S, D = hn3.shape
    E, _, F = wg.shape
    fc = min(512, F)
    assert D % B == 0 and F % B == 0 and (D // B) % 16 == 0 and (F // B) % 16 == 0
    hbm = pl.BlockSpec(memory_space=pl.ANY)
    return pl.pallas_call(
        functools.partial(_ffn_body, C=C, S=S, fc=fc),
        grid=(E, B),
        in_specs=[pl.BlockSpec((1, 1, 1, S), lambda e, b: (b, e, 0, 0)),
                  pl.BlockSpec((1, S, D), lambda e, b: (b, 0, 0)),
                  hbm, hbm, hbm],
        out_specs=pl.BlockSpec((1, 1, C, D), lambda e, b: (b, e, 0, 0)),
        out_shape=jax.ShapeDtypeStruct((B, E, C, D), BF16),
        scratch_shapes=[pltpu.VMEM((2, D, F), BF16), pltpu.VMEM((2, D, F), BF16),
                        pltpu.VMEM((2, F, D), BF16),
                        pltpu.VMEM((D // B, F), F32), pltpu.VMEM((D // B, F), F32),
                        pltpu.VMEM((F // B, D), F32),
                        pltpu.SemaphoreType.DMA((3,))],
        compiler_params=_params("arbitrary", "arbitrary"),
        name="expert_ffn",
    )(pos_rows, hn3, wg, wu, wd)


def _combine_body(pos_ref, aff_ref, y_ref, h_ref, o_ref, *, E, C):
    pc = pos_ref[0]
    ac = aff_ref[0]
    ts = pc.shape[0]
    slot = lax.broadcasted_iota(jnp.int32, (ts, C), 1)
    pieces = [jnp.where(pc[:, e:e + 1] == slot, ac[:, e:e + 1], 0.0).astype(BF16) for e in range(E)]
    scatter = jnp.concatenate(pieces, axis=1)
    y = y_ref[0].reshape(E * C, y_ref.shape[3])
    o_ref[0] = h_ref[0] + jnp.dot(scatter, y, preferred_element_type=F32)


def _combine(pos_cols, aff_cols, y, h3):
    B, S, D = h3.shape
    _, E, C, _ = y.shape
    ts = min(512, S)
    return pl.pallas_call(
        functools.partial(_combine_body, E=E, C=C),
        grid=(B, S // ts),
        in_specs=[pl.BlockSpec((1, ts, E), lambda b, i: (b, i, 0)),
                  pl.BlockSpec((1, ts, E), lambda b, i: (b, i, 0)),
                  pl.BlockSpec((1, E, C, D), lambda b, i: (b, 0, 0, 0)),
                  pl.BlockSpec((1, ts, D), lambda b, i: (b, i, 0))],
        out_specs=pl.BlockSpec((1, ts, D), lambda b, i: (b, i, 0)),
        out_shape=jax.ShapeDtypeStruct((B, S, D), F32),
        compiler_params=_params("parallel", "arbitrary"),
        name="moe_combine",
    )(pos_cols, aff_cols, y, h3)


def kernel(x, g_mix, w_in, g_q, g_k, lam_q1, lam_k1, lam_q2, lam_k2, g_subln, rel_bias, conv_w, conv_b,
           gate_r_w, gate_r_b, gate_i_w, gate_i_b, lru_lambda, w_proj_attn, w_proj_lru, w_out, g_ffn,
           w_router, w_gate_e, w_up_e, w_down_e):
    B, S, D = x.shape
    depth = w_in.shape[0]
    H = rel_bias.shape[1]
    hd = g_q.shape[-1]
    vd = g_subln.shape[-1]
    qk_w = H * 2 * hd
    attn_w = H * vd
    lru_w = conv_w.shape[-1]
    NB, LB = gate_r_w.shape[2], gate_r_w.shape[3]
    E = w_router.shape[-1]
    C = EC_CAPACITY_FACTOR * S // E
    off_k = qk_w
    off_v = off_k + qk_w
    off_x = off_v + attn_w
    off_y = off_x + lru_w
    off_ga = off_y + lru_w
    off_gr = off_ga + D

    h2 = x.reshape(B * S, D)
    for layer in range(depth):
        lam_init = 0.8 - 0.6 * math.exp(-0.3 * layer)
        proj = _inproj(h2, g_mix[layer][None], w_in[layer].astype(BF16))

        lam_params = jnp.stack([lam_q1[layer], lam_k1[layer], lam_q2[layer], lam_k2[layer]])
        o_attn = _attention(proj, rel_bias, g_q[layer], g_k[layer], lam_params, g_subln[layer],
                            B=B, S=S, H=H, hd=hd, vd=vd, off_k=off_k, off_v=off_v, lam_init=lam_init)

        w_cat = jnp.concatenate([gate_r_w[layer, 0], gate_i_w[layer, 0],
                                 gate_r_w[layer, 1], gate_i_w[layer, 1]], axis=-1).astype(BF16)
        b_cat = jnp.stack([gate_r_b[layer, 0].reshape(NB, LB), gate_i_b[layer, 0].reshape(NB, LB),
                           gate_r_b[layer, 1].reshape(NB, LB), gate_i_b[layer, 1].reshape(NB, LB)],
                          axis=1).reshape(NB, 1, 4 * LB)
        lru_out = _rg_lru(proj, conv_w[layer], conv_b[layer], w_cat, b_cat, lru_lambda[layer],
                          B=B, S=S, off_x=off_x, off_y=off_y)

        wr = w_router[layer]
        wr_hi = wr.astype(BF16)
        wr2 = jnp.concatenate([wr_hi, (wr - wr_hi.astype(F32)).astype(BF16)], axis=1)
        h2, hn, aff = _merge(o_attn, lru_out, proj, h2, w_proj_attn[layer].astype(BF16),
                             w_proj_lru[layer].astype(BF16), w_out[layer].astype(BF16),
                             g_ffn[layer][None], wr2, off_ga=off_ga, off_gr=off_gr, E=E)

        aff_cols = aff.reshape(B, S, E)
        pos_rows = _select(aff_cols.transpose(0, 2, 1), C)
        y = _expert_ffn(pos_rows.reshape(B, E, 1, S), hn.reshape(B, S, D), w_gate_e[layer],
                        w_up_e[layer], w_down_e[layer], C)
        h3 = _combine(pos_rows.transpose(0, 2, 1), aff_cols, y, h2.reshape(B, S, D))
        h2 = h3.reshape(B * S, D)
    return h2.reshape(B, S, D)
```

```python
import functools
import math

import jax
import jax.numpy as jnp
import numpy as np
from jax import lax
from jax.experimental import pallas as pl
from jax.experimental.pallas import tpu as pltpu

F32 = jnp.float32
BF16 = jnp.bfloat16
EPS = 1e-6
LANES = 128
SUBLANES = 8
VMEM_LIMIT = 56 * 1024 * 1024
LOG2E = 1.4426950408889634
NEG_BIG = -1e30
REL_MAX_DIST = 128
LRU_C = 8.0
EC_CAPACITY_FACTOR = 2


def _params(*sem):
    return pltpu.CompilerParams(dimension_semantics=sem, vmem_limit_bytes=VMEM_LIMIT)


def _inproj_body(x_ref, g_ref, w_ref, o_ref, xn_ref):
    @pl.when(pl.program_id(1) == 0)
    def _():
        x = x_ref[...]
        ms = jnp.mean(x * x, axis=-1, keepdims=True)
        xn_ref[...] = (x * lax.rsqrt(ms + EPS) * g_ref[...]).astype(BF16)

    o_ref[...] = jnp.dot(xn_ref[...], w_ref[...], preferred_element_type=F32).astype(o_ref.dtype)


def _inproj(x2, g, w_bf):
    T, D = x2.shape
    N = w_bf.shape[1]
    tm = min(1024, T)
    tn = min(1024, N)
    return pl.pallas_call(
        _inproj_body,
        grid=(T // tm, N // tn),
        in_specs=[pl.BlockSpec((tm, D), lambda i, j: (i, 0)),
                  pl.BlockSpec((1, D), lambda i, j: (0, 0)),
                  pl.BlockSpec((D, tn), lambda i, j: (0, j))],
        out_specs=pl.BlockSpec((tm, tn), lambda i, j: (i, j)),
        out_shape=jax.ShapeDtypeStruct((T, N), BF16),
        scratch_shapes=[pltpu.VMEM((tm, D), BF16)],
        compiler_params=_params("parallel", "arbitrary"),
        name="in_proj",
    )(x2, g, w_bf)


def _bucket_table(S, n_buckets):
    rel = np.arange(2 * S) - S
    half = n_buckets // 2
    max_exact = half // 2
    ret = np.where(rel > 0, half, 0)
    n = np.abs(rel)
    nf = np.maximum(n, max_exact).astype(np.float64)
    large = max_exact + (np.log(nf / max_exact) / math.log(REL_MAX_DIST / max_exact)
                         * (half - max_exact)).astype(np.int32)
    large = np.minimum(large, half - 1)
    return (ret + np.where(n < max_exact, n, large)).astype(np.int32).reshape(1, 2 * S)


def _seg_sumsq(x, seg_ones):
    x2 = x * x
    hi = x2.astype(BF16)
    lo = (x2 - hi.astype(F32)).astype(BF16)
    return (jnp.dot(hi, seg_ones, preferred_element_type=F32)
            + jnp.dot(lo, seg_ones, preferred_element_type=F32))


NORM_SLACK = 1.01
SAFE_LOG2_SPAN = 100.0


def _attn_body(bucket_ref, relb_ref, q_ref, k_ref, v_ref, gq_ref, gk_ref, lamp_ref, gs_ref, segq_ref,
               segv_ref, o_ref, e_ref, fast_ref, kn_ref, va_ref, m_ref, acc_ref,
               *, S, tq, n_sub, kc, hd, vd, lam_init, n_buckets):
    h = pl.program_id(0)
    b = pl.program_id(1)
    i = pl.program_id(2)
    q_scale = hd ** -0.5 * LOG2E

    @pl.when((b == 0) & (i == 0))
    def _():
        bk = bucket_ref[...]
        tab = jnp.zeros((1, 2 * S), F32)
        for n in range(n_buckets):
            tab = jnp.where(bk == n, relb_ref[h, n], tab)
        tab = tab * LOG2E
        bmax = jnp.max(tab, axis=-1, keepdims=True)
        bmin = jnp.min(tab, axis=-1, keepdims=True)
        bound = (NORM_SLACK * hd * q_scale) * (jnp.max(jnp.abs(gq_ref[...]), axis=-1, keepdims=True)
                                               * jnp.max(jnp.abs(gk_ref[...]), axis=-1, keepdims=True))
        span = 2.0 * bound + (bmax - bmin)
        fast_ref[0] = (span[0, 0] <= SAFE_LOG2_SPAN).astype(jnp.int32)
        tab = jnp.broadcast_to(tab - (bound + bmax), (tq, 2 * S))
        e_ref[...] = pltpu.roll(tab, 0, 1, stride=1, stride_axis=0)

    @pl.when(i == 0)
    def _():
        k = k_ref[...].astype(F32)
        ms = _seg_sumsq(k, segq_ref[...]) * (1.0 / hd)
        kn_ref[...] = (k * lax.rsqrt(ms + EPS) * gk_ref[...]).astype(BF16)
        va_ref[:, :vd] = v_ref[...]
        va_ref[:, vd:] = jnp.ones((S, vd), BF16)

    nt_dims = (((1,), (1,)), ((), ()))
    lp = lamp_ref[...]
    lam = (jnp.exp(jnp.sum(lp[0:1] * lp[1:2], axis=-1, keepdims=True))
           - jnp.exp(jnp.sum(lp[2:3] * lp[3:4], axis=-1, keepdims=True)) + lam_init)

    def q_maps_of(t):
        q = q_ref[t * tq:(t + 1) * tq, :].astype(F32)
        ms = _seg_sumsq(q, segq_ref[...]) * (1.0 / hd)
        qn = q * lax.rsqrt(ms + EPS) * (gq_ref[...] * q_scale)
        lo = lax.broadcasted_iota(jnp.int32, qn.shape, 1) < hd
        return jnp.where(lo, qn, 0.0).astype(BF16), jnp.where(lo, 0.0, qn).astype(BF16)

    def e_start(t, k0):
        return pl.multiple_of(S + k0 - (i * n_sub + t) * tq, LANES)

    def finish(t, a1, a2):
        o = a1[:, :vd] / a1[:, vd:] - lam * (a2[:, :vd] / a2[:, vd:])
        o = o * lax.rsqrt(_seg_sumsq(o, segv_ref[...]) * (1.0 / vd) + EPS)
        o_ref[t * tq:(t + 1) * tq, :] = (o * (gs_ref[...] * (1.0 - lam_init))).astype(o_ref.dtype)

    @pl.when(fast_ref[0] == 1)
    def _():
        for t in range(n_sub):
            q_maps = q_maps_of(t)
            bias = e_ref[:, pl.ds(e_start(t, 0), S)]
            accs = []
            for mi in range(2):
                s = lax.dot_general(q_maps[mi], kn_ref[...], nt_dims, preferred_element_type=F32)
                accs.append(jnp.dot(jnp.exp2(s + bias).astype(BF16), va_ref[...],
                                    preferred_element_type=F32))
            finish(t, *accs)

    @pl.when(fast_ref[0] != 1)
    def _():
        for t in range(n_sub):
            q_maps = q_maps_of(t)
            m_ref[...] = jnp.full(m_ref.shape, NEG_BIG, F32)
            acc_ref[...] = jnp.zeros(acc_ref.shape, F32)

            def chunk(c, carry):
                k0 = pl.multiple_of(c * kc, kc)
                bias = e_ref[:, pl.ds(e_start(t, k0), kc)]
                for mi in range(2):
                    s = lax.dot_general(q_maps[mi], kn_ref[pl.ds(k0, kc), :], nt_dims,
                                        preferred_element_type=F32) + bias
                    m_old = m_ref[mi]
                    m_new = jnp.maximum(m_old, jnp.max(s, axis=-1, keepdims=True))
                    p = jnp.exp2(s - m_new).astype(BF16)
                    acc_ref[mi] = (jnp.exp2(m_old - m_new) * acc_ref[mi]
                                   + jnp.dot(p, va_ref[pl.ds(k0, kc), :], preferred_element_type=F32))
                    m_ref[mi] = m_new
                return carry

            lax.fori_loop(0, S // kc, chunk, 0)
            finish(t, acc_ref[0], acc_ref[1])


def _attention(proj, rel_bias, g_q, g_k, lam_params, g_subln, *, B, S, H, hd, vd, off_k, off_v, lam_init):
    T = B * S
    tq = min(512, S)
    kc = min(512, S)
    n_sub = math.gcd(4, S // tq)
    tqb = n_sub * tq
    nq = S // tqb
    n_buckets = rel_bias.shape[0]
    hw = 2 * hd
    bucket = jnp.asarray(_bucket_table(S, n_buckets))
    body = functools.partial(_attn_body, S=S, tq=tq, n_sub=n_sub, kc=kc, hd=hd, vd=vd, lam_init=lam_init,
                             n_buckets=n_buckets)
    kblk = off_k // hw
    vblk = off_v // vd
    seg_q = np.kron(np.eye(2, dtype=np.float32), np.ones((hd, hd), np.float32))
    return pl.pallas_call(
        body,
        grid=(H, B, nq),
        in_specs=[pl.BlockSpec((1, 2 * S), lambda h, b, i: (0, 0)),
                  pl.BlockSpec(memory_space=pltpu.SMEM),
                  pl.BlockSpec((tqb, hw), lambda h, b, i: (b * nq + i, h)),
                  pl.BlockSpec((S, hw), lambda h, b, i: (b, kblk + h)),
                  pl.BlockSpec((S, vd), lambda h, b, i: (b, vblk + h)),
                  pl.BlockSpec((1, hw), lambda h, b, i: (0, 0)),
                  pl.BlockSpec((1, hw), lambda h, b, i: (0, 0)),
                  pl.BlockSpec((4, hd), lambda h, b, i: (0, 0)),
                  pl.BlockSpec((1, vd), lambda h, b, i: (0, 0)),
                  pl.BlockSpec((hw, hw), lambda h, b, i: (0, 0)),
                  pl.BlockSpec((vd, vd), lambda h, b, i: (0, 0))],
        out_specs=pl.BlockSpec((tqb, vd), lambda h, b, i: (b * nq + i, h)),
        out_shape=jax.ShapeDtypeStruct((T, H * vd), BF16),
        scratch_shapes=[pltpu.VMEM((tq, 2 * S), F32),
                        pltpu.SMEM((1,), jnp.int32),
                        pltpu.VMEM((S, hw), BF16),
                        pltpu.VMEM((S, 2 * vd), BF16),
                        pltpu.VMEM((2, tq, 1), F32),
                        pltpu.VMEM((2, tq, 2 * vd), F32)],
        compiler_params=_params("arbitrary", "arbitrary", "arbitrary"),
        name="diff_attention",
    )(bucket, rel_bias.T, proj, proj, proj, jnp.tile(g_q, 2)[None], jnp.tile(g_k, 2)[None],
      lam_params, g_subln[None], jnp.asarray(seg_q, BF16), jnp.ones((vd, vd), BF16))


LRU_SEGMENTS = SUBLANES
LRU_PITCH_PAD = 8


def _sublane_scan(a, bb, reverse):
    ri = lax.broadcasted_iota(jnp.int32, a.shape, 0)
    for d in (1, 2, 4):
        if reverse:
            keep, sh = ri < SUBLANES - d, SUBLANES - d
        else:
            keep, sh = ri >= d, d
        bb = bb + a * jnp.where(keep, pltpu.roll(bb, sh, 0), 0.0)
        a = a * jnp.where(keep, pltpu.roll(a, sh, 0), 1.0)
    return bb


def _lru_body(x_ref, y_ref, cw_ref, cb_ref, w_ref, bias_ref, lam_ref, o_ref,
              af_ref, bf_ref, ab_ref, bb_ref, hf_ref, pf_ref, hb_ref, pb_ref, hs_ref, *, S, W):
    L = S // LRU_SEGMENTS
    pitch = L + LRU_PITCH_PAD
    x = x_ref[...].astype(F32)
    row = lax.broadcasted_iota(jnp.int32, (S, W), 0)
    cw = cw_ref[...]
    xc = (cw[0:1] * jnp.where(row >= 2, pltpu.roll(x, 2, 0), 0.0)
          + cw[1:2] * jnp.where(row >= 1, pltpu.roll(x, 1, 0), 0.0)
          + cw[2:3] * x
          + cw[3:4] * jnp.where(row < S - 1, pltpu.roll(x, S - 1, 0), 0.0)
          + cb_ref[...])
    gates = jnp.dot(xc.astype(BF16), w_ref[0], preferred_element_type=F32) + bias_ref[0]
    lam = lam_ref[...]
    for d, (a_ref, b_ref) in enumerate(((af_ref, bf_ref), (ab_ref, bb_ref))):
        r = jax.nn.sigmoid(gates[:, (2 * d) * W:(2 * d + 1) * W])
        ig = jax.nn.sigmoid(gates[:, (2 * d + 1) * W:(2 * d + 2) * W])
        nl = -lam[d:d + 1]
        softplus = jnp.maximum(nl, 0.0) + jnp.log1p(jnp.exp(-jnp.abs(nl)))
        a = jnp.exp2((-LRU_C * LOG2E * softplus) * r)
        y2 = 1.0 - a * a
        bvals = jnp.where(y2 > 0.0, y2 * lax.rsqrt(y2), 0.0) * ig * xc
        for sg in range(LRU_SEGMENTS):
            a_ref[sg * pitch:sg * pitch + L, :] = a[sg * L:(sg + 1) * L]
            b_ref[sg * pitch:sg * pitch + L, :] = bvals[sg * L:(sg + 1) * L]

    def seg_rows(ref, j):
        return ref[pl.ds(j, LRU_SEGMENTS, stride=pitch), :]

    def step(j, carry):
        hf, pf, hb, pb = carry
        a = seg_rows(af_ref, j)
        hf = a * hf + seg_rows(bf_ref, j)
        pf = a * pf
        hf_ref[j] = hf
        pf_ref[j] = pf
        jb = L - 1 - j
        a = seg_rows(ab_ref, jb)
        hb = a * hb + seg_rows(bb_ref, jb)
        pb = a * pb
        hb_ref[jb] = hb
        pb_ref[jb] = pb
        return hf, pf, hb, pb

    zero = jnp.zeros((LRU_SEGMENTS, W), F32)
    one = jnp.ones((LRU_SEGMENTS, W), F32)
    hf, pf, hb, pb = lax.fori_loop(0, L, step, (zero, one, zero, one), unroll=8)

    si = lax.broadcasted_iota(jnp.int32, (LRU_SEGMENTS, W), 0)
    cf = jnp.where(si >= 1, pltpu.roll(_sublane_scan(pf, hf, False), 1, 0), 0.0)
    cb = jnp.where(si < LRU_SEGMENTS - 1,
                   pltpu.roll(_sublane_scan(pb, hb, True), LRU_SEGMENTS - 1, 0), 0.0)

    def fix(j, carry):
        hs_ref[pl.ds(j, LRU_SEGMENTS, stride=pitch), :] = (
            hf_ref[j] + pf_ref[j] * cf + hb_ref[j] + pb_ref[j] * cb)
        return carry

    lax.fori_loop(0, L, fix, 0, unroll=8)

    for sg in range(LRU_SEGMENTS):
        y = y_ref[sg * L:(sg + 1) * L, :].astype(F32)
        o_ref[sg * L:(sg + 1) * L, :] = (hs_ref[sg * pitch:sg * pitch + L, :]
                                         * jax.nn.gelu(y)).astype(o_ref.dtype)


def _rg_lru(proj, conv_w, conv_b, w_cat, b_cat, lru_lambda, *, B, S, off_x, off_y):
    T = B * S
    NB, W, _ = w_cat.shape
    xblk = off_x // W
    yblk = off_y // W
    body = functools.partial(_lru_body, S=S, W=W)
    padded = LRU_SEGMENTS * (S // LRU_SEGMENTS + LRU_PITCH_PAD)
    return pl.pallas_call(
        body,
        grid=(B, NB),
        in_specs=[pl.BlockSpec((S, W), lambda b, n: (b, xblk + n)),
                  pl.BlockSpec((S, W), lambda b, n: (b, yblk + n)),
                  pl.BlockSpec((conv_w.shape[0], W), lambda b, n: (0, n)),
                  pl.BlockSpec((1, W), lambda b, n: (0, n)),
                  pl.BlockSpec((1, W, 4 * W), lambda b, n: (n, 0, 0)),
                  pl.BlockSpec((1, 1, 4 * W), lambda b, n: (n, 0, 0)),
                  pl.BlockSpec((2, W), lambda b, n: (0, n))],
        out_specs=pl.BlockSpec((S, W), lambda b, n: (b, n)),
        out_shape=jax.ShapeDtypeStruct((T, NB * W), BF16),
        scratch_shapes=([pltpu.VMEM((padded, W), F32)] * 4
                        + [pltpu.VMEM((S // LRU_SEGMENTS, LRU_SEGMENTS, W), F32)] * 4
                        + [pltpu.VMEM((padded, W), F32)]),
        compiler_params=_params("parallel", "parallel"),
        name="rg_lru",
    )(proj, proj, conv_w, conv_b[None], w_cat, b_cat, lru_lambda)


def _merge_body(o_ref, r_ref, ga_ref, gr_ref, x_ref, wpa_ref, wpl_ref, wo_ref, g_ref, wr_ref,
                h_ref, hn_ref, aff_ref, *, E, n_sub):
    ts = o_ref.shape[0] // n_sub
    for t in range(n_sub):
        rows = slice(t * ts, (t + 1) * ts)
        ba = jnp.dot(o_ref[rows, :], wpa_ref[...], preferred_element_type=F32)
        br = jnp.dot(r_ref[rows, :], wpl_ref[...], preferred_element_type=F32)
        mixed = (jax.nn.sigmoid(ga_ref[rows, :].astype(F32)) * ba
                 + jax.nn.sigmoid(gr_ref[rows, :].astype(F32)) * br)
        h = x_ref[rows, :] + jnp.dot(mixed.astype(BF16), wo_ref[...], preferred_element_type=F32)
        h_ref[rows, :] = h
        hn = h * lax.rsqrt(jnp.mean(h * h, axis=-1, keepdims=True) + EPS) * g_ref[...]
        hn_hi = hn.astype(BF16)
        hn_ref[rows, :] = hn_hi
        hn_lo = (hn - hn_hi.astype(F32)).astype(BF16)
        lg = (jnp.dot(hn_hi, wr_ref[...], preferred_element_type=F32)
              + jnp.dot(hn_lo, wr_ref[...], preferred_element_type=F32))
        logits = lg[:, :E] + lg[:, E:2 * E]
        logits = logits - jnp.max(logits, axis=-1, keepdims=True)
        ex = jnp.exp(logits)
        aff_ref[rows, :] = ex / jnp.sum(ex, axis=-1, keepdims=True)


def _merge(o_attn, lru_out, proj, x2, wpa, wpl, wo, g_ffn, wr2, *, off_ga, off_gr, E):
    T, D = x2.shape
    tm = min(1024, T)
    n_sub = 2 if tm % 1024 == 0 else 1
    resident = dict(pipeline_mode=pl.Buffered(1))
    gab = off_ga // D
    grb = off_gr // D
    row = lambda i: (i, 0)
    const = lambda i: (0, 0)
    return pl.pallas_call(
        functools.partial(_merge_body, E=E, n_sub=n_sub),
        grid=(T // tm,),
        in_specs=[pl.BlockSpec((tm, D), row), pl.BlockSpec((tm, D), row),
                  pl.BlockSpec((tm, D), lambda i: (i, gab)), pl.BlockSpec((tm, D), lambda i: (i, grb)),
                  pl.BlockSpec((tm, D), row),
                  pl.BlockSpec((D, D), const, **resident), pl.BlockSpec((D, D), const, **resident),
                  pl.BlockSpec((D, D), const, **resident),
                  pl.BlockSpec((1, D), const), pl.BlockSpec((D, 2 * E), const)],
        out_specs=[pl.BlockSpec((tm, D), row), pl.BlockSpec((tm, D), row), pl.BlockSpec((tm, E), row)],
        out_shape=[jax.ShapeDtypeStruct((T, D), F32), jax.ShapeDtypeStruct((T, D), BF16),
                   jax.ShapeDtypeStruct((T, E), F32)],
        compiler_params=_params("parallel"),
        name="merge_router",
    )(o_attn, lru_out, proj, proj, x2, wpa, wpl, wo, g_ffn, wr2)


def _select_body(aff_ref, pos_ref, *, E, S, C):
    bits = pltpu.bitcast(aff_ref[0], jnp.int32)
    t = jnp.zeros((E, 1), jnp.int32)
    for bit in range(30, -1, -1):
        cand = t | (1 << bit)
        cnt = jnp.sum(jnp.where(bits >= cand, 1.0, 0.0), axis=-1, keepdims=True)
        t = jnp.where(cnt >= C, cand, t)
    gt = bits > t
    eq = bits == t
    need = C - jnp.sum(jnp.where(gt, 1.0, 0.0), axis=-1, keepdims=True).astype(jnp.int32)
    packed = jnp.where(gt, 1 << 16, 0) + jnp.where(eq, 1, 0)
    lane = lax.broadcasted_iota(jnp.int32, (E, S), 1)
    incl = packed
    d = 1
    while d < S:
        incl = incl + jnp.where(lane >= d, pltpu.roll(incl, d, 1), 0)
        d *= 2
    excl = incl - packed
    n_gt = excl >> 16
    n_eq = excl & 0xFFFF
    sel = gt | (eq & (n_eq < need))
    pos_ref[0] = jnp.where(sel, n_gt + jnp.minimum(n_eq, need), -1)


def _select(aff_t, C):
    B, E, S = aff_t.shape
    return pl.pallas_call(
        functools.partial(_select_body, E=E, S=S, C=C),
        grid=(B,),
        in_specs=[pl.BlockSpec((1, E, S), lambda b: (b, 0, 0))],
        out_specs=pl.BlockSpec((1, E, S), lambda b: (b, 0, 0)),
        out_shape=jax.ShapeDtypeStruct((B, E, S), jnp.int32),
        compiler_params=_params("parallel"),
        name="topc_select",
    )(aff_t)


def _ffn_body(pos_ref, hn_ref, wg_hbm, wu_hbm, wd_hbm, y_ref,
              wg_s, wu_s, wd_s, stg_g, stg_u, stg_d, sems, *, C, S, fc):
    e = pl.program_id(0)
    b = pl.program_id(1)
    n_experts = pl.num_programs(0)
    rows_in = stg_g.shape[0]
    rows_dn = stg_d.shape[0]
    n_chunks = wg_s.shape[1] // rows_in

    def chunk_copies(ee, k):
        r_in = pl.multiple_of(k * rows_in, rows_in)
        r_dn = pl.multiple_of(k * rows_dn, rows_dn)
        return (pltpu.make_async_copy(wg_hbm.at[ee, pl.ds(r_in, rows_in), :], stg_g, sems.at[0]),
                pltpu.make_async_copy(wu_hbm.at[ee, pl.ds(r_in, rows_in), :], stg_u, sems.at[1]),
                pltpu.make_async_copy(wd_hbm.at[ee, pl.ds(r_dn, rows_dn), :], stg_d, sems.at[2]))

    def cast_chunk(slot_, k):
        r_in = pl.multiple_of(k * rows_in, rows_in)
        r_dn = pl.multiple_of(k * rows_dn, rows_dn)
        wg_s[slot_, pl.ds(r_in, rows_in), :] = stg_g[...].astype(BF16)
        wu_s[slot_, pl.ds(r_in, rows_in), :] = stg_u[...].astype(BF16)
        wd_s[slot_, pl.ds(r_dn, rows_dn), :] = stg_d[...].astype(BF16)

    @pl.when((e == 0) & (b == 0))
    def _():
        def stage(k, carry):
            for cp in chunk_copies(0, k):
                cp.start()
            for cp in chunk_copies(0, k):
                cp.wait()
            cast_chunk(0, k)
            return carry

        lax.fori_loop(0, n_chunks, stage, 0)

        @pl.when(n_experts > 1)
        def _():
            for cp in chunk_copies(1, 0):
                cp.start()

    cur = e % 2
    slot = lax.broadcasted_iota(jnp.int32, (C, S), 0)
    onehot = jnp.where(pos_ref[0, 0] == slot, 1.0, 0.0).astype(BF16)
    xg = jnp.dot(onehot, hn_ref[0], preferred_element_type=F32).astype(BF16)
    F = wg_s.shape[2]
    y = jnp.zeros(y_ref.shape[2:], F32)
    for f0 in range(0, F, fc):
        g = jnp.dot(xg, wg_s[cur, :, f0:f0 + fc], preferred_element_type=F32)
        u = jnp.dot(xg, wu_s[cur, :, f0:f0 + fc], preferred_element_type=F32)
        hid = (g * jax.nn.sigmoid(g) * u).astype(BF16)
        y = y + jnp.dot(hid, wd_s[cur, f0:f0 + fc, :], preferred_element_type=F32)
    y_ref[0, 0] = y.astype(y_ref.dtype)

    @pl.when(e + 1 < n_experts)
    def _():
        for cp in chunk_copies(e + 1, b):
            cp.wait()
        cast_chunk(1 - cur, b)

    wrap = b + 1 == n_chunks
    e_nxt = jnp.where(wrap, e + 2, e + 1)
    k_nxt = jnp.where(wrap, 0, b + 1)

    @pl.when(e_nxt < n_experts)
    def _():
        for cp in chunk_copies(e_nxt, k_nxt):
            cp.start()


def _expert_ffn(pos_rows, hn3, wg, wu, wd, C):
    B, S, D = hn3.shape
    E, _, F = wg.shape
    fc = min(512, F)
    assert D % B == 0 and F % B == 0 and (D // B) % 16 == 0 and (F // B) % 16 == 0
    hbm = pl.BlockSpec(memory_space=pl.ANY)
    return pl.pallas_call(
        functools.partial(_ffn_body, C=C, S=S, fc=fc),
        grid=(E, B),
        in_specs=[pl.BlockSpec((1, 1, 1, S), lambda e, b: (b, e, 0, 0)),
                  pl.BlockSpec((1, S, D), lambda e, b: (b, 0, 0)),
                  hbm, hbm, hbm],
        out_specs=pl.BlockSpec((1, 1, C, D), lambda e, b: (b, e, 0, 0)),
        out_shape=jax.ShapeDtypeStruct((B, E, C, D), BF16),
        scratch_shapes=[pltpu.VMEM((2, D, F), BF16), pltpu.VMEM((2, D, F), BF16),
                        pltpu.VMEM((2, F, D), BF16),
                        pltpu.VMEM((D // B, F), F32), pltpu.VMEM((D // B, F), F32),
                        pltpu.VMEM((F // B, D), F32),
                        pltpu.SemaphoreType.DMA((3,))],
        compiler_params=_params("arbitrary", "arbitrary"),
        name="expert_ffn",
    )(pos_rows, hn3, wg, wu, wd)


def _combine_body(pos_ref, aff_ref, y_ref, h_ref, o_ref, *, E, C):
    pc = pos_ref[0]
    ac = aff_ref[0]
    ts = pc.shape[0]
    slot = lax.broadcasted_iota(jnp.int32, (ts, C), 1)
    pieces = [jnp.where(pc[:, e:e + 1] == slot, ac[:, e:e + 1], 0.0).astype(BF16) for e in range(E)]
    scatter = jnp.concatenate(pieces, axis=1)
    y = y_ref[0].reshape(E * C, y_ref.shape[3])
    o_ref[0] = h_ref[0] + jnp.dot(scatter, y, preferred_element_type=F32)


def _combine(pos_cols, aff_cols, y, h3):
    B, S, D = h3.shape
    _, E, C, _ = y.shape
    ts = min(512, S)
    return pl.pallas_call(
        functools.partial(_combine_body, E=E, C=C),
        grid=(B, S // ts),
        in_specs=[pl.BlockSpec((1, ts, E), lambda b, i: (b, i, 0)),
                  pl.BlockSpec((1, ts, E), lambda b, i: (b, i, 0)),
                  pl.BlockSpec((1, E, C, D), lambda b, i: (b, 0, 0, 0)),
                  pl.BlockSpec((1, ts, D), lambda b, i: (b, i, 0))],
        out_specs=pl.BlockSpec((1, ts, D), lambda b, i: (b, i, 0)),
        out_shape=jax.ShapeDtypeStruct((B, S, D), F32),
        compiler_params=_params("parallel", "arbitrary"),
        name="moe_combine",
    )(pos_cols, aff_cols, y, h3)


def kernel(x, g_mix, w_in, g_q, g_k, lam_q1, lam_k1, lam_q2, lam_k2, g_subln, rel_bias, conv_w, conv_b,
           gate_r_w, gate_r_b, gate_i_w, gate_i_b, lru_lambda, w_proj_attn, w_proj_lru, w_out, g_ffn,
           w_router, w_gate_e, w_up_e, w_down_e):
    B, S, D = x.shape
    depth = w_in.shape[0]
    H = rel_bias.shape[1]
    hd = g_q.shape[-1]
    vd = g_subln.shape[-1]
    qk_w = H * 2 * hd
    attn_w = H * vd
    lru_w = conv_w.shape[-1]
    NB, LB = gate_r_w.shape[2], gate_r_w.shape[3]
    E = w_router.shape[-1]
    C = EC_CAPACITY_FACTOR * S // E
    off_k = qk_w
    off_v = off_k + qk_w
    off_x = off_v + attn_w
    off_y = off_x + lru_w
    off_ga = off_y + lru_w
    off_gr = off_ga + D

    h2 = x.reshape(B * S, D)
    for layer in range(depth):
        lam_init = 0.8 - 0.6 * math.exp(-0.3 * layer)
        proj = _inproj(h2, g_mix[layer][None], w_in[layer].astype(BF16))

        lam_params = jnp.stack([lam_q1[layer], lam_k1[layer], lam_q2[layer], lam_k2[layer]])
        o_attn = _attention(proj, rel_bias, g_q[layer], g_k[layer], lam_params, g_subln[layer],
                            B=B, S=S, H=H, hd=hd, vd=vd, off_k=off_k, off_v=off_v, lam_init=lam_init)

        w_cat = jnp.concatenate([gate_r_w[layer, 0], gate_i_w[layer, 0],
                                 gate_r_w[layer, 1], gate_i_w[layer, 1]], axis=-1).astype(BF16)
        b_cat = jnp.stack([gate_r_b[layer, 0].reshape(NB, LB), gate_i_b[layer, 0].reshape(NB, LB),
                           gate_r_b[layer, 1].reshape(NB, LB), gate_i_b[layer, 1].reshape(NB, LB)],
                          axis=1).reshape(NB, 1, 4 * LB)
        lru_out = _rg_lru(proj, conv_w[layer], conv_b[layer], w_cat, b_cat, lru_lambda[layer],
                          B=B, S=S, off_x=off_x, off_y=off_y)

        wr = w_router[layer]
        wr_hi = wr.astype(BF16)
        wr2 = jnp.concatenate([wr_hi, (wr - wr_hi.astype(F32)).astype(BF16)], axis=1)
        h2, hn, aff = _merge(o_attn, lru_out, proj, h2, w_proj_attn[layer].astype(BF16),
                             w_proj_lru[layer].astype(BF16), w_out[layer].astype(BF16),
                             g_ffn[layer][None], wr2, off_ga=off_ga, off_gr=off_gr, E=E)

        aff_cols = aff.reshape(B, S, E)
        pos_rows = _select(aff_cols.transpose(0, 2, 1), C)
        y = _expert_ffn(pos_rows.reshape(B, E, 1, S), hn.reshape(B, S, D), w_gate_e[layer],
                        w_up_e[layer], w_down_e[layer], C)
        h3 = _combine(pos_rows.transpose(0, 2, 1), aff_cols, y, h2.reshape(B, S, D))
        h2 = h3.reshape(B * S, D)
    return h2.reshape(B, S, D)
```

```python
import functools
import math

import jax
import jax.numpy as jnp
import numpy as np
from jax import lax
from jax.experimental import pallas as pl
from jax.experimental.pallas import tpu as pltpu

F32 = jnp.float32
BF16 = jnp.bfloat16
EPS = 1e-6
LANES = 128
SUBLANES = 8
VMEM_LIMIT = 56 * 1024 * 1024
LOG2E = 1.4426950408889634
NEG_BIG = -1e30
REL_MAX_DIST = 128
LRU_C = 8.0
EC_CAPACITY_FACTOR = 2


def _params(*sem):
    return pltpu.CompilerParams(dimension_semantics=sem, vmem_limit_bytes=VMEM_LIMIT)


def _inproj_body(x_ref, g_ref, w_ref, o_ref, xn_ref):
    @pl.when(pl.program_id(1) == 0)
    def _():
        x = x_ref[...]
        ms = jnp.mean(x * x, axis=-1, keepdims=True)
        xn_ref[...] = (x * lax.rsqrt(ms + EPS) * g_ref[...]).astype(BF16)

    o_ref[...] = jnp.dot(xn_ref[...], w_ref[...], preferred_element_type=F32).astype(o_ref.dtype)


def _inproj(x2, g, w_bf):
    T, D = x2.shape
    N = w_bf.shape[1]
    tm = min(1024, T)
    tn = min(1024, N)
    return pl.pallas_call(
        _inproj_body,
        grid=(T // tm, N // tn),
        in_specs=[pl.BlockSpec((tm, D), lambda i, j: (i, 0)),
                  pl.BlockSpec((1, D), lambda i, j: (0, 0)),
                  pl.BlockSpec((D, tn), lambda i, j: (0, j))],
        out_specs=pl.BlockSpec((tm, tn), lambda i, j: (i, j)),
        out_shape=jax.ShapeDtypeStruct((T, N), BF16),
        scratch_shapes=[pltpu.VMEM((tm, D), BF16)],
        compiler_params=_params("parallel", "arbitrary"),
        name="in_proj",
    )(x2, g, w_bf)


def _bucket_table(S, n_buckets, rel):
    half = n_buckets // 2
    max_exact = half // 2
    ret = np.where(rel > 0, half, 0)
    n = np.abs(rel)
    nf = np.maximum(n, max_exact).astype(np.float64)
    large = max_exact + (np.log(nf / max_exact) / math.log(REL_MAX_DIST / max_exact)
                         * (half - max_exact)).astype(np.int32)
    large = np.minimum(large, half - 1)
    return (ret + np.where(n < max_exact, n, large)).astype(np.int32).reshape(1, 2 * S)


def _seg_sumsq(x, seg_ones):
    x2 = x * x
    hi = x2.astype(BF16)
    lo = (x2 - hi.astype(F32)).astype(BF16)
    return (jnp.dot(hi, seg_ones, preferred_element_type=F32)
            + jnp.dot(lo, seg_ones, preferred_element_type=F32))


NORM_SLACK = 1.01
SAFE_LOG2_SPAN = 100.0


def _attn_body(bucket_ref, bucket_t_ref, relb_ref, q_ref, k_ref, v_ref, gq_ref, gk_ref, lamp_ref, gs_ref,
               segq_ref, segv_ref, o_ref, e_ref, et_ref, fast_ref, kn_ref, va_ref, vat_ref, m_ref, acc_ref,
               *, S, tq, n_sub, kc, hd, vd, lam_init, n_buckets):
    h = pl.program_id(0)
    b = pl.program_id(1)
    i = pl.program_id(2)
    q_scale = hd ** -0.5 * LOG2E

    @pl.when((b == 0) & (i == 0))
    def _():
        def bias_row(bk):
            tab = jnp.zeros((1, 2 * S), F32)
            for n in range(n_buckets):
                tab = jnp.where(bk == n, relb_ref[h, n], tab)
            return tab * LOG2E

        tab = bias_row(bucket_ref[...])
        bmax = jnp.max(tab, axis=-1, keepdims=True)
        bmin = jnp.min(tab, axis=-1, keepdims=True)
        bound = (NORM_SLACK * hd * q_scale) * (jnp.max(jnp.abs(gq_ref[...]), axis=-1, keepdims=True)
                                               * jnp.max(jnp.abs(gk_ref[...]), axis=-1, keepdims=True))
        span = 2.0 * bound + (bmax - bmin)
        fast_ref[0] = (span[0, 0] <= SAFE_LOG2_SPAN).astype(jnp.int32)
        shift = bound + bmax
        e_ref[...] = pltpu.roll(jnp.broadcast_to(tab - shift, (tq, 2 * S)), 0, 1, stride=1, stride_axis=0)
        tab_t = bias_row(bucket_t_ref[...])
        et_ref[...] = pltpu.roll(jnp.broadcast_to(tab_t - shift, (tq, 2 * S)), 0, 1, stride=1, stride_axis=0)

    @pl.when(i == 0)
    def _():
        k = k_ref[...].astype(F32)
        ms = _seg_sumsq(k, segq_ref[...]) * (1.0 / hd)
        kn_ref[...] = (k * lax.rsqrt(ms + EPS) * gk_ref[...]).astype(BF16)
        va_ref[:, :vd] = v_ref[...]
        va_ref[:, vd:] = jnp.ones((S, vd), BF16)
        vat_ref[:vd, :] = v_ref[...].astype(F32).T.astype(BF16)
        vat_ref[vd:, :] = jnp.ones((vat_ref.shape[0] - vd, S), BF16)

    nt_dims = (((1,), (1,)), ((), ()))
    lp = lamp_ref[...]
    lam = (jnp.exp(jnp.sum(lp[0:1] * lp[1:2], axis=-1, keepdims=True))
           - jnp.exp(jnp.sum(lp[2:3] * lp[3:4], axis=-1, keepdims=True)) + lam_init)

    def q_maps_of(t):
        q = q_ref[t * tq:(t + 1) * tq, :].astype(F32)
        ms = _seg_sumsq(q, segq_ref[...]) * (1.0 / hd)
        qn = q * lax.rsqrt(ms + EPS) * (gq_ref[...] * q_scale)
        lo = lax.broadcasted_iota(jnp.int32, qn.shape, 1) < hd
        return jnp.where(lo, qn, 0.0).astype(BF16), jnp.where(lo, 0.0, qn).astype(BF16)

    def e_start(t, k0):
        return pl.multiple_of(S + k0 - (i * n_sub + t) * tq, LANES)

    def finish(t, a1, a2):
        o = a1[:, :vd] / a1[:, vd:] - lam * (a2[:, :vd] / a2[:, vd:])
        o = o * lax.rsqrt(_seg_sumsq(o, segv_ref[...]) * (1.0 / vd) + EPS)
        o_ref[t * tq:(t + 1) * tq, :] = (o * (gs_ref[...] * (1.0 - lam_init))).astype(o_ref.dtype)

    def finish_t(t, a1, a2):
        ot = a1[:vd] / a1[vd:vd + 1] - lam * (a2[:vd] / a2[vd:vd + 1])
        o = ot.T
        o = o * lax.rsqrt(_seg_sumsq(o, segv_ref[...]) * (1.0 / vd) + EPS)
        o_ref[t * tq:(t + 1) * tq, :] = (o * (gs_ref[...] * (1.0 - lam_init))).astype(o_ref.dtype)

    @pl.when(fast_ref[0] == 1)
    def _():
        for t in range(n_sub):
            q_maps = q_maps_of(t)
            tile = i * n_sub + t
            accs = []
            for mi in range(2):
                chunks = []
                for c in range(S // tq):
                    st = lax.dot_general(kn_ref[c * tq:(c + 1) * tq, :], q_maps[mi], nt_dims,
                                         preferred_element_type=F32)
                    start = pl.multiple_of(S + (tile - c) * tq, LANES)
                    chunks.append(jnp.exp2(st + et_ref[:, pl.ds(start, tq)]).astype(BF16))
                accs.append(jnp.dot(vat_ref[...], jnp.concatenate(chunks, axis=0),
                                    preferred_element_type=F32))
            finish_t(t, *accs)

    @pl.when(fast_ref[0] != 1)
    def _():
        for t in range(n_sub):
            q_maps = q_maps_of(t)
            m_ref[...] = jnp.full(m_ref.shape, NEG_BIG, F32)
            acc_ref[...] = jnp.zeros(acc_ref.shape, F32)

            def chunk(c, carry):
                k0 = pl.multiple_of(c * kc, kc)
                bias = e_ref[:, pl.ds(e_start(t, k0), kc)]
                for mi in range(2):
                    s = lax.dot_general(q_maps[mi], kn_ref[pl.ds(k0, kc), :], nt_dims,
                                        preferred_element_type=F32) + bias
                    m_old = m_ref[mi]
                    m_new = jnp.maximum(m_old, jnp.max(s, axis=-1, keepdims=True))
                    p = jnp.exp2(s - m_new).astype(BF16)
                    acc_ref[mi] = (jnp.exp2(m_old - m_new) * acc_ref[mi]
                                   + jnp.dot(p, va_ref[pl.ds(k0, kc), :], preferred_element_type=F32))
                    m_ref[mi] = m_new
                return carry

            lax.fori_loop(0, S // kc, chunk, 0)
            finish(t, acc_ref[0], acc_ref[1])


def _attention(proj, rel_bias, g_q, g_k, lam_params, g_subln, *, B, S, H, hd, vd, off_k, off_v, lam_init):
    T = B * S
    tq = min(512, S)
    kc = min(512, S)
    n_sub = math.gcd(4, S // tq)
    tqb = n_sub * tq
    nq = S // tqb
    n_buckets = rel_bias.shape[0]
    hw = 2 * hd
    bucket = jnp.asarray(_bucket_table(S, n_buckets, np.arange(2 * S) - S))
    bucket_t = jnp.asarray(_bucket_table(S, n_buckets, S - np.arange(2 * S)))
    body = functools.partial(_attn_body, S=S, tq=tq, n_sub=n_sub, kc=kc, hd=hd, vd=vd, lam_init=lam_init,
                             n_buckets=n_buckets)
    kblk = off_k // hw
    vblk = off_v // vd
    seg_q = np.kron(np.eye(2, dtype=np.float32), np.ones((hd, hd), np.float32))
    return pl.pallas_call(
        body,
        grid=(H, B, nq),
        in_specs=[pl.BlockSpec((1, 2 * S), lambda h, b, i: (0, 0)),
                  pl.BlockSpec((1, 2 * S), lambda h, b, i: (0, 0)),
                  pl.BlockSpec(memory_space=pltpu.SMEM),
                  pl.BlockSpec((tqb, hw), lambda h, b, i: (b * nq + i, h)),
                  pl.BlockSpec((S, hw), lambda h, b, i: (b, kblk + h)),
                  pl.BlockSpec((S, vd), lambda h, b, i: (b, vblk + h)),
                  pl.BlockSpec((1, hw), lambda h, b, i: (0, 0)),
                  pl.BlockSpec((1, hw), lambda h, b, i: (0, 0)),
                  pl.BlockSpec((4, hd), lambda h, b, i: (0, 0)),
                  pl.BlockSpec((1, vd), lambda h, b, i: (0, 0)),
                  pl.BlockSpec((hw, hw), lambda h, b, i: (0, 0)),
                  pl.BlockSpec((vd, vd), lambda h, b, i: (0, 0))],
        out_specs=pl.BlockSpec((tqb, vd), lambda h, b, i: (b * nq + i, h)),
        out_shape=jax.ShapeDtypeStruct((T, H * vd), BF16),
        scratch_shapes=[pltpu.VMEM((tq, 2 * S), F32),
                        pltpu.VMEM((tq, 2 * S), F32),
                        pltpu.SMEM((1,), jnp.int32),
                        pltpu.VMEM((S, hw), BF16),
                        pltpu.VMEM((S, 2 * vd), BF16),
                        pltpu.VMEM((vd + 16, S), BF16),
                        pltpu.VMEM((2, tq, 1), F32),
                        pltpu.VMEM((2, tq, 2 * vd), F32)],
        compiler_params=_params("arbitrary", "arbitrary", "arbitrary"),
        name="diff_attention",
    )(bucket, bucket_t, rel_bias.T, proj, proj, proj, jnp.tile(g_q, 2)[None], jnp.tile(g_k, 2)[None],
      lam_params, g_subln[None], jnp.asarray(seg_q, BF16), jnp.ones((vd, vd), BF16))


LRU_SEGMENTS = SUBLANES
LRU_PITCH_PAD = 8


def _sublane_scan(a, bb, reverse):
    ri = lax.broadcasted_iota(jnp.int32, a.shape, 0)
    for d in (1, 2, 4):
        if reverse:
            keep, sh = ri < SUBLANES - d, SUBLANES - d
        else:
            keep, sh = ri >= d, d
        bb = bb + a * jnp.where(keep, pltpu.roll(bb, sh, 0), 0.0)
        a = a * jnp.where(keep, pltpu.roll(a, sh, 0), 1.0)
    return bb


def _lru_body(x_ref, y_ref, cw_ref, cb_ref, w_ref, bias_ref, lam_ref, o_ref,
              af_ref, bf_ref, ab_ref, bb_ref, hf_ref, pf_ref, hb_ref, pb_ref, hs_ref, *, S, W):
    L = S // LRU_SEGMENTS
    pitch = L + LRU_PITCH_PAD
    x = x_ref[...].astype(F32)
    row = lax.broadcasted_iota(jnp.int32, (S, W), 0)
    cw = cw_ref[...]
    xc = (cw[0:1] * jnp.where(row >= 2, pltpu.roll(x, 2, 0), 0.0)
          + cw[1:2] * jnp.where(row >= 1, pltpu.roll(x, 1, 0), 0.0)
          + cw[2:3] * x
          + cw[3:4] * jnp.where(row < S - 1, pltpu.roll(x, S - 1, 0), 0.0)
          + cb_ref[...])
    gates = jnp.dot(xc.astype(BF16), w_ref[0], preferred_element_type=F32) + bias_ref[0]
    lam = lam_ref[...]
    for d, (a_ref, b_ref) in enumerate(((af_ref, bf_ref), (ab_ref, bb_ref))):
        r = jax.nn.sigmoid(gates[:, (2 * d) * W:(2 * d + 1) * W])
        ig = jax.nn.sigmoid(gates[:, (2 * d + 1) * W:(2 * d + 2) * W])
        nl = -lam[d:d + 1]
        softplus = jnp.maximum(nl, 0.0) + jnp.log1p(jnp.exp(-jnp.abs(nl)))
        a = jnp.exp2((-LRU_C * LOG2E * softplus) * r)
        y2 = 1.0 - a * a
        bvals = jnp.where(y2 > 0.0, y2 * lax.rsqrt(y2), 0.0) * ig * xc
        for sg in range(LRU_SEGMENTS):
            a_ref[sg * pitch:sg * pitch + L, :] = a[sg * L:(sg + 1) * L]
            b_ref[sg * pitch:sg * pitch + L, :] = bvals[sg * L:(sg + 1) * L]

    def seg_rows(ref, j):
        return ref[pl.ds(j, LRU_SEGMENTS, stride=pitch), :]

    def step(j, carry):
        hf, pf, hb, pb = carry
        a = seg_rows(af_ref, j)
        hf = a * hf + seg_rows(bf_ref, j)
        pf = a * pf
        hf_ref[j] = hf
        pf_ref[j] = pf
        jb = L - 1 - j
        a = seg_rows(ab_ref, jb)
        hb = a * hb + seg_rows(bb_ref, jb)
        pb = a * pb
        hb_ref[jb] = hb
        pb_ref[jb] = pb
        return hf, pf, hb, pb

    zero = jnp.zeros((LRU_SEGMENTS, W), F32)
    one = jnp.ones((LRU_SEGMENTS, W), F32)
    hf, pf, hb, pb = lax.fori_loop(0, L, step, (zero, one, zero, one), unroll=8)

    si = lax.broadcasted_iota(jnp.int32, (LRU_SEGMENTS, W), 0)
    cf = jnp.where(si >= 1, pltpu.roll(_sublane_scan(pf, hf, False), 1, 0), 0.0)
    cb = jnp.where(si < LRU_SEGMENTS - 1,
                   pltpu.roll(_sublane_scan(pb, hb, True), LRU_SEGMENTS - 1, 0), 0.0)

    def fix(j, carry):
        hs_ref[pl.ds(j, LRU_SEGMENTS, stride=pitch), :] = (
            hf_ref[j] + pf_ref[j] * cf + hb_ref[j] + pb_ref[j] * cb)
        return carry

    lax.fori_loop(0, L, fix, 0, unroll=8)

    for sg in range(LRU_SEGMENTS):
        y = y_ref[sg * L:(sg + 1) * L, :].astype(F32)
        o_ref[sg * L:(sg + 1) * L, :] = (hs_ref[sg * pitch:sg * pitch + L, :]
                                         * jax.nn.gelu(y)).astype(o_ref.dtype)


def _rg_lru(proj, conv_w, conv_b, w_cat, b_cat, lru_lambda, *, B, S, off_x, off_y):
    T = B * S
    NB, W, _ = w_cat.shape
    xblk = off_x // W
    yblk = off_y // W
    body = functools.partial(_lru_body, S=S, W=W)
    padded = LRU_SEGMENTS * (S // LRU_SEGMENTS + LRU_PITCH_PAD)
    return pl.pallas_call(
        body,
        grid=(B, NB),
        in_specs=[pl.BlockSpec((S, W), lambda b, n: (b, xblk + n)),
                  pl.BlockSpec((S, W), lambda b, n: (b, yblk + n)),
                  pl.BlockSpec((conv_w.shape[0], W), lambda b, n: (0, n)),
                  pl.BlockSpec((1, W), lambda b, n: (0, n)),
                  pl.BlockSpec((1, W, 4 * W), lambda b, n: (n, 0, 0)),
                  pl.BlockSpec((1, 1, 4 * W), lambda b, n: (n, 0, 0)),
                  pl.BlockSpec((2, W), lambda b, n: (0, n))],
        out_specs=pl.BlockSpec((S, W), lambda b, n: (b, n)),
        out_shape=jax.ShapeDtypeStruct((T, NB * W), BF16),
        scratch_shapes=([pltpu.VMEM((padded, W), F32)] * 4
                        + [pltpu.VMEM((S // LRU_SEGMENTS, LRU_SEGMENTS, W), F32)] * 4
                        + [pltpu.VMEM((padded, W), F32)]),
        compiler_params=_params("parallel", "parallel"),
        name="rg_lru",
    )(proj, proj, conv_w, conv_b[None], w_cat, b_cat, lru_lambda)


def _merge_body(o_ref, r_ref, ga_ref, gr_ref, x_ref, wpa_ref, wpl_ref, wo_ref, g_ref, wr_ref,
                h_ref, hn_ref, aff_ref, *, E, n_sub):
    ts = o_ref.shape[0] // n_sub
    for t in range(n_sub):
        rows = slice(t * ts, (t + 1) * ts)
        ba = jnp.dot(o_ref[rows, :], wpa_ref[...], preferred_element_type=F32)
        br = jnp.dot(r_ref[rows, :], wpl_ref[...], preferred_element_type=F32)
        mixed = (jax.nn.sigmoid(ga_ref[rows, :].astype(F32)) * ba
                 + jax.nn.sigmoid(gr_ref[rows, :].astype(F32)) * br)
        h = x_ref[rows, :] + jnp.dot(mixed.astype(BF16), wo_ref[...], preferred_element_type=F32)
        h_ref[rows, :] = h
        hn = h * lax.rsqrt(jnp.mean(h * h, axis=-1, keepdims=True) + EPS) * g_ref[...]
        hn_hi = hn.astype(BF16)
        hn_ref[rows, :] = hn_hi
        hn_lo = (hn - hn_hi.astype(F32)).astype(BF16)
        lg = (jnp.dot(hn_hi, wr_ref[...], preferred_element_type=F32)
              + jnp.dot(hn_lo, wr_ref[...], preferred_element_type=F32))
        logits = lg[:, :E] + lg[:, E:2 * E]
        logits = logits - jnp.max(logits, axis=-1, keepdims=True)
        ex = jnp.exp(logits)
        aff_ref[rows, :] = ex / jnp.sum(ex, axis=-1, keepdims=True)


def _merge(o_attn, lru_out, proj, x2, wpa, wpl, wo, g_ffn, wr2, *, off_ga, off_gr, E):
    T, D = x2.shape
    tm = min(1024, T)
    n_sub = 2 if tm % 1024 == 0 else 1
    resident = dict(pipeline_mode=pl.Buffered(1))
    gab = off_ga // D
    grb = off_gr // D
    row = lambda i: (i, 0)
    const = lambda i: (0, 0)
    return pl.pallas_call(
        functools.partial(_merge_body, E=E, n_sub=n_sub),
        grid=(T // tm,),
        in_specs=[pl.BlockSpec((tm, D), row), pl.BlockSpec((tm, D), row),
                  pl.BlockSpec((tm, D), lambda i: (i, gab)), pl.BlockSpec((tm, D), lambda i: (i, grb)),
                  pl.BlockSpec((tm, D), row),
                  pl.BlockSpec((D, D), const, **resident), pl.BlockSpec((D, D), const, **resident),
                  pl.BlockSpec((D, D), const, **resident),
                  pl.BlockSpec((1, D), const), pl.BlockSpec((D, 2 * E), const)],
        out_specs=[pl.BlockSpec((tm, D), row), pl.BlockSpec((tm, D), row), pl.BlockSpec((tm, E), row)],
        out_shape=[jax.ShapeDtypeStruct((T, D), F32), jax.ShapeDtypeStruct((T, D), BF16),
                   jax.ShapeDtypeStruct((T, E), F32)],
        compiler_params=_params("parallel"),
        name="merge_router",
    )(o_attn, lru_out, proj, proj, x2, wpa, wpl, wo, g_ffn, wr2)


def _select_body(aff_ref, pos_ref, *, E, S, C):
    bits = pltpu.bitcast(aff_ref[0], jnp.int32)
    t = jnp.zeros((E, 1), jnp.int32)
    for bit in range(30, -1, -1):
        cand = t | (1 << bit)
        cnt = jnp.sum(jnp.where(bits >= cand, 1.0, 0.0), axis=-1, keepdims=True)
        t = jnp.where(cnt >= C, cand, t)
    gt = bits > t
    eq = bits == t
    need = C - jnp.sum(jnp.where(gt, 1.0, 0.0), axis=-1, keepdims=True).astype(jnp.int32)
    packed = jnp.where(gt, 1 << 16, 0) + jnp.where(eq, 1, 0)
    lane = lax.broadcasted_iota(jnp.int32, (E, S), 1)
    incl = packed
    d = 1
    while d < S:
        incl = incl + jnp.where(lane >= d, pltpu.roll(incl, d, 1), 0)
        d *= 2
    excl = incl - packed
    n_gt = excl >> 16
    n_eq = excl & 0xFFFF
    sel = gt | (eq & (n_eq < need))
    pos_ref[0] = jnp.where(sel, n_gt + jnp.minimum(n_eq, need), -1)


def _select(aff_t, C):
    B, E, S = aff_t.shape
    return pl.pallas_call(
        functools.partial(_select_body, E=E, S=S, C=C),
        grid=(B,),
        in_specs=[pl.BlockSpec((1, E, S), lambda b: (b, 0, 0))],
        out_specs=pl.BlockSpec((1, E, S), lambda b: (b, 0, 0)),
        out_shape=jax.ShapeDtypeStruct((B, E, S), jnp.int32),
        compiler_params=_params("parallel"),
        name="topc_select",
    )(aff_t)


def _ffn_body(pos_ref, hn_ref, wg_hbm, wu_hbm, wd_hbm, y_ref,
              wg_s, wu_s, wd_s, stg_g, stg_u, stg_d, sems, *, C, S, fc):
    e = pl.program_id(0)
    b = pl.program_id(1)
    n_experts = pl.num_programs(0)
    rows_in = stg_g.shape[0]
    rows_dn = stg_d.shape[0]
    n_chunks = wg_s.shape[1] // rows_in

    def chunk_copies(ee, k):
        r_in = pl.multiple_of(k * rows_in, rows_in)
        r_dn = pl.multiple_of(k * rows_dn, rows_dn)
        return (pltpu.make_async_copy(wg_hbm.at[ee, pl.ds(r_in, rows_in), :], stg_g, sems.at[0]),
                pltpu.make_async_copy(wu_hbm.at[ee, pl.ds(r_in, rows_in), :], stg_u, sems.at[1]),
                pltpu.make_async_copy(wd_hbm.at[ee, pl.ds(r_dn, rows_dn), :], stg_d, sems.at[2]))

    def cast_chunk(slot_, k):
        r_in = pl.multiple_of(k * rows_in, rows_in)
        r_dn = pl.multiple_of(k * rows_dn, rows_dn)
        wg_s[slot_, pl.ds(r_in, rows_in), :] = stg_g[...].astype(BF16)
        wu_s[slot_, pl.ds(r_in, rows_in), :] = stg_u[...].astype(BF16)
        wd_s[slot_, pl.ds(r_dn, rows_dn), :] = stg_d[...].astype(BF16)

    @pl.when((e == 0) & (b == 0))
    def _():
        def stage(k, carry):
            for cp in chunk_copies(0, k):
                cp.start()
            for cp in chunk_copies(0, k):
                cp.wait()
            cast_chunk(0, k)
            return carry

        lax.fori_loop(0, n_chunks, stage, 0)

        @pl.when(n_experts > 1)
        def _():
            for cp in chunk_copies(1, 0):
                cp.start()

    cur = e % 2
    slot = lax.broadcasted_iota(jnp.int32, (C, S), 0)
    onehot = jnp.where(pos_ref[0, 0] == slot, 1.0, 0.0).astype(BF16)
    xg = jnp.dot(onehot, hn_ref[0], preferred_element_type=F32).astype(BF16)
    F = wg_s.shape[2]
    y = jnp.zeros(y_ref.shape[2:], F32)
    for f0 in range(0, F, fc):
        g = jnp.dot(xg, wg_s[cur, :, f0:f0 + fc], preferred_element_type=F32)
        u = jnp.dot(xg, wu_s[cur, :, f0:f0 + fc], preferred_element_type=F32)
        hid = (g * jax.nn.sigmoid(g) * u).astype(BF16)
        y = y + jnp.dot(hid, wd_s[cur, f0:f0 + fc, :], preferred_element_type=F32)
    y_ref[0, 0] = y.astype(y_ref.dtype)

    @pl.when(e + 1 < n_experts)
    def _():
        for cp in chunk_copies(e + 1, b):
            cp.wait()
        cast_chunk(1 - cur, b)

    wrap = b + 1 == n_chunks
    e_nxt = jnp.where(wrap, e + 2, e + 1)
    k_nxt = jnp.where(wrap, 0, b + 1)

    @pl.when(e_nxt < n_experts)
    def _():
        for cp in chunk_copies(e_nxt, k_nxt):
            cp.start()


def _expert_ffn(pos_rows, hn3, wg, wu, wd, C):
    B, S, D = hn3.shape
    E, _, F = wg.shape
    fc = min(512, F)
    assert D % B == 0 and F % B == 0 and (D // B) % 16 == 0 and (F // B) % 16 == 0
    hbm = pl.BlockSpec(memory_space=pl.ANY)
    return pl.pallas_call(
        functools.partial(_ffn_body, C=C, S=S, fc=fc),
        grid=(E, B),
        in_specs=[pl.BlockSpec((1, 1, 1, S), lambda e, b: (b, e, 0, 0)),
                  pl.BlockSpec((1, S, D), lambda e, b: (b, 0, 0)),
                  hbm, hbm, hbm],
        out_specs=pl.BlockSpec((1, 1, C, D), lambda e, b: (b, e, 0, 0)),
        out_shape=jax.ShapeDtypeStruct((B, E, C, D), BF16),
        scratch_shapes=[pltpu.VMEM((2, D, F), BF16), pltpu.VMEM((2, D, F), BF16),
                        pltpu.VMEM((2, F, D), BF16),
                        pltpu.VMEM((D // B, F), F32), pltpu.VMEM((D // B, F), F32),
                        pltpu.VMEM((F // B, D), F32),
                        pltpu.SemaphoreType.DMA((3,))],
        compiler_params=_params("arbitrary", "arbitrary"),
        name="expert_ffn",
    )(pos_rows, hn3, wg, wu, wd)


def _combine_body(pos_ref, aff_ref, y_ref, h_ref, o_ref, *, E, C):
    pc = pos_ref[0]
    ac = aff_ref[0]
    ts = pc.shape[0]
    slot = lax.broadcasted_iota(jnp.int32, (ts, C), 1)
    pieces = [jnp.where(pc[:, e:e + 1] == slot, ac[:, e:e + 1], 0.0).astype(BF16) for e in range(E)]
    scatter = jnp.concatenate(pieces, axis=1)
    y = y_ref[0].reshape(E * C, y_ref.shape[3])
    o_ref[0] = h_ref[0] + jnp.dot(scatter, y, preferred_element_type=F32)


def _combine(pos_cols, aff_cols, y, h3):
    B, S, D = h3.shape
    _, E, C, _ = y.shape
    ts = min(512, S)
    return pl.pallas_call(
        functools.partial(_combine_body, E=E, C=C),
        grid=(B, S // ts),
        in_specs=[pl.BlockSpec((1, ts, E), lambda b, i: (b, i, 0)),
                  pl.BlockSpec((1, ts, E), lambda b, i: (b, i, 0)),
                  pl.BlockSpec((1, E, C, D), lambda b, i: (b, 0, 0, 0)),
                  pl.BlockSpec((1, ts, D), lambda b, i: (b, i, 0))],
        out_specs=pl.BlockSpec((1, ts, D), lambda b, i: (b, i, 0)),
        out_shape=jax.ShapeDtypeStruct((B, S, D), F32),
        compiler_params=_params("parallel", "arbitrary"),
        name="moe_combine",
    )(pos_cols, aff_cols, y, h3)


def kernel(x, g_mix, w_in, g_q, g_k, lam_q1, lam_k1, lam_q2, lam_k2, g_subln, rel_bias, conv_w, conv_b,
           gate_r_w, gate_r_b, gate_i_w, gate_i_b, lru_lambda, w_proj_attn, w_proj_lru, w_out, g_ffn,
           w_router, w_gate_e, w_up_e, w_down_e):
    B, S, D = x.shape
    depth = w_in.shape[0]
    H = rel_bias.shape[1]
    hd = g_q.shape[-1]
    vd = g_subln.shape[-1]
    qk_w = H * 2 * hd
    attn_w = H * vd
    lru_w = conv_w.shape[-1]
    NB, LB = gate_r_w.shape[2], gate_r_w.shape[3]
    E = w_router.shape[-1]
    C = EC_CAPACITY_FACTOR * S // E
    off_k = qk_w
    off_v = off_k + qk_w
    off_x = off_v + attn_w
    off_y = off_x + lru_w
    off_ga = off_y + lru_w
    off_gr = off_ga + D

    h2 = x.reshape(B * S, D)
    for layer in range(depth):
        lam_init = 0.8 - 0.6 * math.exp(-0.3 * layer)
        proj = _inproj(h2, g_mix[layer][None], w_in[layer].astype(BF16))

        lam_params = jnp.stack([lam_q1[layer], lam_k1[layer], lam_q2[layer], lam_k2[layer]])
        o_attn = _attention(proj, rel_bias, g_q[layer], g_k[layer], lam_params, g_subln[layer],
                            B=B, S=S, H=H, hd=hd, vd=vd, off_k=off_k, off_v=off_v, lam_init=lam_init)

        w_cat = jnp.concatenate([gate_r_w[layer, 0], gate_i_w[layer, 0],
                                 gate_r_w[layer, 1], gate_i_w[layer, 1]], axis=-1).astype(BF16)
        b_cat = jnp.stack([gate_r_b[layer, 0].reshape(NB, LB), gate_i_b[layer, 0].reshape(NB, LB),
                           gate_r_b[layer, 1].reshape(NB, LB), gate_i_b[layer, 1].reshape(NB, LB)],
                          axis=1).reshape(NB, 1, 4 * LB)
        lru_out = _rg_lru(proj, conv_w[layer], conv_b[layer], w_cat, b_cat, lru_lambda[layer],
                          B=B, S=S, off_x=off_x, off_y=off_y)

        wr = w_router[layer]
        wr_hi = wr.astype(BF16)
        wr2 = jnp.concatenate([wr_hi, (wr - wr_hi.astype(F32)).astype(BF16)], axis=1)
        h2, hn, aff = _merge(o_attn, lru_out, proj, h2, w_proj_attn[layer].astype(BF16),
                             w_proj_lru[layer].astype(BF16), w_out[layer].astype(BF16),
                             g_ffn[layer][None], wr2, off_ga=off_ga, off_gr=off_gr, E=E)

        aff_cols = aff.reshape(B, S, E)
        pos_rows = _select(aff_cols.transpose(0, 2, 1), C)
        y = _expert_ffn(pos_rows.reshape(B, E, 1, S), hn.reshape(B, S, D), w_gate_e[layer],
                        w_up_e[layer], w_down_e[layer], C)
        h3 = _combine(pos_rows.transpose(0, 2, 1), aff_cols, y, h2.reshape(B, S, D))
        h2 = h3.reshape(B * S, D)
    return h2.reshape(B, S, D)
```

```python
import functools
import math

import jax
import jax.numpy as jnp
import numpy as np
from jax import lax
from jax.experimental import pallas as pl
from jax.experimental.pallas import tpu as pltpu

F32 = jnp.float32
BF16 = jnp.bfloat16
EPS = 1e-6
LANES = 128
SUBLANES = 8
VMEM_LIMIT = 56 * 1024 * 1024
LOG2E = 1.4426950408889634
NEG_BIG = -1e30
REL_MAX_DIST = 128
LRU_C = 8.0
EC_CAPACITY_FACTOR = 2


def _params(*sem):
    return pltpu.CompilerParams(dimension_semantics=sem, vmem_limit_bytes=VMEM_LIMIT)


def _inproj_body(x_ref, g_ref, w_ref, o_ref, xn_ref):
    @pl.when(pl.program_id(1) == 0)
    def _():
        x = x_ref[...]
        ms = jnp.mean(x * x, axis=-1, keepdims=True)
        xn_ref[...] = (x * lax.rsqrt(ms + EPS) * g_ref[...]).astype(BF16)

    o_ref[...] = jnp.dot(xn_ref[...], w_ref[...], preferred_element_type=F32).astype(o_ref.dtype)


def _inproj(x2, g, w_bf):
    T, D = x2.shape
    N = w_bf.shape[1]
    tm = min(1024, T)
    tn = min(1024, N)
    return pl.pallas_call(
        _inproj_body,
        grid=(T // tm, N // tn),
        in_specs=[pl.BlockSpec((tm, D), lambda i, j: (i, 0)),
                  pl.BlockSpec((1, D), lambda i, j: (0, 0)),
                  pl.BlockSpec((D, tn), lambda i, j: (0, j))],
        out_specs=pl.BlockSpec((tm, tn), lambda i, j: (i, j)),
        out_shape=jax.ShapeDtypeStruct((T, N), BF16),
        scratch_shapes=[pltpu.VMEM((tm, D), BF16)],
        compiler_params=_params("parallel", "arbitrary"),
        name="in_proj",
    )(x2, g, w_bf)


def _bucket_table(S, n_buckets, rel):
    half = n_buckets // 2
    max_exact = half // 2
    ret = np.where(rel > 0, half, 0)
    n = np.abs(rel)
    nf = np.maximum(n, max_exact).astype(np.float64)
    large = max_exact + (np.log(nf / max_exact) / math.log(REL_MAX_DIST / max_exact)
                         * (half - max_exact)).astype(np.int32)
    large = np.minimum(large, half - 1)
    return (ret + np.where(n < max_exact, n, large)).astype(np.int32).reshape(1, 2 * S)


def _seg_sumsq(x, seg_ones):
    x2 = x * x
    hi = x2.astype(BF16)
    lo = (x2 - hi.astype(F32)).astype(BF16)
    return (jnp.dot(hi, seg_ones, preferred_element_type=F32)
            + jnp.dot(lo, seg_ones, preferred_element_type=F32))


NORM_SLACK = 1.01
SAFE_LOG2_SPAN = 100.0


def _attn_body(bucket_ref, bucket_t_ref, relb_ref, q_ref, k_ref, v_ref, gq_ref, gk_ref, lamp_ref, gs_ref,
               segq_ref, segv_ref, o_ref, e_ref, et_ref, fast_ref, kn_ref, va_ref, vat_ref, m_ref, acc_ref,
               *, S, tq, n_sub, kc, hd, vd, lam_init, n_buckets):
    h = pl.program_id(0)
    b = pl.program_id(1)
    i = pl.program_id(2)
    q_scale = hd ** -0.5 * LOG2E

    @pl.when((b == 0) & (i == 0))
    def _():
        def bias_row(bk):
            tab = jnp.zeros((1, 2 * S), F32)
            for n in range(n_buckets):
                tab = jnp.where(bk == n, relb_ref[h, n], tab)
            return tab * LOG2E

        tab = bias_row(bucket_ref[...])
        bmax = jnp.max(tab, axis=-1, keepdims=True)
        bmin = jnp.min(tab, axis=-1, keepdims=True)
        bound = (NORM_SLACK * hd * q_scale) * (jnp.max(jnp.abs(gq_ref[...]), axis=-1, keepdims=True)
                                               * jnp.max(jnp.abs(gk_ref[...]), axis=-1, keepdims=True))
        span = 2.0 * bound + (bmax - bmin)
        fast_ref[0] = (span[0, 0] <= SAFE_LOG2_SPAN).astype(jnp.int32)
        shift = bound + bmax
        e_ref[...] = pltpu.roll(jnp.broadcast_to(tab - shift, (tq, 2 * S)), 0, 1, stride=1, stride_axis=0)
        tab_t = bias_row(bucket_t_ref[...])
        et_ref[...] = pltpu.roll(jnp.broadcast_to(tab_t - shift, (tq, 2 * S)), 0, 1, stride=1, stride_axis=0)

    @pl.when(i == 0)
    def _():
        k = k_ref[...].astype(F32)
        ms = _seg_sumsq(k, segq_ref[...]) * (1.0 / hd)
        kn_ref[...] = (k * lax.rsqrt(ms + EPS) * gk_ref[...]).astype(BF16)
        va_ref[:, :vd] = v_ref[...]
        va_ref[:, vd:] = jnp.ones((S, vd), BF16)
        vat_ref[:vd, :] = v_ref[...].astype(F32).T.astype(BF16)
        vat_ref[vd:, :] = jnp.ones((vat_ref.shape[0] - vd, S), BF16)

    nt_dims = (((1,), (1,)), ((), ()))
    lp = lamp_ref[...]
    lam = (jnp.exp(jnp.sum(lp[0:1] * lp[1:2], axis=-1, keepdims=True))
           - jnp.exp(jnp.sum(lp[2:3] * lp[3:4], axis=-1, keepdims=True)) + lam_init)

    def q_maps_of(t):
        q = q_ref[t * tq:(t + 1) * tq, :].astype(F32)
        ms = _seg_sumsq(q, segq_ref[...]) * (1.0 / hd)
        qn = q * lax.rsqrt(ms + EPS) * (gq_ref[...] * q_scale)
        lo = lax.broadcasted_iota(jnp.int32, qn.shape, 1) < hd
        return jnp.where(lo, qn, 0.0).astype(BF16), jnp.where(lo, 0.0, qn).astype(BF16)

    def e_start(t, k0):
        return pl.multiple_of(S + k0 - (i * n_sub + t) * tq, LANES)

    def finish(t, a1, a2):
        o = a1[:, :vd] / a1[:, vd:] - lam * (a2[:, :vd] / a2[:, vd:])
        o = o * lax.rsqrt(_seg_sumsq(o, segv_ref[...]) * (1.0 / vd) + EPS)
        o_ref[t * tq:(t + 1) * tq, :] = (o * (gs_ref[...] * (1.0 - lam_init))).astype(o_ref.dtype)

    def finish_t(t, a1, a2):
        ot = a1[:vd] / a1[vd:vd + 1] - lam * (a2[:vd] / a2[vd:vd + 1])
        o = ot.T
        o = o * lax.rsqrt(_seg_sumsq(o, segv_ref[...]) * (1.0 / vd) + EPS)
        o_ref[t * tq:(t + 1) * tq, :] = (o * (gs_ref[...] * (1.0 - lam_init))).astype(o_ref.dtype)

    @pl.when(fast_ref[0] == 1)
    def _():
        for t in range(n_sub):
            q_maps = q_maps_of(t)
            tile = i * n_sub + t
            accs = []
            for mi in range(2):
                chunks = []
                for c in range(S // tq):
                    st = lax.dot_general(kn_ref[c * tq:(c + 1) * tq, :], q_maps[mi], nt_dims,
                                         preferred_element_type=F32)
                    start = pl.multiple_of(S + (tile - c) * tq, LANES)
                    chunks.append(jnp.exp2(st + et_ref[:, pl.ds(start, tq)]).astype(BF16))
                accs.append(jnp.dot(vat_ref[...], jnp.concatenate(chunks, axis=0),
                                    preferred_element_type=F32))
            finish_t(t, *accs)

    @pl.when(fast_ref[0] != 1)
    def _():
        for t in range(n_sub):
            q_maps = q_maps_of(t)
            m_ref[...] = jnp.full(m_ref.shape, NEG_BIG, F32)
            acc_ref[...] = jnp.zeros(acc_ref.shape, F32)

            def chunk(c, carry):
                k0 = pl.multiple_of(c * kc, kc)
                bias = e_ref[:, pl.ds(e_start(t, k0), kc)]
                for mi in range(2):
                    s = lax.dot_general(q_maps[mi], kn_ref[pl.ds(k0, kc), :], nt_dims,
                                        preferred_element_type=F32) + bias
                    m_old = m_ref[mi]
                    m_new = jnp.maximum(m_old, jnp.max(s, axis=-1, keepdims=True))
                    p = jnp.exp2(s - m_new).astype(BF16)
                    acc_ref[mi] = (jnp.exp2(m_old - m_new) * acc_ref[mi]
                                   + jnp.dot(p, va_ref[pl.ds(k0, kc), :], preferred_element_type=F32))
                    m_ref[mi] = m_new
                return carry

            lax.fori_loop(0, S // kc, chunk, 0)
            finish(t, acc_ref[0], acc_ref[1])


def _attention(proj, rel_bias, g_q, g_k, lam_params, g_subln, *, B, S, H, hd, vd, off_k, off_v, lam_init):
    T = B * S
    tq = min(512, S)
    kc = min(512, S)
    n_sub = math.gcd(4, S // tq)
    tqb = n_sub * tq
    nq = S // tqb
    n_buckets = rel_bias.shape[0]
    hw = 2 * hd
    bucket = jnp.asarray(_bucket_table(S, n_buckets, np.arange(2 * S) - S))
    bucket_t = jnp.asarray(_bucket_table(S, n_buckets, S - np.arange(2 * S)))
    body = functools.partial(_attn_body, S=S, tq=tq, n_sub=n_sub, kc=kc, hd=hd, vd=vd, lam_init=lam_init,
                             n_buckets=n_buckets)
    kblk = off_k // hw
    vblk = off_v // vd
    seg_q = np.kron(np.eye(2, dtype=np.float32), np.ones((hd, hd), np.float32))
    return pl.pallas_call(
        body,
        grid=(H, B, nq),
        in_specs=[pl.BlockSpec((1, 2 * S), lambda h, b, i: (0, 0)),
                  pl.BlockSpec((1, 2 * S), lambda h, b, i: (0, 0)),
                  pl.BlockSpec(memory_space=pltpu.SMEM),
                  pl.BlockSpec((tqb, hw), lambda h, b, i: (b * nq + i, h)),
                  pl.BlockSpec((S, hw), lambda h, b, i: (b, kblk + h)),
                  pl.BlockSpec((S, vd), lambda h, b, i: (b, vblk + h)),
                  pl.BlockSpec((1, hw), lambda h, b, i: (0, 0)),
                  pl.BlockSpec((1, hw), lambda h, b, i: (0, 0)),
                  pl.BlockSpec((4, hd), lambda h, b, i: (0, 0)),
                  pl.BlockSpec((1, vd), lambda h, b, i: (0, 0)),
                  pl.BlockSpec((hw, hw), lambda h, b, i: (0, 0)),
                  pl.BlockSpec((vd, vd), lambda h, b, i: (0, 0))],
        out_specs=pl.BlockSpec((tqb, vd), lambda h, b, i: (b * nq + i, h)),
        out_shape=jax.ShapeDtypeStruct((T, H * vd), BF16),
        scratch_shapes=[pltpu.VMEM((tq, 2 * S), F32),
                        pltpu.VMEM((tq, 2 * S), F32),
                        pltpu.SMEM((1,), jnp.int32),
                        pltpu.VMEM((S, hw), BF16),
                        pltpu.VMEM((S, 2 * vd), BF16),
                        pltpu.VMEM((vd + 16, S), BF16),
                        pltpu.VMEM((2, tq, 1), F32),
                        pltpu.VMEM((2, tq, 2 * vd), F32)],
        compiler_params=_params("arbitrary", "arbitrary", "arbitrary"),
        name="diff_attention",
    )(bucket, bucket_t, rel_bias.T, proj, proj, proj, jnp.tile(g_q, 2)[None], jnp.tile(g_k, 2)[None],
      lam_params, g_subln[None], jnp.asarray(seg_q, BF16), jnp.ones((vd, vd), BF16))


LRU_SEGMENTS = SUBLANES
LRU_PITCH_PAD = 8


def _sublane_scan(a, bb, reverse):
    ri = lax.broadcasted_iota(jnp.int32, a.shape, 0)
    for d in (1, 2, 4):
        if reverse:
            keep, sh = ri < SUBLANES - d, SUBLANES - d
        else:
            keep, sh = ri >= d, d
        bb = bb + a * jnp.where(keep, pltpu.roll(bb, sh, 0), 0.0)
        a = a * jnp.where(keep, pltpu.roll(a, sh, 0), 1.0)
    return bb


def _lru_body(x_ref, y_ref, cw_ref, cb_ref, w_ref, bias_ref, lam_ref, o_ref,
              af_ref, bf_ref, ab_ref, bb_ref, hf_ref, pf_ref, hb_ref, pb_ref, hs_ref, *, S, W):
    L = S // LRU_SEGMENTS
    pitch = L + LRU_PITCH_PAD
    x = x_ref[...].astype(F32)
    row = lax.broadcasted_iota(jnp.int32, (S, W), 0)
    cw = cw_ref[...]
    xc = (cw[0:1] * jnp.where(row >= 2, pltpu.roll(x, 2, 0), 0.0)
          + cw[1:2] * jnp.where(row >= 1, pltpu.roll(x, 1, 0), 0.0)
          + cw[2:3] * x
          + cw[3:4] * jnp.where(row < S - 1, pltpu.roll(x, S - 1, 0), 0.0)
          + cb_ref[...])
    gates = jnp.dot(xc.astype(BF16), w_ref[0], preferred_element_type=F32) + bias_ref[0]
    lam = lam_ref[...]
    for d, (a_ref, b_ref) in enumerate(((af_ref, bf_ref), (ab_ref, bb_ref))):
        r = jax.nn.sigmoid(gates[:, (2 * d) * W:(2 * d + 1) * W])
        ig = jax.nn.sigmoid(gates[:, (2 * d + 1) * W:(2 * d + 2) * W])
        nl = -lam[d:d + 1]
        softplus = jnp.maximum(nl, 0.0) + jnp.log1p(jnp.exp(-jnp.abs(nl)))
        a = jnp.exp2((-LRU_C * LOG2E * softplus) * r)
        y2 = 1.0 - a * a
        bvals = jnp.where(y2 > 0.0, y2 * lax.rsqrt(y2), 0.0) * ig * xc
        for sg in range(LRU_SEGMENTS):
            a_ref[sg * pitch:sg * pitch + L, :] = a[sg * L:(sg + 1) * L]
            b_ref[sg * pitch:sg * pitch + L, :] = bvals[sg * L:(sg + 1) * L]

    def seg_rows(ref, j):
        return ref[pl.ds(j, LRU_SEGMENTS, stride=pitch), :]

    def step(j, carry):
        hf, pf, hb, pb = carry
        a = seg_rows(af_ref, j)
        hf = a * hf + seg_rows(bf_ref, j)
        pf = a * pf
        hf_ref[j] = hf
        pf_ref[j] = pf
        jb = L - 1 - j
        a = seg_rows(ab_ref, jb)
        hb = a * hb + seg_rows(bb_ref, jb)
        pb = a * pb
        hb_ref[jb] = hb
        pb_ref[jb] = pb
        return hf, pf, hb, pb

    zero = jnp.zeros((LRU_SEGMENTS, W), F32)
    one = jnp.ones((LRU_SEGMENTS, W), F32)
    hf, pf, hb, pb = lax.fori_loop(0, L, step, (zero, one, zero, one), unroll=8)

    si = lax.broadcasted_iota(jnp.int32, (LRU_SEGMENTS, W), 0)
    cf = jnp.where(si >= 1, pltpu.roll(_sublane_scan(pf, hf, False), 1, 0), 0.0)
    cb = jnp.where(si < LRU_SEGMENTS - 1,
                   pltpu.roll(_sublane_scan(pb, hb, True), LRU_SEGMENTS - 1, 0), 0.0)

    def fix(j, carry):
        hs_ref[pl.ds(j, LRU_SEGMENTS, stride=pitch), :] = (
            hf_ref[j] + pf_ref[j] * cf + hb_ref[j] + pb_ref[j] * cb)
        return carry

    lax.fori_loop(0, L, fix, 0, unroll=8)

    for sg in range(LRU_SEGMENTS):
        y = y_ref[sg * L:(sg + 1) * L, :].astype(F32)
        o_ref[sg * L:(sg + 1) * L, :] = (hs_ref[sg * pitch:sg * pitch + L, :]
                                         * jax.nn.gelu(y)).astype(o_ref.dtype)


def _rg_lru(proj, conv_w, conv_b, w_cat, b_cat, lru_lambda, *, B, S, off_x, off_y):
    T = B * S
    NB, W, _ = w_cat.shape
    xblk = off_x // W
    yblk = off_y // W
    body = functools.partial(_lru_body, S=S, W=W)
    padded = LRU_SEGMENTS * (S // LRU_SEGMENTS + LRU_PITCH_PAD)
    return pl.pallas_call(
        body,
        grid=(B, NB),
        in_specs=[pl.BlockSpec((S, W), lambda b, n: (b, xblk + n)),
                  pl.BlockSpec((S, W), lambda b, n: (b, yblk + n)),
                  pl.BlockSpec((conv_w.shape[0], W), lambda b, n: (0, n)),
                  pl.BlockSpec((1, W), lambda b, n: (0, n)),
                  pl.BlockSpec((1, W, 4 * W), lambda b, n: (n, 0, 0)),
                  pl.BlockSpec((1, 1, 4 * W), lambda b, n: (n, 0, 0)),
                  pl.BlockSpec((2, W), lambda b, n: (0, n))],
        out_specs=pl.BlockSpec((S, W), lambda b, n: (b, n)),
        out_shape=jax.ShapeDtypeStruct((T, NB * W), BF16),
        scratch_shapes=([pltpu.VMEM((padded, W), F32)] * 4
                        + [pltpu.VMEM((S // LRU_SEGMENTS, LRU_SEGMENTS, W), F32)] * 4
                        + [pltpu.VMEM((padded, W), F32)]),
        compiler_params=_params("parallel", "parallel"),
        name="rg_lru",
    )(proj, proj, conv_w, conv_b[None], w_cat, b_cat, lru_lambda)


def _merge_body(o_ref, r_ref, ga_ref, gr_ref, x_ref, wpa_ref, wpl_ref, wo_ref, g_ref, wr_ref,
                h_ref, hn_ref, aff_ref, *, E, n_sub):
    ts = o_ref.shape[0] // n_sub
    for t in range(n_sub):
        rows = slice(t * ts, (t + 1) * ts)
        ba = jnp.dot(o_ref[rows, :], wpa_ref[...], preferred_element_type=F32)
        br = jnp.dot(r_ref[rows, :], wpl_ref[...], preferred_element_type=F32)
        mixed = (jax.nn.sigmoid(ga_ref[rows, :].astype(F32)) * ba
                 + jax.nn.sigmoid(gr_ref[rows, :].astype(F32)) * br)
        h = x_ref[rows, :] + jnp.dot(mixed.astype(BF16), wo_ref[...], preferred_element_type=F32)
        h_ref[rows, :] = h
        hn = h * lax.rsqrt(jnp.mean(h * h, axis=-1, keepdims=True) + EPS) * g_ref[...]
        hn_hi = hn.astype(BF16)
        hn_ref[rows, :] = hn_hi
        hn_lo = (hn - hn_hi.astype(F32)).astype(BF16)
        lg = (jnp.dot(hn_hi, wr_ref[...], preferred_element_type=F32)
              + jnp.dot(hn_lo, wr_ref[...], preferred_element_type=F32))
        logits = lg[:, :E] + lg[:, E:2 * E]
        logits = logits - jnp.max(logits, axis=-1, keepdims=True)
        ex = jnp.exp(logits)
        aff_ref[rows, :] = ex / jnp.sum(ex, axis=-1, keepdims=True)


def _merge(o_attn, lru_out, proj, x2, wpa, wpl, wo, g_ffn, wr2, *, off_ga, off_gr, E):
    T, D = x2.shape
    tm = min(1024, T)
    n_sub = 2 if tm % 1024 == 0 else 1
    resident = dict(pipeline_mode=pl.Buffered(1))
    gab = off_ga // D
    grb = off_gr // D
    row = lambda i: (i, 0)
    const = lambda i: (0, 0)
    return pl.pallas_call(
        functools.partial(_merge_body, E=E, n_sub=n_sub),
        grid=(T // tm,),
        in_specs=[pl.BlockSpec((tm, D), row), pl.BlockSpec((tm, D), row),
                  pl.BlockSpec((tm, D), lambda i: (i, gab)), pl.BlockSpec((tm, D), lambda i: (i, grb)),
                  pl.BlockSpec((tm, D), row),
                  pl.BlockSpec((D, D), const, **resident), pl.BlockSpec((D, D), const, **resident),
                  pl.BlockSpec((D, D), const, **resident),
                  pl.BlockSpec((1, D), const), pl.BlockSpec((D, 2 * E), const)],
        out_specs=[pl.BlockSpec((tm, D), row), pl.BlockSpec((tm, D), row), pl.BlockSpec((tm, E), row)],
        out_shape=[jax.ShapeDtypeStruct((T, D), F32), jax.ShapeDtypeStruct((T, D), BF16),
                   jax.ShapeDtypeStruct((T, E), F32)],
        compiler_params=_params("parallel"),
        name="merge_router",
    )(o_attn, lru_out, proj, proj, x2, wpa, wpl, wo, g_ffn, wr2)


def _select_body(aff_ref, pos_ref, cnt_ref, *, E, S, C):
    bits = pltpu.bitcast(aff_ref[0], jnp.int32)
    t = jnp.zeros((E, 1), jnp.int32)
    for bit in range(30, -1, -1):
        cand = t | (1 << bit)
        cnt = jnp.sum(jnp.where(bits >= cand, 1.0, 0.0), axis=-1, keepdims=True)
        t = jnp.where(cnt >= C, cand, t)
    gt = bits > t
    eq = bits == t
    need = C - jnp.sum(jnp.where(gt, 1.0, 0.0), axis=-1, keepdims=True).astype(jnp.int32)
    packed = jnp.where(gt, 1 << 16, 0) + jnp.where(eq, 1, 0)
    lane = lax.broadcasted_iota(jnp.int32, (E, S), 1)
    incl = packed
    d = 1
    while d < S:
        incl = incl + jnp.where(lane >= d, pltpu.roll(incl, d, 1), 0)
        d *= 2
    excl = incl - packed
    n_gt = excl >> 16
    n_eq = excl & 0xFFFF
    sel = gt | (eq & (n_eq < need))
    before = n_gt + jnp.minimum(n_eq, need)
    cnt_ref[0] = before
    pos_ref[0] = jnp.where(sel, before, -1)


def _select(aff_t, C):
    B, E, S = aff_t.shape
    spec = pl.BlockSpec((1, E, S), lambda b: (b, 0, 0))
    return pl.pallas_call(
        functools.partial(_select_body, E=E, S=S, C=C),
        grid=(B,),
        in_specs=[spec],
        out_specs=[spec, spec],
        out_shape=[jax.ShapeDtypeStruct((B, E, S), jnp.int32)] * 2,
        compiler_params=_params("parallel"),
        name="topc_select",
    )(aff_t)


def _ffn_body(pos_ref, aff_ref, hn_ref, wg_hbm, wu_hbm, wd_hbm, y_ref,
              wg_s, wu_s, wd_s, stg_g, stg_u, stg_d, sems, *, C, S, fc):
    e = pl.program_id(0)
    b = pl.program_id(1)
    n_experts = pl.num_programs(0)
    rows_in = stg_g.shape[0]
    rows_dn = stg_d.shape[0]
    n_chunks = wg_s.shape[1] // rows_in

    def chunk_copies(ee, k):
        r_in = pl.multiple_of(k * rows_in, rows_in)
        r_dn = pl.multiple_of(k * rows_dn, rows_dn)
        return (pltpu.make_async_copy(wg_hbm.at[ee, pl.ds(r_in, rows_in), :], stg_g, sems.at[0]),
                pltpu.make_async_copy(wu_hbm.at[ee, pl.ds(r_in, rows_in), :], stg_u, sems.at[1]),
                pltpu.make_async_copy(wd_hbm.at[ee, pl.ds(r_dn, rows_dn), :], stg_d, sems.at[2]))

    def cast_chunk(slot_, k):
        r_in = pl.multiple_of(k * rows_in, rows_in)
        r_dn = pl.multiple_of(k * rows_dn, rows_dn)
        wg_s[slot_, pl.ds(r_in, rows_in), :] = stg_g[...].astype(BF16)
        wu_s[slot_, pl.ds(r_in, rows_in), :] = stg_u[...].astype(BF16)
        wd_s[slot_, pl.ds(r_dn, rows_dn), :] = stg_d[...].astype(BF16)

    @pl.when((e == 0) & (b == 0))
    def _():
        def stage(k, carry):
            for cp in chunk_copies(0, k):
                cp.start()
            for cp in chunk_copies(0, k):
                cp.wait()
            cast_chunk(0, k)
            return carry

        lax.fori_loop(0, n_chunks, stage, 0)

        @pl.when(n_experts > 1)
        def _():
            for cp in chunk_copies(1, 0):
                cp.start()

    cur = e % 2
    slot = lax.broadcasted_iota(jnp.int32, (C, S), 0)
    hit = pos_ref[0, 0] == slot
    onehot = jnp.where(hit, 1.0, 0.0).astype(BF16)
    val = jnp.sum(jnp.where(hit, aff_ref[0, 0], 0.0), axis=1, keepdims=True)
    xg = jnp.dot(onehot, hn_ref[0], preferred_element_type=F32).astype(BF16)
    F = wg_s.shape[2]
    y = jnp.zeros(y_ref.shape[2:], F32)
    for f0 in range(0, F, fc):
        g = jnp.dot(xg, wg_s[cur, :, f0:f0 + fc], preferred_element_type=F32)
        u = jnp.dot(xg, wu_s[cur, :, f0:f0 + fc], preferred_element_type=F32)
        hid = (g * jax.nn.sigmoid(g) * u).astype(BF16)
        y = y + jnp.dot(hid, wd_s[cur, f0:f0 + fc, :], preferred_element_type=F32)
    y_ref[0, 0] = (y * val).astype(y_ref.dtype)

    @pl.when(e + 1 < n_experts)
    def _():
        for cp in chunk_copies(e + 1, b):
            cp.wait()
        cast_chunk(1 - cur, b)

    wrap = b + 1 == n_chunks
    e_nxt = jnp.where(wrap, e + 2, e + 1)
    k_nxt = jnp.where(wrap, 0, b + 1)

    @pl.when(e_nxt < n_experts)
    def _():
        for cp in chunk_copies(e_nxt, k_nxt):
            cp.start()


def _expert_ffn(pos_rows, aff_rows, hn3, wg, wu, wd, C):
    B, S, D = hn3.shape
    E, _, F = wg.shape
    fc = min(512, F)
    assert D % B == 0 and F % B == 0 and (D // B) % 16 == 0 and (F // B) % 16 == 0
    hbm = pl.BlockSpec(memory_space=pl.ANY)
    return pl.pallas_call(
        functools.partial(_ffn_body, C=C, S=S, fc=fc),
        grid=(E, B),
        in_specs=[pl.BlockSpec((1, 1, 1, S), lambda e, b: (b, e, 0, 0)),
                  pl.BlockSpec((1, 1, 1, S), lambda e, b: (b, e, 0, 0)),
                  pl.BlockSpec((1, S, D), lambda e, b: (b, 0, 0)),
                  hbm, hbm, hbm],
        out_specs=pl.BlockSpec((1, 1, C, D), lambda e, b: (b, e, 0, 0)),
        out_shape=jax.ShapeDtypeStruct((B, E, C, D), BF16),
        scratch_shapes=[pltpu.VMEM((2, D, F), BF16), pltpu.VMEM((2, D, F), BF16),
                        pltpu.VMEM((2, F, D), BF16),
                        pltpu.VMEM((D // B, F), F32), pltpu.VMEM((D // B, F), F32),
                        pltpu.VMEM((F // B, D), F32),
                        pltpu.SemaphoreType.DMA((3,))],
        compiler_params=_params("arbitrary", "arbitrary"),
        name="expert_ffn",
    )(pos_rows, aff_rows, hn3, wg, wu, wd)


COMBINE_WIN = 64
SLOT_ALIGN = 16


def _combine_body(lo_ref, pos_ref, y_ref, h_ref, o_ref, *, E, C, ts, n_tiles):
    b = pl.program_id(0)
    n_sub = pos_ref.shape[1] // ts
    lane = lax.broadcasted_iota(jnp.int32, (ts, LANES), 1)
    upper = lane >= COMBINE_WIN
    lane_in = jnp.where(upper, lane - COMBINE_WIN, lane)

    extra = []
    for t in range(n_sub):
        tile = pl.program_id(1) * n_sub + t
        rows = slice(t * ts, (t + 1) * ts)
        pc = pos_ref[0, rows, :]
        nxt = jnp.minimum(tile + 1, n_tiles - 1)
        starts = [(lo_ref[b, tile, e] // SLOT_ALIGN) * SLOT_ALIGN for e in range(E)]
        ends = [jnp.where(tile + 1 < n_tiles, lo_ref[b, nxt, e], C) for e in range(E)]

        def window_product(w, first, pc=pc, starts=starts):
            pieces, y_wins = [], []
            for p in range(E // 2):
                targets = []
                for e in (2 * p, 2 * p + 1):
                    start = starts[e] + w * COMBINE_WIN
                    row0 = jnp.minimum(start, C - COMBINE_WIN)
                    pe = pc[:, e:e + 1]
                    if not first:
                        pe = jnp.where(pe >= start, pe, -1)
                    targets.append(pe - row0)
                    y_wins.append(y_ref[0, e, pl.ds(pl.multiple_of(row0, SLOT_ALIGN), COMBINE_WIN), :])
                hit = jnp.where(upper, targets[1], targets[0]) == lane_in
                pieces.append(jnp.where(hit, 1.0, 0.0).astype(BF16))
            return jnp.dot(jnp.concatenate(pieces, axis=1), jnp.concatenate(y_wins, axis=0),
                           preferred_element_type=F32)

        o_ref[0, rows, :] = h_ref[0, rows, :] + window_product(0, True)

        n_pass = jnp.int32(1)
        for e in range(E):
            n_pass = jnp.maximum(n_pass, (ends[e] - starts[e] + COMBINE_WIN - 1) // COMBINE_WIN)
        extra.append((n_pass, rows, window_product))

    for n_pass, rows, window_product in extra:
        def more(w, carry, rows=rows, window_product=window_product):
            o_ref[0, rows, :] += window_product(w, False)
            return carry

        lax.fori_loop(1, n_pass, more, 0)


def _combine(lo, pos_cols, y, h3, ts):
    B, S, D = h3.shape
    _, E, C, _ = y.shape
    n_tiles = S // ts
    n_sub = math.gcd(4, n_tiles)
    tb = ts * n_sub
    assert E % 2 == 0 and C % COMBINE_WIN == 0 and 2 * COMBINE_WIN == LANES
    return pl.pallas_call(
        functools.partial(_combine_body, E=E, C=C, ts=ts, n_tiles=n_tiles),
        grid=(B, n_tiles // n_sub),
        in_specs=[pl.BlockSpec(memory_space=pltpu.SMEM),
                  pl.BlockSpec((1, tb, E), lambda b, i: (b, i, 0)),
                  pl.BlockSpec((1, E, C, D), lambda b, i: (b, 0, 0, 0)),
                  pl.BlockSpec((1, tb, D), lambda b, i: (b, i, 0))],
        out_specs=pl.BlockSpec((1, tb, D), lambda b, i: (b, i, 0)),
        out_shape=jax.ShapeDtypeStruct((B, S, D), F32),
        compiler_params=_params("parallel", "arbitrary"),
        name="moe_combine",
    )(lo, pos_cols, y, h3)


def kernel(x, g_mix, w_in, g_q, g_k, lam_q1, lam_k1, lam_q2, lam_k2, g_subln, rel_bias, conv_w, conv_b,
           gate_r_w, gate_r_b, gate_i_w, gate_i_b, lru_lambda, w_proj_attn, w_proj_lru, w_out, g_ffn,
           w_router, w_gate_e, w_up_e, w_down_e):
    B, S, D = x.shape
    depth = w_in.shape[0]
    H = rel_bias.shape[1]
    hd = g_q.shape[-1]
    vd = g_subln.shape[-1]
    qk_w = H * 2 * hd
    attn_w = H * vd
    lru_w = conv_w.shape[-1]
    NB, LB = gate_r_w.shape[2], gate_r_w.shape[3]
    E = w_router.shape[-1]
    C = EC_CAPACITY_FACTOR * S // E
    off_k = qk_w
    off_v = off_k + qk_w
    off_x = off_v + attn_w
    off_y = off_x + lru_w
    off_ga = off_y + lru_w
    off_gr = off_ga + D

    h2 = x.reshape(B * S, D)
    for layer in range(depth):
        lam_init = 0.8 - 0.6 * math.exp(-0.3 * layer)
        proj = _inproj(h2, g_mix[layer][None], w_in[layer].astype(BF16))

        lam_params = jnp.stack([lam_q1[layer], lam_k1[layer], lam_q2[layer], lam_k2[layer]])
        o_attn = _attention(proj, rel_bias, g_q[layer], g_k[layer], lam_params, g_subln[layer],
                            B=B, S=S, H=H, hd=hd, vd=vd, off_k=off_k, off_v=off_v, lam_init=lam_init)

        w_cat = jnp.concatenate([gate_r_w[layer, 0], gate_i_w[layer, 0],
                                 gate_r_w[layer, 1], gate_i_w[layer, 1]], axis=-1).astype(BF16)
        b_cat = jnp.stack([gate_r_b[layer, 0].reshape(NB, LB), gate_i_b[layer, 0].reshape(NB, LB),
                           gate_r_b[layer, 1].reshape(NB, LB), gate_i_b[layer, 1].reshape(NB, LB)],
                          axis=1).reshape(NB, 1, 4 * LB)
        lru_out = _rg_lru(proj, conv_w[layer], conv_b[layer], w_cat, b_cat, lru_lambda[layer],
                          B=B, S=S, off_x=off_x, off_y=off_y)

        wr = w_router[layer]
        wr_hi = wr.astype(BF16)
        wr2 = jnp.concatenate([wr_hi, (wr - wr_hi.astype(F32)).astype(BF16)], axis=1)
        h2, hn, aff = _merge(o_attn, lru_out, proj, h2, w_proj_attn[layer].astype(BF16),
                             w_proj_lru[layer].astype(BF16), w_out[layer].astype(BF16),
                             g_ffn[layer][None], wr2, off_ga=off_ga, off_gr=off_gr, E=E)

        aff_rows = aff.reshape(B, S, E).transpose(0, 2, 1)
        pos_rows, cnt_rows = _select(aff_rows, C)
        y = _expert_ffn(pos_rows.reshape(B, E, 1, S), aff_rows.reshape(B, E, 1, S), hn.reshape(B, S, D),
                        w_gate_e[layer],
                        w_up_e[layer], w_down_e[layer], C)
        ts = min(256, S)
        lo = cnt_rows[:, :, ::ts].transpose(0, 2, 1)
        h3 = _combine(lo, pos_rows.transpose(0, 2, 1), y, h2.reshape(B, S, D), ts)
        h2 = h3.reshape(B * S, D)
    return h2.reshape(B, S, D)
```

```python
import functools
import math

import jax
import jax.numpy as jnp
import numpy as np
from jax import lax
from jax.experimental import pallas as pl
from jax.experimental.pallas import tpu as pltpu

F32 = jnp.float32
BF16 = jnp.bfloat16
EPS = 1e-6
LANES = 128
SUBLANES = 8
VMEM_LIMIT = 56 * 1024 * 1024
LOG2E = 1.4426950408889634
NEG_BIG = -1e30
REL_MAX_DIST = 128
LRU_C = 8.0
EC_CAPACITY_FACTOR = 2


def _params(*sem):
    return pltpu.CompilerParams(dimension_semantics=sem, vmem_limit_bytes=VMEM_LIMIT)


def _inproj_body(x_ref, g_ref, w_ref, o_ref, xn_ref):
    @pl.when(pl.program_id(1) == 0)
    def _():
        x = x_ref[...]
        ms = jnp.mean(x * x, axis=-1, keepdims=True)
        xn_ref[...] = (x * lax.rsqrt(ms + EPS) * g_ref[...]).astype(BF16)

    o_ref[...] = jnp.dot(xn_ref[...], w_ref[...], preferred_element_type=F32).astype(o_ref.dtype)


def _inproj(x2, g, w_bf):
    T, D = x2.shape
    N = w_bf.shape[1]
    tm = min(1024, T)
    tn = min(1024, N)
    return pl.pallas_call(
        _inproj_body,
        grid=(T // tm, N // tn),
        in_specs=[pl.BlockSpec((tm, D), lambda i, j: (i, 0)),
                  pl.BlockSpec((1, D), lambda i, j: (0, 0)),
                  pl.BlockSpec((D, tn), lambda i, j: (0, j))],
        out_specs=pl.BlockSpec((tm, tn), lambda i, j: (i, j)),
        out_shape=jax.ShapeDtypeStruct((T, N), BF16),
        scratch_shapes=[pltpu.VMEM((tm, D), BF16)],
        compiler_params=_params("parallel", "arbitrary"),
        name="in_proj",
    )(x2, g, w_bf)


def _bucket_table(S, n_buckets, rel):
    half = n_buckets // 2
    max_exact = half // 2
    ret = np.where(rel > 0, half, 0)
    n = np.abs(rel)
    nf = np.maximum(n, max_exact).astype(np.float64)
    large = max_exact + (np.log(nf / max_exact) / math.log(REL_MAX_DIST / max_exact)
                         * (half - max_exact)).astype(np.int32)
    large = np.minimum(large, half - 1)
    return (ret + np.where(n < max_exact, n, large)).astype(np.int32).reshape(1, 2 * S)


def _seg_sumsq(x, seg_ones):
    x2 = x * x
    hi = x2.astype(BF16)
    lo = (x2 - hi.astype(F32)).astype(BF16)
    return (jnp.dot(hi, seg_ones, preferred_element_type=F32)
            + jnp.dot(lo, seg_ones, preferred_element_type=F32))


NORM_SLACK = 1.01
SAFE_LOG2_SPAN = 100.0


def _attn_body(bucket_ref, bucket_t_ref, relb_ref, q_ref, k_ref, v_ref, gq_ref, gk_ref, lamp_ref, gs_ref,
               segq_ref, segv_ref, o_ref, e_ref, et_ref, fast_ref, kn_ref, va_ref, vat_ref, m_ref, acc_ref,
               *, S, tq, n_sub, kc, hd, vd, lam_init, n_buckets):
    h = pl.program_id(0)
    b = pl.program_id(1)
    i = pl.program_id(2)
    q_scale = hd ** -0.5 * LOG2E

    @pl.when((b == 0) & (i == 0))
    def _():
        def bias_row(bk):
            tab = jnp.zeros((1, 2 * S), F32)
            for n in range(n_buckets):
                tab = jnp.where(bk == n, relb_ref[h, n], tab)
            return tab * LOG2E

        tab = bias_row(bucket_ref[...])
        bmax = jnp.max(tab, axis=-1, keepdims=True)
        bmin = jnp.min(tab, axis=-1, keepdims=True)
        bound = (NORM_SLACK * hd * q_scale) * (jnp.max(jnp.abs(gq_ref[...]), axis=-1, keepdims=True)
                                               * jnp.max(jnp.abs(gk_ref[...]), axis=-1, keepdims=True))
        span = 2.0 * bound + (bmax - bmin)
        fast_ref[0] = (span[0, 0] <= SAFE_LOG2_SPAN).astype(jnp.int32)
        shift = bound + bmax
        e_ref[...] = pltpu.roll(jnp.broadcast_to(tab - shift, (tq, 2 * S)), 0, 1, stride=1, stride_axis=0)
        tab_t = bias_row(bucket_t_ref[...])
        et_ref[...] = pltpu.roll(jnp.broadcast_to(tab_t - shift, (tq, 2 * S)), 0, 1, stride=1, stride_axis=0)

    @pl.when(i == 0)
    def _():
        k = k_ref[...].astype(F32)
        ms = _seg_sumsq(k, segq_ref[...]) * (1.0 / hd)
        kn_ref[...] = (k * lax.rsqrt(ms + EPS) * gk_ref[...]).astype(BF16)
        va_ref[:, :vd] = v_ref[...]
        va_ref[:, vd:] = jnp.ones((S, vd), BF16)
        vat_ref[:vd, :] = v_ref[...].astype(F32).T.astype(BF16)
        vat_ref[vd:, :] = jnp.ones((vat_ref.shape[0] - vd, S), BF16)

    nt_dims = (((1,), (1,)), ((), ()))
    lp = lamp_ref[...]
    lam = (jnp.exp(jnp.sum(lp[0:1] * lp[1:2], axis=-1, keepdims=True))
           - jnp.exp(jnp.sum(lp[2:3] * lp[3:4], axis=-1, keepdims=True)) + lam_init)

    def q_maps_of(t):
        q = q_ref[t * tq:(t + 1) * tq, :].astype(F32)
        ms = _seg_sumsq(q, segq_ref[...]) * (1.0 / hd)
        qn = q * lax.rsqrt(ms + EPS) * (gq_ref[...] * q_scale)
        lo = lax.broadcasted_iota(jnp.int32, qn.shape, 1) < hd
        return jnp.where(lo, qn, 0.0).astype(BF16), jnp.where(lo, 0.0, qn).astype(BF16)

    def e_start(t, k0):
        return pl.multiple_of(S + k0 - (i * n_sub + t) * tq, LANES)

    def finish(t, a1, a2):
        o = a1[:, :vd] / a1[:, vd:] - lam * (a2[:, :vd] / a2[:, vd:])
        o = o * lax.rsqrt(_seg_sumsq(o, segv_ref[...]) * (1.0 / vd) + EPS)
        o_ref[t * tq:(t + 1) * tq, :] = (o * (gs_ref[...] * (1.0 - lam_init))).astype(o_ref.dtype)

    def finish_t(t, a1, a2):
        ot = a1[:vd] / a1[vd:vd + 1] - lam * (a2[:vd] / a2[vd:vd + 1])
        o = ot.T
        o = o * lax.rsqrt(_seg_sumsq(o, segv_ref[...]) * (1.0 / vd) + EPS)
        o_ref[t * tq:(t + 1) * tq, :] = (o * (gs_ref[...] * (1.0 - lam_init))).astype(o_ref.dtype)

    @pl.when(fast_ref[0] == 1)
    def _():
        for t in range(n_sub):
            q_maps = q_maps_of(t)
            tile = i * n_sub + t
            accs = []
            for mi in range(2):
                chunks = []
                for c in range(S // tq):
                    st = lax.dot_general(kn_ref[c * tq:(c + 1) * tq, :], q_maps[mi], nt_dims,
                                         preferred_element_type=F32)
                    start = pl.multiple_of(S + (tile - c) * tq, LANES)
                    chunks.append(jnp.exp2(st + et_ref[:, pl.ds(start, tq)]).astype(BF16))
                accs.append(jnp.dot(vat_ref[...], jnp.concatenate(chunks, axis=0),
                                    preferred_element_type=F32))
            finish_t(t, *accs)

    @pl.when(fast_ref[0] != 1)
    def _():
        for t in range(n_sub):
            q_maps = q_maps_of(t)
            m_ref[...] = jnp.full(m_ref.shape, NEG_BIG, F32)
            acc_ref[...] = jnp.zeros(acc_ref.shape, F32)

            def chunk(c, carry):
                k0 = pl.multiple_of(c * kc, kc)
                bias = e_ref[:, pl.ds(e_start(t, k0), kc)]
                for mi in range(2):
                    s = lax.dot_general(q_maps[mi], kn_ref[pl.ds(k0, kc), :], nt_dims,
                                        preferred_element_type=F32) + bias
                    m_old = m_ref[mi]
                    m_new = jnp.maximum(m_old, jnp.max(s, axis=-1, keepdims=True))
                    p = jnp.exp2(s - m_new).astype(BF16)
                    acc_ref[mi] = (jnp.exp2(m_old - m_new) * acc_ref[mi]
                                   + jnp.dot(p, va_ref[pl.ds(k0, kc), :], preferred_element_type=F32))
                    m_ref[mi] = m_new
                return carry

            lax.fori_loop(0, S // kc, chunk, 0)
            finish(t, acc_ref[0], acc_ref[1])


def _attention(proj, rel_bias, g_q, g_k, lam_params, g_subln, *, B, S, H, hd, vd, off_k, off_v, lam_init):
    T = B * S
    tq = min(512, S)
    kc = min(512, S)
    n_sub = math.gcd(4, S // tq)
    tqb = n_sub * tq
    nq = S // tqb
    n_buckets = rel_bias.shape[0]
    hw = 2 * hd
    bucket = jnp.asarray(_bucket_table(S, n_buckets, np.arange(2 * S) - S))
    bucket_t = jnp.asarray(_bucket_table(S, n_buckets, S - np.arange(2 * S)))
    body = functools.partial(_attn_body, S=S, tq=tq, n_sub=n_sub, kc=kc, hd=hd, vd=vd, lam_init=lam_init,
                             n_buckets=n_buckets)
    kblk = off_k // hw
    vblk = off_v // vd
    seg_q = np.kron(np.eye(2, dtype=np.float32), np.ones((hd, hd), np.float32))
    return pl.pallas_call(
        body,
        grid=(H, B, nq),
        in_specs=[pl.BlockSpec((1, 2 * S), lambda h, b, i: (0, 0)),
                  pl.BlockSpec((1, 2 * S), lambda h, b, i: (0, 0)),
                  pl.BlockSpec(memory_space=pltpu.SMEM),
                  pl.BlockSpec((tqb, hw), lambda h, b, i: (b * nq + i, h)),
                  pl.BlockSpec((S, hw), lambda h, b, i: (b, kblk + h)),
                  pl.BlockSpec((S, vd), lambda h, b, i: (b, vblk + h)),
                  pl.BlockSpec((1, hw), lambda h, b, i: (0, 0)),
                  pl.BlockSpec((1, hw), lambda h, b, i: (0, 0)),
                  pl.BlockSpec((4, hd), lambda h, b, i: (0, 0)),
                  pl.BlockSpec((1, vd), lambda h, b, i: (0, 0)),
                  pl.BlockSpec((hw, hw), lambda h, b, i: (0, 0)),
                  pl.BlockSpec((vd, vd), lambda h, b, i: (0, 0))],
        out_specs=pl.BlockSpec((tqb, vd), lambda h, b, i: (b * nq + i, h)),
        out_shape=jax.ShapeDtypeStruct((T, H * vd), BF16),
        scratch_shapes=[pltpu.VMEM((tq, 2 * S), F32),
                        pltpu.VMEM((tq, 2 * S), F32),
                        pltpu.SMEM((1,), jnp.int32),
                        pltpu.VMEM((S, hw), BF16),
                        pltpu.VMEM((S, 2 * vd), BF16),
                        pltpu.VMEM((vd + 16, S), BF16),
                        pltpu.VMEM((2, tq, 1), F32),
                        pltpu.VMEM((2, tq, 2 * vd), F32)],
        compiler_params=_params("arbitrary", "arbitrary", "arbitrary"),
        name="diff_attention",
    )(bucket, bucket_t, rel_bias.T, proj, proj, proj, jnp.tile(g_q, 2)[None], jnp.tile(g_k, 2)[None],
      lam_params, g_subln[None], jnp.asarray(seg_q, BF16), jnp.ones((vd, vd), BF16))


LRU_SEGMENTS = SUBLANES
LRU_PITCH_PAD = 8


def _sublane_scan(a, bb, reverse):
    ri = lax.broadcasted_iota(jnp.int32, a.shape, 0)
    for d in (1, 2, 4):
        if reverse:
            keep, sh = ri < SUBLANES - d, SUBLANES - d
        else:
            keep, sh = ri >= d, d
        bb = bb + a * jnp.where(keep, pltpu.roll(bb, sh, 0), 0.0)
        a = a * jnp.where(keep, pltpu.roll(a, sh, 0), 1.0)
    return bb


def _lru_body(x_ref, y_ref, cw_ref, cb_ref, w_ref, bias_ref, lam_ref, o_ref,
              af_ref, bf_ref, ab_ref, bb_ref, hf_ref, pf_ref, hb_ref, pb_ref, hs_ref, *, S, W):
    L = S // LRU_SEGMENTS
    pitch = L + LRU_PITCH_PAD
    x = x_ref[...].astype(F32)
    row = lax.broadcasted_iota(jnp.int32, (S, W), 0)
    cw = cw_ref[...]
    xc = (cw[0:1] * jnp.where(row >= 2, pltpu.roll(x, 2, 0), 0.0)
          + cw[1:2] * jnp.where(row >= 1, pltpu.roll(x, 1, 0), 0.0)
          + cw[2:3] * x
          + cw[3:4] * jnp.where(row < S - 1, pltpu.roll(x, S - 1, 0), 0.0)
          + cb_ref[...])
    gates = jnp.dot(xc.astype(BF16), w_ref[0], preferred_element_type=F32) + bias_ref[0]
    lam = lam_ref[...]
    for d, (a_ref, b_ref) in enumerate(((af_ref, bf_ref), (ab_ref, bb_ref))):
        r = jax.nn.sigmoid(gates[:, (2 * d) * W:(2 * d + 1) * W])
        ig = jax.nn.sigmoid(gates[:, (2 * d + 1) * W:(2 * d + 2) * W])
        nl = -lam[d:d + 1]
        softplus = jnp.maximum(nl, 0.0) + jnp.log1p(jnp.exp(-jnp.abs(nl)))
        a = jnp.exp2((-LRU_C * LOG2E * softplus) * r)
        y2 = 1.0 - a * a
        bvals = jnp.where(y2 > 0.0, y2 * lax.rsqrt(y2), 0.0) * ig * xc
        for sg in range(LRU_SEGMENTS):
            a_ref[sg * pitch:sg * pitch + L, :] = a[sg * L:(sg + 1) * L]
            b_ref[sg * pitch:sg * pitch + L, :] = bvals[sg * L:(sg + 1) * L]

    def seg_rows(ref, j):
        return ref[pl.ds(j, LRU_SEGMENTS, stride=pitch), :]

    def step(j, carry):
        hf, pf, hb, pb = carry
        a = seg_rows(af_ref, j)
        hf = a * hf + seg_rows(bf_ref, j)
        pf = a * pf
        hf_ref[j] = hf
        pf_ref[j] = pf
        jb = L - 1 - j
        a = seg_rows(ab_ref, jb)
        hb = a * hb + seg_rows(bb_ref, jb)
        pb = a * pb
        hb_ref[jb] = hb
        pb_ref[jb] = pb
        return hf, pf, hb, pb

    zero = jnp.zeros((LRU_SEGMENTS, W), F32)
    one = jnp.ones((LRU_SEGMENTS, W), F32)
    hf, pf, hb, pb = lax.fori_loop(0, L, step, (zero, one, zero, one), unroll=8)

    si = lax.broadcasted_iota(jnp.int32, (LRU_SEGMENTS, W), 0)
    cf = jnp.where(si >= 1, pltpu.roll(_sublane_scan(pf, hf, False), 1, 0), 0.0)
    cb = jnp.where(si < LRU_SEGMENTS - 1,
                   pltpu.roll(_sublane_scan(pb, hb, True), LRU_SEGMENTS - 1, 0), 0.0)

    def fix(j, carry):
        hs_ref[pl.ds(j, LRU_SEGMENTS, stride=pitch), :] = (
            hf_ref[j] + pf_ref[j] * cf + hb_ref[j] + pb_ref[j] * cb)
        return carry

    lax.fori_loop(0, L, fix, 0, unroll=8)

    for sg in range(LRU_SEGMENTS):
        y = y_ref[sg * L:(sg + 1) * L, :].astype(F32)
        o_ref[sg * L:(sg + 1) * L, :] = (hs_ref[sg * pitch:sg * pitch + L, :]
                                         * jax.nn.gelu(y)).astype(o_ref.dtype)


def _rg_lru(proj, conv_w, conv_b, w_cat, b_cat, lru_lambda, *, B, S, off_x, off_y):
    T = B * S
    NB, W, _ = w_cat.shape
    xblk = off_x // W
    yblk = off_y // W
    body = functools.partial(_lru_body, S=S, W=W)
    padded = LRU_SEGMENTS * (S // LRU_SEGMENTS + LRU_PITCH_PAD)
    return pl.pallas_call(
        body,
        grid=(B, NB),
        in_specs=[pl.BlockSpec((S, W), lambda b, n: (b, xblk + n)),
                  pl.BlockSpec((S, W), lambda b, n: (b, yblk + n)),
                  pl.BlockSpec((conv_w.shape[0], W), lambda b, n: (0, n)),
                  pl.BlockSpec((1, W), lambda b, n: (0, n)),
                  pl.BlockSpec((1, W, 4 * W), lambda b, n: (n, 0, 0)),
                  pl.BlockSpec((1, 1, 4 * W), lambda b, n: (n, 0, 0)),
                  pl.BlockSpec((2, W), lambda b, n: (0, n))],
        out_specs=pl.BlockSpec((S, W), lambda b, n: (b, n)),
        out_shape=jax.ShapeDtypeStruct((T, NB * W), BF16),
        scratch_shapes=([pltpu.VMEM((padded, W), F32)] * 4
                        + [pltpu.VMEM((S // LRU_SEGMENTS, LRU_SEGMENTS, W), F32)] * 4
                        + [pltpu.VMEM((padded, W), F32)]),
        compiler_params=_params("parallel", "parallel"),
        name="rg_lru",
    )(proj, proj, conv_w, conv_b[None], w_cat, b_cat, lru_lambda)


def _merge_body(o_ref, r_ref, ga_ref, gr_ref, x_ref, wpa_ref, wpl_ref, wo_ref, g_ref, wr_ref,
                h_ref, hn_ref, aff_ref, *, E, n_sub):
    ts = o_ref.shape[0] // n_sub
    for t in range(n_sub):
        rows = slice(t * ts, (t + 1) * ts)
        ba = jnp.dot(o_ref[rows, :], wpa_ref[...], preferred_element_type=F32)
        br = jnp.dot(r_ref[rows, :], wpl_ref[...], preferred_element_type=F32)
        mixed = (jax.nn.sigmoid(ga_ref[rows, :].astype(F32)) * ba
                 + jax.nn.sigmoid(gr_ref[rows, :].astype(F32)) * br)
        h = x_ref[rows, :] + jnp.dot(mixed.astype(BF16), wo_ref[...], preferred_element_type=F32)
        h_ref[rows, :] = h
        hn = h * lax.rsqrt(jnp.mean(h * h, axis=-1, keepdims=True) + EPS) * g_ref[...]
        hn_hi = hn.astype(BF16)
        hn_ref[rows, :] = hn_hi
        hn_lo = (hn - hn_hi.astype(F32)).astype(BF16)
        lg = (jnp.dot(hn_hi, wr_ref[...], preferred_element_type=F32)
              + jnp.dot(hn_lo, wr_ref[...], preferred_element_type=F32))
        logits = lg[:, :E] + lg[:, E:2 * E]
        logits = logits - jnp.max(logits, axis=-1, keepdims=True)
        ex = jnp.exp(logits)
        aff_ref[rows, :] = ex / jnp.sum(ex, axis=-1, keepdims=True)


def _merge(o_attn, lru_out, proj, x2, wpa, wpl, wo, g_ffn, wr2, *, off_ga, off_gr, E):
    T, D = x2.shape
    tm = min(1024, T)
    n_sub = 2 if tm % 1024 == 0 else 1
    resident = dict(pipeline_mode=pl.Buffered(1))
    gab = off_ga // D
    grb = off_gr // D
    row = lambda i: (i, 0)
    const = lambda i: (0, 0)
    return pl.pallas_call(
        functools.partial(_merge_body, E=E, n_sub=n_sub),
        grid=(T // tm,),
        in_specs=[pl.BlockSpec((tm, D), row), pl.BlockSpec((tm, D), row),
                  pl.BlockSpec((tm, D), lambda i: (i, gab)), pl.BlockSpec((tm, D), lambda i: (i, grb)),
                  pl.BlockSpec((tm, D), row),
                  pl.BlockSpec((D, D), const, **resident), pl.BlockSpec((D, D), const, **resident),
                  pl.BlockSpec((D, D), const, **resident),
                  pl.BlockSpec((1, D), const), pl.BlockSpec((D, 2 * E), const)],
        out_specs=[pl.BlockSpec((tm, D), row), pl.BlockSpec((tm, D), row), pl.BlockSpec((tm, E), row)],
        out_shape=[jax.ShapeDtypeStruct((T, D), F32), jax.ShapeDtypeStruct((T, D), BF16),
                   jax.ShapeDtypeStruct((T, E), F32)],
        compiler_params=_params("parallel"),
        name="merge_router",
    )(o_attn, lru_out, proj, proj, x2, wpa, wpl, wo, g_ffn, wr2)


def _select_body(aff_ref, pos_ref, cnt_ref, *, E, S, C):
    bits = pltpu.bitcast(aff_ref[0], jnp.int32)
    t = jnp.zeros((E, 1), jnp.int32)
    for bit in range(30, -1, -1):
        cand = t | (1 << bit)
        cnt = jnp.sum(jnp.where(bits >= cand, 1.0, 0.0), axis=-1, keepdims=True)
        t = jnp.where(cnt >= C, cand, t)
    gt = bits > t
    eq = bits == t
    need = C - jnp.sum(jnp.where(gt, 1.0, 0.0), axis=-1, keepdims=True).astype(jnp.int32)
    packed = jnp.where(gt, 1 << 16, 0) + jnp.where(eq, 1, 0)
    lane = lax.broadcasted_iota(jnp.int32, (E, S), 1)
    incl = packed
    d = 1
    while d < S:
        incl = incl + jnp.where(lane >= d, pltpu.roll(incl, d, 1), 0)
        d *= 2
    excl = incl - packed
    n_gt = excl >> 16
    n_eq = excl & 0xFFFF
    sel = gt | (eq & (n_eq < need))
    before = n_gt + jnp.minimum(n_eq, need)
    cnt_ref[0] = before
    pos_ref[0] = jnp.where(sel, before, -1)


def _select(aff_t, C):
    B, E, S = aff_t.shape
    spec = pl.BlockSpec((1, E, S), lambda b: (b, 0, 0))
    return pl.pallas_call(
        functools.partial(_select_body, E=E, S=S, C=C),
        grid=(B,),
        in_specs=[spec],
        out_specs=[spec, spec],
        out_shape=[jax.ShapeDtypeStruct((B, E, S), jnp.int32)] * 2,
        compiler_params=_params("parallel"),
        name="topc_select",
    )(aff_t)


GATHER_WIN = 64
SLOT_ALIGN = 16


def _gather_body(lo_ref, pos_ref, hn_ref, xg_ref, *, E, C, S, ts):
    b = pl.program_id(0)
    n_tiles = S // ts
    xg_ref[...] = jnp.zeros(xg_ref.shape, xg_ref.dtype)
    sub = lax.broadcasted_iota(jnp.int32, (GATHER_WIN, ts), 0)

    def tile_body(k, carry):
        t0 = pl.multiple_of(k * ts, ts)
        nxt = jnp.minimum(k + 1, n_tiles - 1)
        starts = [(lo_ref[b, k, e] // SLOT_ALIGN) * SLOT_ALIGN for e in range(E)]
        ends = [jnp.where(k + 1 < n_tiles, lo_ref[b, nxt, e], C) for e in range(E)]
        hn_t = hn_ref[0, pl.ds(t0, ts), :]

        def one_pass(w, first):
            sel, row0s = [], []
            for e in range(E):
                start = starts[e] + w * GATHER_WIN
                row0 = jnp.minimum(start, C - GATHER_WIN)
                pe = pos_ref[0, e:e + 1, pl.ds(t0, ts)]
                if not first:
                    pe = jnp.where(pe >= start, pe, -1)
                sel.append(jnp.where(pe - row0 == sub, 1.0, 0.0).astype(BF16))
                row0s.append(pl.multiple_of(row0, SLOT_ALIGN))
            rows = jnp.dot(jnp.concatenate(sel, axis=0), hn_t,
                           preferred_element_type=F32).astype(xg_ref.dtype)
            for e in range(E):
                xg_ref[0, e, pl.ds(row0s[e], GATHER_WIN), :] += rows[e * GATHER_WIN:(e + 1) * GATHER_WIN]

        one_pass(0, True)
        n_pass = jnp.int32(1)
        for e in range(E):
            n_pass = jnp.maximum(n_pass, (ends[e] - starts[e] + GATHER_WIN - 1) // GATHER_WIN)

        def more(w, c):
            one_pass(w, False)
            return c

        lax.fori_loop(1, n_pass, more, 0)
        return carry

    lax.fori_loop(0, n_tiles, tile_body, 0)


def _gather(lo, pos_rows, hn3, C, ts):
    B, S, D = hn3.shape
    E = pos_rows.shape[1]
    assert C % GATHER_WIN == 0
    return pl.pallas_call(
        functools.partial(_gather_body, E=E, C=C, S=S, ts=ts),
        grid=(B,),
        in_specs=[pl.BlockSpec(memory_space=pltpu.SMEM),
                  pl.BlockSpec((1, E, S), lambda b: (b, 0, 0)),
                  pl.BlockSpec((1, S, D), lambda b: (b, 0, 0))],
        out_specs=pl.BlockSpec((1, E, C, D), lambda b: (b, 0, 0, 0)),
        out_shape=jax.ShapeDtypeStruct((B, E, C, D), BF16),
        compiler_params=_params("parallel"),
        name="moe_gather",
    )(lo, pos_rows, hn3)


def _ffn_body(pos_ref, aff_ref, xg_ref, wg_hbm, wu_hbm, wd_hbm, y_ref,
              wg_s, wu_s, wd_s, stg_g, stg_u, stg_d, sems, *, C, S, fc):
    e = pl.program_id(0)
    b = pl.program_id(1)
    n_experts = pl.num_programs(0)
    rows_in = stg_g.shape[0]
    rows_dn = stg_d.shape[0]
    n_chunks = wg_s.shape[1] // rows_in

    def chunk_copies(ee, k):
        r_in = pl.multiple_of(k * rows_in, rows_in)
        r_dn = pl.multiple_of(k * rows_dn, rows_dn)
        return (pltpu.make_async_copy(wg_hbm.at[ee, pl.ds(r_in, rows_in), :], stg_g, sems.at[0]),
                pltpu.make_async_copy(wu_hbm.at[ee, pl.ds(r_in, rows_in), :], stg_u, sems.at[1]),
                pltpu.make_async_copy(wd_hbm.at[ee, pl.ds(r_dn, rows_dn), :], stg_d, sems.at[2]))

    def cast_chunk(slot_, k):
        r_in = pl.multiple_of(k * rows_in, rows_in)
        r_dn = pl.multiple_of(k * rows_dn, rows_dn)
        wg_s[slot_, pl.ds(r_in, rows_in), :] = stg_g[...].astype(BF16)
        wu_s[slot_, pl.ds(r_in, rows_in), :] = stg_u[...].astype(BF16)
        wd_s[slot_, pl.ds(r_dn, rows_dn), :] = stg_d[...].astype(BF16)

    @pl.when((e == 0) & (b == 0))
    def _():
        def stage(k, carry):
            for cp in chunk_copies(0, k):
                cp.start()
            for cp in chunk_copies(0, k):
                cp.wait()
            cast_chunk(0, k)
            return carry

        lax.fori_loop(0, n_chunks, stage, 0)

        @pl.when(n_experts > 1)
        def _():
            for cp in chunk_copies(1, 0):
                cp.start()

    cur = e % 2
    slot = lax.broadcasted_iota(jnp.int32, (C, S), 0)
    hit = pos_ref[0, 0] == slot
    val = jnp.sum(jnp.where(hit, aff_ref[0, 0], 0.0), axis=1, keepdims=True)
    xg = xg_ref[0, 0]
    F = wg_s.shape[2]
    y = jnp.zeros(y_ref.shape[2:], F32)
    for f0 in range(0, F, fc):
        g = jnp.dot(xg, wg_s[cur, :, f0:f0 + fc], preferred_element_type=F32)
        u = jnp.dot(xg, wu_s[cur, :, f0:f0 + fc], preferred_element_type=F32)
        hid = (g * jax.nn.sigmoid(g) * u).astype(BF16)
        y = y + jnp.dot(hid, wd_s[cur, f0:f0 + fc, :], preferred_element_type=F32)
    y_ref[0, 0] = (y * val).astype(y_ref.dtype)

    @pl.when(e + 1 < n_experts)
    def _():
        for cp in chunk_copies(e + 1, b):
            cp.wait()
        cast_chunk(1 - cur, b)

    wrap = b + 1 == n_chunks
    e_nxt = jnp.where(wrap, e + 2, e + 1)
    k_nxt = jnp.where(wrap, 0, b + 1)

    @pl.when(e_nxt < n_experts)
    def _():
        for cp in chunk_copies(e_nxt, k_nxt):
            cp.start()


def _expert_ffn(pos_rows, aff_rows, xg, wg, wu, wd):
    B, _, C, D = xg.shape
    S = pos_rows.shape[-1]
    E, _, F = wg.shape
    fc = min(512, F)
    assert D % B == 0 and F % B == 0 and (D // B) % 16 == 0 and (F // B) % 16 == 0
    hbm = pl.BlockSpec(memory_space=pl.ANY)
    return pl.pallas_call(
        functools.partial(_ffn_body, C=C, S=S, fc=fc),
        grid=(E, B),
        in_specs=[pl.BlockSpec((1, 1, 1, S), lambda e, b: (b, e, 0, 0)),
                  pl.BlockSpec((1, 1, 1, S), lambda e, b: (b, e, 0, 0)),
                  pl.BlockSpec((1, 1, C, D), lambda e, b: (b, e, 0, 0)),
                  hbm, hbm, hbm],
        out_specs=pl.BlockSpec((1, 1, C, D), lambda e, b: (b, e, 0, 0)),
        out_shape=jax.ShapeDtypeStruct((B, E, C, D), BF16),
        scratch_shapes=[pltpu.VMEM((2, D, F), BF16), pltpu.VMEM((2, D, F), BF16),
                        pltpu.VMEM((2, F, D), BF16),
                        pltpu.VMEM((D // B, F), F32), pltpu.VMEM((D // B, F), F32),
                        pltpu.VMEM((F // B, D), F32),
                        pltpu.SemaphoreType.DMA((3,))],
        compiler_params=_params("arbitrary", "arbitrary"),
        name="expert_ffn",
    )(pos_rows, aff_rows, xg, wg, wu, wd)


COMBINE_WIN = 64


def _combine_body(lo_ref, pos_ref, y_ref, h_ref, o_ref, *, E, C, ts, n_tiles):
    b = pl.program_id(0)
    n_sub = pos_ref.shape[1] // ts
    lane = lax.broadcasted_iota(jnp.int32, (ts, LANES), 1)
    upper = lane >= COMBINE_WIN
    lane_in = jnp.where(upper, lane - COMBINE_WIN, lane)

    extra = []
    for t in range(n_sub):
        tile = pl.program_id(1) * n_sub + t
        rows = slice(t * ts, (t + 1) * ts)
        pc = pos_ref[0, rows, :]
        nxt = jnp.minimum(tile + 1, n_tiles - 1)
        starts = [(lo_ref[b, tile, e] // SLOT_ALIGN) * SLOT_ALIGN for e in range(E)]
        ends = [jnp.where(tile + 1 < n_tiles, lo_ref[b, nxt, e], C) for e in range(E)]

        def window_product(w, first, pc=pc, starts=starts):
            pieces, y_wins = [], []
            for p in range(E // 2):
                targets = []
                for e in (2 * p, 2 * p + 1):
                    start = starts[e] + w * COMBINE_WIN
                    row0 = jnp.minimum(start, C - COMBINE_WIN)
                    pe = pc[:, e:e + 1]
                    if not first:
                        pe = jnp.where(pe >= start, pe, -1)
                    targets.append(pe - row0)
                    y_wins.append(y_ref[0, e, pl.ds(pl.multiple_of(row0, SLOT_ALIGN), COMBINE_WIN), :])
                hit = jnp.where(upper, targets[1], targets[0]) == lane_in
                pieces.append(jnp.where(hit, 1.0, 0.0).astype(BF16))
            return jnp.dot(jnp.concatenate(pieces, axis=1), jnp.concatenate(y_wins, axis=0),
                           preferred_element_type=F32)

        o_ref[0, rows, :] = h_ref[0, rows, :] + window_product(0, True)

        n_pass = jnp.int32(1)
        for e in range(E):
            n_pass = jnp.maximum(n_pass, (ends[e] - starts[e] + COMBINE_WIN - 1) // COMBINE_WIN)
        extra.append((n_pass, rows, window_product))

    for n_pass, rows, window_product in extra:
        def more(w, carry, rows=rows, window_product=window_product):
            o_ref[0, rows, :] += window_product(w, False)
            return carry

        lax.fori_loop(1, n_pass, more, 0)


def _combine(lo, pos_cols, y, h3, ts):
    B, S, D = h3.shape
    _, E, C, _ = y.shape
    n_tiles = S // ts
    n_sub = math.gcd(4, n_tiles)
    tb = ts * n_sub
    assert E % 2 == 0 and C % COMBINE_WIN == 0 and 2 * COMBINE_WIN == LANES
    return pl.pallas_call(
        functools.partial(_combine_body, E=E, C=C, ts=ts, n_tiles=n_tiles),
        grid=(B, n_tiles // n_sub),
        in_specs=[pl.BlockSpec(memory_space=pltpu.SMEM),
                  pl.BlockSpec((1, tb, E), lambda b, i: (b, i, 0)),
                  pl.BlockSpec((1, E, C, D), lambda b, i: (b, 0, 0, 0)),
                  pl.BlockSpec((1, tb, D), lambda b, i: (b, i, 0))],
        out_specs=pl.BlockSpec((1, tb, D), lambda b, i: (b, i, 0)),
        out_shape=jax.ShapeDtypeStruct((B, S, D), F32),
        compiler_params=_params("parallel", "arbitrary"),
        name="moe_combine",
    )(lo, pos_cols, y, h3)


def kernel(x, g_mix, w_in, g_q, g_k, lam_q1, lam_k1, lam_q2, lam_k2, g_subln, rel_bias, conv_w, conv_b,
           gate_r_w, gate_r_b, gate_i_w, gate_i_b, lru_lambda, w_proj_attn, w_proj_lru, w_out, g_ffn,
           w_router, w_gate_e, w_up_e, w_down_e):
    B, S, D = x.shape
    depth = w_in.shape[0]
    H = rel_bias.shape[1]
    hd = g_q.shape[-1]
    vd = g_subln.shape[-1]
    qk_w = H * 2 * hd
    attn_w = H * vd
    lru_w = conv_w.shape[-1]
    NB, LB = gate_r_w.shape[2], gate_r_w.shape[3]
    E = w_router.shape[-1]
    C = EC_CAPACITY_FACTOR * S // E
    off_k = qk_w
    off_v = off_k + qk_w
    off_x = off_v + attn_w
    off_y = off_x + lru_w
    off_ga = off_y + lru_w
    off_gr = off_ga + D

    h2 = x.reshape(B * S, D)
    for layer in range(depth):
        lam_init = 0.8 - 0.6 * math.exp(-0.3 * layer)
        proj = _inproj(h2, g_mix[layer][None], w_in[layer].astype(BF16))

        lam_params = jnp.stack([lam_q1[layer], lam_k1[layer], lam_q2[layer], lam_k2[layer]])
        o_attn = _attention(proj, rel_bias, g_q[layer], g_k[layer], lam_params, g_subln[layer],
                            B=B, S=S, H=H, hd=hd, vd=vd, off_k=off_k, off_v=off_v, lam_init=lam_init)

        w_cat = jnp.concatenate([gate_r_w[layer, 0], gate_i_w[layer, 0],
                                 gate_r_w[layer, 1], gate_i_w[layer, 1]], axis=-1).astype(BF16)
        b_cat = jnp.stack([gate_r_b[layer, 0].reshape(NB, LB), gate_i_b[layer, 0].reshape(NB, LB),
                           gate_r_b[layer, 1].reshape(NB, LB), gate_i_b[layer, 1].reshape(NB, LB)],
                          axis=1).reshape(NB, 1, 4 * LB)
        lru_out = _rg_lru(proj, conv_w[layer], conv_b[layer], w_cat, b_cat, lru_lambda[layer],
                          B=B, S=S, off_x=off_x, off_y=off_y)

        wr = w_router[layer]
        wr_hi = wr.astype(BF16)
        wr2 = jnp.concatenate([wr_hi, (wr - wr_hi.astype(F32)).astype(BF16)], axis=1)
        h2, hn, aff = _merge(o_attn, lru_out, proj, h2, w_proj_attn[layer].astype(BF16),
                             w_proj_lru[layer].astype(BF16), w_out[layer].astype(BF16),
                             g_ffn[layer][None], wr2, off_ga=off_ga, off_gr=off_gr, E=E)

        aff_rows = aff.reshape(B, S, E).transpose(0, 2, 1)
        pos_rows, cnt_rows = _select(aff_rows, C)
        ts = min(256, S)
        lo = cnt_rows[:, :, ::ts].transpose(0, 2, 1)
        xg = _gather(lo, pos_rows, hn.reshape(B, S, D), C, ts)
        y = _expert_ffn(pos_rows.reshape(B, E, 1, S), aff_rows.reshape(B, E, 1, S), xg,
                        w_gate_e[layer], w_up_e[layer], w_down_e[layer])
        h3 = _combine(lo, pos_rows.transpose(0, 2, 1), y, h2.reshape(B, S, D), ts)
        h2 = h3.reshape(B * S, D)
    return h2.reshape(B, S, D)
```

```python
import functools
import math

import jax
import jax.numpy as jnp
import numpy as np
from jax import lax
from jax.experimental import pallas as pl
from jax.experimental.pallas import tpu as pltpu

F32 = jnp.float32
BF16 = jnp.bfloat16
EPS = 1e-6
LANES = 128
SUBLANES = 8
VMEM_LIMIT = 56 * 1024 * 1024
LOG2E = 1.4426950408889634
NEG_BIG = -1e30
REL_MAX_DIST = 128
LRU_C = 8.0
EC_CAPACITY_FACTOR = 2


def _params(*sem):
    return pltpu.CompilerParams(dimension_semantics=sem, vmem_limit_bytes=VMEM_LIMIT)


def _inproj_body(x_ref, g_ref, w_ref, o_ref, xn_ref):
    @pl.when(pl.program_id(1) == 0)
    def _():
        x = x_ref[...]
        ms = jnp.mean(x * x, axis=-1, keepdims=True)
        xn_ref[...] = (x * lax.rsqrt(ms + EPS) * g_ref[...]).astype(BF16)

    o_ref[...] = jnp.dot(xn_ref[...], w_ref[...], preferred_element_type=F32).astype(o_ref.dtype)


def _inproj(x2, g, w_bf):
    T, D = x2.shape
    N = w_bf.shape[1]
    tm = min(1024, T)
    tn = min(1024, N)
    return pl.pallas_call(
        _inproj_body,
        grid=(T // tm, N // tn),
        in_specs=[pl.BlockSpec((tm, D), lambda i, j: (i, 0)),
                  pl.BlockSpec((1, D), lambda i, j: (0, 0)),
                  pl.BlockSpec((D, tn), lambda i, j: (0, j))],
        out_specs=pl.BlockSpec((tm, tn), lambda i, j: (i, j)),
        out_shape=jax.ShapeDtypeStruct((T, N), BF16),
        scratch_shapes=[pltpu.VMEM((tm, D), BF16)],
        compiler_params=_params("parallel", "arbitrary"),
        name="in_proj",
    )(x2, g, w_bf)


def _bucket_table(S, n_buckets, rel):
    half = n_buckets // 2
    max_exact = half // 2
    ret = np.where(rel > 0, half, 0)
    n = np.abs(rel)
    nf = np.maximum(n, max_exact).astype(np.float64)
    large = max_exact + (np.log(nf / max_exact) / math.log(REL_MAX_DIST / max_exact)
                         * (half - max_exact)).astype(np.int32)
    large = np.minimum(large, half - 1)
    return (ret + np.where(n < max_exact, n, large)).astype(np.int32).reshape(1, 2 * S)


def _seg_sumsq(x, seg_ones):
    x2 = x * x
    hi = x2.astype(BF16)
    lo = (x2 - hi.astype(F32)).astype(BF16)
    return (jnp.dot(hi, seg_ones, preferred_element_type=F32)
            + jnp.dot(lo, seg_ones, preferred_element_type=F32))


NORM_SLACK = 1.01
SAFE_LOG2_SPAN = 100.0


def _attn_body(bucket_ref, bucket_t_ref, relb_ref, q_ref, k_ref, v_ref, gq_ref, gk_ref, lamp_ref, gs_ref,
               segq_ref, segv_ref, o_ref, e_ref, et_ref, fast_ref, kn_ref, va_ref, vat_ref, m_ref, acc_ref,
               *, S, tq, n_sub, kc, hd, vd, lam_init, n_buckets):
    h = pl.program_id(0)
    b = pl.program_id(1)
    i = pl.program_id(2)
    q_scale = hd ** -0.5 * LOG2E

    @pl.when((b == 0) & (i == 0))
    def _():
        def bias_row(bk):
            tab = jnp.zeros((1, 2 * S), F32)
            for n in range(n_buckets):
                tab = jnp.where(bk == n, relb_ref[h, n], tab)
            return tab * LOG2E

        tab = bias_row(bucket_ref[...])
        bmax = jnp.max(tab, axis=-1, keepdims=True)
        bmin = jnp.min(tab, axis=-1, keepdims=True)
        bound = (NORM_SLACK * hd * q_scale) * (jnp.max(jnp.abs(gq_ref[...]), axis=-1, keepdims=True)
                                               * jnp.max(jnp.abs(gk_ref[...]), axis=-1, keepdims=True))
        span = 2.0 * bound + (bmax - bmin)
        fast_ref[0] = (span[0, 0] <= SAFE_LOG2_SPAN).astype(jnp.int32)
        shift = bound + bmax
        e_ref[...] = pltpu.roll(jnp.broadcast_to(tab - shift, (tq, 2 * S)), 0, 1, stride=1, stride_axis=0)
        tab_t = bias_row(bucket_t_ref[...])
        et_ref[...] = pltpu.roll(jnp.broadcast_to(tab_t - shift, (tq, 2 * S)), 0, 1, stride=1, stride_axis=0)

    @pl.when(i == 0)
    def _():
        k = k_ref[...].astype(F32)
        ms = _seg_sumsq(k, segq_ref[...]) * (1.0 / hd)
        kn_ref[...] = (k * lax.rsqrt(ms + EPS) * gk_ref[...]).astype(BF16)
        va_ref[:, :vd] = v_ref[...]
        va_ref[:, vd:] = jnp.ones((S, vd), BF16)
        vat_ref[:vd, :] = v_ref[...].astype(F32).T.astype(BF16)
        vat_ref[vd:, :] = jnp.ones((vat_ref.shape[0] - vd, S), BF16)

    nt_dims = (((1,), (1,)), ((), ()))
    lp = lamp_ref[...]
    lam = (jnp.exp(jnp.sum(lp[0:1] * lp[1:2], axis=-1, keepdims=True))
           - jnp.exp(jnp.sum(lp[2:3] * lp[3:4], axis=-1, keepdims=True)) + lam_init)

    def q_maps_of(t):
        q = q_ref[t * tq:(t + 1) * tq, :].astype(F32)
        ms = _seg_sumsq(q, segq_ref[...]) * (1.0 / hd)
        qn = q * lax.rsqrt(ms + EPS) * (gq_ref[...] * q_scale)
        lo = lax.broadcasted_iota(jnp.int32, qn.shape, 1) < hd
        return jnp.where(lo, qn, 0.0).astype(BF16), jnp.where(lo, 0.0, qn).astype(BF16)

    def e_start(t, k0):
        return pl.multiple_of(S + k0 - (i * n_sub + t) * tq, LANES)

    def finish(t, a1, a2):
        o = a1[:, :vd] / a1[:, vd:] - lam * (a2[:, :vd] / a2[:, vd:])
        o = o * lax.rsqrt(_seg_sumsq(o, segv_ref[...]) * (1.0 / vd) + EPS)
        o_ref[t * tq:(t + 1) * tq, :] = (o * (gs_ref[...] * (1.0 - lam_init))).astype(o_ref.dtype)

    def finish_t(t, a1, a2):
        ot = a1[:vd] / a1[vd:vd + 1] - lam * (a2[:vd] / a2[vd:vd + 1])
        o = ot.T
        o = o * lax.rsqrt(_seg_sumsq(o, segv_ref[...]) * (1.0 / vd) + EPS)
        o_ref[t * tq:(t + 1) * tq, :] = (o * (gs_ref[...] * (1.0 - lam_init))).astype(o_ref.dtype)

    @pl.when(fast_ref[0] == 1)
    def _():
        for t in range(n_sub):
            q_maps = q_maps_of(t)
            tile = i * n_sub + t
            accs = []
            for mi in range(2):
                chunks = []
                for c in range(S // tq):
                    st = lax.dot_general(kn_ref[c * tq:(c + 1) * tq, :], q_maps[mi], nt_dims,
                                         preferred_element_type=F32)
                    start = pl.multiple_of(S + (tile - c) * tq, LANES)
                    chunks.append(jnp.exp2(st + et_ref[:, pl.ds(start, tq)]).astype(BF16))
                accs.append(jnp.dot(vat_ref[...], jnp.concatenate(chunks, axis=0),
                                    preferred_element_type=F32))
            finish_t(t, *accs)

    @pl.when(fast_ref[0] != 1)
    def _():
        for t in range(n_sub):
            q_maps = q_maps_of(t)
            m_ref[...] = jnp.full(m_ref.shape, NEG_BIG, F32)
            acc_ref[...] = jnp.zeros(acc_ref.shape, F32)

            def chunk(c, carry):
                k0 = pl.multiple_of(c * kc, kc)
                bias = e_ref[:, pl.ds(e_start(t, k0), kc)]
                for mi in range(2):
                    s = lax.dot_general(q_maps[mi], kn_ref[pl.ds(k0, kc), :], nt_dims,
                                        preferred_element_type=F32) + bias
                    m_old = m_ref[mi]
                    m_new = jnp.maximum(m_old, jnp.max(s, axis=-1, keepdims=True))
                    p = jnp.exp2(s - m_new).astype(BF16)
                    acc_ref[mi] = (jnp.exp2(m_old - m_new) * acc_ref[mi]
                                   + jnp.dot(p, va_ref[pl.ds(k0, kc), :], preferred_element_type=F32))
                    m_ref[mi] = m_new
                return carry

            lax.fori_loop(0, S // kc, chunk, 0)
            finish(t, acc_ref[0], acc_ref[1])


def _attention(proj, rel_bias, g_q, g_k, lam_params, g_subln, *, B, S, H, hd, vd, off_k, off_v, lam_init):
    T = B * S
    tq = min(512, S)
    kc = min(512, S)
    n_sub = math.gcd(4, S // tq)
    tqb = n_sub * tq
    nq = S // tqb
    n_buckets = rel_bias.shape[0]
    hw = 2 * hd
    bucket = jnp.asarray(_bucket_table(S, n_buckets, np.arange(2 * S) - S))
    bucket_t = jnp.asarray(_bucket_table(S, n_buckets, S - np.arange(2 * S)))
    body = functools.partial(_attn_body, S=S, tq=tq, n_sub=n_sub, kc=kc, hd=hd, vd=vd, lam_init=lam_init,
                             n_buckets=n_buckets)
    kblk = off_k // hw
    vblk = off_v // vd
    seg_q = np.kron(np.eye(2, dtype=np.float32), np.ones((hd, hd), np.float32))
    return pl.pallas_call(
        body,
        grid=(H, B, nq),
        in_specs=[pl.BlockSpec((1, 2 * S), lambda h, b, i: (0, 0)),
                  pl.BlockSpec((1, 2 * S), lambda h, b, i: (0, 0)),
                  pl.BlockSpec(memory_space=pltpu.SMEM),
                  pl.BlockSpec((tqb, hw), lambda h, b, i: (b * nq + i, h)),
                  pl.BlockSpec((S, hw), lambda h, b, i: (b, kblk + h)),
                  pl.BlockSpec((S, vd), lambda h, b, i: (b, vblk + h)),
                  pl.BlockSpec((1, hw), lambda h, b, i: (0, 0)),
                  pl.BlockSpec((1, hw), lambda h, b, i: (0, 0)),
                  pl.BlockSpec((4, hd), lambda h, b, i: (0, 0)),
                  pl.BlockSpec((1, vd), lambda h, b, i: (0, 0)),
                  pl.BlockSpec((hw, hw), lambda h, b, i: (0, 0)),
                  pl.BlockSpec((vd, vd), lambda h, b, i: (0, 0))],
        out_specs=pl.BlockSpec((tqb, vd), lambda h, b, i: (b * nq + i, h)),
        out_shape=jax.ShapeDtypeStruct((T, H * vd), BF16),
        scratch_shapes=[pltpu.VMEM((tq, 2 * S), F32),
                        pltpu.VMEM((tq, 2 * S), F32),
                        pltpu.SMEM((1,), jnp.int32),
                        pltpu.VMEM((S, hw), BF16),
                        pltpu.VMEM((S, 2 * vd), BF16),
                        pltpu.VMEM((vd + 16, S), BF16),
                        pltpu.VMEM((2, tq, 1), F32),
                        pltpu.VMEM((2, tq, 2 * vd), F32)],
        compiler_params=_params("arbitrary", "arbitrary", "arbitrary"),
        name="diff_attention",
    )(bucket, bucket_t, rel_bias.T, proj, proj, proj, jnp.tile(g_q, 2)[None], jnp.tile(g_k, 2)[None],
      lam_params, g_subln[None], jnp.asarray(seg_q, BF16), jnp.ones((vd, vd), BF16))


LRU_SEGMENTS = SUBLANES
LRU_PITCH_PAD = 8


def _sublane_scan(a, bb, reverse):
    ri = lax.broadcasted_iota(jnp.int32, a.shape, 0)
    for d in (1, 2, 4):
        if reverse:
            keep, sh = ri < SUBLANES - d, SUBLANES - d
        else:
            keep, sh = ri >= d, d
        bb = bb + a * jnp.where(keep, pltpu.roll(bb, sh, 0), 0.0)
        a = a * jnp.where(keep, pltpu.roll(a, sh, 0), 1.0)
    return bb


def _lru_body(x_ref, y_ref, cw_ref, cb_ref, w_ref, bias_ref, lam_ref, o_ref,
              af_ref, bf_ref, ab_ref, bb_ref, hf_ref, pf_ref, hb_ref, pb_ref, hs_ref, *, S, W):
    L = S // LRU_SEGMENTS
    pitch = L + LRU_PITCH_PAD
    x = x_ref[...].astype(F32)
    row = lax.broadcasted_iota(jnp.int32, (S, W), 0)
    cw = cw_ref[...]
    xc = (cw[0:1] * jnp.where(row >= 2, pltpu.roll(x, 2, 0), 0.0)
          + cw[1:2] * jnp.where(row >= 1, pltpu.roll(x, 1, 0), 0.0)
          + cw[2:3] * x
          + cw[3:4] * jnp.where(row < S - 1, pltpu.roll(x, S - 1, 0), 0.0)
          + cb_ref[...])
    gates = jnp.dot(xc.astype(BF16), w_ref[0], preferred_element_type=F32) + bias_ref[0]
    lam = lam_ref[...]
    for d, (a_ref, b_ref) in enumerate(((af_ref, bf_ref), (ab_ref, bb_ref))):
        r = jax.nn.sigmoid(gates[:, (2 * d) * W:(2 * d + 1) * W])
        ig = jax.nn.sigmoid(gates[:, (2 * d + 1) * W:(2 * d + 2) * W])
        nl = -lam[d:d + 1]
        softplus = jnp.maximum(nl, 0.0) + jnp.log1p(jnp.exp(-jnp.abs(nl)))
        a = jnp.exp2((-LRU_C * LOG2E * softplus) * r)
        y2 = 1.0 - a * a
        bvals = jnp.where(y2 > 0.0, y2 * lax.rsqrt(y2), 0.0) * ig * xc
        for sg in range(LRU_SEGMENTS):
            a_ref[sg * pitch:sg * pitch + L, :] = a[sg * L:(sg + 1) * L]
            b_ref[sg * pitch:sg * pitch + L, :] = bvals[sg * L:(sg + 1) * L]

    def seg_rows(ref, j):
        return ref[pl.ds(j, LRU_SEGMENTS, stride=pitch), :]

    def step(j, carry):
        hf, pf, hb, pb = carry
        a = seg_rows(af_ref, j)
        hf = a * hf + seg_rows(bf_ref, j)
        pf = a * pf
        hf_ref[j] = hf
        pf_ref[j] = pf
        jb = L - 1 - j
        a = seg_rows(ab_ref, jb)
        hb = a * hb + seg_rows(bb_ref, jb)
        pb = a * pb
        hb_ref[jb] = hb
        pb_ref[jb] = pb
        return hf, pf, hb, pb

    zero = jnp.zeros((LRU_SEGMENTS, W), F32)
    one = jnp.ones((LRU_SEGMENTS, W), F32)
    hf, pf, hb, pb = lax.fori_loop(0, L, step, (zero, one, zero, one), unroll=8)

    si = lax.broadcasted_iota(jnp.int32, (LRU_SEGMENTS, W), 0)
    cf = jnp.where(si >= 1, pltpu.roll(_sublane_scan(pf, hf, False), 1, 0), 0.0)
    cb = jnp.where(si < LRU_SEGMENTS - 1,
                   pltpu.roll(_sublane_scan(pb, hb, True), LRU_SEGMENTS - 1, 0), 0.0)

    def fix(j, carry):
        hs_ref[pl.ds(j, LRU_SEGMENTS, stride=pitch), :] = (
            hf_ref[j] + pf_ref[j] * cf + hb_ref[j] + pb_ref[j] * cb)
        return carry

    lax.fori_loop(0, L, fix, 0, unroll=8)

    for sg in range(LRU_SEGMENTS):
        y = y_ref[sg * L:(sg + 1) * L, :].astype(F32)
        o_ref[sg * L:(sg + 1) * L, :] = (hs_ref[sg * pitch:sg * pitch + L, :]
                                         * jax.nn.gelu(y)).astype(o_ref.dtype)


def _rg_lru(proj, conv_w, conv_b, w_cat, b_cat, lru_lambda, *, B, S, off_x, off_y):
    T = B * S
    NB, W, _ = w_cat.shape
    xblk = off_x // W
    yblk = off_y // W
    body = functools.partial(_lru_body, S=S, W=W)
    padded = LRU_SEGMENTS * (S // LRU_SEGMENTS + LRU_PITCH_PAD)
    return pl.pallas_call(
        body,
        grid=(B, NB),
        in_specs=[pl.BlockSpec((S, W), lambda b, n: (b, xblk + n)),
                  pl.BlockSpec((S, W), lambda b, n: (b, yblk + n)),
                  pl.BlockSpec((conv_w.shape[0], W), lambda b, n: (0, n)),
                  pl.BlockSpec((1, W), lambda b, n: (0, n)),
                  pl.BlockSpec((1, W, 4 * W), lambda b, n: (n, 0, 0)),
                  pl.BlockSpec((1, 1, 4 * W), lambda b, n: (n, 0, 0)),
                  pl.BlockSpec((2, W), lambda b, n: (0, n))],
        out_specs=pl.BlockSpec((S, W), lambda b, n: (b, n)),
        out_shape=jax.ShapeDtypeStruct((T, NB * W), BF16),
        scratch_shapes=([pltpu.VMEM((padded, W), F32)] * 4
                        + [pltpu.VMEM((S // LRU_SEGMENTS, LRU_SEGMENTS, W), F32)] * 4
                        + [pltpu.VMEM((padded, W), F32)]),
        compiler_params=_params("parallel", "parallel"),
        name="rg_lru",
    )(proj, proj, conv_w, conv_b[None], w_cat, b_cat, lru_lambda)


def _merge_body(o_ref, r_ref, ga_ref, gr_ref, x_ref, wpa_ref, wpl_ref, wo_ref, g_ref, wr_ref,
                h_ref, hn_ref, aff_ref, *, E, n_sub):
    ts = o_ref.shape[0] // n_sub
    for t in range(n_sub):
        rows = slice(t * ts, (t + 1) * ts)
        ba = jnp.dot(o_ref[rows, :], wpa_ref[...], preferred_element_type=F32)
        br = jnp.dot(r_ref[rows, :], wpl_ref[...], preferred_element_type=F32)
        mixed = (jax.nn.sigmoid(ga_ref[rows, :].astype(F32)) * ba
                 + jax.nn.sigmoid(gr_ref[rows, :].astype(F32)) * br)
        h = x_ref[rows, :] + jnp.dot(mixed.astype(BF16), wo_ref[...], preferred_element_type=F32)
        h_ref[rows, :] = h
        hn = h * lax.rsqrt(jnp.mean(h * h, axis=-1, keepdims=True) + EPS) * g_ref[...]
        hn_hi = hn.astype(BF16)
        hn_ref[rows, :] = hn_hi
        hn_lo = (hn - hn_hi.astype(F32)).astype(BF16)
        lg = (jnp.dot(hn_hi, wr_ref[...], preferred_element_type=F32)
              + jnp.dot(hn_lo, wr_ref[...], preferred_element_type=F32))
        logits = lg[:, :E] + lg[:, E:2 * E]
        logits = logits - jnp.max(logits, axis=-1, keepdims=True)
        ex = jnp.exp(logits)
        aff_ref[rows, :] = ex / jnp.sum(ex, axis=-1, keepdims=True)


def _merge(o_attn, lru_out, proj, x2, wpa, wpl, wo, g_ffn, wr2, *, off_ga, off_gr, E):
    T, D = x2.shape
    tm = min(1024, T)
    n_sub = 2 if tm % 1024 == 0 else 1
    resident = dict(pipeline_mode=pl.Buffered(1))
    gab = off_ga // D
    grb = off_gr // D
    row = lambda i: (i, 0)
    const = lambda i: (0, 0)
    return pl.pallas_call(
        functools.partial(_merge_body, E=E, n_sub=n_sub),
        grid=(T // tm,),
        in_specs=[pl.BlockSpec((tm, D), row), pl.BlockSpec((tm, D), row),
                  pl.BlockSpec((tm, D), lambda i: (i, gab)), pl.BlockSpec((tm, D), lambda i: (i, grb)),
                  pl.BlockSpec((tm, D), row),
                  pl.BlockSpec((D, D), const, **resident), pl.BlockSpec((D, D), const, **resident),
                  pl.BlockSpec((D, D), const, **resident),
                  pl.BlockSpec((1, D), const), pl.BlockSpec((D, 2 * E), const)],
        out_specs=[pl.BlockSpec((tm, D), row), pl.BlockSpec((tm, D), row), pl.BlockSpec((tm, E), row)],
        out_shape=[jax.ShapeDtypeStruct((T, D), F32), jax.ShapeDtypeStruct((T, D), BF16),
                   jax.ShapeDtypeStruct((T, E), F32)],
        compiler_params=_params("parallel"),
        name="merge_router",
    )(o_attn, lru_out, proj, proj, x2, wpa, wpl, wo, g_ffn, wr2)


SELECT_SEQS_PER_STEP = 8


def _select_body(aff_ref, pos_ref, cnt_ref, *, S, C):
    rows = aff_ref.shape[0] * aff_ref.shape[1]
    bits = pltpu.bitcast(aff_ref[...].reshape(rows, S), jnp.int32)

    def refine(k, t):
        cand = t | jnp.left_shift(jnp.int32(1), 30 - k)
        cnt = jnp.sum(jnp.where(bits >= cand, 1.0, 0.0), axis=-1, keepdims=True)
        return jnp.where(cnt >= C, cand, t)

    t = lax.fori_loop(0, 31, refine, jnp.zeros((rows, 1), jnp.int32))
    gt = bits > t
    eq = bits == t
    need = C - jnp.sum(jnp.where(gt, 1.0, 0.0), axis=-1, keepdims=True).astype(jnp.int32)
    packed = jnp.where(gt, 1 << 16, 0) + jnp.where(eq, 1, 0)
    lane = lax.broadcasted_iota(jnp.int32, (rows, S), 1)
    incl = packed
    d = 1
    while d < S:
        incl = incl + jnp.where(lane >= d, pltpu.roll(incl, d, 1), 0)
        d *= 2
    excl = incl - packed
    n_gt = excl >> 16
    n_eq = excl & 0xFFFF
    sel = gt | (eq & (n_eq < need))
    before = n_gt + jnp.minimum(n_eq, need)
    cnt_ref[...] = before.reshape(cnt_ref.shape)
    pos_ref[...] = jnp.where(sel, before, -1).reshape(pos_ref.shape)


def _select(aff_t, C):
    B, E, S = aff_t.shape
    nb = math.gcd(SELECT_SEQS_PER_STEP, B)
    spec = pl.BlockSpec((nb, E, S), lambda b: (b, 0, 0))
    return pl.pallas_call(
        functools.partial(_select_body, S=S, C=C),
        grid=(B // nb,),
        in_specs=[spec],
        out_specs=[spec, spec],
        out_shape=[jax.ShapeDtypeStruct((B, E, S), jnp.int32)] * 2,
        compiler_params=_params("parallel"),
        name="topc_select",
    )(aff_t)


GATHER_WIN = 64
SLOT_ALIGN = 16


def _gather_body(lo_ref, pos_ref, hn_ref, xg_ref, *, E, C, S, ts):
    b = pl.program_id(0)
    n_tiles = S // ts
    xg_ref[...] = jnp.zeros(xg_ref.shape, xg_ref.dtype)
    sub = lax.broadcasted_iota(jnp.int32, (GATHER_WIN, ts), 0)

    def tile_body(k, carry):
        t0 = pl.multiple_of(k * ts, ts)
        nxt = jnp.minimum(k + 1, n_tiles - 1)
        starts = [(lo_ref[b, k, e] // SLOT_ALIGN) * SLOT_ALIGN for e in range(E)]
        ends = [jnp.where(k + 1 < n_tiles, lo_ref[b, nxt, e], C) for e in range(E)]
        hn_t = hn_ref[0, pl.ds(t0, ts), :]

        def one_pass(w, first):
            sel, row0s = [], []
            for e in range(E):
                start = starts[e] + w * GATHER_WIN
                row0 = jnp.minimum(start, C - GATHER_WIN)
                pe = pos_ref[0, e:e + 1, pl.ds(t0, ts)]
                if not first:
                    pe = jnp.where(pe >= start, pe, -1)
                sel.append(jnp.where(pe - row0 == sub, 1.0, 0.0).astype(BF16))
                row0s.append(pl.multiple_of(row0, SLOT_ALIGN))
            rows = jnp.dot(jnp.concatenate(sel, axis=0), hn_t,
                           preferred_element_type=F32).astype(xg_ref.dtype)
            for e in range(E):
                xg_ref[0, e, pl.ds(row0s[e], GATHER_WIN), :] += rows[e * GATHER_WIN:(e + 1) * GATHER_WIN]

        one_pass(0, True)
        n_pass = jnp.int32(1)
        for e in range(E):
            n_pass = jnp.maximum(n_pass, (ends[e] - starts[e] + GATHER_WIN - 1) // GATHER_WIN)

        def more(w, c):
            one_pass(w, False)
            return c

        lax.fori_loop(1, n_pass, more, 0)
        return carry

    lax.fori_loop(0, n_tiles, tile_body, 0)


def _gather(lo, pos_rows, hn3, C, ts):
    B, S, D = hn3.shape
    E = pos_rows.shape[1]
    assert C % GATHER_WIN == 0
    return pl.pallas_call(
        functools.partial(_gather_body, E=E, C=C, S=S, ts=ts),
        grid=(B,),
        in_specs=[pl.BlockSpec(memory_space=pltpu.SMEM),
                  pl.BlockSpec((1, E, S), lambda b: (b, 0, 0)),
                  pl.BlockSpec((1, S, D), lambda b: (b, 0, 0))],
        out_specs=pl.BlockSpec((1, E, C, D), lambda b: (b, 0, 0, 0)),
        out_shape=jax.ShapeDtypeStruct((B, E, C, D), BF16),
        compiler_params=_params("parallel"),
        name="moe_gather",
    )(lo, pos_rows, hn3)


def _ffn_body(pos_ref, aff_ref, xg_ref, wg_hbm, wu_hbm, wd_hbm, y_ref,
              wg_s, wu_s, wd_s, stg_g, stg_u, stg_d, sems, *, C, S, fc):
    e = pl.program_id(0)
    b = pl.program_id(1)
    n_experts = pl.num_programs(0)
    rows_in = stg_g.shape[0]
    rows_dn = stg_d.shape[0]
    n_chunks = wg_s.shape[1] // rows_in

    def chunk_copies(ee, k):
        r_in = pl.multiple_of(k * rows_in, rows_in)
        r_dn = pl.multiple_of(k * rows_dn, rows_dn)
        return (pltpu.make_async_copy(wg_hbm.at[ee, pl.ds(r_in, rows_in), :], stg_g, sems.at[0]),
                pltpu.make_async_copy(wu_hbm.at[ee, pl.ds(r_in, rows_in), :], stg_u, sems.at[1]),
                pltpu.make_async_copy(wd_hbm.at[ee, pl.ds(r_dn, rows_dn), :], stg_d, sems.at[2]))

    def cast_chunk(slot_, k):
        r_in = pl.multiple_of(k * rows_in, rows_in)
        r_dn = pl.multiple_of(k * rows_dn, rows_dn)
        wg_s[slot_, pl.ds(r_in, rows_in), :] = stg_g[...].astype(BF16)
        wu_s[slot_, pl.ds(r_in, rows_in), :] = stg_u[...].astype(BF16)
        wd_s[slot_, pl.ds(r_dn, rows_dn), :] = stg_d[...].astype(BF16)

    @pl.when((e == 0) & (b == 0))
    def _():
        def stage(k, carry):
            for cp in chunk_copies(0, k):
                cp.start()
            for cp in chunk_copies(0, k):
                cp.wait()
            cast_chunk(0, k)
            return carry

        lax.fori_loop(0, n_chunks, stage, 0)

        @pl.when(n_experts > 1)
        def _():
            for cp in chunk_copies(1, 0):
                cp.start()

    cur = e % 2
    nb = xg_ref.shape[0]
    slot = lax.broadcasted_iota(jnp.int32, (C, S), 0)
    vals = [jnp.sum(jnp.where(pos_ref[t, 0] == slot, aff_ref[t, 0], 0.0), axis=1, keepdims=True)
            for t in range(nb)]
    val = jnp.concatenate(vals, axis=0)
    xg = xg_ref[...].reshape(nb * C, xg_ref.shape[3])
    F = wg_s.shape[2]
    y = jnp.zeros(xg.shape, F32)
    for f0 in range(0, F, fc):
        g = jnp.dot(xg, wg_s[cur, :, f0:f0 + fc], preferred_element_type=F32)
        u = jnp.dot(xg, wu_s[cur, :, f0:f0 + fc], preferred_element_type=F32)
        hid = (g * jax.nn.sigmoid(g) * u).astype(BF16)
        y = y + jnp.dot(hid, wd_s[cur, f0:f0 + fc, :], preferred_element_type=F32)
    y_ref[...] = (y * val).astype(y_ref.dtype).reshape(y_ref.shape)

    @pl.when(e + 1 < n_experts)
    def _():
        for cp in chunk_copies(e + 1, b):
            cp.wait()
        cast_chunk(1 - cur, b)

    wrap = b + 1 == n_chunks
    e_nxt = jnp.where(wrap, e + 2, e + 1)
    k_nxt = jnp.where(wrap, 0, b + 1)

    @pl.when(e_nxt < n_experts)
    def _():
        for cp in chunk_copies(e_nxt, k_nxt):
            cp.start()


FFN_SEQS_PER_STEP = 2


def _expert_ffn(pos_rows, aff_rows, xg, wg, wu, wd):
    B, _, C, D = xg.shape
    S = pos_rows.shape[-1]
    E, _, F = wg.shape
    fc = min(512, F)
    nb = math.gcd(FFN_SEQS_PER_STEP, B)
    steps = B // nb
    assert D % steps == 0 and F % steps == 0 and (D // steps) % 16 == 0 and (F // steps) % 16 == 0
    hbm = pl.BlockSpec(memory_space=pl.ANY)
    return pl.pallas_call(
        functools.partial(_ffn_body, C=C, S=S, fc=fc),
        grid=(E, steps),
        in_specs=[pl.BlockSpec((nb, 1, 1, S), lambda e, b: (b, e, 0, 0)),
                  pl.BlockSpec((nb, 1, 1, S), lambda e, b: (b, e, 0, 0)),
                  pl.BlockSpec((nb, 1, C, D), lambda e, b: (b, e, 0, 0)),
                  hbm, hbm, hbm],
        out_specs=pl.BlockSpec((nb, 1, C, D), lambda e, b: (b, e, 0, 0)),
        out_shape=jax.ShapeDtypeStruct((B, E, C, D), BF16),
        scratch_shapes=[pltpu.VMEM((2, D, F), BF16), pltpu.VMEM((2, D, F), BF16),
                        pltpu.VMEM((2, F, D), BF16),
                        pltpu.VMEM((D // steps, F), F32), pltpu.VMEM((D // steps, F), F32),
                        pltpu.VMEM((F // steps, D), F32),
                        pltpu.SemaphoreType.DMA((3,))],
        compiler_params=_params("arbitrary", "arbitrary"),
        name="expert_ffn",
    )(pos_rows, aff_rows, xg, wg, wu, wd)


COMBINE_WIN = 64


def _combine_body(lo_ref, pos_ref, y_ref, h_ref, o_ref, *, E, C, ts, n_tiles):
    b = pl.program_id(0)
    n_sub = pos_ref.shape[1] // ts
    lane = lax.broadcasted_iota(jnp.int32, (ts, LANES), 1)
    upper = lane >= COMBINE_WIN
    lane_in = jnp.where(upper, lane - COMBINE_WIN, lane)

    extra = []
    for t in range(n_sub):
        tile = pl.program_id(1) * n_sub + t
        rows = slice(t * ts, (t + 1) * ts)
        pc = pos_ref[0, rows, :]
        nxt = jnp.minimum(tile + 1, n_tiles - 1)
        starts = [(lo_ref[b, tile, e] // SLOT_ALIGN) * SLOT_ALIGN for e in range(E)]
        ends = [jnp.where(tile + 1 < n_tiles, lo_ref[b, nxt, e], C) for e in range(E)]

        def window_product(w, first, pc=pc, starts=starts):
            pieces, y_wins = [], []
            for p in range(E // 2):
                targets = []
                for e in (2 * p, 2 * p + 1):
                    start = starts[e] + w * COMBINE_WIN
                    row0 = jnp.minimum(start, C - COMBINE_WIN)
                    pe = pc[:, e:e + 1]
                    if not first:
                        pe = jnp.where(pe >= start, pe, -1)
                    targets.append(pe - row0)
                    y_wins.append(y_ref[0, e, pl.ds(pl.multiple_of(row0, SLOT_ALIGN), COMBINE_WIN), :])
                hit = jnp.where(upper, targets[1], targets[0]) == lane_in
                pieces.append(jnp.where(hit, 1.0, 0.0).astype(BF16))
            return jnp.dot(jnp.concatenate(pieces, axis=1), jnp.concatenate(y_wins, axis=0),
                           preferred_element_type=F32)

        o_ref[0, rows, :] = h_ref[0, rows, :] + window_product(0, True)

        n_pass = jnp.int32(1)
        for e in range(E):
            n_pass = jnp.maximum(n_pass, (ends[e] - starts[e] + COMBINE_WIN - 1) // COMBINE_WIN)
        extra.append((n_pass, rows, window_product))

    for n_pass, rows, window_product in extra:
        def more(w, carry, rows=rows, window_product=window_product):
            o_ref[0, rows, :] += window_product(w, False)
            return carry

        lax.fori_loop(1, n_pass, more, 0)


def _combine(lo, pos_cols, y, h3, ts):
    B, S, D = h3.shape
    _, E, C, _ = y.shape
    n_tiles = S // ts
    n_sub = math.gcd(4, n_tiles)
    tb = ts * n_sub
    assert E % 2 == 0 and C % COMBINE_WIN == 0 and 2 * COMBINE_WIN == LANES
    return pl.pallas_call(
        functools.partial(_combine_body, E=E, C=C, ts=ts, n_tiles=n_tiles),
        grid=(B, n_tiles // n_sub),
        in_specs=[pl.BlockSpec(memory_space=pltpu.SMEM),
                  pl.BlockSpec((1, tb, E), lambda b, i: (b, i, 0)),
                  pl.BlockSpec((1, E, C, D), lambda b, i: (b, 0, 0, 0)),
                  pl.BlockSpec((1, tb, D), lambda b, i: (b, i, 0))],
        out_specs=pl.BlockSpec((1, tb, D), lambda b, i: (b, i, 0)),
        out_shape=jax.ShapeDtypeStruct((B, S, D), F32),
        compiler_params=_params("parallel", "arbitrary"),
        name="moe_combine",
    )(lo, pos_cols, y, h3)


def kernel(x, g_mix, w_in, g_q, g_k, lam_q1, lam_k1, lam_q2, lam_k2, g_subln, rel_bias, conv_w, conv_b,
           gate_r_w, gate_r_b, gate_i_w, gate_i_b, lru_lambda, w_proj_attn, w_proj_lru, w_out, g_ffn,
           w_router, w_gate_e, w_up_e, w_down_e):
    B, S, D = x.shape
    depth = w_in.shape[0]
    H = rel_bias.shape[1]
    hd = g_q.shape[-1]
    vd = g_subln.shape[-1]
    qk_w = H * 2 * hd
    attn_w = H * vd
    lru_w = conv_w.shape[-1]
    NB, LB = gate_r_w.shape[2], gate_r_w.shape[3]
    E = w_router.shape[-1]
    C = EC_CAPACITY_FACTOR * S // E
    off_k = qk_w
    off_v = off_k + qk_w
    off_x = off_v + attn_w
    off_y = off_x + lru_w
    off_ga = off_y + lru_w
    off_gr = off_ga + D

    h2 = x.reshape(B * S, D)
    for layer in range(depth):
        lam_init = 0.8 - 0.6 * math.exp(-0.3 * layer)
        proj = _inproj(h2, g_mix[layer][None], w_in[layer].astype(BF16))

        lam_params = jnp.stack([lam_q1[layer], lam_k1[layer], lam_q2[layer], lam_k2[layer]])
        o_attn = _attention(proj, rel_bias, g_q[layer], g_k[layer], lam_params, g_subln[layer],
                            B=B, S=S, H=H, hd=hd, vd=vd, off_k=off_k, off_v=off_v, lam_init=lam_init)

        w_cat = jnp.concatenate([gate_r_w[layer, 0], gate_i_w[layer, 0],
                                 gate_r_w[layer, 1], gate_i_w[layer, 1]], axis=-1).astype(BF16)
        b_cat = jnp.stack([gate_r_b[layer, 0].reshape(NB, LB), gate_i_b[layer, 0].reshape(NB, LB),
                           gate_r_b[layer, 1].reshape(NB, LB), gate_i_b[layer, 1].reshape(NB, LB)],
                          axis=1).reshape(NB, 1, 4 * LB)
        lru_out = _rg_lru(proj, conv_w[layer], conv_b[layer], w_cat, b_cat, lru_lambda[layer],
                          B=B, S=S, off_x=off_x, off_y=off_y)

        wr = w_router[layer]
        wr_hi = wr.astype(BF16)
        wr2 = jnp.concatenate([wr_hi, (wr - wr_hi.astype(F32)).astype(BF16)], axis=1)
        h2, hn, aff = _merge(o_attn, lru_out, proj, h2, w_proj_attn[layer].astype(BF16),
                             w_proj_lru[layer].astype(BF16), w_out[layer].astype(BF16),
                             g_ffn[layer][None], wr2, off_ga=off_ga, off_gr=off_gr, E=E)

        aff_rows = aff.reshape(B, S, E).transpose(0, 2, 1)
        pos_rows, cnt_rows = _select(aff_rows, C)
        ts = min(256, S)
        lo = cnt_rows[:, :, ::ts].transpose(0, 2, 1)
        xg = _gather(lo, pos_rows, hn.reshape(B, S, D), C, ts)
        y = _expert_ffn(pos_rows.reshape(B, E, 1, S), aff_rows.reshape(B, E, 1, S), xg,
                        w_gate_e[layer], w_up_e[layer], w_down_e[layer])
        h3 = _combine(lo, pos_rows.transpose(0, 2, 1), y, h2.reshape(B, S, D), ts)
        h2 = h3.reshape(B * S, D)
    return h2.reshape(B, S, D)
```

```python
import functools
import math

import jax
import jax.numpy as jnp
import numpy as np
from jax import lax
from jax.experimental import pallas as pl
from jax.experimental.pallas import tpu as pltpu

F32 = jnp.float32
BF16 = jnp.bfloat16
EPS = 1e-6
LANES = 128
SUBLANES = 8
VMEM_LIMIT = 56 * 1024 * 1024
LOG2E = 1.4426950408889634
NEG_BIG = -1e30
REL_MAX_DIST = 128
LRU_C = 8.0
EC_CAPACITY_FACTOR = 2


def _params(*sem):
    return pltpu.CompilerParams(dimension_semantics=sem, vmem_limit_bytes=VMEM_LIMIT)


def _inproj_body(x_ref, g_ref, w_ref, o_ref, xn_ref):
    @pl.when(pl.program_id(1) == 0)
    def _():
        x = x_ref[...]
        ms = jnp.mean(x * x, axis=-1, keepdims=True)
        xn_ref[...] = (x * lax.rsqrt(ms + EPS) * g_ref[...]).astype(BF16)

    o_ref[...] = jnp.dot(xn_ref[...], w_ref[...], preferred_element_type=F32).astype(o_ref.dtype)


def _inproj(x2, g, w_bf):
    T, D = x2.shape
    N = w_bf.shape[1]
    tm = min(1024, T)
    tn = N // 4 if N % (4 * LANES) == 0 else N
    return pl.pallas_call(
        _inproj_body,
        grid=(T // tm, N // tn),
        in_specs=[pl.BlockSpec((tm, D), lambda i, j: (i, 0)),
                  pl.BlockSpec((1, D), lambda i, j: (0, 0)),
                  pl.BlockSpec((D, tn), lambda i, j: (0, j))],
        out_specs=pl.BlockSpec((tm, tn), lambda i, j: (i, j)),
        out_shape=jax.ShapeDtypeStruct((T, N), BF16),
        scratch_shapes=[pltpu.VMEM((tm, D), BF16)],
        compiler_params=_params("parallel", "arbitrary"),
        name="in_proj",
    )(x2, g, w_bf)


def _bucket_table(S, n_buckets, rel):
    half = n_buckets // 2
    max_exact = half // 2
    ret = np.where(rel > 0, half, 0)
    n = np.abs(rel)
    nf = np.maximum(n, max_exact).astype(np.float64)
    large = max_exact + (np.log(nf / max_exact) / math.log(REL_MAX_DIST / max_exact)
                         * (half - max_exact)).astype(np.int32)
    large = np.minimum(large, half - 1)
    return (ret + np.where(n < max_exact, n, large)).astype(np.int32).reshape(1, 2 * S)


def _seg_sumsq(x, seg_ones):
    x2 = x * x
    hi = x2.astype(BF16)
    lo = (x2 - hi.astype(F32)).astype(BF16)
    return (jnp.dot(hi, seg_ones, preferred_element_type=F32)
            + jnp.dot(lo, seg_ones, preferred_element_type=F32))


NORM_SLACK = 1.01
SAFE_LOG2_SPAN = 100.0


def _attn_body(bucket_ref, bucket_t_ref, relb_ref, q_ref, k_ref, v_ref, gq_ref, gk_ref, lamp_ref, gs_ref,
               segq_ref, segv_ref, o_ref, e_ref, et_ref, fast_ref, kn_ref, va_ref, vat_ref, m_ref, acc_ref,
               *, S, tq, n_sub, kc, hd, vd, lam_init, n_buckets):
    h = pl.program_id(0)
    b = pl.program_id(1)
    i = pl.program_id(2)
    q_scale = hd ** -0.5 * LOG2E

    @pl.when((b == 0) & (i == 0))
    def _():
        def bias_row(bk):
            tab = jnp.zeros((1, 2 * S), F32)
            for n in range(n_buckets):
                tab = jnp.where(bk == n, relb_ref[h, n], tab)
            return tab * LOG2E

        tab = bias_row(bucket_ref[...])
        bmax = jnp.max(tab, axis=-1, keepdims=True)
        bmin = jnp.min(tab, axis=-1, keepdims=True)
        bound = (NORM_SLACK * hd * q_scale) * (jnp.max(jnp.abs(gq_ref[...]), axis=-1, keepdims=True)
                                               * jnp.max(jnp.abs(gk_ref[...]), axis=-1, keepdims=True))
        span = 2.0 * bound + (bmax - bmin)
        fast_ref[0] = (span[0, 0] <= SAFE_LOG2_SPAN).astype(jnp.int32)
        shift = bound + bmax
        e_ref[...] = pltpu.roll(jnp.broadcast_to(tab - shift, (tq, 2 * S)), 0, 1, stride=1, stride_axis=0)
        tab_t = bias_row(bucket_t_ref[...])
        et_ref[...] = pltpu.roll(jnp.broadcast_to(tab_t - shift, (tq, 2 * S)), 0, 1, stride=1, stride_axis=0)

    @pl.when(i == 0)
    def _():
        k = k_ref[...].astype(F32)
        ms = _seg_sumsq(k, segq_ref[...]) * (1.0 / hd)
        kn_ref[...] = (k * lax.rsqrt(ms + EPS) * gk_ref[...]).astype(BF16)
        va_ref[:, :vd] = v_ref[...]
        va_ref[:, vd:] = jnp.ones((S, vd), BF16)
        vat_ref[:vd, :] = v_ref[...].astype(F32).T.astype(BF16)
        vat_ref[vd:, :] = jnp.ones((vat_ref.shape[0] - vd, S), BF16)

    nt_dims = (((1,), (1,)), ((), ()))
    lp = lamp_ref[...]
    lam = (jnp.exp(jnp.sum(lp[0:1] * lp[1:2], axis=-1, keepdims=True))
           - jnp.exp(jnp.sum(lp[2:3] * lp[3:4], axis=-1, keepdims=True)) + lam_init)

    def q_maps_of(t):
        q = q_ref[t * tq:(t + 1) * tq, :].astype(F32)
        ms = _seg_sumsq(q, segq_ref[...]) * (1.0 / hd)
        qn = q * lax.rsqrt(ms + EPS) * (gq_ref[...] * q_scale)
        lo = lax.broadcasted_iota(jnp.int32, qn.shape, 1) < hd
        return jnp.where(lo, qn, 0.0).astype(BF16), jnp.where(lo, 0.0, qn).astype(BF16)

    def e_start(t, k0):
        return pl.multiple_of(S + k0 - (i * n_sub + t) * tq, LANES)

    def finish(t, a1, a2):
        o = a1[:, :vd] / a1[:, vd:] - lam * (a2[:, :vd] / a2[:, vd:])
        o = o * lax.rsqrt(_seg_sumsq(o, segv_ref[...]) * (1.0 / vd) + EPS)
        o_ref[t * tq:(t + 1) * tq, :] = (o * (gs_ref[...] * (1.0 - lam_init))).astype(o_ref.dtype)

    def finish_t(t, a1, a2):
        ot = a1[:vd] / a1[vd:vd + 1] - lam * (a2[:vd] / a2[vd:vd + 1])
        o = ot.T
        o = o * lax.rsqrt(_seg_sumsq(o, segv_ref[...]) * (1.0 / vd) + EPS)
        o_ref[t * tq:(t + 1) * tq, :] = (o * (gs_ref[...] * (1.0 - lam_init))).astype(o_ref.dtype)

    @pl.when(fast_ref[0] == 1)
    def _():
        for t in range(n_sub):
            q_maps = q_maps_of(t)
            tile = i * n_sub + t
            accs = []
            for mi in range(2):
                chunks = []
                for c in range(S // tq):
                    st = lax.dot_general(kn_ref[c * tq:(c + 1) * tq, :], q_maps[mi], nt_dims,
                                         preferred_element_type=F32)
                    start = pl.multiple_of(S + (tile - c) * tq, LANES)
                    chunks.append(jnp.exp2(st + et_ref[:, pl.ds(start, tq)]).astype(BF16))
                accs.append(jnp.dot(vat_ref[...], jnp.concatenate(chunks, axis=0),
                                    preferred_element_type=F32))
            finish_t(t, *accs)

    @pl.when(fast_ref[0] != 1)
    def _():
        for t in range(n_sub):
            q_maps = q_maps_of(t)
            m_ref[...] = jnp.full(m_ref.shape, NEG_BIG, F32)
            acc_ref[...] = jnp.zeros(acc_ref.shape, F32)

            def chunk(c, carry):
                k0 = pl.multiple_of(c * kc, kc)
                bias = e_ref[:, pl.ds(e_start(t, k0), kc)]
                for mi in range(2):
                    s = lax.dot_general(q_maps[mi], kn_ref[pl.ds(k0, kc), :], nt_dims,
                                        preferred_element_type=F32) + bias
                    m_old = m_ref[mi]
                    m_new = jnp.maximum(m_old, jnp.max(s, axis=-1, keepdims=True))
                    p = jnp.exp2(s - m_new).astype(BF16)
                    acc_ref[mi] = (jnp.exp2(m_old - m_new) * acc_ref[mi]
                                   + jnp.dot(p, va_ref[pl.ds(k0, kc), :], preferred_element_type=F32))
                    m_ref[mi] = m_new
                return carry

            lax.fori_loop(0, S // kc, chunk, 0)
            finish(t, acc_ref[0], acc_ref[1])


def _attention(proj, rel_bias, g_q, g_k, lam_params, g_subln, *, B, S, H, hd, vd, off_k, off_v, lam_init):
    T = B * S
    tq = min(512, S)
    kc = min(512, S)
    n_sub = math.gcd(4, S // tq)
    tqb = n_sub * tq
    nq = S // tqb
    n_buckets = rel_bias.shape[0]
    hw = 2 * hd
    bucket = jnp.asarray(_bucket_table(S, n_buckets, np.arange(2 * S) - S))
    bucket_t = jnp.asarray(_bucket_table(S, n_buckets, S - np.arange(2 * S)))
    body = functools.partial(_attn_body, S=S, tq=tq, n_sub=n_sub, kc=kc, hd=hd, vd=vd, lam_init=lam_init,
                             n_buckets=n_buckets)
    kblk = off_k // hw
    vblk = off_v // vd
    seg_q = np.kron(np.eye(2, dtype=np.float32), np.ones((hd, hd), np.float32))
    return pl.pallas_call(
        body,
        grid=(H, B, nq),
        in_specs=[pl.BlockSpec((1, 2 * S), lambda h, b, i: (0, 0)),
                  pl.BlockSpec((1, 2 * S), lambda h, b, i: (0, 0)),
                  pl.BlockSpec(memory_space=pltpu.SMEM),
                  pl.BlockSpec((tqb, hw), lambda h, b, i: (b * nq + i, h)),
                  pl.BlockSpec((S, hw), lambda h, b, i: (b, kblk + h)),
                  pl.BlockSpec((S, vd), lambda h, b, i: (b, vblk + h)),
                  pl.BlockSpec((1, hw), lambda h, b, i: (0, 0)),
                  pl.BlockSpec((1, hw), lambda h, b, i: (0, 0)),
                  pl.BlockSpec((4, hd), lambda h, b, i: (0, 0)),
                  pl.BlockSpec((1, vd), lambda h, b, i: (0, 0)),
                  pl.BlockSpec((hw, hw), lambda h, b, i: (0, 0)),
                  pl.BlockSpec((vd, vd), lambda h, b, i: (0, 0))],
        out_specs=pl.BlockSpec((tqb, vd), lambda h, b, i: (b * nq + i, h)),
        out_shape=jax.ShapeDtypeStruct((T, H * vd), BF16),
        scratch_shapes=[pltpu.VMEM((tq, 2 * S), F32),
                        pltpu.VMEM((tq, 2 * S), F32),
                        pltpu.SMEM((1,), jnp.int32),
                        pltpu.VMEM((S, hw), BF16),
                        pltpu.VMEM((S, 2 * vd), BF16),
                        pltpu.VMEM((vd + 16, S), BF16),
                        pltpu.VMEM((2, tq, 1), F32),
                        pltpu.VMEM((2, tq, 2 * vd), F32)],
        compiler_params=_params("arbitrary", "arbitrary", "arbitrary"),
        name="diff_attention",
    )(bucket, bucket_t, rel_bias.T, proj, proj, proj, jnp.tile(g_q, 2)[None], jnp.tile(g_k, 2)[None],
      lam_params, g_subln[None], jnp.asarray(seg_q, BF16), jnp.ones((vd, vd), BF16))


LRU_SEGMENTS = SUBLANES
LRU_PITCH_PAD = 8


def _sublane_scan(a, bb, reverse):
    ri = lax.broadcasted_iota(jnp.int32, a.shape, 0)
    for d in (1, 2, 4):
        if reverse:
            keep, sh = ri < SUBLANES - d, SUBLANES - d
        else:
            keep, sh = ri >= d, d
        bb = bb + a * jnp.where(keep, pltpu.roll(bb, sh, 0), 0.0)
        a = a * jnp.where(keep, pltpu.roll(a, sh, 0), 1.0)
    return bb


def _lru_body(x_ref, y_ref, cw_ref, cb_ref, w_ref, bias_ref, lam_ref, o_ref,
              af_ref, bf_ref, ab_ref, bb_ref, hf_ref, pf_ref, hb_ref, pb_ref, hs_ref, *, S, W):
    L = S // LRU_SEGMENTS
    pitch = L + LRU_PITCH_PAD
    x = x_ref[...].astype(F32)
    row = lax.broadcasted_iota(jnp.int32, (S, W), 0)
    cw = cw_ref[...]
    xc = (cw[0:1] * jnp.where(row >= 2, pltpu.roll(x, 2, 0), 0.0)
          + cw[1:2] * jnp.where(row >= 1, pltpu.roll(x, 1, 0), 0.0)
          + cw[2:3] * x
          + cw[3:4] * jnp.where(row < S - 1, pltpu.roll(x, S - 1, 0), 0.0)
          + cb_ref[...])
    gates = jnp.dot(xc.astype(BF16), w_ref[0], preferred_element_type=F32) + bias_ref[0]
    lam = lam_ref[...]
    for d, (a_ref, b_ref) in enumerate(((af_ref, bf_ref), (ab_ref, bb_ref))):
        r = jax.nn.sigmoid(gates[:, (2 * d) * W:(2 * d + 1) * W])
        ig = jax.nn.sigmoid(gates[:, (2 * d + 1) * W:(2 * d + 2) * W])
        nl = -lam[d:d + 1]
        softplus = jnp.maximum(nl, 0.0) + jnp.log1p(jnp.exp(-jnp.abs(nl)))
        a = jnp.exp2((-LRU_C * LOG2E * softplus) * r)
        y2 = 1.0 - a * a
        bvals = jnp.where(y2 > 0.0, y2 * lax.rsqrt(y2), 0.0) * ig * xc
        for sg in range(LRU_SEGMENTS):
            a_ref[sg * pitch:sg * pitch + L, :] = a[sg * L:(sg + 1) * L]
            b_ref[sg * pitch:sg * pitch + L, :] = bvals[sg * L:(sg + 1) * L]

    def seg_rows(ref, j):
        return ref[pl.ds(j, LRU_SEGMENTS, stride=pitch), :]

    def step(j, carry):
        hf, pf, hb, pb = carry
        a = seg_rows(af_ref, j)
        hf = a * hf + seg_rows(bf_ref, j)
        pf = a * pf
        hf_ref[j] = hf
        pf_ref[j] = pf
        jb = L - 1 - j
        a = seg_rows(ab_ref, jb)
        hb = a * hb + seg_rows(bb_ref, jb)
        pb = a * pb
        hb_ref[jb] = hb
        pb_ref[jb] = pb
        return hf, pf, hb, pb

    zero = jnp.zeros((LRU_SEGMENTS, W), F32)
    one = jnp.ones((LRU_SEGMENTS, W), F32)
    hf, pf, hb, pb = lax.fori_loop(0, L, step, (zero, one, zero, one), unroll=8)

    si = lax.broadcasted_iota(jnp.int32, (LRU_SEGMENTS, W), 0)
    cf = jnp.where(si >= 1, pltpu.roll(_sublane_scan(pf, hf, False), 1, 0), 0.0)
    cb = jnp.where(si < LRU_SEGMENTS - 1,
                   pltpu.roll(_sublane_scan(pb, hb, True), LRU_SEGMENTS - 1, 0), 0.0)

    def fix(j, carry):
        hs_ref[pl.ds(j, LRU_SEGMENTS, stride=pitch), :] = (
            hf_ref[j] + pf_ref[j] * cf + hb_ref[j] + pb_ref[j] * cb)
        return carry

    lax.fori_loop(0, L, fix, 0, unroll=8)

    for sg in range(LRU_SEGMENTS):
        y = y_ref[sg * L:(sg + 1) * L, :].astype(F32)
        o_ref[sg * L:(sg + 1) * L, :] = (hs_ref[sg * pitch:sg * pitch + L, :]
                                         * jax.nn.gelu(y)).astype(o_ref.dtype)


def _rg_lru(proj, conv_w, conv_b, w_cat, b_cat, lru_lambda, *, B, S, off_x, off_y):
    T = B * S
    NB, W, _ = w_cat.shape
    xblk = off_x // W
    yblk = off_y // W
    body = functools.partial(_lru_body, S=S, W=W)
    padded = LRU_SEGMENTS * (S // LRU_SEGMENTS + LRU_PITCH_PAD)
    return pl.pallas_call(
        body,
        grid=(B, NB),
        in_specs=[pl.BlockSpec((S, W), lambda b, n: (b, xblk + n)),
                  pl.BlockSpec((S, W), lambda b, n: (b, yblk + n)),
                  pl.BlockSpec((conv_w.shape[0], W), lambda b, n: (0, n)),
                  pl.BlockSpec((1, W), lambda b, n: (0, n)),
                  pl.BlockSpec((1, W, 4 * W), lambda b, n: (n, 0, 0)),
                  pl.BlockSpec((1, 1, 4 * W), lambda b, n: (n, 0, 0)),
                  pl.BlockSpec((2, W), lambda b, n: (0, n))],
        out_specs=pl.BlockSpec((S, W), lambda b, n: (b, n)),
        out_shape=jax.ShapeDtypeStruct((T, NB * W), BF16),
        scratch_shapes=([pltpu.VMEM((padded, W), F32)] * 4
                        + [pltpu.VMEM((S // LRU_SEGMENTS, LRU_SEGMENTS, W), F32)] * 4
                        + [pltpu.VMEM((padded, W), F32)]),
        compiler_params=_params("parallel", "parallel"),
        name="rg_lru",
    )(proj, proj, conv_w, conv_b[None], w_cat, b_cat, lru_lambda)


def _merge_body(o_ref, r_ref, ga_ref, gr_ref, x_ref, wpa_ref, wpl_ref, wo_ref, g_ref, wr_ref,
                h_ref, hn_ref, aff_ref, *, E, n_sub):
    ts = o_ref.shape[0] // n_sub
    for t in range(n_sub):
        rows = slice(t * ts, (t + 1) * ts)
        ba = jnp.dot(o_ref[rows, :], wpa_ref[...], preferred_element_type=F32)
        br = jnp.dot(r_ref[rows, :], wpl_ref[...], preferred_element_type=F32)
        mixed = (jax.nn.sigmoid(ga_ref[rows, :].astype(F32)) * ba
                 + jax.nn.sigmoid(gr_ref[rows, :].astype(F32)) * br)
        h = x_ref[rows, :] + jnp.dot(mixed.astype(BF16), wo_ref[...], preferred_element_type=F32)
        h_ref[rows, :] = h
        hn = h * lax.rsqrt(jnp.mean(h * h, axis=-1, keepdims=True) + EPS) * g_ref[...]
        hn_hi = hn.astype(BF16)
        hn_ref[rows, :] = hn_hi
        hn_lo = (hn - hn_hi.astype(F32)).astype(BF16)
        lg = (jnp.dot(hn_hi, wr_ref[...], preferred_element_type=F32)
              + jnp.dot(hn_lo, wr_ref[...], preferred_element_type=F32))
        logits = lg[:, :E] + lg[:, E:2 * E]
        logits = logits - jnp.max(logits, axis=-1, keepdims=True)
        ex = jnp.exp(logits)
        aff_ref[rows, :] = ex / jnp.sum(ex, axis=-1, keepdims=True)


def _merge(o_attn, lru_out, proj, x2, wpa, wpl, wo, g_ffn, wr2, *, off_ga, off_gr, E):
    T, D = x2.shape
    tm = min(1024, T)
    n_sub = 2 if tm % 1024 == 0 else 1
    resident = dict(pipeline_mode=pl.Buffered(1))
    gab = off_ga // D
    grb = off_gr // D
    row = lambda i: (i, 0)
    const = lambda i: (0, 0)
    return pl.pallas_call(
        functools.partial(_merge_body, E=E, n_sub=n_sub),
        grid=(T // tm,),
        in_specs=[pl.BlockSpec((tm, D), row), pl.BlockSpec((tm, D), row),
                  pl.BlockSpec((tm, D), lambda i: (i, gab)), pl.BlockSpec((tm, D), lambda i: (i, grb)),
                  pl.BlockSpec((tm, D), row),
                  pl.BlockSpec((D, D), const, **resident), pl.BlockSpec((D, D), const, **resident),
                  pl.BlockSpec((D, D), const, **resident),
                  pl.BlockSpec((1, D), const), pl.BlockSpec((D, 2 * E), const)],
        out_specs=[pl.BlockSpec((tm, D), row), pl.BlockSpec((tm, D), row), pl.BlockSpec((tm, E), row)],
        out_shape=[jax.ShapeDtypeStruct((T, D), F32), jax.ShapeDtypeStruct((T, D), BF16),
                   jax.ShapeDtypeStruct((T, E), F32)],
        compiler_params=_params("parallel"),
        name="merge_router",
    )(o_attn, lru_out, proj, proj, x2, wpa, wpl, wo, g_ffn, wr2)


SELECT_SEQS_PER_STEP = 8


def _select_body(aff_ref, pos_ref, cnt_ref, *, S, C):
    rows = aff_ref.shape[0] * aff_ref.shape[1]
    bits = pltpu.bitcast(aff_ref[...].reshape(rows, S), jnp.int32)

    def refine(k, t):
        cand = t | jnp.left_shift(jnp.int32(1), 30 - k)
        cnt = jnp.sum(jnp.where(bits >= cand, 1.0, 0.0), axis=-1, keepdims=True)
        return jnp.where(cnt >= C, cand, t)

    t = lax.fori_loop(0, 31, refine, jnp.zeros((rows, 1), jnp.int32))
    gt = bits > t
    eq = bits == t
    need = C - jnp.sum(jnp.where(gt, 1.0, 0.0), axis=-1, keepdims=True).astype(jnp.int32)
    packed = jnp.where(gt, 1 << 16, 0) + jnp.where(eq, 1, 0)
    lane = lax.broadcasted_iota(jnp.int32, (rows, S), 1)
    incl = packed
    d = 1
    while d < S:
        incl = incl + jnp.where(lane >= d, pltpu.roll(incl, d, 1), 0)
        d *= 2
    excl = incl - packed
    n_gt = excl >> 16
    n_eq = excl & 0xFFFF
    sel = gt | (eq & (n_eq < need))
    before = n_gt + jnp.minimum(n_eq, need)
    cnt_ref[...] = before.reshape(cnt_ref.shape)
    pos_ref[...] = jnp.where(sel, before, -1).reshape(pos_ref.shape)


def _select(aff_t, C):
    B, E, S = aff_t.shape
    nb = math.gcd(SELECT_SEQS_PER_STEP, B)
    spec = pl.BlockSpec((nb, E, S), lambda b: (b, 0, 0))
    return pl.pallas_call(
        functools.partial(_select_body, S=S, C=C),
        grid=(B // nb,),
        in_specs=[spec],
        out_specs=[spec, spec],
        out_shape=[jax.ShapeDtypeStruct((B, E, S), jnp.int32)] * 2,
        compiler_params=_params("parallel"),
        name="topc_select",
    )(aff_t)


GATHER_WIN = 64
SLOT_ALIGN = 16


def _gather_body(lo_ref, pos_ref, hn_ref, xg_ref, *, E, C, S, ts):
    b = pl.program_id(0)
    n_tiles = S // ts
    xg_ref[...] = jnp.zeros(xg_ref.shape, xg_ref.dtype)
    sub = lax.broadcasted_iota(jnp.int32, (GATHER_WIN, ts), 0)

    def tile_body(k, carry):
        t0 = pl.multiple_of(k * ts, ts)
        nxt = jnp.minimum(k + 1, n_tiles - 1)
        starts = [(lo_ref[b, k, e] // SLOT_ALIGN) * SLOT_ALIGN for e in range(E)]
        ends = [jnp.where(k + 1 < n_tiles, lo_ref[b, nxt, e], C) for e in range(E)]
        hn_t = hn_ref[0, pl.ds(t0, ts), :]

        def one_pass(w, first):
            sel, row0s = [], []
            for e in range(E):
                start = starts[e] + w * GATHER_WIN
                row0 = jnp.minimum(start, C - GATHER_WIN)
                pe = pos_ref[0, e:e + 1, pl.ds(t0, ts)]
                if not first:
                    pe = jnp.where(pe >= start, pe, -1)
                sel.append(jnp.where(pe - row0 == sub, 1.0, 0.0).astype(BF16))
                row0s.append(pl.multiple_of(row0, SLOT_ALIGN))
            rows = jnp.dot(jnp.concatenate(sel, axis=0), hn_t,
                           preferred_element_type=F32).astype(xg_ref.dtype)
            for e in range(E):
                xg_ref[0, e, pl.ds(row0s[e], GATHER_WIN), :] += rows[e * GATHER_WIN:(e + 1) * GATHER_WIN]

        one_pass(0, True)
        n_pass = jnp.int32(1)
        for e in range(E):
            n_pass = jnp.maximum(n_pass, (ends[e] - starts[e] + GATHER_WIN - 1) // GATHER_WIN)

        def more(w, c):
            one_pass(w, False)
            return c

        lax.fori_loop(1, n_pass, more, 0)
        return carry

    lax.fori_loop(0, n_tiles, tile_body, 0)


def _gather(lo, pos_rows, hn3, C, ts):
    B, S, D = hn3.shape
    E = pos_rows.shape[1]
    assert C % GATHER_WIN == 0
    return pl.pallas_call(
        functools.partial(_gather_body, E=E, C=C, S=S, ts=ts),
        grid=(B,),
        in_specs=[pl.BlockSpec(memory_space=pltpu.SMEM),
                  pl.BlockSpec((1, E, S), lambda b: (b, 0, 0)),
                  pl.BlockSpec((1, S, D), lambda b: (b, 0, 0))],
        out_specs=pl.BlockSpec((1, E, C, D), lambda b: (b, 0, 0, 0)),
        out_shape=jax.ShapeDtypeStruct((B, E, C, D), BF16),
        compiler_params=_params("parallel"),
        name="moe_gather",
    )(lo, pos_rows, hn3)


def _ffn_body(pos_ref, aff_ref, xg_ref, wg_hbm, wu_hbm, wd_hbm, y_ref,
              wg_s, wu_s, wd_s, stg_g, stg_u, stg_d, sems, *, C, S, fc):
    e = pl.program_id(0)
    b = pl.program_id(1)
    n_experts = pl.num_programs(0)
    rows_in = stg_g.shape[0]
    rows_dn = stg_d.shape[0]
    n_chunks = wg_s.shape[1] // rows_in

    def chunk_copies(ee, k):
        r_in = pl.multiple_of(k * rows_in, rows_in)
        r_dn = pl.multiple_of(k * rows_dn, rows_dn)
        return (pltpu.make_async_copy(wg_hbm.at[ee, pl.ds(r_in, rows_in), :], stg_g, sems.at[0]),
                pltpu.make_async_copy(wu_hbm.at[ee, pl.ds(r_in, rows_in), :], stg_u, sems.at[1]),
                pltpu.make_async_copy(wd_hbm.at[ee, pl.ds(r_dn, rows_dn), :], stg_d, sems.at[2]))

    def cast_chunk(slot_, k):
        r_in = pl.multiple_of(k * rows_in, rows_in)
        r_dn = pl.multiple_of(k * rows_dn, rows_dn)
        wg_s[slot_, pl.ds(r_in, rows_in), :] = stg_g[...].astype(BF16)
        wu_s[slot_, pl.ds(r_in, rows_in), :] = stg_u[...].astype(BF16)
        wd_s[slot_, pl.ds(r_dn, rows_dn), :] = stg_d[...].astype(BF16)

    @pl.when((e == 0) & (b == 0))
    def _():
        def stage(k, carry):
            for cp in chunk_copies(0, k):
                cp.start()
            for cp in chunk_copies(0, k):
                cp.wait()
            cast_chunk(0, k)
            return carry

        lax.fori_loop(0, n_chunks, stage, 0)

        @pl.when(n_experts > 1)
        def _():
            for cp in chunk_copies(1, 0):
                cp.start()

    cur = e % 2
    nb = xg_ref.shape[0]
    slot = lax.broadcasted_iota(jnp.int32, (C, S), 0)
    vals = [jnp.sum(jnp.where(pos_ref[t, 0] == slot, aff_ref[t, 0], 0.0), axis=1, keepdims=True)
            for t in range(nb)]
    val = jnp.concatenate(vals, axis=0)
    xg = xg_ref[...].reshape(nb * C, xg_ref.shape[3])
    F = wg_s.shape[2]
    y = jnp.zeros(xg.shape, F32)
    for f0 in range(0, F, fc):
        g = jnp.dot(xg, wg_s[cur, :, f0:f0 + fc], preferred_element_type=F32)
        u = jnp.dot(xg, wu_s[cur, :, f0:f0 + fc], preferred_element_type=F32)
        hid = (g * jax.nn.sigmoid(g) * u).astype(BF16)
        y = y + jnp.dot(hid, wd_s[cur, f0:f0 + fc, :], preferred_element_type=F32)
    y_ref[...] = (y * val).astype(y_ref.dtype).reshape(y_ref.shape)

    @pl.when(e + 1 < n_experts)
    def _():
        for cp in chunk_copies(e + 1, b):
            cp.wait()
        cast_chunk(1 - cur, b)

    wrap = b + 1 == n_chunks
    e_nxt = jnp.where(wrap, e + 2, e + 1)
    k_nxt = jnp.where(wrap, 0, b + 1)

    @pl.when(e_nxt < n_experts)
    def _():
        for cp in chunk_copies(e_nxt, k_nxt):
            cp.start()


FFN_SEQS_PER_STEP = 4


def _expert_ffn(pos_rows, aff_rows, xg, wg, wu, wd):
    B, _, C, D = xg.shape
    S = pos_rows.shape[-1]
    E, _, F = wg.shape
    fc = min(512, F)
    nb = math.gcd(FFN_SEQS_PER_STEP, B)
    steps = B // nb
    assert D % steps == 0 and F % steps == 0 and (D // steps) % 16 == 0 and (F // steps) % 16 == 0
    hbm = pl.BlockSpec(memory_space=pl.ANY)
    return pl.pallas_call(
        functools.partial(_ffn_body, C=C, S=S, fc=fc),
        grid=(E, steps),
        in_specs=[pl.BlockSpec((nb, 1, 1, S), lambda e, b: (b, e, 0, 0)),
                  pl.BlockSpec((nb, 1, 1, S), lambda e, b: (b, e, 0, 0)),
                  pl.BlockSpec((nb, 1, C, D), lambda e, b: (b, e, 0, 0)),
                  hbm, hbm, hbm],
        out_specs=pl.BlockSpec((nb, 1, C, D), lambda e, b: (b, e, 0, 0)),
        out_shape=jax.ShapeDtypeStruct((B, E, C, D), BF16),
        scratch_shapes=[pltpu.VMEM((2, D, F), BF16), pltpu.VMEM((2, D, F), BF16),
                        pltpu.VMEM((2, F, D), BF16),
                        pltpu.VMEM((D // steps, F), F32), pltpu.VMEM((D // steps, F), F32),
                        pltpu.VMEM((F // steps, D), F32),
                        pltpu.SemaphoreType.DMA((3,))],
        compiler_params=_params("arbitrary", "arbitrary"),
        name="expert_ffn",
    )(pos_rows, aff_rows, xg, wg, wu, wd)


COMBINE_WIN = 64


def _combine_body(lo_ref, pos_ref, y_ref, h_ref, o_ref, *, E, C, ts, n_tiles):
    b = pl.program_id(0)
    n_sub = pos_ref.shape[1] // ts
    lane = lax.broadcasted_iota(jnp.int32, (ts, LANES), 1)
    upper = lane >= COMBINE_WIN
    lane_in = jnp.where(upper, lane - COMBINE_WIN, lane)

    extra = []
    for t in range(n_sub):
        tile = pl.program_id(1) * n_sub + t
        rows = slice(t * ts, (t + 1) * ts)
        pc = pos_ref[0, rows, :]
        nxt = jnp.minimum(tile + 1, n_tiles - 1)
        starts = [(lo_ref[b, tile, e] // SLOT_ALIGN) * SLOT_ALIGN for e in range(E)]
        ends = [jnp.where(tile + 1 < n_tiles, lo_ref[b, nxt, e], C) for e in range(E)]

        def window_product(w, first, pc=pc, starts=starts):
            pieces, y_wins = [], []
            for p in range(E // 2):
                targets = []
                for e in (2 * p, 2 * p + 1):
                    start = starts[e] + w * COMBINE_WIN
                    row0 = jnp.minimum(start, C - COMBINE_WIN)
                    pe = pc[:, e:e + 1]
                    if not first:
                        pe = jnp.where(pe >= start, pe, -1)
                    targets.append(pe - row0)
                    y_wins.append(y_ref[0, e, pl.ds(pl.multiple_of(row0, SLOT_ALIGN), COMBINE_WIN), :])
                hit = jnp.where(upper, targets[1], targets[0]) == lane_in
                pieces.append(jnp.where(hit, 1.0, 0.0).astype(BF16))
            return jnp.dot(jnp.concatenate(pieces, axis=1), jnp.concatenate(y_wins, axis=0),
                           preferred_element_type=F32)

        o_ref[0, rows, :] = h_ref[0, rows, :] + window_product(0, True)

        n_pass = jnp.int32(1)
        for e in range(E):
            n_pass = jnp.maximum(n_pass, (ends[e] - starts[e] + COMBINE_WIN - 1) // COMBINE_WIN)
        extra.append((n_pass, rows, window_product))

    for n_pass, rows, window_product in extra:
        def more(w, carry, rows=rows, window_product=window_product):
            o_ref[0, rows, :] += window_product(w, False)
            return carry

        lax.fori_loop(1, n_pass, more, 0)


def _combine(lo, pos_cols, y, h3, ts):
    B, S, D = h3.shape
    _, E, C, _ = y.shape
    n_tiles = S // ts
    n_sub = math.gcd(4, n_tiles)
    tb = ts * n_sub
    assert E % 2 == 0 and C % COMBINE_WIN == 0 and 2 * COMBINE_WIN == LANES
    return pl.pallas_call(
        functools.partial(_combine_body, E=E, C=C, ts=ts, n_tiles=n_tiles),
        grid=(B, n_tiles // n_sub),
        in_specs=[pl.BlockSpec(memory_space=pltpu.SMEM),
                  pl.BlockSpec((1, tb, E), lambda b, i: (b, i, 0)),
                  pl.BlockSpec((1, E, C, D), lambda b, i: (b, 0, 0, 0)),
                  pl.BlockSpec((1, tb, D), lambda b, i: (b, i, 0))],
        out_specs=pl.BlockSpec((1, tb, D), lambda b, i: (b, i, 0)),
        out_shape=jax.ShapeDtypeStruct((B, S, D), F32),
        compiler_params=_params("parallel", "arbitrary"),
        name="moe_combine",
    )(lo, pos_cols, y, h3)


def kernel(x, g_mix, w_in, g_q, g_k, lam_q1, lam_k1, lam_q2, lam_k2, g_subln, rel_bias, conv_w, conv_b,
           gate_r_w, gate_r_b, gate_i_w, gate_i_b, lru_lambda, w_proj_attn, w_proj_lru, w_out, g_ffn,
           w_router, w_gate_e, w_up_e, w_down_e):
    B, S, D = x.shape
    depth = w_in.shape[0]
    H = rel_bias.shape[1]
    hd = g_q.shape[-1]
    vd = g_subln.shape[-1]
    qk_w = H * 2 * hd
    attn_w = H * vd
    lru_w = conv_w.shape[-1]
    NB, LB = gate_r_w.shape[2], gate_r_w.shape[3]
    E = w_router.shape[-1]
    C = EC_CAPACITY_FACTOR * S // E
    off_k = qk_w
    off_v = off_k + qk_w
    off_x = off_v + attn_w
    off_y = off_x + lru_w
    off_ga = off_y + lru_w
    off_gr = off_ga + D

    h2 = x.reshape(B * S, D)
    for layer in range(depth):
        lam_init = 0.8 - 0.6 * math.exp(-0.3 * layer)
        proj = _inproj(h2, g_mix[layer][None], w_in[layer].astype(BF16))

        lam_params = jnp.stack([lam_q1[layer], lam_k1[layer], lam_q2[layer], lam_k2[layer]])
        o_attn = _attention(proj, rel_bias, g_q[layer], g_k[layer], lam_params, g_subln[layer],
                            B=B, S=S, H=H, hd=hd, vd=vd, off_k=off_k, off_v=off_v, lam_init=lam_init)

        w_cat = jnp.concatenate([gate_r_w[layer, 0], gate_i_w[layer, 0],
                                 gate_r_w[layer, 1], gate_i_w[layer, 1]], axis=-1).astype(BF16)
        b_cat = jnp.stack([gate_r_b[layer, 0].reshape(NB, LB), gate_i_b[layer, 0].reshape(NB, LB),
                           gate_r_b[layer, 1].reshape(NB, LB), gate_i_b[layer, 1].reshape(NB, LB)],
                          axis=1).reshape(NB, 1, 4 * LB)
        lru_out = _rg_lru(proj, conv_w[layer], conv_b[layer], w_cat, b_cat, lru_lambda[layer],
                          B=B, S=S, off_x=off_x, off_y=off_y)

        wr = w_router[layer]
        wr_hi = wr.astype(BF16)
        wr2 = jnp.concatenate([wr_hi, (wr - wr_hi.astype(F32)).astype(BF16)], axis=1)
        h2, hn, aff = _merge(o_attn, lru_out, proj, h2, w_proj_attn[layer].astype(BF16),
                             w_proj_lru[layer].astype(BF16), w_out[layer].astype(BF16),
                             g_ffn[layer][None], wr2, off_ga=off_ga, off_gr=off_gr, E=E)

        aff_rows = aff.reshape(B, S, E).transpose(0, 2, 1)
        pos_rows, cnt_rows = _select(aff_rows, C)
        ts = min(256, S)
        lo = cnt_rows[:, :, ::ts].transpose(0, 2, 1)
        xg = _gather(lo, pos_rows, hn.reshape(B, S, D), C, ts)
        y = _expert_ffn(pos_rows.reshape(B, E, 1, S), aff_rows.reshape(B, E, 1, S), xg,
                        w_gate_e[layer], w_up_e[layer], w_down_e[layer])
        h3 = _combine(lo, pos_rows.transpose(0, 2, 1), y, h2.reshape(B, S, D), ts)
        h2 = h3.reshape(B * S, D)
    return h2.reshape(B, S, D)
```

```python
import functools
import math

import jax
import jax.numpy as jnp
import numpy as np
from jax import lax
from jax.experimental import pallas as pl
from jax.experimental.pallas import tpu as pltpu

F32 = jnp.float32
BF16 = jnp.bfloat16
EPS = 1e-6
LANES = 128
SUBLANES = 8
VMEM_LIMIT = 56 * 1024 * 1024
LOG2E = 1.4426950408889634
NEG_BIG = -1e30
REL_MAX_DIST = 128
LRU_C = 8.0
EC_CAPACITY_FACTOR = 2


def _params(*sem):
    return pltpu.CompilerParams(dimension_semantics=sem, vmem_limit_bytes=VMEM_LIMIT)


def _inproj_body(x_ref, g_ref, w_ref, o_ref, xn_ref):
    @pl.when(pl.program_id(1) == 0)
    def _():
        x = x_ref[...]
        ms = jnp.mean(x * x, axis=-1, keepdims=True)
        xn_ref[...] = (x * lax.rsqrt(ms + EPS) * g_ref[...]).astype(BF16)

    o_ref[...] = jnp.dot(xn_ref[...], w_ref[...], preferred_element_type=F32).astype(o_ref.dtype)


def _inproj(x2, g, w_bf):
    T, D = x2.shape
    N = w_bf.shape[1]
    tm = min(1024, T)
    tn = N // 4 if N % (4 * LANES) == 0 else N
    return pl.pallas_call(
        _inproj_body,
        grid=(T // tm, N // tn),
        in_specs=[pl.BlockSpec((tm, D), lambda i, j: (i, 0)),
                  pl.BlockSpec((1, D), lambda i, j: (0, 0)),
                  pl.BlockSpec((D, tn), lambda i, j: (0, j))],
        out_specs=pl.BlockSpec((tm, tn), lambda i, j: (i, j)),
        out_shape=jax.ShapeDtypeStruct((T, N), BF16),
        scratch_shapes=[pltpu.VMEM((tm, D), BF16)],
        compiler_params=_params("parallel", "arbitrary"),
        name="in_proj",
    )(x2, g, w_bf)


def _bucket_table(S, n_buckets, rel):
    half = n_buckets // 2
    max_exact = half // 2
    ret = np.where(rel > 0, half, 0)
    n = np.abs(rel)
    nf = np.maximum(n, max_exact).astype(np.float64)
    large = max_exact + (np.log(nf / max_exact) / math.log(REL_MAX_DIST / max_exact)
                         * (half - max_exact)).astype(np.int32)
    large = np.minimum(large, half - 1)
    return (ret + np.where(n < max_exact, n, large)).astype(np.int32).reshape(1, 2 * S)


def _seg_sumsq(x, seg_ones, split=True):
    x2 = x * x
    hi = x2.astype(BF16)
    out = jnp.dot(hi, seg_ones, preferred_element_type=F32)
    if split:
        lo = (x2 - hi.astype(F32)).astype(BF16)
        out = out + jnp.dot(lo, seg_ones, preferred_element_type=F32)
    return out


NORM_SLACK = 1.01
SAFE_LOG2_SPAN = 100.0


def _attn_body(bucket_ref, bucket_t_ref, relb_ref, q_ref, k_ref, v_ref, gq_ref, gk_ref, lamp_ref, gs_ref,
               segq_ref, segv_ref, o_ref, e_ref, et_ref, fast_ref, kn_ref, va_ref, vat_ref, m_ref, acc_ref,
               *, S, tq, n_sub, kc, hd, vd, lam_init, n_buckets):
    h = pl.program_id(0)
    b = pl.program_id(1)
    i = pl.program_id(2)
    q_scale = hd ** -0.5 * LOG2E

    @pl.when((b == 0) & (i == 0))
    def _():
        def bias_row(bk):
            tab = jnp.zeros((1, 2 * S), F32)
            for n in range(n_buckets):
                tab = jnp.where(bk == n, relb_ref[h, n], tab)
            return tab * LOG2E

        tab = bias_row(bucket_ref[...])
        bmax = jnp.max(tab, axis=-1, keepdims=True)
        bmin = jnp.min(tab, axis=-1, keepdims=True)
        bound = (NORM_SLACK * hd * q_scale) * (jnp.max(jnp.abs(gq_ref[...]), axis=-1, keepdims=True)
                                               * jnp.max(jnp.abs(gk_ref[...]), axis=-1, keepdims=True))
        span = 2.0 * bound + (bmax - bmin)
        fast_ref[0] = (span[0, 0] <= SAFE_LOG2_SPAN).astype(jnp.int32)
        shift = bound + bmax
        e_ref[...] = pltpu.roll(jnp.broadcast_to(tab - shift, (tq, 2 * S)), 0, 1, stride=1, stride_axis=0)
        tab_t = bias_row(bucket_t_ref[...])
        et_ref[...] = pltpu.roll(jnp.broadcast_to(tab_t - shift, (tq, 2 * S)), 0, 1, stride=1, stride_axis=0)

    @pl.when(i == 0)
    def _():
        k = k_ref[...].astype(F32)
        ms = _seg_sumsq(k, segq_ref[...], split=False) * (1.0 / hd)
        kn_ref[...] = (k * lax.rsqrt(ms + EPS) * gk_ref[...]).astype(BF16)
        va_ref[:, :vd] = v_ref[...]
        va_ref[:, vd:] = jnp.ones((S, vd), BF16)
        vat_ref[:vd, :] = v_ref[...].astype(F32).T.astype(BF16)
        vat_ref[vd:, :] = jnp.ones((vat_ref.shape[0] - vd, S), BF16)

    nt_dims = (((1,), (1,)), ((), ()))
    lp = lamp_ref[...]
    lam = (jnp.exp(jnp.sum(lp[0:1] * lp[1:2], axis=-1, keepdims=True))
           - jnp.exp(jnp.sum(lp[2:3] * lp[3:4], axis=-1, keepdims=True)) + lam_init)

    def q_maps_of(t):
        q = q_ref[t * tq:(t + 1) * tq, :].astype(F32)
        ms = _seg_sumsq(q, segq_ref[...]) * (1.0 / hd)
        qn = q * lax.rsqrt(ms + EPS) * (gq_ref[...] * q_scale)
        lo = lax.broadcasted_iota(jnp.int32, qn.shape, 1) < hd
        return jnp.where(lo, qn, 0.0).astype(BF16), jnp.where(lo, 0.0, qn).astype(BF16)

    def e_start(t, k0):
        return pl.multiple_of(S + k0 - (i * n_sub + t) * tq, LANES)

    def finish(t, a1, a2):
        o = a1[:, :vd] / a1[:, vd:] - lam * (a2[:, :vd] / a2[:, vd:])
        o = o * lax.rsqrt(_seg_sumsq(o, segv_ref[...]) * (1.0 / vd) + EPS)
        o_ref[t * tq:(t + 1) * tq, :] = (o * (gs_ref[...] * (1.0 - lam_init))).astype(o_ref.dtype)

    def finish_t(t, a1, a2):
        ot = a1[:vd] / a1[vd:vd + 1] - lam * (a2[:vd] / a2[vd:vd + 1])
        o = ot.T
        o = o * lax.rsqrt(_seg_sumsq(o, segv_ref[...]) * (1.0 / vd) + EPS)
        o_ref[t * tq:(t + 1) * tq, :] = (o * (gs_ref[...] * (1.0 - lam_init))).astype(o_ref.dtype)

    @pl.when(fast_ref[0] == 1)
    def _():
        for t in range(n_sub):
            q_maps = q_maps_of(t)
            tile = i * n_sub + t
            accs = []
            for mi in range(2):
                chunks = []
                for c in range(S // tq):
                    st = lax.dot_general(kn_ref[c * tq:(c + 1) * tq, :], q_maps[mi], nt_dims,
                                         preferred_element_type=F32)
                    start = pl.multiple_of(S + (tile - c) * tq, LANES)
                    chunks.append(jnp.exp2(st + et_ref[:, pl.ds(start, tq)]).astype(BF16))
                accs.append(jnp.dot(vat_ref[...], jnp.concatenate(chunks, axis=0),
                                    preferred_element_type=F32))
            finish_t(t, *accs)

    @pl.when(fast_ref[0] != 1)
    def _():
        for t in range(n_sub):
            q_maps = q_maps_of(t)
            m_ref[...] = jnp.full(m_ref.shape, NEG_BIG, F32)
            acc_ref[...] = jnp.zeros(acc_ref.shape, F32)

            def chunk(c, carry):
                k0 = pl.multiple_of(c * kc, kc)
                bias = e_ref[:, pl.ds(e_start(t, k0), kc)]
                for mi in range(2):
                    s = lax.dot_general(q_maps[mi], kn_ref[pl.ds(k0, kc), :], nt_dims,
                                        preferred_element_type=F32) + bias
                    m_old = m_ref[mi]
                    m_new = jnp.maximum(m_old, jnp.max(s, axis=-1, keepdims=True))
                    p = jnp.exp2(s - m_new).astype(BF16)
                    acc_ref[mi] = (jnp.exp2(m_old - m_new) * acc_ref[mi]
                                   + jnp.dot(p, va_ref[pl.ds(k0, kc), :], preferred_element_type=F32))
                    m_ref[mi] = m_new
                return carry

            lax.fori_loop(0, S // kc, chunk, 0)
            finish(t, acc_ref[0], acc_ref[1])


def _attention(proj, rel_bias, g_q, g_k, lam_params, g_subln, *, B, S, H, hd, vd, off_k, off_v, lam_init):
    T = B * S
    tq = min(512, S)
    kc = min(512, S)
    n_sub = math.gcd(4, S // tq)
    tqb = n_sub * tq
    nq = S // tqb
    n_buckets = rel_bias.shape[0]
    hw = 2 * hd
    bucket = jnp.asarray(_bucket_table(S, n_buckets, np.arange(2 * S) - S))
    bucket_t = jnp.asarray(_bucket_table(S, n_buckets, S - np.arange(2 * S)))
    body = functools.partial(_attn_body, S=S, tq=tq, n_sub=n_sub, kc=kc, hd=hd, vd=vd, lam_init=lam_init,
                             n_buckets=n_buckets)
    kblk = off_k // hw
    vblk = off_v // vd
    seg_q = np.kron(np.eye(2, dtype=np.float32), np.ones((hd, hd), np.float32))
    return pl.pallas_call(
        body,
        grid=(H, B, nq),
        in_specs=[pl.BlockSpec((1, 2 * S), lambda h, b, i: (0, 0)),
                  pl.BlockSpec((1, 2 * S), lambda h, b, i: (0, 0)),
                  pl.BlockSpec(memory_space=pltpu.SMEM),
                  pl.BlockSpec((tqb, hw), lambda h, b, i: (b * nq + i, h)),
                  pl.BlockSpec((S, hw), lambda h, b, i: (b, kblk + h)),
                  pl.BlockSpec((S, vd), lambda h, b, i: (b, vblk + h)),
                  pl.BlockSpec((1, hw), lambda h, b, i: (0, 0)),
                  pl.BlockSpec((1, hw), lambda h, b, i: (0, 0)),
                  pl.BlockSpec((4, hd), lambda h, b, i: (0, 0)),
                  pl.BlockSpec((1, vd), lambda h, b, i: (0, 0)),
                  pl.BlockSpec((hw, hw), lambda h, b, i: (0, 0)),
                  pl.BlockSpec((vd, vd), lambda h, b, i: (0, 0))],
        out_specs=pl.BlockSpec((tqb, vd), lambda h, b, i: (b * nq + i, h)),
        out_shape=jax.ShapeDtypeStruct((T, H * vd), BF16),
        scratch_shapes=[pltpu.VMEM((tq, 2 * S), F32),
                        pltpu.VMEM((tq, 2 * S), F32),
                        pltpu.SMEM((1,), jnp.int32),
                        pltpu.VMEM((S, hw), BF16),
                        pltpu.VMEM((S, 2 * vd), BF16),
                        pltpu.VMEM((vd + 16, S), BF16),
                        pltpu.VMEM((2, tq, 1), F32),
                        pltpu.VMEM((2, tq, 2 * vd), F32)],
        compiler_params=_params("arbitrary", "arbitrary", "arbitrary"),
        name="diff_attention",
    )(bucket, bucket_t, rel_bias.T, proj, proj, proj, jnp.tile(g_q, 2)[None], jnp.tile(g_k, 2)[None],
      lam_params, g_subln[None], jnp.asarray(seg_q, BF16), jnp.ones((vd, vd), BF16))


LRU_SEGMENTS = SUBLANES
LRU_PITCH_PAD = 8


def _sublane_scan(a, bb, reverse):
    ri = lax.broadcasted_iota(jnp.int32, a.shape, 0)
    for d in (1, 2, 4):
        if reverse:
            keep, sh = ri < SUBLANES - d, SUBLANES - d
        else:
            keep, sh = ri >= d, d
        bb = bb + a * jnp.where(keep, pltpu.roll(bb, sh, 0), 0.0)
        a = a * jnp.where(keep, pltpu.roll(a, sh, 0), 1.0)
    return bb


LRU_BLOCKS_PER_STEP = 2


def _lru_body(x_ref, y_ref, cw_ref, cb_ref, w_ref, bias_ref, lam_ref, o_ref,
              af_ref, bf_ref, ab_ref, bb_ref, hf_ref, pf_ref, hb_ref, pb_ref, hs_ref, *, S, W):
    nb = w_ref.shape[0]
    L = S // LRU_SEGMENTS
    pitch = L + LRU_PITCH_PAD
    x = x_ref[...].astype(F32)
    row = lax.broadcasted_iota(jnp.int32, x.shape, 0)
    cw = cw_ref[...]
    xc_all = (cw[0:1] * jnp.where(row >= 2, pltpu.roll(x, 2, 0), 0.0)
              + cw[1:2] * jnp.where(row >= 1, pltpu.roll(x, 1, 0), 0.0)
              + cw[2:3] * x
              + cw[3:4] * jnp.where(row < S - 1, pltpu.roll(x, S - 1, 0), 0.0)
              + cb_ref[...])
    lam = lam_ref[...]
    for n in range(nb):
        xc = xc_all[:, n * W:(n + 1) * W]
        gates = jnp.dot(xc.astype(BF16), w_ref[n], preferred_element_type=F32) + bias_ref[n]
        for d, (a_ref, b_ref) in enumerate(((af_ref, bf_ref), (ab_ref, bb_ref))):
            r = jax.nn.sigmoid(gates[:, (2 * d) * W:(2 * d + 1) * W])
            ig = jax.nn.sigmoid(gates[:, (2 * d + 1) * W:(2 * d + 2) * W])
            nl = -lam[d:d + 1, n * W:(n + 1) * W]
            softplus = jnp.maximum(nl, 0.0) + jnp.log1p(jnp.exp(-jnp.abs(nl)))
            a = jnp.exp2((-LRU_C * LOG2E * softplus) * r)
            y2 = 1.0 - a * a
            bvals = jnp.where(y2 > 0.0, y2 * lax.rsqrt(y2), 0.0) * ig * xc
            for sg in range(LRU_SEGMENTS):
                a_ref[n, sg * pitch:sg * pitch + L, :] = a[sg * L:(sg + 1) * L]
                b_ref[n, sg * pitch:sg * pitch + L, :] = bvals[sg * L:(sg + 1) * L]

    def seg_rows(ref, n, j):
        return ref[n, pl.ds(j, LRU_SEGMENTS, stride=pitch), :]

    def step(j, carry):
        out = []
        jb = L - 1 - j
        for n in range(nb):
            hf, pf, hb, pb = carry[4 * n:4 * n + 4]
            a = seg_rows(af_ref, n, j)
            hf = a * hf + seg_rows(bf_ref, n, j)
            pf = a * pf
            hf_ref[n, j] = hf
            pf_ref[n, j] = pf
            a = seg_rows(ab_ref, n, jb)
            hb = a * hb + seg_rows(bb_ref, n, jb)
            pb = a * pb
            hb_ref[n, jb] = hb
            pb_ref[n, jb] = pb
            out += [hf, pf, hb, pb]
        return tuple(out)

    zero = jnp.zeros((LRU_SEGMENTS, W), F32)
    one = jnp.ones((LRU_SEGMENTS, W), F32)
    ends = lax.fori_loop(0, L, step, (zero, one, zero, one) * nb, unroll=8)

    si = lax.broadcasted_iota(jnp.int32, (LRU_SEGMENTS, W), 0)
    cfs, cbs = [], []
    for n in range(nb):
        hf, pf, hb, pb = ends[4 * n:4 * n + 4]
        cfs.append(jnp.where(si >= 1, pltpu.roll(_sublane_scan(pf, hf, False), 1, 0), 0.0))
        cbs.append(jnp.where(si < LRU_SEGMENTS - 1,
                             pltpu.roll(_sublane_scan(pb, hb, True), LRU_SEGMENTS - 1, 0), 0.0))

    def fix(j, carry):
        for n in range(nb):
            hs_ref[n, pl.ds(j, LRU_SEGMENTS, stride=pitch), :] = (
                hf_ref[n, j] + pf_ref[n, j] * cfs[n] + hb_ref[n, j] + pb_ref[n, j] * cbs[n])
        return carry

    lax.fori_loop(0, L, fix, 0, unroll=8)

    for n in range(nb):
        for sg in range(LRU_SEGMENTS):
            y = y_ref[sg * L:(sg + 1) * L, n * W:(n + 1) * W].astype(F32)
            o_ref[sg * L:(sg + 1) * L, n * W:(n + 1) * W] = (
                hs_ref[n, sg * pitch:sg * pitch + L, :] * jax.nn.gelu(y)).astype(o_ref.dtype)


def _rg_lru(proj, conv_w, conv_b, w_cat, b_cat, lru_lambda, *, B, S, off_x, off_y):
    T = B * S
    NB, W, _ = w_cat.shape
    nb = math.gcd(LRU_BLOCKS_PER_STEP, NB)
    wb = nb * W
    xblk = off_x // wb
    yblk = off_y // wb
    body = functools.partial(_lru_body, S=S, W=W)
    padded = LRU_SEGMENTS * (S // LRU_SEGMENTS + LRU_PITCH_PAD)
    return pl.pallas_call(
        body,
        grid=(B, NB // nb),
        in_specs=[pl.BlockSpec((S, wb), lambda b, n: (b, xblk + n)),
                  pl.BlockSpec((S, wb), lambda b, n: (b, yblk + n)),
                  pl.BlockSpec((conv_w.shape[0], wb), lambda b, n: (0, n)),
                  pl.BlockSpec((1, wb), lambda b, n: (0, n)),
                  pl.BlockSpec((nb, W, 4 * W), lambda b, n: (n, 0, 0)),
                  pl.BlockSpec((nb, 1, 4 * W), lambda b, n: (n, 0, 0)),
                  pl.BlockSpec((2, wb), lambda b, n: (0, n))],
        out_specs=pl.BlockSpec((S, wb), lambda b, n: (b, n)),
        out_shape=jax.ShapeDtypeStruct((T, NB * W), BF16),
        scratch_shapes=([pltpu.VMEM((nb, padded, W), F32)] * 4
                        + [pltpu.VMEM((nb, S // LRU_SEGMENTS, LRU_SEGMENTS, W), F32)] * 4
                        + [pltpu.VMEM((nb, padded, W), F32)]),
        compiler_params=_params("parallel", "parallel"),
        name="rg_lru",
    )(proj, proj, conv_w, conv_b[None], w_cat, b_cat, lru_lambda)


def _merge_body(o_ref, r_ref, ga_ref, gr_ref, x_ref, wpa_ref, wpl_ref, wo_ref, g_ref, wr_ref,
                h_ref, hn_ref, aff_ref, *, E, n_sub):
    ts = o_ref.shape[0] // n_sub
    for t in range(n_sub):
        rows = slice(t * ts, (t + 1) * ts)
        ba = jnp.dot(o_ref[rows, :], wpa_ref[...], preferred_element_type=F32)
        br = jnp.dot(r_ref[rows, :], wpl_ref[...], preferred_element_type=F32)
        mixed = (jax.nn.sigmoid(ga_ref[rows, :].astype(F32)) * ba
                 + jax.nn.sigmoid(gr_ref[rows, :].astype(F32)) * br)
        h = x_ref[rows, :] + jnp.dot(mixed.astype(BF16), wo_ref[...], preferred_element_type=F32)
        h_ref[rows, :] = h
        hn = h * lax.rsqrt(jnp.mean(h * h, axis=-1, keepdims=True) + EPS) * g_ref[...]
        hn_hi = hn.astype(BF16)
        hn_ref[rows, :] = hn_hi
        hn_lo = (hn - hn_hi.astype(F32)).astype(BF16)
        lg = (jnp.dot(hn_hi, wr_ref[...], preferred_element_type=F32)
              + jnp.dot(hn_lo, wr_ref[...], preferred_element_type=F32))
        logits = lg[:, :E] + lg[:, E:2 * E]
        logits = logits - jnp.max(logits, axis=-1, keepdims=True)
        ex = jnp.exp(logits)
        aff_ref[rows, :] = ex / jnp.sum(ex, axis=-1, keepdims=True)


def _merge(o_attn, lru_out, proj, x2, wpa, wpl, wo, g_ffn, wr2, *, off_ga, off_gr, E):
    T, D = x2.shape
    tm = min(1024, T)
    n_sub = 2 if tm % 1024 == 0 else 1
    resident = dict(pipeline_mode=pl.Buffered(1))
    gab = off_ga // D
    grb = off_gr // D
    row = lambda i: (i, 0)
    const = lambda i: (0, 0)
    return pl.pallas_call(
        functools.partial(_merge_body, E=E, n_sub=n_sub),
        grid=(T // tm,),
        in_specs=[pl.BlockSpec((tm, D), row), pl.BlockSpec((tm, D), row),
                  pl.BlockSpec((tm, D), lambda i: (i, gab)), pl.BlockSpec((tm, D), lambda i: (i, grb)),
                  pl.BlockSpec((tm, D), row),
                  pl.BlockSpec((D, D), const, **resident), pl.BlockSpec((D, D), const, **resident),
                  pl.BlockSpec((D, D), const, **resident),
                  pl.BlockSpec((1, D), const), pl.BlockSpec((D, 2 * E), const)],
        out_specs=[pl.BlockSpec((tm, D), row), pl.BlockSpec((tm, D), row), pl.BlockSpec((tm, E), row)],
        out_shape=[jax.ShapeDtypeStruct((T, D), F32), jax.ShapeDtypeStruct((T, D), BF16),
                   jax.ShapeDtypeStruct((T, E), F32)],
        compiler_params=_params("parallel"),
        name="merge_router",
    )(o_attn, lru_out, proj, proj, x2, wpa, wpl, wo, g_ffn, wr2)


SELECT_SEQS_PER_STEP = 8


def _select_body(aff_ref, pos_ref, cnt_ref, *, S, C):
    rows = aff_ref.shape[0] * aff_ref.shape[1]
    bits = pltpu.bitcast(aff_ref[...].reshape(rows, S), jnp.int32)

    def refine(k, t):
        cand = t | jnp.left_shift(jnp.int32(1), 30 - k)
        cnt = jnp.sum(jnp.where(bits >= cand, 1.0, 0.0), axis=-1, keepdims=True)
        return jnp.where(cnt >= C, cand, t)

    t = lax.fori_loop(0, 31, refine, jnp.zeros((rows, 1), jnp.int32))
    gt = bits > t
    eq = bits == t
    need = C - jnp.sum(jnp.where(gt, 1.0, 0.0), axis=-1, keepdims=True).astype(jnp.int32)
    packed = jnp.where(gt, 1 << 16, 0) + jnp.where(eq, 1, 0)
    lane = lax.broadcasted_iota(jnp.int32, (rows, S), 1)
    incl = packed
    d = 1
    while d < S:
        incl = incl + jnp.where(lane >= d, pltpu.roll(incl, d, 1), 0)
        d *= 2
    excl = incl - packed
    n_gt = excl >> 16
    n_eq = excl & 0xFFFF
    sel = gt | (eq & (n_eq < need))
    before = n_gt + jnp.minimum(n_eq, need)
    cnt_ref[...] = before.reshape(cnt_ref.shape)
    pos_ref[...] = jnp.where(sel, before, -1).reshape(pos_ref.shape)


def _select(aff_t, C):
    B, E, S = aff_t.shape
    nb = math.gcd(SELECT_SEQS_PER_STEP, B)
    spec = pl.BlockSpec((nb, E, S), lambda b: (b, 0, 0))
    return pl.pallas_call(
        functools.partial(_select_body, S=S, C=C),
        grid=(B // nb,),
        in_specs=[spec],
        out_specs=[spec, spec],
        out_shape=[jax.ShapeDtypeStruct((B, E, S), jnp.int32)] * 2,
        compiler_params=_params("parallel"),
        name="topc_select",
    )(aff_t)


GATHER_WIN = 64
SLOT_ALIGN = 16


def _gather_body(lo_ref, pos_ref, hn_ref, xg_ref, *, E, C, S, ts):
    b = pl.program_id(0)
    n_tiles = S // ts
    xg_ref[...] = jnp.zeros(xg_ref.shape, xg_ref.dtype)
    sub = lax.broadcasted_iota(jnp.int32, (GATHER_WIN, ts), 0)

    def tile_body(k, carry):
        t0 = pl.multiple_of(k * ts, ts)
        nxt = jnp.minimum(k + 1, n_tiles - 1)
        starts = [(lo_ref[b, k, e] // SLOT_ALIGN) * SLOT_ALIGN for e in range(E)]
        ends = [jnp.where(k + 1 < n_tiles, lo_ref[b, nxt, e], C) for e in range(E)]
        hn_t = hn_ref[0, pl.ds(t0, ts), :]

        def one_pass(w, first):
            sel, row0s = [], []
            for e in range(E):
                start = starts[e] + w * GATHER_WIN
                row0 = jnp.minimum(start, C - GATHER_WIN)
                pe = pos_ref[0, e:e + 1, pl.ds(t0, ts)]
                if not first:
                    pe = jnp.where(pe >= start, pe, -1)
                sel.append(jnp.where(pe - row0 == sub, 1.0, 0.0).astype(BF16))
                row0s.append(pl.multiple_of(row0, SLOT_ALIGN))
            rows = jnp.dot(jnp.concatenate(sel, axis=0), hn_t,
                           preferred_element_type=F32).astype(xg_ref.dtype)
            for e in range(E):
                xg_ref[0, e, pl.ds(row0s[e], GATHER_WIN), :] += rows[e * GATHER_WIN:(e + 1) * GATHER_WIN]

        one_pass(0, True)
        n_pass = jnp.int32(1)
        for e in range(E):
            n_pass = jnp.maximum(n_pass, (ends[e] - starts[e] + GATHER_WIN - 1) // GATHER_WIN)

        def more(w, c):
            one_pass(w, False)
            return c

        lax.fori_loop(1, n_pass, more, 0)
        return carry

    lax.fori_loop(0, n_tiles, tile_body, 0)


def _gather(lo, pos_rows, hn3, C, ts):
    B, S, D = hn3.shape
    E = pos_rows.shape[1]
    assert C % GATHER_WIN == 0
    return pl.pallas_call(
        functools.partial(_gather_body, E=E, C=C, S=S, ts=ts),
        grid=(B,),
        in_specs=[pl.BlockSpec(memory_space=pltpu.SMEM),
                  pl.BlockSpec((1, E, S), lambda b: (b, 0, 0)),
                  pl.BlockSpec((1, S, D), lambda b: (b, 0, 0))],
        out_specs=pl.BlockSpec((1, E, C, D), lambda b: (b, 0, 0, 0)),
        out_shape=jax.ShapeDtypeStruct((B, E, C, D), BF16),
        compiler_params=_params("parallel"),
        name="moe_gather",
    )(lo, pos_rows, hn3)


def _ffn_body(pos_ref, aff_ref, xg_ref, wg_hbm, wu_hbm, wd_hbm, y_ref,
              wg_s, wu_s, wd_s, stg_g, stg_u, stg_d, sems, *, C, S, fc):
    e = pl.program_id(0)
    b = pl.program_id(1)
    n_experts = pl.num_programs(0)
    rows_in = stg_g.shape[0]
    rows_dn = stg_d.shape[0]
    n_chunks = wg_s.shape[1] // rows_in

    def chunk_copies(ee, k):
        r_in = pl.multiple_of(k * rows_in, rows_in)
        r_dn = pl.multiple_of(k * rows_dn, rows_dn)
        return (pltpu.make_async_copy(wg_hbm.at[ee, pl.ds(r_in, rows_in), :], stg_g, sems.at[0]),
                pltpu.make_async_copy(wu_hbm.at[ee, pl.ds(r_in, rows_in), :], stg_u, sems.at[1]),
                pltpu.make_async_copy(wd_hbm.at[ee, pl.ds(r_dn, rows_dn), :], stg_d, sems.at[2]))

    def cast_chunk(slot_, k):
        r_in = pl.multiple_of(k * rows_in, rows_in)
        r_dn = pl.multiple_of(k * rows_dn, rows_dn)
        wg_s[slot_, pl.ds(r_in, rows_in), :] = stg_g[...].astype(BF16)
        wu_s[slot_, pl.ds(r_in, rows_in), :] = stg_u[...].astype(BF16)
        wd_s[slot_, pl.ds(r_dn, rows_dn), :] = stg_d[...].astype(BF16)

    @pl.when((e == 0) & (b == 0))
    def _():
        def stage(k, carry):
            for cp in chunk_copies(0, k):
                cp.start()
            for cp in chunk_copies(0, k):
                cp.wait()
            cast_chunk(0, k)
            return carry

        lax.fori_loop(0, n_chunks, stage, 0)

        @pl.when(n_experts > 1)
        def _():
            for cp in chunk_copies(1, 0):
                cp.start()

    cur = e % 2
    nb = xg_ref.shape[0]
    slot = lax.broadcasted_iota(jnp.int32, (C, S), 0)
    vals = [jnp.sum(jnp.where(pos_ref[t, 0] == slot, aff_ref[t, 0], 0.0), axis=1, keepdims=True)
            for t in range(nb)]
    val = jnp.concatenate(vals, axis=0)
    xg = xg_ref[...].reshape(nb * C, xg_ref.shape[3])
    F = wg_s.shape[2]
    y = jnp.zeros(xg.shape, F32)
    for f0 in range(0, F, fc):
        g = jnp.dot(xg, wg_s[cur, :, f0:f0 + fc], preferred_element_type=F32)
        u = jnp.dot(xg, wu_s[cur, :, f0:f0 + fc], preferred_element_type=F32)
        hid = (g * jax.nn.sigmoid(g) * u).astype(BF16)
        y = y + jnp.dot(hid, wd_s[cur, f0:f0 + fc, :], preferred_element_type=F32)
    y_ref[...] = (y * val).astype(y_ref.dtype).reshape(y_ref.shape)

    @pl.when(e + 1 < n_experts)
    def _():
        for cp in chunk_copies(e + 1, b):
            cp.wait()
        cast_chunk(1 - cur, b)

    wrap = b + 1 == n_chunks
    e_nxt = jnp.where(wrap, e + 2, e + 1)
    k_nxt = jnp.where(wrap, 0, b + 1)

    @pl.when(e_nxt < n_experts)
    def _():
        for cp in chunk_copies(e_nxt, k_nxt):
            cp.start()


FFN_SEQS_PER_STEP = 4


def _expert_ffn(pos_rows, aff_rows, xg, wg, wu, wd):
    B, _, C, D = xg.shape
    S = pos_rows.shape[-1]
    E, _, F = wg.shape
    fc = min(512, F)
    nb = math.gcd(FFN_SEQS_PER_STEP, B)
    steps = B // nb
    assert D % steps == 0 and F % steps == 0 and (D // steps) % 16 == 0 and (F // steps) % 16 == 0
    hbm = pl.BlockSpec(memory_space=pl.ANY)
    return pl.pallas_call(
        functools.partial(_ffn_body, C=C, S=S, fc=fc),
        grid=(E, steps),
        in_specs=[pl.BlockSpec((nb, 1, 1, S), lambda e, b: (b, e, 0, 0)),
                  pl.BlockSpec((nb, 1, 1, S), lambda e, b: (b, e, 0, 0)),
                  pl.BlockSpec((nb, 1, C, D), lambda e, b: (b, e, 0, 0)),
                  hbm, hbm, hbm],
        out_specs=pl.BlockSpec((nb, 1, C, D), lambda e, b: (b, e, 0, 0)),
        out_shape=jax.ShapeDtypeStruct((B, E, C, D), BF16),
        scratch_shapes=[pltpu.VMEM((2, D, F), BF16), pltpu.VMEM((2, D, F), BF16),
                        pltpu.VMEM((2, F, D), BF16),
                        pltpu.VMEM((D // steps, F), F32), pltpu.VMEM((D // steps, F), F32),
                        pltpu.VMEM((F // steps, D), F32),
                        pltpu.SemaphoreType.DMA((3,))],
        compiler_params=_params("arbitrary", "arbitrary"),
        name="expert_ffn",
    )(pos_rows, aff_rows, xg, wg, wu, wd)


COMBINE_WIN = 64


def _combine_body(lo_ref, pos_ref, y_ref, h_ref, o_ref, *, E, C, ts, n_tiles):
    b = pl.program_id(0)
    n_sub = pos_ref.shape[1] // ts
    lane = lax.broadcasted_iota(jnp.int32, (ts, LANES), 1)
    upper = lane >= COMBINE_WIN
    lane_in = jnp.where(upper, lane - COMBINE_WIN, lane)

    extra = []
    for t in range(n_sub):
        tile = pl.program_id(1) * n_sub + t
        rows = slice(t * ts, (t + 1) * ts)
        pc = pos_ref[0, rows, :]
        nxt = jnp.minimum(tile + 1, n_tiles - 1)
        starts = [(lo_ref[b, tile, e] // SLOT_ALIGN) * SLOT_ALIGN for e in range(E)]
        ends = [jnp.where(tile + 1 < n_tiles, lo_ref[b, nxt, e], C) for e in range(E)]

        def window_product(w, first, pc=pc, starts=starts):
            pieces, y_wins = [], []
            for p in range(E // 2):
                targets = []
                for e in (2 * p, 2 * p + 1):
                    start = starts[e] + w * COMBINE_WIN
                    row0 = jnp.minimum(start, C - COMBINE_WIN)
                    pe = pc[:, e:e + 1]
                    if not first:
                        pe = jnp.where(pe >= start, pe, -1)
                    targets.append(pe - row0)
                    y_wins.append(y_ref[0, e, pl.ds(pl.multiple_of(row0, SLOT_ALIGN), COMBINE_WIN), :])
                hit = jnp.where(upper, targets[1], targets[0]) == lane_in
                pieces.append(jnp.where(hit, 1.0, 0.0).astype(BF16))
            return jnp.dot(jnp.concatenate(pieces, axis=1), jnp.concatenate(y_wins, axis=0),
                           preferred_element_type=F32)

        o_ref[0, rows, :] = h_ref[0, rows, :] + window_product(0, True)

        n_pass = jnp.int32(1)
        for e in range(E):
            n_pass = jnp.maximum(n_pass, (ends[e] - starts[e] + COMBINE_WIN - 1) // COMBINE_WIN)
        extra.append((n_pass, rows, window_product))

    for n_pass, rows, window_product in extra:
        def more(w, carry, rows=rows, window_product=window_product):
            o_ref[0, rows, :] += window_product(w, False)
            return carry

        lax.fori_loop(1, n_pass, more, 0)


def _combine(lo, pos_cols, y, h3, ts):
    B, S, D = h3.shape
    _, E, C, _ = y.shape
    n_tiles = S // ts
    n_sub = math.gcd(4, n_tiles)
    tb = ts * n_sub
    assert E % 2 == 0 and C % COMBINE_WIN == 0 and 2 * COMBINE_WIN == LANES
    return pl.pallas_call(
        functools.partial(_combine_body, E=E, C=C, ts=ts, n_tiles=n_tiles),
        grid=(B, n_tiles // n_sub),
        in_specs=[pl.BlockSpec(memory_space=pltpu.SMEM),
                  pl.BlockSpec((1, tb, E), lambda b, i: (b, i, 0)),
                  pl.BlockSpec((1, E, C, D), lambda b, i: (b, 0, 0, 0)),
                  pl.BlockSpec((1, tb, D), lambda b, i: (b, i, 0))],
        out_specs=pl.BlockSpec((1, tb, D), lambda b, i: (b, i, 0)),
        out_shape=jax.ShapeDtypeStruct((B, S, D), F32),
        compiler_params=_params("parallel", "arbitrary"),
        name="moe_combine",
    )(lo, pos_cols, y, h3)


def kernel(x, g_mix, w_in, g_q, g_k, lam_q1, lam_k1, lam_q2, lam_k2, g_subln, rel_bias, conv_w, conv_b,
           gate_r_w, gate_r_b, gate_i_w, gate_i_b, lru_lambda, w_proj_attn, w_proj_lru, w_out, g_ffn,
           w_router, w_gate_e, w_up_e, w_down_e):
    B, S, D = x.shape
    depth = w_in.shape[0]
    H = rel_bias.shape[1]
    hd = g_q.shape[-1]
    vd = g_subln.shape[-1]
    qk_w = H * 2 * hd
    attn_w = H * vd
    lru_w = conv_w.shape[-1]
    NB, LB = gate_r_w.shape[2], gate_r_w.shape[3]
    E = w_router.shape[-1]
    C = EC_CAPACITY_FACTOR * S // E
    off_k = qk_w
    off_v = off_k + qk_w
    off_x = off_v + attn_w
    off_y = off_x + lru_w
    off_ga = off_y + lru_w
    off_gr = off_ga + D

    h2 = x.reshape(B * S, D)
    for layer in range(depth):
        lam_init = 0.8 - 0.6 * math.exp(-0.3 * layer)
        proj = _inproj(h2, g_mix[layer][None], w_in[layer].astype(BF16))

        lam_params = jnp.stack([lam_q1[layer], lam_k1[layer], lam_q2[layer], lam_k2[layer]])
        o_attn = _attention(proj, rel_bias, g_q[layer], g_k[layer], lam_params, g_subln[layer],
                            B=B, S=S, H=H, hd=hd, vd=vd, off_k=off_k, off_v=off_v, lam_init=lam_init)

        w_cat = jnp.concatenate([gate_r_w[layer, 0], gate_i_w[layer, 0],
                                 gate_r_w[layer, 1], gate_i_w[layer, 1]], axis=-1).astype(BF16)
        b_cat = jnp.stack([gate_r_b[layer, 0].reshape(NB, LB), gate_i_b[layer, 0].reshape(NB, LB),
                           gate_r_b[layer, 1].reshape(NB, LB), gate_i_b[layer, 1].reshape(NB, LB)],
                          axis=1).reshape(NB, 1, 4 * LB)
        lru_out = _rg_lru(proj, conv_w[layer], conv_b[layer], w_cat, b_cat, lru_lambda[layer],
                          B=B, S=S, off_x=off_x, off_y=off_y)

        wr = w_router[layer]
        wr_hi = wr.astype(BF16)
        wr2 = jnp.concatenate([wr_hi, (wr - wr_hi.astype(F32)).astype(BF16)], axis=1)
        h2, hn, aff = _merge(o_attn, lru_out, proj, h2, w_proj_attn[layer].astype(BF16),
                             w_proj_lru[layer].astype(BF16), w_out[layer].astype(BF16),
                             g_ffn[layer][None], wr2, off_ga=off_ga, off_gr=off_gr, E=E)

        aff_rows = aff.reshape(B, S, E).transpose(0, 2, 1)
        pos_rows, cnt_rows = _select(aff_rows, C)
        ts = min(256, S)
        lo = cnt_rows[:, :, ::ts].transpose(0, 2, 1)
        xg = _gather(lo, pos_rows, hn.reshape(B, S, D), C, ts)
        y = _expert_ffn(pos_rows.reshape(B, E, 1, S), aff_rows.reshape(B, E, 1, S), xg,
                        w_gate_e[layer], w_up_e[layer], w_down_e[layer])
        h3 = _combine(lo, pos_rows.transpose(0, 2, 1), y, h2.reshape(B, S, D), ts)
        h2 = h3.reshape(B * S, D)
    return h2.reshape(B, S, D)
```

```python
import functools
import math

import jax
import jax.numpy as jnp
import numpy as np
from jax import lax
from jax.experimental import pallas as pl
from jax.experimental.pallas import tpu as pltpu

F32 = jnp.float32
BF16 = jnp.bfloat16
EPS = 1e-6
LANES = 128
SUBLANES = 8
VMEM_LIMIT = 56 * 1024 * 1024
LOG2E = 1.4426950408889634
NEG_BIG = -1e30
REL_MAX_DIST = 128
LRU_C = 8.0
EC_CAPACITY_FACTOR = 2


def _params(*sem):
    return pltpu.CompilerParams(dimension_semantics=sem, vmem_limit_bytes=VMEM_LIMIT)


def _inproj_body(x_ref, g_ref, w_ref, o_ref, xn_ref):
    @pl.when(pl.program_id(1) == 0)
    def _():
        x = x_ref[...]
        ms = jnp.mean(x * x, axis=-1, keepdims=True)
        xn_ref[...] = (x * lax.rsqrt(ms + EPS) * g_ref[...]).astype(BF16)

    o_ref[...] = jnp.dot(xn_ref[...], w_ref[...], preferred_element_type=F32).astype(o_ref.dtype)


def _inproj(x2, g, w_bf):
    T, D = x2.shape
    N = w_bf.shape[1]
    tm = min(1024, T)
    tn = N // 4 if N % (4 * LANES) == 0 else N
    return pl.pallas_call(
        _inproj_body,
        grid=(T // tm, N // tn),
        in_specs=[pl.BlockSpec((tm, D), lambda i, j: (i, 0)),
                  pl.BlockSpec((1, D), lambda i, j: (0, 0)),
                  pl.BlockSpec((D, tn), lambda i, j: (0, j))],
        out_specs=pl.BlockSpec((tm, tn), lambda i, j: (i, j)),
        out_shape=jax.ShapeDtypeStruct((T, N), BF16),
        scratch_shapes=[pltpu.VMEM((tm, D), BF16)],
        compiler_params=_params("parallel", "arbitrary"),
        name="in_proj",
    )(x2, g, w_bf)


def _bucket_table(S, n_buckets, rel):
    half = n_buckets // 2
    max_exact = half // 2
    ret = np.where(rel > 0, half, 0)
    n = np.abs(rel)
    nf = np.maximum(n, max_exact).astype(np.float64)
    large = max_exact + (np.log(nf / max_exact) / math.log(REL_MAX_DIST / max_exact)
                         * (half - max_exact)).astype(np.int32)
    large = np.minimum(large, half - 1)
    return (ret + np.where(n < max_exact, n, large)).astype(np.int32).reshape(1, 2 * S)


def _seg_sumsq(x, seg_ones, split=True):
    x2 = x * x
    hi = x2.astype(BF16)
    out = jnp.dot(hi, seg_ones, preferred_element_type=F32)
    if split:
        lo = (x2 - hi.astype(F32)).astype(BF16)
        out = out + jnp.dot(lo, seg_ones, preferred_element_type=F32)
    return out


NORM_SLACK = 1.01
SAFE_LOG2_SPAN = 100.0


def _attn_body(bucket_ref, bucket_t_ref, relb_ref, q_ref, k_ref, v_ref, gq_ref, gk_ref, lamp_ref, gs_ref,
               segq_ref, segv_ref, o_ref, e_ref, et_ref, fast_ref, kn_ref, va_ref, vat_ref, m_ref, acc_ref,
               *, S, tq, n_sub, kc, hd, vd, lam_init, n_buckets):
    h = pl.program_id(0)
    b = pl.program_id(1)
    i = pl.program_id(2)
    q_scale = hd ** -0.5 * LOG2E

    @pl.when((b == 0) & (i == 0))
    def _():
        def bias_row(bk):
            tab = jnp.zeros((1, 2 * S), F32)
            for n in range(n_buckets):
                tab = jnp.where(bk == n, relb_ref[h, n], tab)
            return tab * LOG2E

        tab = bias_row(bucket_ref[...])
        bmax = jnp.max(tab, axis=-1, keepdims=True)
        bmin = jnp.min(tab, axis=-1, keepdims=True)
        bound = (NORM_SLACK * hd * q_scale) * (jnp.max(jnp.abs(gq_ref[...]), axis=-1, keepdims=True)
                                               * jnp.max(jnp.abs(gk_ref[...]), axis=-1, keepdims=True))
        span = 2.0 * bound + (bmax - bmin)
        fast_ref[0] = (span[0, 0] <= SAFE_LOG2_SPAN).astype(jnp.int32)
        shift = bound + bmax
        e_ref[...] = pltpu.roll(jnp.broadcast_to(tab - shift, (tq, 2 * S)), 0, 1, stride=1, stride_axis=0)
        tab_t = bias_row(bucket_t_ref[...])
        et_ref[...] = pltpu.roll(jnp.broadcast_to(tab_t - shift, (tq, 2 * S)), 0, 1, stride=1, stride_axis=0)

    @pl.when(i == 0)
    def _():
        k = k_ref[...].astype(F32)
        ms = _seg_sumsq(k, segq_ref[...], split=False) * (1.0 / hd)
        kn_ref[...] = (k * lax.rsqrt(ms + EPS) * gk_ref[...]).astype(BF16)
        va_ref[:, :vd] = v_ref[...]
        va_ref[:, vd:] = jnp.ones((S, vd), BF16)
        vat_ref[:vd, :] = v_ref[...].astype(F32).T.astype(BF16)
        vat_ref[vd:, :] = jnp.ones((vat_ref.shape[0] - vd, S), BF16)

    nt_dims = (((1,), (1,)), ((), ()))
    lp = lamp_ref[...]
    lam = (jnp.exp(jnp.sum(lp[0:1] * lp[1:2], axis=-1, keepdims=True))
           - jnp.exp(jnp.sum(lp[2:3] * lp[3:4], axis=-1, keepdims=True)) + lam_init)

    def q_maps_of(t):
        q = q_ref[t * tq:(t + 1) * tq, :].astype(F32)
        lo = lax.broadcasted_iota(jnp.int32, q.shape, 1) < hd
        q2 = q * q
        s_lo = jnp.sum(jnp.where(lo, q2, 0.0), axis=-1, keepdims=True)
        s_hi = jnp.sum(jnp.where(lo, 0.0, q2), axis=-1, keepdims=True)
        ms = jnp.where(lo, s_lo, s_hi) * (1.0 / hd)
        qn = q * lax.rsqrt(ms + EPS) * (gq_ref[...] * q_scale)
        return jnp.where(lo, qn, 0.0).astype(BF16), jnp.where(lo, 0.0, qn).astype(BF16)

    def e_start(t, k0):
        return pl.multiple_of(S + k0 - (i * n_sub + t) * tq, LANES)

    def finish(t, a1, a2):
        o = a1[:, :vd] / a1[:, vd:] - lam * (a2[:, :vd] / a2[:, vd:])
        o = o * lax.rsqrt(_seg_sumsq(o, segv_ref[...]) * (1.0 / vd) + EPS)
        o_ref[t * tq:(t + 1) * tq, :] = (o * (gs_ref[...] * (1.0 - lam_init))).astype(o_ref.dtype)

    def finish_t(t, a1, a2):
        ot = a1[:vd] / a1[vd:vd + 1] - lam * (a2[:vd] / a2[vd:vd + 1])
        ot = ot * lax.rsqrt(jnp.mean(ot * ot, axis=0, keepdims=True) + EPS)
        o = ot.T
        o_ref[t * tq:(t + 1) * tq, :] = (o * (gs_ref[...] * (1.0 - lam_init))).astype(o_ref.dtype)

    @pl.when(fast_ref[0] == 1)
    def _():
        for t in range(n_sub):
            q_maps = q_maps_of(t)
            tile = i * n_sub + t
            accs = []
            for mi in range(2):
                chunks = []
                for c in range(S // tq):
                    st = lax.dot_general(kn_ref[c * tq:(c + 1) * tq, :], q_maps[mi], nt_dims,
                                         preferred_element_type=F32)
                    start = pl.multiple_of(S + (tile - c) * tq, LANES)
                    chunks.append(jnp.exp2(st + et_ref[:, pl.ds(start, tq)]).astype(BF16))
                accs.append(jnp.dot(vat_ref[...], jnp.concatenate(chunks, axis=0),
                                    preferred_element_type=F32))
            finish_t(t, *accs)

    @pl.when(fast_ref[0] != 1)
    def _():
        for t in range(n_sub):
            q_maps = q_maps_of(t)
            m_ref[...] = jnp.full(m_ref.shape, NEG_BIG, F32)
            acc_ref[...] = jnp.zeros(acc_ref.shape, F32)

            def chunk(c, carry):
                k0 = pl.multiple_of(c * kc, kc)
                bias = e_ref[:, pl.ds(e_start(t, k0), kc)]
                for mi in range(2):
                    s = lax.dot_general(q_maps[mi], kn_ref[pl.ds(k0, kc), :], nt_dims,
                                        preferred_element_type=F32) + bias
                    m_old = m_ref[mi]
                    m_new = jnp.maximum(m_old, jnp.max(s, axis=-1, keepdims=True))
                    p = jnp.exp2(s - m_new).astype(BF16)
                    acc_ref[mi] = (jnp.exp2(m_old - m_new) * acc_ref[mi]
                                   + jnp.dot(p, va_ref[pl.ds(k0, kc), :], preferred_element_type=F32))
                    m_ref[mi] = m_new
                return carry

            lax.fori_loop(0, S // kc, chunk, 0)
            finish(t, acc_ref[0], acc_ref[1])


def _attention(proj, rel_bias, g_q, g_k, lam_params, g_subln, *, B, S, H, hd, vd, off_k, off_v, lam_init):
    T = B * S
    tq = min(512, S)
    kc = min(512, S)
    n_sub = math.gcd(4, S // tq)
    tqb = n_sub * tq
    nq = S // tqb
    n_buckets = rel_bias.shape[0]
    hw = 2 * hd
    bucket = jnp.asarray(_bucket_table(S, n_buckets, np.arange(2 * S) - S))
    bucket_t = jnp.asarray(_bucket_table(S, n_buckets, S - np.arange(2 * S)))
    body = functools.partial(_attn_body, S=S, tq=tq, n_sub=n_sub, kc=kc, hd=hd, vd=vd, lam_init=lam_init,
                             n_buckets=n_buckets)
    kblk = off_k // hw
    vblk = off_v // vd
    seg_q = np.kron(np.eye(2, dtype=np.float32), np.ones((hd, hd), np.float32))
    return pl.pallas_call(
        body,
        grid=(H, B, nq),
        in_specs=[pl.BlockSpec((1, 2 * S), lambda h, b, i: (0, 0)),
                  pl.BlockSpec((1, 2 * S), lambda h, b, i: (0, 0)),
                  pl.BlockSpec(memory_space=pltpu.SMEM),
                  pl.BlockSpec((tqb, hw), lambda h, b, i: (b * nq + i, h)),
                  pl.BlockSpec((S, hw), lambda h, b, i: (b, kblk + h)),
                  pl.BlockSpec((S, vd), lambda h, b, i: (b, vblk + h)),
                  pl.BlockSpec((1, hw), lambda h, b, i: (0, 0)),
                  pl.BlockSpec((1, hw), lambda h, b, i: (0, 0)),
                  pl.BlockSpec((4, hd), lambda h, b, i: (0, 0)),
                  pl.BlockSpec((1, vd), lambda h, b, i: (0, 0)),
                  pl.BlockSpec((hw, hw), lambda h, b, i: (0, 0)),
                  pl.BlockSpec((vd, vd), lambda h, b, i: (0, 0))],
        out_specs=pl.BlockSpec((tqb, vd), lambda h, b, i: (b * nq + i, h)),
        out_shape=jax.ShapeDtypeStruct((T, H * vd), BF16),
        scratch_shapes=[pltpu.VMEM((tq, 2 * S), F32),
                        pltpu.VMEM((tq, 2 * S), F32),
                        pltpu.SMEM((1,), jnp.int32),
                        pltpu.VMEM((S, hw), BF16),
                        pltpu.VMEM((S, 2 * vd), BF16),
                        pltpu.VMEM((vd + 16, S), BF16),
                        pltpu.VMEM((2, tq, 1), F32),
                        pltpu.VMEM((2, tq, 2 * vd), F32)],
        compiler_params=_params("arbitrary", "arbitrary", "arbitrary"),
        name="diff_attention",
    )(bucket, bucket_t, rel_bias.T, proj, proj, proj, jnp.tile(g_q, 2)[None], jnp.tile(g_k, 2)[None],
      lam_params, g_subln[None], jnp.asarray(seg_q, BF16), jnp.ones((vd, vd), BF16))


LRU_SEGMENTS = SUBLANES
LRU_PITCH_PAD = 8


def _sublane_scan(a, bb, reverse):
    ri = lax.broadcasted_iota(jnp.int32, a.shape, 0)
    for d in (1, 2, 4):
        if reverse:
            keep, sh = ri < SUBLANES - d, SUBLANES - d
        else:
            keep, sh = ri >= d, d
        bb = bb + a * jnp.where(keep, pltpu.roll(bb, sh, 0), 0.0)
        a = a * jnp.where(keep, pltpu.roll(a, sh, 0), 1.0)
    return bb


LRU_BLOCKS_PER_STEP = 2


def _lru_body(x_ref, y_ref, cw_ref, cb_ref, w_ref, lam_ref, o_ref,
              af_ref, bf_ref, ab_ref, bb_ref, hf_ref, pf_ref, hb_ref, pb_ref, hs_ref, *, S, W):
    nb = w_ref.shape[0]
    L = S // LRU_SEGMENTS
    pitch = L + LRU_PITCH_PAD
    x = x_ref[...].astype(F32)
    row = lax.broadcasted_iota(jnp.int32, x.shape, 0)
    cw = cw_ref[...]
    xc_all = (cw[0:1] * jnp.where(row >= 2, pltpu.roll(x, 2, 0), 0.0)
              + cw[1:2] * jnp.where(row >= 1, pltpu.roll(x, 1, 0), 0.0)
              + cw[2:3] * x
              + cw[3:4] * jnp.where(row < S - 1, pltpu.roll(x, S - 1, 0), 0.0)
              + cb_ref[...])
    lam = lam_ref[...]
    bias_lanes = jnp.where(lax.broadcasted_iota(jnp.int32, (S, W), 1) < 2, 1.0, 0.0).astype(BF16)
    for n in range(nb):
        xc = xc_all[:, n * W:(n + 1) * W]
        gneg = jnp.dot(jnp.concatenate([xc.astype(BF16), bias_lanes], axis=1), w_ref[n],
                       preferred_element_type=F32)
        for d, (a_ref, b_ref) in enumerate(((af_ref, bf_ref), (ab_ref, bb_ref))):
            r = 1.0 / (1.0 + jnp.exp2(gneg[:, (2 * d) * W:(2 * d + 1) * W]))
            ig = 1.0 / (1.0 + jnp.exp2(gneg[:, (2 * d + 1) * W:(2 * d + 2) * W]))
            nl = -lam[d:d + 1, n * W:(n + 1) * W]
            softplus = jnp.maximum(nl, 0.0) + jnp.log1p(jnp.exp(-jnp.abs(nl)))
            a = jnp.exp2((-LRU_C * LOG2E * softplus) * r)
            y2 = 1.0 - a * a
            bvals = jnp.where(y2 > 0.0, y2 * lax.rsqrt(y2), 0.0) * ig * xc
            for sg in range(LRU_SEGMENTS):
                a_ref[n, sg * pitch:sg * pitch + L, :] = a[sg * L:(sg + 1) * L]
                b_ref[n, sg * pitch:sg * pitch + L, :] = bvals[sg * L:(sg + 1) * L]

    def seg_rows(ref, n, j):
        return ref[n, pl.ds(j, LRU_SEGMENTS, stride=pitch), :]

    def step(j, carry):
        out = []
        jb = L - 1 - j
        for n in range(nb):
            hf, pf, hb, pb = carry[4 * n:4 * n + 4]
            a = seg_rows(af_ref, n, j)
            hf = a * hf + seg_rows(bf_ref, n, j)
            pf = a * pf
            hf_ref[n, j] = hf
            pf_ref[n, j] = pf
            a = seg_rows(ab_ref, n, jb)
            hb = a * hb + seg_rows(bb_ref, n, jb)
            pb = a * pb
            hb_ref[n, jb] = hb
            pb_ref[n, jb] = pb
            out += [hf, pf, hb, pb]
        return tuple(out)

    zero = jnp.zeros((LRU_SEGMENTS, W), F32)
    one = jnp.ones((LRU_SEGMENTS, W), F32)
    ends = lax.fori_loop(0, L, step, (zero, one, zero, one) * nb, unroll=8)

    si = lax.broadcasted_iota(jnp.int32, (LRU_SEGMENTS, W), 0)
    cfs, cbs = [], []
    for n in range(nb):
        hf, pf, hb, pb = ends[4 * n:4 * n + 4]
        cfs.append(jnp.where(si >= 1, pltpu.roll(_sublane_scan(pf, hf, False), 1, 0), 0.0))
        cbs.append(jnp.where(si < LRU_SEGMENTS - 1,
                             pltpu.roll(_sublane_scan(pb, hb, True), LRU_SEGMENTS - 1, 0), 0.0))

    def fix(j, carry):
        for n in range(nb):
            hs_ref[n, pl.ds(j, LRU_SEGMENTS, stride=pitch), :] = (
                hf_ref[n, j] + pf_ref[n, j] * cfs[n] + hb_ref[n, j] + pb_ref[n, j] * cbs[n])
        return carry

    lax.fori_loop(0, L, fix, 0, unroll=8)

    for n in range(nb):
        for sg in range(LRU_SEGMENTS):
            y = y_ref[sg * L:(sg + 1) * L, n * W:(n + 1) * W].astype(F32)
            o_ref[sg * L:(sg + 1) * L, n * W:(n + 1) * W] = (
                hs_ref[n, sg * pitch:sg * pitch + L, :] * jax.nn.gelu(y)).astype(o_ref.dtype)


def _rg_lru(proj, conv_w, conv_b, w_ext, lru_lambda, *, B, S, off_x, off_y):
    T = B * S
    NB, W2, _ = w_ext.shape
    W = W2 // 2
    nb = math.gcd(LRU_BLOCKS_PER_STEP, NB)
    wb = nb * W
    xblk = off_x // wb
    yblk = off_y // wb
    body = functools.partial(_lru_body, S=S, W=W)
    padded = LRU_SEGMENTS * (S // LRU_SEGMENTS + LRU_PITCH_PAD)
    return pl.pallas_call(
        body,
        grid=(B, NB // nb),
        in_specs=[pl.BlockSpec((S, wb), lambda b, n: (b, xblk + n)),
                  pl.BlockSpec((S, wb), lambda b, n: (b, yblk + n)),
                  pl.BlockSpec((conv_w.shape[0], wb), lambda b, n: (0, n)),
                  pl.BlockSpec((1, wb), lambda b, n: (0, n)),
                  pl.BlockSpec((nb, 2 * W, 4 * W), lambda b, n: (n, 0, 0)),
                  pl.BlockSpec((2, wb), lambda b, n: (0, n))],
        out_specs=pl.BlockSpec((S, wb), lambda b, n: (b, n)),
        out_shape=jax.ShapeDtypeStruct((T, NB * W), BF16),
        scratch_shapes=([pltpu.VMEM((nb, padded, W), F32)] * 4
                        + [pltpu.VMEM((nb, S // LRU_SEGMENTS, LRU_SEGMENTS, W), F32)] * 4
                        + [pltpu.VMEM((nb, padded, W), F32)]),
        compiler_params=_params("parallel", "parallel"),
        name="rg_lru",
    )(proj, proj, conv_w, conv_b[None], w_ext, lru_lambda)


def _merge_body(o_ref, r_ref, ga_ref, gr_ref, x_ref, wpa_ref, wpl_ref, wo_ref, g_ref, wr_ref,
                h_ref, hn_ref, aff_ref, *, E, n_sub):
    ts = o_ref.shape[0] // n_sub
    for t in range(n_sub):
        rows = slice(t * ts, (t + 1) * ts)
        ba = jnp.dot(o_ref[rows, :], wpa_ref[...], preferred_element_type=F32)
        br = jnp.dot(r_ref[rows, :], wpl_ref[...], preferred_element_type=F32)
        mixed = (jax.nn.sigmoid(ga_ref[rows, :].astype(F32)) * ba
                 + jax.nn.sigmoid(gr_ref[rows, :].astype(F32)) * br)
        h = x_ref[rows, :] + jnp.dot(mixed.astype(BF16), wo_ref[...], preferred_element_type=F32)
        h_ref[rows, :] = h
        hn = h * lax.rsqrt(jnp.mean(h * h, axis=-1, keepdims=True) + EPS) * g_ref[...]
        hn_hi = hn.astype(BF16)
        hn_ref[rows, :] = hn_hi
        hn_lo = (hn - hn_hi.astype(F32)).astype(BF16)
        lg = (jnp.dot(hn_hi, wr_ref[...], preferred_element_type=F32)
              + jnp.dot(hn_lo, wr_ref[...], preferred_element_type=F32))
        logits = lg[:, :E] + lg[:, E:2 * E]
        logits = logits - jnp.max(logits, axis=-1, keepdims=True)
        ex = jnp.exp(logits)
        aff_ref[rows, :] = ex / jnp.sum(ex, axis=-1, keepdims=True)


def _merge(o_attn, lru_out, proj, x2, wpa, wpl, wo, g_ffn, wr2, *, off_ga, off_gr, E):
    T, D = x2.shape
    tm = min(1024, T)
    n_sub = 2 if tm % 1024 == 0 else 1
    resident = dict(pipeline_mode=pl.Buffered(1))
    gab = off_ga // D
    grb = off_gr // D
    row = lambda i: (i, 0)
    const = lambda i: (0, 0)
    return pl.pallas_call(
        functools.partial(_merge_body, E=E, n_sub=n_sub),
        grid=(T // tm,),
        in_specs=[pl.BlockSpec((tm, D), row), pl.BlockSpec((tm, D), row),
                  pl.BlockSpec((tm, D), lambda i: (i, gab)), pl.BlockSpec((tm, D), lambda i: (i, grb)),
                  pl.BlockSpec((tm, D), row),
                  pl.BlockSpec((D, D), const, **resident), pl.BlockSpec((D, D), const, **resident),
                  pl.BlockSpec((D, D), const, **resident),
                  pl.BlockSpec((1, D), const), pl.BlockSpec((D, 2 * E), const)],
        out_specs=[pl.BlockSpec((tm, D), row), pl.BlockSpec((tm, D), row), pl.BlockSpec((tm, E), row)],
        out_shape=[jax.ShapeDtypeStruct((T, D), F32), jax.ShapeDtypeStruct((T, D), BF16),
                   jax.ShapeDtypeStruct((T, E), F32)],
        compiler_params=_params("parallel"),
        name="merge_router",
    )(o_attn, lru_out, proj, proj, x2, wpa, wpl, wo, g_ffn, wr2)


SELECT_SEQS_PER_STEP = 8


def _select_body(aff_ref, pos_ref, cnt_ref, *, S, C):
    rows = aff_ref.shape[0] * aff_ref.shape[1]
    bits = pltpu.bitcast(aff_ref[...].reshape(rows, S), jnp.int32)

    def refine(k, t):
        cand = t | jnp.left_shift(jnp.int32(1), 30 - k)
        cnt = jnp.sum(jnp.where(bits >= cand, 1.0, 0.0), axis=-1, keepdims=True)
        return jnp.where(cnt >= C, cand, t)

    t = lax.fori_loop(0, 31, refine, jnp.zeros((rows, 1), jnp.int32))
    gt = bits > t
    eq = bits == t
    need = C - jnp.sum(jnp.where(gt, 1.0, 0.0), axis=-1, keepdims=True).astype(jnp.int32)
    packed = jnp.where(gt, 1 << 16, 0) + jnp.where(eq, 1, 0)
    lane = lax.broadcasted_iota(jnp.int32, (rows, S), 1)
    incl = packed
    d = 1
    while d < S:
        incl = incl + jnp.where(lane >= d, pltpu.roll(incl, d, 1), 0)
        d *= 2
    excl = incl - packed
    n_gt = excl >> 16
    n_eq = excl & 0xFFFF
    sel = gt | (eq & (n_eq < need))
    before = n_gt + jnp.minimum(n_eq, need)
    cnt_ref[...] = before.reshape(cnt_ref.shape)
    pos_ref[...] = jnp.where(sel, before, -1).reshape(pos_ref.shape)


def _select(aff_t, C):
    B, E, S = aff_t.shape
    nb = math.gcd(SELECT_SEQS_PER_STEP, B)
    spec = pl.BlockSpec((nb, E, S), lambda b: (b, 0, 0))
    return pl.pallas_call(
        functools.partial(_select_body, S=S, C=C),
        grid=(B // nb,),
        in_specs=[spec],
        out_specs=[spec, spec],
        out_shape=[jax.ShapeDtypeStruct((B, E, S), jnp.int32)] * 2,
        compiler_params=_params("parallel"),
        name="topc_select",
    )(aff_t)


GATHER_WIN = 64
SLOT_ALIGN = 16


def _gather_body(lo_ref, pos_ref, hn_ref, xg_ref, *, E, C, S, ts):
    b = pl.program_id(0)
    n_tiles = S // ts
    xg_ref[...] = jnp.zeros(xg_ref.shape, xg_ref.dtype)
    sub = lax.broadcasted_iota(jnp.int32, (GATHER_WIN, ts), 0)

    def tile_body(k, carry):
        t0 = pl.multiple_of(k * ts, ts)
        nxt = jnp.minimum(k + 1, n_tiles - 1)
        starts = [(lo_ref[b, k, e] // SLOT_ALIGN) * SLOT_ALIGN for e in range(E)]
        ends = [jnp.where(k + 1 < n_tiles, lo_ref[b, nxt, e], C) for e in range(E)]
        hn_t = hn_ref[0, pl.ds(t0, ts), :]

        def one_pass(w, first):
            sel, row0s = [], []
            for e in range(E):
                start = starts[e] + w * GATHER_WIN
                row0 = jnp.minimum(start, C - GATHER_WIN)
                pe = pos_ref[0, e:e + 1, pl.ds(t0, ts)]
                if not first:
                    pe = jnp.where(pe >= start, pe, -1)
                sel.append(jnp.where(pe - row0 == sub, 1.0, 0.0).astype(BF16))
                row0s.append(pl.multiple_of(row0, SLOT_ALIGN))
            rows = jnp.dot(jnp.concatenate(sel, axis=0), hn_t,
                           preferred_element_type=F32).astype(xg_ref.dtype)
            for e in range(E):
                xg_ref[0, e, pl.ds(row0s[e], GATHER_WIN), :] += rows[e * GATHER_WIN:(e + 1) * GATHER_WIN]

        one_pass(0, True)
        n_pass = jnp.int32(1)
        for e in range(E):
            n_pass = jnp.maximum(n_pass, (ends[e] - starts[e] + GATHER_WIN - 1) // GATHER_WIN)

        def more(w, c):
            one_pass(w, False)
            return c

        lax.fori_loop(1, n_pass, more, 0)
        return carry

    lax.fori_loop(0, n_tiles, tile_body, 0)


def _gather(lo, pos_rows, hn3, C, ts):
    B, S, D = hn3.shape
    E = pos_rows.shape[1]
    assert C % GATHER_WIN == 0
    return pl.pallas_call(
        functools.partial(_gather_body, E=E, C=C, S=S, ts=ts),
        grid=(B,),
        in_specs=[pl.BlockSpec(memory_space=pltpu.SMEM),
                  pl.BlockSpec((1, E, S), lambda b: (b, 0, 0)),
                  pl.BlockSpec((1, S, D), lambda b: (b, 0, 0))],
        out_specs=pl.BlockSpec((1, E, C, D), lambda b: (b, 0, 0, 0)),
        out_shape=jax.ShapeDtypeStruct((B, E, C, D), BF16),
        compiler_params=_params("parallel"),
        name="moe_gather",
    )(lo, pos_rows, hn3)


def _ffn_body(pos_ref, aff_ref, xg_ref, wg_hbm, wu_hbm, wd_hbm, y_ref,
              wg_s, wu_s, wd_s, stg_g, stg_u, stg_d, sems, *, C, S, fc):
    e = pl.program_id(0)
    b = pl.program_id(1)
    n_experts = pl.num_programs(0)
    rows_in = stg_g.shape[0]
    rows_dn = stg_d.shape[0]
    n_chunks = wg_s.shape[1] // rows_in

    def chunk_copies(ee, k):
        r_in = pl.multiple_of(k * rows_in, rows_in)
        r_dn = pl.multiple_of(k * rows_dn, rows_dn)
        return (pltpu.make_async_copy(wg_hbm.at[ee, pl.ds(r_in, rows_in), :], stg_g, sems.at[0]),
                pltpu.make_async_copy(wu_hbm.at[ee, pl.ds(r_in, rows_in), :], stg_u, sems.at[1]),
                pltpu.make_async_copy(wd_hbm.at[ee, pl.ds(r_dn, rows_dn), :], stg_d, sems.at[2]))

    def cast_chunk(slot_, k):
        r_in = pl.multiple_of(k * rows_in, rows_in)
        r_dn = pl.multiple_of(k * rows_dn, rows_dn)
        wg_s[slot_, pl.ds(r_in, rows_in), :] = stg_g[...].astype(BF16)
        wu_s[slot_, pl.ds(r_in, rows_in), :] = stg_u[...].astype(BF16)
        wd_s[slot_, pl.ds(r_dn, rows_dn), :] = stg_d[...].astype(BF16)

    @pl.when((e == 0) & (b == 0))
    def _():
        def stage(k, carry):
            for cp in chunk_copies(0, k):
                cp.start()
            for cp in chunk_copies(0, k):
                cp.wait()
            cast_chunk(0, k)
            return carry

        lax.fori_loop(0, n_chunks, stage, 0)

        @pl.when(n_experts > 1)
        def _():
            for cp in chunk_copies(1, 0):
                cp.start()

    cur = e % 2
    nb = xg_ref.shape[0]
    slot = lax.broadcasted_iota(jnp.int32, (C, S), 0)
    vals = [jnp.sum(jnp.where(pos_ref[t, 0] == slot, aff_ref[t, 0], 0.0), axis=1, keepdims=True)
            for t in range(nb)]
    val = jnp.concatenate(vals, axis=0)
    xg = xg_ref[...].reshape(nb * C, xg_ref.shape[3])
    F = wg_s.shape[2]
    y = jnp.zeros(xg.shape, F32)
    for f0 in range(0, F, fc):
        g = jnp.dot(xg, wg_s[cur, :, f0:f0 + fc], preferred_element_type=F32)
        u = jnp.dot(xg, wu_s[cur, :, f0:f0 + fc], preferred_element_type=F32)
        hid = (g * jax.nn.sigmoid(g) * u).astype(BF16)
        y = y + jnp.dot(hid, wd_s[cur, f0:f0 + fc, :], preferred_element_type=F32)
    y_ref[...] = (y * val).astype(y_ref.dtype).reshape(y_ref.shape)

    @pl.when(e + 1 < n_experts)
    def _():
        for cp in chunk_copies(e + 1, b):
            cp.wait()
        cast_chunk(1 - cur, b)

    wrap = b + 1 == n_chunks
    e_nxt = jnp.where(wrap, e + 2, e + 1)
    k_nxt = jnp.where(wrap, 0, b + 1)

    @pl.when(e_nxt < n_experts)
    def _():
        for cp in chunk_copies(e_nxt, k_nxt):
            cp.start()


FFN_SEQS_PER_STEP = 4


def _expert_ffn(pos_rows, aff_rows, xg, wg, wu, wd):
    B, _, C, D = xg.shape
    S = pos_rows.shape[-1]
    E, _, F = wg.shape
    fc = min(512, F)
    nb = math.gcd(FFN_SEQS_PER_STEP, B)
    steps = B // nb
    assert D % steps == 0 and F % steps == 0 and (D // steps) % 16 == 0 and (F // steps) % 16 == 0
    hbm = pl.BlockSpec(memory_space=pl.ANY)
    return pl.pallas_call(
        functools.partial(_ffn_body, C=C, S=S, fc=fc),
        grid=(E, steps),
        in_specs=[pl.BlockSpec((nb, 1, 1, S), lambda e, b: (b, e, 0, 0)),
                  pl.BlockSpec((nb, 1, 1, S), lambda e, b: (b, e, 0, 0)),
                  pl.BlockSpec((nb, 1, C, D), lambda e, b: (b, e, 0, 0)),
                  hbm, hbm, hbm],
        out_specs=pl.BlockSpec((nb, 1, C, D), lambda e, b: (b, e, 0, 0)),
        out_shape=jax.ShapeDtypeStruct((B, E, C, D), BF16),
        scratch_shapes=[pltpu.VMEM((2, D, F), BF16), pltpu.VMEM((2, D, F), BF16),
                        pltpu.VMEM((2, F, D), BF16),
                        pltpu.VMEM((D // steps, F), F32), pltpu.VMEM((D // steps, F), F32),
                        pltpu.VMEM((F // steps, D), F32),
                        pltpu.SemaphoreType.DMA((3,))],
        compiler_params=_params("arbitrary", "arbitrary"),
        name="expert_ffn",
    )(pos_rows, aff_rows, xg, wg, wu, wd)


COMBINE_WIN = 64


def _combine_body(lo_ref, pos_ref, y_ref, h_ref, o_ref, *, E, C, ts, n_tiles):
    b = pl.program_id(0)
    n_sub = pos_ref.shape[1] // ts
    lane = lax.broadcasted_iota(jnp.int32, (ts, LANES), 1)
    upper = lane >= COMBINE_WIN
    lane_in = jnp.where(upper, lane - COMBINE_WIN, lane)

    extra = []
    for t in range(n_sub):
        tile = pl.program_id(1) * n_sub + t
        rows = slice(t * ts, (t + 1) * ts)
        pc = pos_ref[0, rows, :]
        nxt = jnp.minimum(tile + 1, n_tiles - 1)
        starts = [(lo_ref[b, tile, e] // SLOT_ALIGN) * SLOT_ALIGN for e in range(E)]
        ends = [jnp.where(tile + 1 < n_tiles, lo_ref[b, nxt, e], C) for e in range(E)]

        def window_product(w, first, pc=pc, starts=starts):
            pieces, y_wins = [], []
            for p in range(E // 2):
                targets = []
                for e in (2 * p, 2 * p + 1):
                    start = starts[e] + w * COMBINE_WIN
                    row0 = jnp.minimum(start, C - COMBINE_WIN)
                    pe = pc[:, e:e + 1]
                    if not first:
                        pe = jnp.where(pe >= start, pe, -1)
                    targets.append(pe - row0)
                    y_wins.append(y_ref[0, e, pl.ds(pl.multiple_of(row0, SLOT_ALIGN), COMBINE_WIN), :])
                hit = jnp.where(upper, targets[1], targets[0]) == lane_in
                pieces.append(jnp.where(hit, 1.0, 0.0).astype(BF16))
            return jnp.dot(jnp.concatenate(pieces, axis=1), jnp.concatenate(y_wins, axis=0),
                           preferred_element_type=F32)

        o_ref[0, rows, :] = h_ref[0, rows, :] + window_product(0, True)

        n_pass = jnp.int32(1)
        for e in range(E):
            n_pass = jnp.maximum(n_pass, (ends[e] - starts[e] + COMBINE_WIN - 1) // COMBINE_WIN)
        extra.append((n_pass, rows, window_product))

    for n_pass, rows, window_product in extra:
        def more(w, carry, rows=rows, window_product=window_product):
            o_ref[0, rows, :] += window_product(w, False)
            return carry

        lax.fori_loop(1, n_pass, more, 0)


def _combine(lo, pos_cols, y, h3, ts):
    B, S, D = h3.shape
    _, E, C, _ = y.shape
    n_tiles = S // ts
    n_sub = math.gcd(4, n_tiles)
    tb = ts * n_sub
    assert E % 2 == 0 and C % COMBINE_WIN == 0 and 2 * COMBINE_WIN == LANES
    return pl.pallas_call(
        functools.partial(_combine_body, E=E, C=C, ts=ts, n_tiles=n_tiles),
        grid=(B, n_tiles // n_sub),
        in_specs=[pl.BlockSpec(memory_space=pltpu.SMEM),
                  pl.BlockSpec((1, tb, E), lambda b, i: (b, i, 0)),
                  pl.BlockSpec((1, E, C, D), lambda b, i: (b, 0, 0, 0)),
                  pl.BlockSpec((1, tb, D), lambda b, i: (b, i, 0))],
        out_specs=pl.BlockSpec((1, tb, D), lambda b, i: (b, i, 0)),
        out_shape=jax.ShapeDtypeStruct((B, S, D), F32),
        compiler_params=_params("parallel", "arbitrary"),
        name="moe_combine",
    )(lo, pos_cols, y, h3)


def kernel(x, g_mix, w_in, g_q, g_k, lam_q1, lam_k1, lam_q2, lam_k2, g_subln, rel_bias, conv_w, conv_b,
           gate_r_w, gate_r_b, gate_i_w, gate_i_b, lru_lambda, w_proj_attn, w_proj_lru, w_out, g_ffn,
           w_router, w_gate_e, w_up_e, w_down_e):
    B, S, D = x.shape
    depth = w_in.shape[0]
    H = rel_bias.shape[1]
    hd = g_q.shape[-1]
    vd = g_subln.shape[-1]
    qk_w = H * 2 * hd
    attn_w = H * vd
    lru_w = conv_w.shape[-1]
    NB, LB = gate_r_w.shape[2], gate_r_w.shape[3]
    E = w_router.shape[-1]
    C = EC_CAPACITY_FACTOR * S // E
    off_k = qk_w
    off_v = off_k + qk_w
    off_x = off_v + attn_w
    off_y = off_x + lru_w
    off_ga = off_y + lru_w
    off_gr = off_ga + D

    h2 = x.reshape(B * S, D)
    for layer in range(depth):
        lam_init = 0.8 - 0.6 * math.exp(-0.3 * layer)
        proj = _inproj(h2, g_mix[layer][None], w_in[layer].astype(BF16))

        lam_params = jnp.stack([lam_q1[layer], lam_k1[layer], lam_q2[layer], lam_k2[layer]])
        o_attn = _attention(proj, rel_bias, g_q[layer], g_k[layer], lam_params, g_subln[layer],
                            B=B, S=S, H=H, hd=hd, vd=vd, off_k=off_k, off_v=off_v, lam_init=lam_init)

        w_cat = -LOG2E * jnp.concatenate([gate_r_w[layer, 0], gate_i_w[layer, 0],
                                          gate_r_w[layer, 1], gate_i_w[layer, 1]], axis=-1)
        b_cat = -LOG2E * jnp.stack([gate_r_b[layer, 0].reshape(NB, LB), gate_i_b[layer, 0].reshape(NB, LB),
                                    gate_r_b[layer, 1].reshape(NB, LB), gate_i_b[layer, 1].reshape(NB, LB)],
                                   axis=1).reshape(NB, 1, 4 * LB)
        b_hi = b_cat.astype(BF16)
        b_lo = (b_cat - b_hi.astype(F32)).astype(BF16)
        w_ext = jnp.concatenate([w_cat.astype(BF16), b_hi, b_lo,
                                 jnp.zeros((NB, LB - 2, 4 * LB), BF16)], axis=1)
        lru_out = _rg_lru(proj, conv_w[layer], conv_b[layer], w_ext, lru_lambda[layer],
                          B=B, S=S, off_x=off_x, off_y=off_y)

        wr = w_router[layer]
        wr_hi = wr.astype(BF16)
        wr2 = jnp.concatenate([wr_hi, (wr - wr_hi.astype(F32)).astype(BF16)], axis=1)
        h2, hn, aff = _merge(o_attn, lru_out, proj, h2, w_proj_attn[layer].astype(BF16),
                             w_proj_lru[layer].astype(BF16), w_out[layer].astype(BF16),
                             g_ffn[layer][None], wr2, off_ga=off_ga, off_gr=off_gr, E=E)

        aff_rows = aff.reshape(B, S, E).transpose(0, 2, 1)
        pos_rows, cnt_rows = _select(aff_rows, C)
        ts = min(256, S)
        lo = cnt_rows[:, :, ::ts].transpose(0, 2, 1)
        xg = _gather(lo, pos_rows, hn.reshape(B, S, D), C, ts)
        y = _expert_ffn(pos_rows.reshape(B, E, 1, S), aff_rows.reshape(B, E, 1, S), xg,
                        w_gate_e[layer], w_up_e[layer], w_down_e[layer])
        h3 = _combine(lo, pos_rows.transpose(0, 2, 1), y, h2.reshape(B, S, D), ts)
        h2 = h3.reshape(B * S, D)
    return h2.reshape(B, S, D)
```

```python
import functools
import math

import jax
import jax.numpy as jnp
import numpy as np
from jax import lax
from jax.experimental import pallas as pl
from jax.experimental.pallas import tpu as pltpu

F32 = jnp.float32
BF16 = jnp.bfloat16
EPS = 1e-6
LANES = 128
SUBLANES = 8
VMEM_LIMIT = 56 * 1024 * 1024
LOG2E = 1.4426950408889634
NEG_BIG = -1e30
REL_MAX_DIST = 128
LRU_C = 8.0
EC_CAPACITY_FACTOR = 2


def _params(*sem):
    return pltpu.CompilerParams(dimension_semantics=sem, vmem_limit_bytes=VMEM_LIMIT)


def _inproj_body(x_ref, g_ref, w_ref, o_ref, xn_ref):
    @pl.when(pl.program_id(1) == 0)
    def _():
        x = x_ref[...]
        ms = jnp.mean(x * x, axis=-1, keepdims=True)
        xn_ref[...] = (x * lax.rsqrt(ms + EPS) * g_ref[...]).astype(BF16)

    o_ref[...] = jnp.dot(xn_ref[...], w_ref[...], preferred_element_type=F32).astype(o_ref.dtype)


def _inproj(x2, g, w_bf):
    T, D = x2.shape
    N = w_bf.shape[1]
    tm = min(1024, T)
    tn = N // 2 if N % (2 * LANES) == 0 else N
    return pl.pallas_call(
        _inproj_body,
        grid=(T // tm, N // tn),
        in_specs=[pl.BlockSpec((tm, D), lambda i, j: (i, 0)),
                  pl.BlockSpec((1, D), lambda i, j: (0, 0)),
                  pl.BlockSpec((D, tn), lambda i, j: (0, j))],
        out_specs=pl.BlockSpec((tm, tn), lambda i, j: (i, j)),
        out_shape=jax.ShapeDtypeStruct((T, N), BF16),
        scratch_shapes=[pltpu.VMEM((tm, D), BF16)],
        compiler_params=_params("parallel", "arbitrary"),
        name="in_proj",
    )(x2, g, w_bf)


def _bucket_table(S, n_buckets, rel):
    half = n_buckets // 2
    max_exact = half // 2
    ret = np.where(rel > 0, half, 0)
    n = np.abs(rel)
    nf = np.maximum(n, max_exact).astype(np.float64)
    large = max_exact + (np.log(nf / max_exact) / math.log(REL_MAX_DIST / max_exact)
                         * (half - max_exact)).astype(np.int32)
    large = np.minimum(large, half - 1)
    return (ret + np.where(n < max_exact, n, large)).astype(np.int32).reshape(1, 2 * S)


def _seg_sumsq(x, seg_ones, split=True):
    x2 = x * x
    hi = x2.astype(BF16)
    out = jnp.dot(hi, seg_ones, preferred_element_type=F32)
    if split:
        lo = (x2 - hi.astype(F32)).astype(BF16)
        out = out + jnp.dot(lo, seg_ones, preferred_element_type=F32)
    return out


NORM_SLACK = 1.01
SAFE_LOG2_SPAN = 100.0


def _attn_body(bucket_ref, bucket_t_ref, relb_ref, q_ref, k_ref, v_ref, gq_ref, gk_ref, lamp_ref, gs_ref,
               segq_ref, segv_ref, o_ref, e_ref, et_ref, fast_ref, kn_ref, va_ref, vat_ref, m_ref, acc_ref,
               *, S, tq, n_sub, kc, hd, vd, lam_init, n_buckets):
    h = pl.program_id(0)
    b = pl.program_id(1)
    i = pl.program_id(2)
    q_scale = hd ** -0.5 * LOG2E

    @pl.when((b == 0) & (i == 0))
    def _():
        def bias_row(bk):
            tab = jnp.zeros((1, 2 * S), F32)
            for n in range(n_buckets):
                tab = jnp.where(bk == n, relb_ref[h, n], tab)
            return tab * LOG2E

        tab = bias_row(bucket_ref[...])
        bmax = jnp.max(tab, axis=-1, keepdims=True)
        bmin = jnp.min(tab, axis=-1, keepdims=True)
        bound = (NORM_SLACK * hd * q_scale) * (jnp.max(jnp.abs(gq_ref[...]), axis=-1, keepdims=True)
                                               * jnp.max(jnp.abs(gk_ref[...]), axis=-1, keepdims=True))
        span = 2.0 * bound + (bmax - bmin)
        fast_ref[0] = (span[0, 0] <= SAFE_LOG2_SPAN).astype(jnp.int32)
        shift = bound + bmax
        e_ref[...] = pltpu.roll(jnp.broadcast_to(tab - shift, (tq, 2 * S)), 0, 1, stride=1, stride_axis=0)
        tab_t = bias_row(bucket_t_ref[...])
        et_ref[...] = pltpu.roll(jnp.broadcast_to(tab_t - shift, (tq, 2 * S)), 0, 1, stride=1, stride_axis=0)

    @pl.when(i == 0)
    def _():
        k = k_ref[...].astype(F32)
        ms = _seg_sumsq(k, segq_ref[...], split=False) * (1.0 / hd)
        kn_ref[...] = (k * lax.rsqrt(ms + EPS) * gk_ref[...]).astype(BF16)
        va_ref[:, :vd] = v_ref[...]
        va_ref[:, vd:] = jnp.ones((S, vd), BF16)
        vat_ref[:vd, :] = v_ref[...].astype(F32).T.astype(BF16)
        vat_ref[vd:, :] = jnp.ones((vat_ref.shape[0] - vd, S), BF16)

    nt_dims = (((1,), (1,)), ((), ()))
    lp = lamp_ref[...]
    lam = (jnp.exp(jnp.sum(lp[0:1] * lp[1:2], axis=-1, keepdims=True))
           - jnp.exp(jnp.sum(lp[2:3] * lp[3:4], axis=-1, keepdims=True)) + lam_init)

    def q_maps_of(t):
        q = q_ref[t * tq:(t + 1) * tq, :].astype(F32)
        lo = lax.broadcasted_iota(jnp.int32, q.shape, 1) < hd
        q2 = q * q
        s_lo = jnp.sum(jnp.where(lo, q2, 0.0), axis=-1, keepdims=True)
        s_hi = jnp.sum(jnp.where(lo, 0.0, q2), axis=-1, keepdims=True)
        ms = jnp.where(lo, s_lo, s_hi) * (1.0 / hd)
        qn = q * lax.rsqrt(ms + EPS) * (gq_ref[...] * q_scale)
        return jnp.where(lo, qn, 0.0).astype(BF16), jnp.where(lo, 0.0, qn).astype(BF16)

    def e_start(t, k0):
        return pl.multiple_of(S + k0 - (i * n_sub + t) * tq, LANES)

    def finish(t, a1, a2):
        o = a1[:, :vd] / a1[:, vd:] - lam * (a2[:, :vd] / a2[:, vd:])
        o = o * lax.rsqrt(_seg_sumsq(o, segv_ref[...]) * (1.0 / vd) + EPS)
        o_ref[t * tq:(t + 1) * tq, :] = (o * (gs_ref[...] * (1.0 - lam_init))).astype(o_ref.dtype)

    def finish_t(t, a1, a2):
        ot = a1[:vd] / a1[vd:vd + 1] - lam * (a2[:vd] / a2[vd:vd + 1])
        ot = ot * lax.rsqrt(jnp.mean(ot * ot, axis=0, keepdims=True) + EPS)
        o = ot.T
        o_ref[t * tq:(t + 1) * tq, :] = (o * (gs_ref[...] * (1.0 - lam_init))).astype(o_ref.dtype)

    @pl.when(fast_ref[0] == 1)
    def _():
        for t in range(n_sub):
            q_maps = q_maps_of(t)
            tile = i * n_sub + t
            accs = []
            for mi in range(2):
                chunks = []
                for c in range(S // tq):
                    st = lax.dot_general(kn_ref[c * tq:(c + 1) * tq, :], q_maps[mi], nt_dims,
                                         preferred_element_type=F32)
                    start = pl.multiple_of(S + (tile - c) * tq, LANES)
                    chunks.append(jnp.exp2(st + et_ref[:, pl.ds(start, tq)]).astype(BF16))
                accs.append(jnp.dot(vat_ref[...], jnp.concatenate(chunks, axis=0),
                                    preferred_element_type=F32))
            finish_t(t, *accs)

    @pl.when(fast_ref[0] != 1)
    def _():
        for t in range(n_sub):
            q_maps = q_maps_of(t)
            m_ref[...] = jnp.full(m_ref.shape, NEG_BIG, F32)
            acc_ref[...] = jnp.zeros(acc_ref.shape, F32)

            def chunk(c, carry):
                k0 = pl.multiple_of(c * kc, kc)
                bias = e_ref[:, pl.ds(e_start(t, k0), kc)]
                for mi in range(2):
                    s = lax.dot_general(q_maps[mi], kn_ref[pl.ds(k0, kc), :], nt_dims,
                                        preferred_element_type=F32) + bias
                    m_old = m_ref[mi]
                    m_new = jnp.maximum(m_old, jnp.max(s, axis=-1, keepdims=True))
                    p = jnp.exp2(s - m_new).astype(BF16)
                    acc_ref[mi] = (jnp.exp2(m_old - m_new) * acc_ref[mi]
                                   + jnp.dot(p, va_ref[pl.ds(k0, kc), :], preferred_element_type=F32))
                    m_ref[mi] = m_new
                return carry

            lax.fori_loop(0, S // kc, chunk, 0)
            finish(t, acc_ref[0], acc_ref[1])


def _attention(proj, rel_bias, g_q, g_k, lam_params, g_subln, *, B, S, H, hd, vd, off_k, off_v, lam_init):
    T = B * S
    tq = min(512, S)
    kc = min(512, S)
    n_sub = math.gcd(4, S // tq)
    tqb = n_sub * tq
    nq = S // tqb
    n_buckets = rel_bias.shape[0]
    hw = 2 * hd
    bucket = jnp.asarray(_bucket_table(S, n_buckets, np.arange(2 * S) - S))
    bucket_t = jnp.asarray(_bucket_table(S, n_buckets, S - np.arange(2 * S)))
    body = functools.partial(_attn_body, S=S, tq=tq, n_sub=n_sub, kc=kc, hd=hd, vd=vd, lam_init=lam_init,
                             n_buckets=n_buckets)
    kblk = off_k // hw
    vblk = off_v // vd
    seg_q = np.kron(np.eye(2, dtype=np.float32), np.ones((hd, hd), np.float32))
    return pl.pallas_call(
        body,
        grid=(H, B, nq),
        in_specs=[pl.BlockSpec((1, 2 * S), lambda h, b, i: (0, 0)),
                  pl.BlockSpec((1, 2 * S), lambda h, b, i: (0, 0)),
                  pl.BlockSpec(memory_space=pltpu.SMEM),
                  pl.BlockSpec((tqb, hw), lambda h, b, i: (b * nq + i, h)),
                  pl.BlockSpec((S, hw), lambda h, b, i: (b, kblk + h)),
                  pl.BlockSpec((S, vd), lambda h, b, i: (b, vblk + h)),
                  pl.BlockSpec((1, hw), lambda h, b, i: (0, 0)),
                  pl.BlockSpec((1, hw), lambda h, b, i: (0, 0)),
                  pl.BlockSpec((4, hd), lambda h, b, i: (0, 0)),
                  pl.BlockSpec((1, vd), lambda h, b, i: (0, 0)),
                  pl.BlockSpec((hw, hw), lambda h, b, i: (0, 0)),
                  pl.BlockSpec((vd, vd), lambda h, b, i: (0, 0))],
        out_specs=pl.BlockSpec((tqb, vd), lambda h, b, i: (b * nq + i, h)),
        out_shape=jax.ShapeDtypeStruct((T, H * vd), BF16),
        scratch_shapes=[pltpu.VMEM((tq, 2 * S), F32),
                        pltpu.VMEM((tq, 2 * S), F32),
                        pltpu.SMEM((1,), jnp.int32),
                        pltpu.VMEM((S, hw), BF16),
                        pltpu.VMEM((S, 2 * vd), BF16),
                        pltpu.VMEM((vd + 16, S), BF16),
                        pltpu.VMEM((2, tq, 1), F32),
                        pltpu.VMEM((2, tq, 2 * vd), F32)],
        compiler_params=_params("arbitrary", "arbitrary", "arbitrary"),
        name="diff_attention",
    )(bucket, bucket_t, rel_bias.T, proj, proj, proj, jnp.tile(g_q, 2)[None], jnp.tile(g_k, 2)[None],
      lam_params, g_subln[None], jnp.asarray(seg_q, BF16), jnp.ones((vd, vd), BF16))


LRU_SEGMENTS = SUBLANES
LRU_PITCH_PAD = 8


def _sublane_scan(a, bb, reverse):
    ri = lax.broadcasted_iota(jnp.int32, a.shape, 0)
    for d in (1, 2, 4):
        if reverse:
            keep, sh = ri < SUBLANES - d, SUBLANES - d
        else:
            keep, sh = ri >= d, d
        bb = bb + a * jnp.where(keep, pltpu.roll(bb, sh, 0), 0.0)
        a = a * jnp.where(keep, pltpu.roll(a, sh, 0), 1.0)
    return bb


LRU_BLOCKS_PER_STEP = 2


def _lru_body(x_ref, y_ref, cw_ref, cb_ref, w_ref, lam_ref, o_ref,
              af_ref, bf_ref, ab_ref, bb_ref, hf_ref, pf_ref, hb_ref, pb_ref, hs_ref, *, S, W):
    nb = w_ref.shape[0]
    L = S // LRU_SEGMENTS
    pitch = L + LRU_PITCH_PAD
    x = x_ref[...].astype(F32)
    row = lax.broadcasted_iota(jnp.int32, x.shape, 0)
    cw = cw_ref[...]
    xc_all = (cw[0:1] * jnp.where(row >= 2, pltpu.roll(x, 2, 0), 0.0)
              + cw[1:2] * jnp.where(row >= 1, pltpu.roll(x, 1, 0), 0.0)
              + cw[2:3] * x
              + cw[3:4] * jnp.where(row < S - 1, pltpu.roll(x, S - 1, 0), 0.0)
              + cb_ref[...])
    lam = lam_ref[...]
    bias_lanes = jnp.where(lax.broadcasted_iota(jnp.int32, (S, W), 1) < 2, 1.0, 0.0).astype(BF16)
    for n in range(nb):
        xc = xc_all[:, n * W:(n + 1) * W]
        gneg = jnp.dot(jnp.concatenate([xc.astype(BF16), bias_lanes], axis=1), w_ref[n],
                       preferred_element_type=F32)
        for d, (a_ref, b_ref) in enumerate(((af_ref, bf_ref), (ab_ref, bb_ref))):
            r = 1.0 / (1.0 + jnp.exp2(gneg[:, (2 * d) * W:(2 * d + 1) * W]))
            ig = 1.0 / (1.0 + jnp.exp2(gneg[:, (2 * d + 1) * W:(2 * d + 2) * W]))
            nl = -lam[d:d + 1, n * W:(n + 1) * W]
            softplus = jnp.maximum(nl, 0.0) + jnp.log1p(jnp.exp(-jnp.abs(nl)))
            a = jnp.exp2((-LRU_C * LOG2E * softplus) * r)
            y2 = 1.0 - a * a
            bvals = jnp.where(y2 > 0.0, y2 * lax.rsqrt(y2), 0.0) * ig * xc
            for sg in range(LRU_SEGMENTS):
                a_ref[n, sg * pitch:sg * pitch + L, :] = a[sg * L:(sg + 1) * L]
                b_ref[n, sg * pitch:sg * pitch + L, :] = bvals[sg * L:(sg + 1) * L]

    def seg_rows(ref, n, j):
        return ref[n, pl.ds(j, LRU_SEGMENTS, stride=pitch), :]

    def step(j, carry):
        out = []
        jb = L - 1 - j
        for n in range(nb):
            hf, pf, hb, pb = carry[4 * n:4 * n + 4]
            a = seg_rows(af_ref, n, j)
            hf = a * hf + seg_rows(bf_ref, n, j)
            pf = a * pf
            hf_ref[n, j] = hf
            pf_ref[n, j] = pf
            a = seg_rows(ab_ref, n, jb)
            hb = a * hb + seg_rows(bb_ref, n, jb)
            pb = a * pb
            hb_ref[n, jb] = hb
            pb_ref[n, jb] = pb
            out += [hf, pf, hb, pb]
        return tuple(out)

    zero = jnp.zeros((LRU_SEGMENTS, W), F32)
    one = jnp.ones((LRU_SEGMENTS, W), F32)
    ends = lax.fori_loop(0, L, step, (zero, one, zero, one) * nb, unroll=8)

    si = lax.broadcasted_iota(jnp.int32, (LRU_SEGMENTS, W), 0)
    cfs, cbs = [], []
    for n in range(nb):
        hf, pf, hb, pb = ends[4 * n:4 * n + 4]
        cfs.append(jnp.where(si >= 1, pltpu.roll(_sublane_scan(pf, hf, False), 1, 0), 0.0))
        cbs.append(jnp.where(si < LRU_SEGMENTS - 1,
                             pltpu.roll(_sublane_scan(pb, hb, True), LRU_SEGMENTS - 1, 0), 0.0))

    def fix(j, carry):
        for n in range(nb):
            hs_ref[n, pl.ds(j, LRU_SEGMENTS, stride=pitch), :] = (
                hf_ref[n, j] + pf_ref[n, j] * cfs[n] + hb_ref[n, j] + pb_ref[n, j] * cbs[n])
        return carry

    lax.fori_loop(0, L, fix, 0, unroll=8)

    for n in range(nb):
        for sg in range(LRU_SEGMENTS):
            y = y_ref[sg * L:(sg + 1) * L, n * W:(n + 1) * W].astype(F32)
            o_ref[sg * L:(sg + 1) * L, n * W:(n + 1) * W] = (
                hs_ref[n, sg * pitch:sg * pitch + L, :] * jax.nn.gelu(y)).astype(o_ref.dtype)


def _rg_lru(proj, conv_w, conv_b, w_ext, lru_lambda, *, B, S, off_x, off_y):
    T = B * S
    NB, W2, _ = w_ext.shape
    W = W2 // 2
    nb = math.gcd(LRU_BLOCKS_PER_STEP, NB)
    wb = nb * W
    xblk = off_x // wb
    yblk = off_y // wb
    body = functools.partial(_lru_body, S=S, W=W)
    padded = LRU_SEGMENTS * (S // LRU_SEGMENTS + LRU_PITCH_PAD)
    return pl.pallas_call(
        body,
        grid=(B, NB // nb),
        in_specs=[pl.BlockSpec((S, wb), lambda b, n: (b, xblk + n)),
                  pl.BlockSpec((S, wb), lambda b, n: (b, yblk + n)),
                  pl.BlockSpec((conv_w.shape[0], wb), lambda b, n: (0, n)),
                  pl.BlockSpec((1, wb), lambda b, n: (0, n)),
                  pl.BlockSpec((nb, 2 * W, 4 * W), lambda b, n: (n, 0, 0)),
                  pl.BlockSpec((2, wb), lambda b, n: (0, n))],
        out_specs=pl.BlockSpec((S, wb), lambda b, n: (b, n)),
        out_shape=jax.ShapeDtypeStruct((T, NB * W), BF16),
        scratch_shapes=([pltpu.VMEM((nb, padded, W), F32)] * 4
                        + [pltpu.VMEM((nb, S // LRU_SEGMENTS, LRU_SEGMENTS, W), F32)] * 4
                        + [pltpu.VMEM((nb, padded, W), F32)]),
        compiler_params=_params("parallel", "parallel"),
        name="rg_lru",
    )(proj, proj, conv_w, conv_b[None], w_ext, lru_lambda)


def _merge_body(o_ref, r_ref, ga_ref, gr_ref, x_ref, wpa_ref, wpl_ref, wo_ref, g_ref, wr_ref,
                h_ref, hn_ref, aff_ref, *, E, n_sub):
    ts = o_ref.shape[0] // n_sub
    for t in range(n_sub):
        rows = slice(t * ts, (t + 1) * ts)
        ba = jnp.dot(o_ref[rows, :], wpa_ref[...], preferred_element_type=F32)
        br = jnp.dot(r_ref[rows, :], wpl_ref[...], preferred_element_type=F32)
        mixed = (jax.nn.sigmoid(ga_ref[rows, :].astype(F32)) * ba
                 + jax.nn.sigmoid(gr_ref[rows, :].astype(F32)) * br)
        h = x_ref[rows, :] + jnp.dot(mixed.astype(BF16), wo_ref[...], preferred_element_type=F32)
        h_ref[rows, :] = h
        hn = h * lax.rsqrt(jnp.mean(h * h, axis=-1, keepdims=True) + EPS) * g_ref[...]
        hn_hi = hn.astype(BF16)
        hn_ref[rows, :] = hn_hi
        hn_lo = (hn - hn_hi.astype(F32)).astype(BF16)
        lg = (jnp.dot(hn_hi, wr_ref[...], preferred_element_type=F32)
              + jnp.dot(hn_lo, wr_ref[...], preferred_element_type=F32))
        logits = lg[:, :E] + lg[:, E:2 * E]
        logits = logits - jnp.max(logits, axis=-1, keepdims=True)
        ex = jnp.exp(logits)
        aff_ref[rows, :] = ex / jnp.sum(ex, axis=-1, keepdims=True)


def _merge(o_attn, lru_out, proj, x2, wpa, wpl, wo, g_ffn, wr2, *, off_ga, off_gr, E):
    T, D = x2.shape
    tm = min(1024, T)
    n_sub = 2 if tm % 1024 == 0 else 1
    resident = dict(pipeline_mode=pl.Buffered(1))
    gab = off_ga // D
    grb = off_gr // D
    row = lambda i: (i, 0)
    const = lambda i: (0, 0)
    return pl.pallas_call(
        functools.partial(_merge_body, E=E, n_sub=n_sub),
        grid=(T // tm,),
        in_specs=[pl.BlockSpec((tm, D), row), pl.BlockSpec((tm, D), row),
                  pl.BlockSpec((tm, D), lambda i: (i, gab)), pl.BlockSpec((tm, D), lambda i: (i, grb)),
                  pl.BlockSpec((tm, D), row),
                  pl.BlockSpec((D, D), const, **resident), pl.BlockSpec((D, D), const, **resident),
                  pl.BlockSpec((D, D), const, **resident),
                  pl.BlockSpec((1, D), const), pl.BlockSpec((D, 2 * E), const)],
        out_specs=[pl.BlockSpec((tm, D), row), pl.BlockSpec((tm, D), row), pl.BlockSpec((tm, E), row)],
        out_shape=[jax.ShapeDtypeStruct((T, D), F32), jax.ShapeDtypeStruct((T, D), BF16),
                   jax.ShapeDtypeStruct((T, E), F32)],
        compiler_params=_params("parallel"),
        name="merge_router",
    )(o_attn, lru_out, proj, proj, x2, wpa, wpl, wo, g_ffn, wr2)


SELECT_SEQS_PER_STEP = 8


def _select_body(aff_ref, pos_ref, cnt_ref, *, S, C):
    rows = aff_ref.shape[0] * aff_ref.shape[1]
    bits = pltpu.bitcast(aff_ref[...].reshape(rows, S), jnp.int32)

    def refine(k, t):
        cand = t | jnp.left_shift(jnp.int32(1), 30 - k)
        cnt = jnp.sum(jnp.where(bits >= cand, 1.0, 0.0), axis=-1, keepdims=True)
        return jnp.where(cnt >= C, cand, t)

    t = lax.fori_loop(0, 31, refine, jnp.zeros((rows, 1), jnp.int32))
    gt = bits > t
    eq = bits == t
    need = C - jnp.sum(jnp.where(gt, 1.0, 0.0), axis=-1, keepdims=True).astype(jnp.int32)
    packed = jnp.where(gt, 1 << 16, 0) + jnp.where(eq, 1, 0)
    lane = lax.broadcasted_iota(jnp.int32, (rows, S), 1)
    incl = packed
    d = 1
    while d < S:
        incl = incl + jnp.where(lane >= d, pltpu.roll(incl, d, 1), 0)
        d *= 2
    excl = incl - packed
    n_gt = excl >> 16
    n_eq = excl & 0xFFFF
    sel = gt | (eq & (n_eq < need))
    before = n_gt + jnp.minimum(n_eq, need)
    cnt_ref[...] = before.reshape(cnt_ref.shape)
    pos_ref[...] = jnp.where(sel, before, -1).reshape(pos_ref.shape)


def _select(aff_t, C):
    B, E, S = aff_t.shape
    nb = math.gcd(SELECT_SEQS_PER_STEP, B)
    spec = pl.BlockSpec((nb, E, S), lambda b: (b, 0, 0))
    return pl.pallas_call(
        functools.partial(_select_body, S=S, C=C),
        grid=(B // nb,),
        in_specs=[spec],
        out_specs=[spec, spec],
        out_shape=[jax.ShapeDtypeStruct((B, E, S), jnp.int32)] * 2,
        compiler_params=_params("parallel"),
        name="topc_select",
    )(aff_t)


GATHER_WIN = 64
SLOT_ALIGN = 16


def _gather_body(lo_ref, pos_ref, hn_ref, xg_ref, *, E, C, S, ts):
    b = pl.program_id(0)
    n_tiles = S // ts
    xg_ref[...] = jnp.zeros(xg_ref.shape, xg_ref.dtype)
    sub = lax.broadcasted_iota(jnp.int32, (GATHER_WIN, ts), 0)

    def tile_body(k, carry):
        t0 = pl.multiple_of(k * ts, ts)
        nxt = jnp.minimum(k + 1, n_tiles - 1)
        starts = [(lo_ref[b, k, e] // SLOT_ALIGN) * SLOT_ALIGN for e in range(E)]
        ends = [jnp.where(k + 1 < n_tiles, lo_ref[b, nxt, e], C) for e in range(E)]
        hn_t = hn_ref[0, pl.ds(t0, ts), :]

        def one_pass(w, first):
            sel, row0s = [], []
            for e in range(E):
                start = starts[e] + w * GATHER_WIN
                row0 = jnp.minimum(start, C - GATHER_WIN)
                pe = pos_ref[0, e:e + 1, pl.ds(t0, ts)]
                if not first:
                    pe = jnp.where(pe >= start, pe, -1)
                sel.append(jnp.where(pe - row0 == sub, 1.0, 0.0).astype(BF16))
                row0s.append(pl.multiple_of(row0, SLOT_ALIGN))
            rows = jnp.dot(jnp.concatenate(sel, axis=0), hn_t,
                           preferred_element_type=F32).astype(xg_ref.dtype)
            for e in range(E):
                xg_ref[0, e, pl.ds(row0s[e], GATHER_WIN), :] += rows[e * GATHER_WIN:(e + 1) * GATHER_WIN]

        one_pass(0, True)
        n_pass = jnp.int32(1)
        for e in range(E):
            n_pass = jnp.maximum(n_pass, (ends[e] - starts[e] + GATHER_WIN - 1) // GATHER_WIN)

        def more(w, c):
            one_pass(w, False)
            return c

        lax.fori_loop(1, n_pass, more, 0)
        return carry

    lax.fori_loop(0, n_tiles, tile_body, 0)


def _gather(lo, pos_rows, hn3, C, ts):
    B, S, D = hn3.shape
    E = pos_rows.shape[1]
    assert C % GATHER_WIN == 0
    return pl.pallas_call(
        functools.partial(_gather_body, E=E, C=C, S=S, ts=ts),
        grid=(B,),
        in_specs=[pl.BlockSpec(memory_space=pltpu.SMEM),
                  pl.BlockSpec((1, E, S), lambda b: (b, 0, 0)),
                  pl.BlockSpec((1, S, D), lambda b: (b, 0, 0))],
        out_specs=pl.BlockSpec((1, E, C, D), lambda b: (b, 0, 0, 0)),
        out_shape=jax.ShapeDtypeStruct((B, E, C, D), BF16),
        compiler_params=_params("parallel"),
        name="moe_gather",
    )(lo, pos_rows, hn3)


def _ffn_body(pos_ref, aff_ref, xg_ref, wg_hbm, wu_hbm, wd_hbm, y_ref,
              wg_s, wu_s, wd_s, stg_g, stg_u, stg_d, sems, *, C, S, fc):
    e = pl.program_id(0)
    b = pl.program_id(1)
    n_experts = pl.num_programs(0)
    rows_in = stg_g.shape[0]
    rows_dn = stg_d.shape[0]
    n_chunks = wg_s.shape[1] // rows_in

    def chunk_copies(ee, k):
        r_in = pl.multiple_of(k * rows_in, rows_in)
        r_dn = pl.multiple_of(k * rows_dn, rows_dn)
        return (pltpu.make_async_copy(wg_hbm.at[ee, pl.ds(r_in, rows_in), :], stg_g, sems.at[0]),
                pltpu.make_async_copy(wu_hbm.at[ee, pl.ds(r_in, rows_in), :], stg_u, sems.at[1]),
                pltpu.make_async_copy(wd_hbm.at[ee, pl.ds(r_dn, rows_dn), :], stg_d, sems.at[2]))

    def cast_chunk(slot_, k):
        r_in = pl.multiple_of(k * rows_in, rows_in)
        r_dn = pl.multiple_of(k * rows_dn, rows_dn)
        wg_s[slot_, pl.ds(r_in, rows_in), :] = stg_g[...].astype(BF16)
        wu_s[slot_, pl.ds(r_in, rows_in), :] = stg_u[...].astype(BF16)
        wd_s[slot_, pl.ds(r_dn, rows_dn), :] = stg_d[...].astype(BF16)

    @pl.when((e == 0) & (b == 0))
    def _():
        def stage(k, carry):
            for cp in chunk_copies(0, k):
                cp.start()
            for cp in chunk_copies(0, k):
                cp.wait()
            cast_chunk(0, k)
            return carry

        lax.fori_loop(0, n_chunks, stage, 0)

        @pl.when(n_experts > 1)
        def _():
            for cp in chunk_copies(1, 0):
                cp.start()

    cur = e % 2
    nb = xg_ref.shape[0]
    slot = lax.broadcasted_iota(jnp.int32, (C, S), 0)
    vals = [jnp.sum(jnp.where(pos_ref[t, 0] == slot, aff_ref[t, 0], 0.0), axis=1, keepdims=True)
            for t in range(nb)]
    val = jnp.concatenate(vals, axis=0)
    xg = xg_ref[...].reshape(nb * C, xg_ref.shape[3])
    F = wg_s.shape[2]
    y = jnp.zeros(xg.shape, F32)
    for f0 in range(0, F, fc):
        g = jnp.dot(xg, wg_s[cur, :, f0:f0 + fc], preferred_element_type=F32)
        u = jnp.dot(xg, wu_s[cur, :, f0:f0 + fc], preferred_element_type=F32)
        hid = (g * jax.nn.sigmoid(g) * u).astype(BF16)
        y = y + jnp.dot(hid, wd_s[cur, f0:f0 + fc, :], preferred_element_type=F32)
    y_ref[...] = (y * val).astype(y_ref.dtype).reshape(y_ref.shape)

    @pl.when(e + 1 < n_experts)
    def _():
        for cp in chunk_copies(e + 1, b):
            cp.wait()
        cast_chunk(1 - cur, b)

    wrap = b + 1 == n_chunks
    e_nxt = jnp.where(wrap, e + 2, e + 1)
    k_nxt = jnp.where(wrap, 0, b + 1)

    @pl.when(e_nxt < n_experts)
    def _():
        for cp in chunk_copies(e_nxt, k_nxt):
            cp.start()


FFN_SEQS_PER_STEP = 4


def _expert_ffn(pos_rows, aff_rows, xg, wg, wu, wd):
    B, _, C, D = xg.shape
    S = pos_rows.shape[-1]
    E, _, F = wg.shape
    fc = min(1024, F)
    nb = math.gcd(FFN_SEQS_PER_STEP, B)
    steps = B // nb
    assert D % steps == 0 and F % steps == 0 and (D // steps) % 16 == 0 and (F // steps) % 16 == 0
    hbm = pl.BlockSpec(memory_space=pl.ANY)
    return pl.pallas_call(
        functools.partial(_ffn_body, C=C, S=S, fc=fc),
        grid=(E, steps),
        in_specs=[pl.BlockSpec((nb, 1, 1, S), lambda e, b: (b, e, 0, 0)),
                  pl.BlockSpec((nb, 1, 1, S), lambda e, b: (b, e, 0, 0)),
                  pl.BlockSpec((nb, 1, C, D), lambda e, b: (b, e, 0, 0)),
                  hbm, hbm, hbm],
        out_specs=pl.BlockSpec((nb, 1, C, D), lambda e, b: (b, e, 0, 0)),
        out_shape=jax.ShapeDtypeStruct((B, E, C, D), BF16),
        scratch_shapes=[pltpu.VMEM((2, D, F), BF16), pltpu.VMEM((2, D, F), BF16),
                        pltpu.VMEM((2, F, D), BF16),
                        pltpu.VMEM((D // steps, F), F32), pltpu.VMEM((D // steps, F), F32),
                        pltpu.VMEM((F // steps, D), F32),
                        pltpu.SemaphoreType.DMA((3,))],
        compiler_params=_params("arbitrary", "arbitrary"),
        name="expert_ffn",
    )(pos_rows, aff_rows, xg, wg, wu, wd)


COMBINE_WIN = 64


def _combine_body(lo_ref, pos_ref, y_ref, h_ref, o_ref, *, E, C, ts, n_tiles):
    b = pl.program_id(0)
    n_sub = pos_ref.shape[1] // ts
    lane = lax.broadcasted_iota(jnp.int32, (ts, LANES), 1)
    upper = lane >= COMBINE_WIN
    lane_in = jnp.where(upper, lane - COMBINE_WIN, lane)

    extra = []
    for t in range(n_sub):
        tile = pl.program_id(1) * n_sub + t
        rows = slice(t * ts, (t + 1) * ts)
        pc = pos_ref[0, rows, :]
        nxt = jnp.minimum(tile + 1, n_tiles - 1)
        starts = [(lo_ref[b, tile, e] // SLOT_ALIGN) * SLOT_ALIGN for e in range(E)]
        ends = [jnp.where(tile + 1 < n_tiles, lo_ref[b, nxt, e], C) for e in range(E)]

        def window_product(w, first, pc=pc, starts=starts):
            pieces, y_wins = [], []
            for p in range(E // 2):
                targets = []
                for e in (2 * p, 2 * p + 1):
                    start = starts[e] + w * COMBINE_WIN
                    row0 = jnp.minimum(start, C - COMBINE_WIN)
                    pe = pc[:, e:e + 1]
                    if not first:
                        pe = jnp.where(pe >= start, pe, -1)
                    targets.append(pe - row0)
                    y_wins.append(y_ref[0, e, pl.ds(pl.multiple_of(row0, SLOT_ALIGN), COMBINE_WIN), :])
                hit = jnp.where(upper, targets[1], targets[0]) == lane_in
                pieces.append(jnp.where(hit, 1.0, 0.0).astype(BF16))
            return jnp.dot(jnp.concatenate(pieces, axis=1), jnp.concatenate(y_wins, axis=0),
                           preferred_element_type=F32)

        o_ref[0, rows, :] = h_ref[0, rows, :] + window_product(0, True)

        n_pass = jnp.int32(1)
        for e in range(E):
            n_pass = jnp.maximum(n_pass, (ends[e] - starts[e] + COMBINE_WIN - 1) // COMBINE_WIN)
        extra.append((n_pass, rows, window_product))

    for n_pass, rows, window_product in extra:
        def more(w, carry, rows=rows, window_product=window_product):
            o_ref[0, rows, :] += window_product(w, False)
            return carry

        lax.fori_loop(1, n_pass, more, 0)


def _combine(lo, pos_cols, y, h3, ts):
    B, S, D = h3.shape
    _, E, C, _ = y.shape
    n_tiles = S // ts
    n_sub = math.gcd(4, n_tiles)
    tb = ts * n_sub
    assert E % 2 == 0 and C % COMBINE_WIN == 0 and 2 * COMBINE_WIN == LANES
    return pl.pallas_call(
        functools.partial(_combine_body, E=E, C=C, ts=ts, n_tiles=n_tiles),
        grid=(B, n_tiles // n_sub),
        in_specs=[pl.BlockSpec(memory_space=pltpu.SMEM),
                  pl.BlockSpec((1, tb, E), lambda b, i: (b, i, 0)),
                  pl.BlockSpec((1, E, C, D), lambda b, i: (b, 0, 0, 0)),
                  pl.BlockSpec((1, tb, D), lambda b, i: (b, i, 0))],
        out_specs=pl.BlockSpec((1, tb, D), lambda b, i: (b, i, 0)),
        out_shape=jax.ShapeDtypeStruct((B, S, D), F32),
        compiler_params=_params("parallel", "arbitrary"),
        name="moe_combine",
    )(lo, pos_cols, y, h3)


def kernel(x, g_mix, w_in, g_q, g_k, lam_q1, lam_k1, lam_q2, lam_k2, g_subln, rel_bias, conv_w, conv_b,
           gate_r_w, gate_r_b, gate_i_w, gate_i_b, lru_lambda, w_proj_attn, w_proj_lru, w_out, g_ffn,
           w_router, w_gate_e, w_up_e, w_down_e):
    B, S, D = x.shape
    depth = w_in.shape[0]
    H = rel_bias.shape[1]
    hd = g_q.shape[-1]
    vd = g_subln.shape[-1]
    qk_w = H * 2 * hd
    attn_w = H * vd
    lru_w = conv_w.shape[-1]
    NB, LB = gate_r_w.shape[2], gate_r_w.shape[3]
    E = w_router.shape[-1]
    C = EC_CAPACITY_FACTOR * S // E
    off_k = qk_w
    off_v = off_k + qk_w
    off_x = off_v + attn_w
    off_y = off_x + lru_w
    off_ga = off_y + lru_w
    off_gr = off_ga + D

    h2 = x.reshape(B * S, D)
    for layer in range(depth):
        lam_init = 0.8 - 0.6 * math.exp(-0.3 * layer)
        proj = _inproj(h2, g_mix[layer][None], w_in[layer].astype(BF16))

        lam_params = jnp.stack([lam_q1[layer], lam_k1[layer], lam_q2[layer], lam_k2[layer]])
        o_attn = _attention(proj, rel_bias, g_q[layer], g_k[layer], lam_params, g_subln[layer],
                            B=B, S=S, H=H, hd=hd, vd=vd, off_k=off_k, off_v=off_v, lam_init=lam_init)

        w_cat = -LOG2E * jnp.concatenate([gate_r_w[layer, 0], gate_i_w[layer, 0],
                                          gate_r_w[layer, 1], gate_i_w[layer, 1]], axis=-1)
        b_cat = -LOG2E * jnp.stack([gate_r_b[layer, 0].reshape(NB, LB), gate_i_b[layer, 0].reshape(NB, LB),
                                    gate_r_b[layer, 1].reshape(NB, LB), gate_i_b[layer, 1].reshape(NB, LB)],
                                   axis=1).reshape(NB, 1, 4 * LB)
        b_hi = b_cat.astype(BF16)
        b_lo = (b_cat - b_hi.astype(F32)).astype(BF16)
        w_ext = jnp.concatenate([w_cat.astype(BF16), b_hi, b_lo,
                                 jnp.zeros((NB, LB - 2, 4 * LB), BF16)], axis=1)
        lru_out = _rg_lru(proj, conv_w[layer], conv_b[layer], w_ext, lru_lambda[layer],
                          B=B, S=S, off_x=off_x, off_y=off_y)

        wr = w_router[layer]
        wr_hi = wr.astype(BF16)
        wr2 = jnp.concatenate([wr_hi, (wr - wr_hi.astype(F32)).astype(BF16)], axis=1)
        h2, hn, aff = _merge(o_attn, lru_out, proj, h2, w_proj_attn[layer].astype(BF16),
                             w_proj_lru[layer].astype(BF16), w_out[layer].astype(BF16),
                             g_ffn[layer][None], wr2, off_ga=off_ga, off_gr=off_gr, E=E)

        aff_rows = aff.reshape(B, S, E).transpose(0, 2, 1)
        pos_rows, cnt_rows = _select(aff_rows, C)
        ts = min(256, S)
        lo = cnt_rows[:, :, ::ts].transpose(0, 2, 1)
        xg = _gather(lo, pos_rows, hn.reshape(B, S, D), C, ts)
        y = _expert_ffn(pos_rows.reshape(B, E, 1, S), aff_rows.reshape(B, E, 1, S), xg,
                        w_gate_e[layer], w_up_e[layer], w_down_e[layer])
        h3 = _combine(lo, pos_rows.transpose(0, 2, 1), y, h2.reshape(B, S, D), ts)
        h2 = h3.reshape(B * S, D)
    return h2.reshape(B, S, D)
```

```python
import functools
import math

import jax
import jax.numpy as jnp
import numpy as np
from jax import lax
from jax.experimental import pallas as pl
from jax.experimental.pallas import tpu as pltpu

F32 = jnp.float32
BF16 = jnp.bfloat16
EPS = 1e-6
LANES = 128
SUBLANES = 8
VMEM_LIMIT = 56 * 1024 * 1024
LOG2E = 1.4426950408889634
NEG_BIG = -1e30
REL_MAX_DIST = 128
LRU_C = 8.0
EC_CAPACITY_FACTOR = 2


def _params(*sem):
    return pltpu.CompilerParams(dimension_semantics=sem, vmem_limit_bytes=VMEM_LIMIT)


def _inproj_body(x_ref, g_ref, w_ref, o_ref, xn_ref):
    @pl.when(pl.program_id(1) == 0)
    def _():
        x = x_ref[...]
        ms = jnp.mean(x * x, axis=-1, keepdims=True)
        xn_ref[...] = (x * lax.rsqrt(ms + EPS) * g_ref[...]).astype(BF16)

    o_ref[...] = jnp.dot(xn_ref[...], w_ref[...], preferred_element_type=F32).astype(o_ref.dtype)


def _inproj(x2, g, w_bf):
    T, D = x2.shape
    N = w_bf.shape[1]
    tm = min(1024, T)
    tn = N // 2 if N % (2 * LANES) == 0 else N
    return pl.pallas_call(
        _inproj_body,
        grid=(T // tm, N // tn),
        in_specs=[pl.BlockSpec((tm, D), lambda i, j: (i, 0)),
                  pl.BlockSpec((1, D), lambda i, j: (0, 0)),
                  pl.BlockSpec((D, tn), lambda i, j: (0, j))],
        out_specs=pl.BlockSpec((tm, tn), lambda i, j: (i, j)),
        out_shape=jax.ShapeDtypeStruct((T, N), BF16),
        scratch_shapes=[pltpu.VMEM((tm, D), BF16)],
        compiler_params=_params("parallel", "arbitrary"),
        name="in_proj",
    )(x2, g, w_bf)


def _bucket_table(S, n_buckets, rel):
    half = n_buckets // 2
    max_exact = half // 2
    ret = np.where(rel > 0, half, 0)
    n = np.abs(rel)
    nf = np.maximum(n, max_exact).astype(np.float64)
    large = max_exact + (np.log(nf / max_exact) / math.log(REL_MAX_DIST / max_exact)
                         * (half - max_exact)).astype(np.int32)
    large = np.minimum(large, half - 1)
    return (ret + np.where(n < max_exact, n, large)).astype(np.int32).reshape(1, 2 * S)


def _seg_sumsq(x, seg_ones, split=True):
    x2 = x * x
    hi = x2.astype(BF16)
    out = jnp.dot(hi, seg_ones, preferred_element_type=F32)
    if split:
        lo = (x2 - hi.astype(F32)).astype(BF16)
        out = out + jnp.dot(lo, seg_ones, preferred_element_type=F32)
    return out


NORM_SLACK = 1.01
SAFE_LOG2_SPAN = 100.0


def _attn_body(bucket_ref, bucket_t_ref, relb_ref, q_ref, k_ref, v_ref, gq_ref, gk_ref, lamp_ref, gs_ref,
               segq_ref, segv_ref, o_ref, e_ref, et_ref, fast_ref, kn_ref, va_ref, vat_ref, m_ref, acc_ref,
               *, S, tq, n_sub, kc, hd, vd, lam_init, n_buckets):
    h = pl.program_id(0)
    b = pl.program_id(1)
    i = pl.program_id(2)
    q_scale = hd ** -0.5 * LOG2E

    @pl.when((b == 0) & (i == 0))
    def _():
        def bias_row(bk):
            tab = jnp.zeros((1, 2 * S), F32)
            for n in range(n_buckets):
                tab = jnp.where(bk == n, relb_ref[h, n], tab)
            return tab * LOG2E

        tab = bias_row(bucket_ref[...])
        bmax = jnp.max(tab, axis=-1, keepdims=True)
        bmin = jnp.min(tab, axis=-1, keepdims=True)
        bound = (NORM_SLACK * hd * q_scale) * (jnp.max(jnp.abs(gq_ref[...]), axis=-1, keepdims=True)
                                               * jnp.max(jnp.abs(gk_ref[...]), axis=-1, keepdims=True))
        span = 2.0 * bound + (bmax - bmin)
        fast_ref[0] = (span[0, 0] <= SAFE_LOG2_SPAN).astype(jnp.int32)
        shift = bound + bmax
        e_ref[...] = pltpu.roll(jnp.broadcast_to(tab - shift, (tq, 2 * S)), 0, 1, stride=1, stride_axis=0)
        tab_t = bias_row(bucket_t_ref[...])
        et_ref[...] = pltpu.roll(jnp.broadcast_to(tab_t - shift, (tq, 2 * S)), 0, 1, stride=1, stride_axis=0)

    @pl.when(i == 0)
    def _():
        k = k_ref[...].astype(F32)
        ms = _seg_sumsq(k, segq_ref[...], split=False) * (1.0 / hd)
        kn_ref[...] = (k * lax.rsqrt(ms + EPS) * gk_ref[...]).astype(BF16)
        va_ref[:, :vd] = v_ref[...]
        va_ref[:, vd:] = jnp.ones((S, vd), BF16)
        vat_ref[:vd, :] = v_ref[...].astype(F32).T.astype(BF16)
        vat_ref[vd:, :] = jnp.ones((vat_ref.shape[0] - vd, S), BF16)

    nt_dims = (((1,), (1,)), ((), ()))
    lp = lamp_ref[...]
    lam = (jnp.exp(jnp.sum(lp[0:1] * lp[1:2], axis=-1, keepdims=True))
           - jnp.exp(jnp.sum(lp[2:3] * lp[3:4], axis=-1, keepdims=True)) + lam_init)

    def q_maps_of(t):
        q = q_ref[t * tq:(t + 1) * tq, :].astype(F32)
        lo = lax.broadcasted_iota(jnp.int32, q.shape, 1) < hd
        q2 = q * q
        s_lo = jnp.sum(jnp.where(lo, q2, 0.0), axis=-1, keepdims=True)
        s_hi = jnp.sum(jnp.where(lo, 0.0, q2), axis=-1, keepdims=True)
        ms = jnp.where(lo, s_lo, s_hi) * (1.0 / hd)
        qn = q * lax.rsqrt(ms + EPS) * (gq_ref[...] * q_scale)
        return jnp.where(lo, qn, 0.0).astype(BF16), jnp.where(lo, 0.0, qn).astype(BF16)

    def e_start(t, k0):
        return pl.multiple_of(S + k0 - (i * n_sub + t) * tq, LANES)

    def finish(t, a1, a2):
        o = a1[:, :vd] / a1[:, vd:] - lam * (a2[:, :vd] / a2[:, vd:])
        o = o * lax.rsqrt(_seg_sumsq(o, segv_ref[...]) * (1.0 / vd) + EPS)
        o_ref[t * tq:(t + 1) * tq, :] = (o * (gs_ref[...] * (1.0 - lam_init))).astype(o_ref.dtype)

    def finish_t(t, a1, a2):
        ot = a1[:vd] / a1[vd:vd + 1] - lam * (a2[:vd] / a2[vd:vd + 1])
        ot = ot * lax.rsqrt(jnp.mean(ot * ot, axis=0, keepdims=True) + EPS)
        o = ot.T
        o_ref[t * tq:(t + 1) * tq, :] = (o * (gs_ref[...] * (1.0 - lam_init))).astype(o_ref.dtype)

    @pl.when(fast_ref[0] == 1)
    def _():
        for t in range(n_sub):
            q_maps = q_maps_of(t)
            tile = i * n_sub + t
            accs = []
            for mi in range(2):
                chunks = []
                for c in range(S // tq):
                    st = lax.dot_general(kn_ref[c * tq:(c + 1) * tq, :], q_maps[mi], nt_dims,
                                         preferred_element_type=F32)
                    start = pl.multiple_of(S + (tile - c) * tq, LANES)
                    chunks.append(jnp.exp2(st + et_ref[:, pl.ds(start, tq)]).astype(BF16))
                accs.append(jnp.dot(vat_ref[...], jnp.concatenate(chunks, axis=0),
                                    preferred_element_type=F32))
            finish_t(t, *accs)

    @pl.when(fast_ref[0] != 1)
    def _():
        for t in range(n_sub):
            q_maps = q_maps_of(t)
            m_ref[...] = jnp.full(m_ref.shape, NEG_BIG, F32)
            acc_ref[...] = jnp.zeros(acc_ref.shape, F32)

            def chunk(c, carry):
                k0 = pl.multiple_of(c * kc, kc)
                bias = e_ref[:, pl.ds(e_start(t, k0), kc)]
                for mi in range(2):
                    s = lax.dot_general(q_maps[mi], kn_ref[pl.ds(k0, kc), :], nt_dims,
                                        preferred_element_type=F32) + bias
                    m_old = m_ref[mi]
                    m_new = jnp.maximum(m_old, jnp.max(s, axis=-1, keepdims=True))
                    p = jnp.exp2(s - m_new).astype(BF16)
                    acc_ref[mi] = (jnp.exp2(m_old - m_new) * acc_ref[mi]
                                   + jnp.dot(p, va_ref[pl.ds(k0, kc), :], preferred_element_type=F32))
                    m_ref[mi] = m_new
                return carry

            lax.fori_loop(0, S // kc, chunk, 0)
            finish(t, acc_ref[0], acc_ref[1])


def _attention(proj, rel_bias, g_q, g_k, lam_params, g_subln, *, B, S, H, hd, vd, off_k, off_v, lam_init):
    T = B * S
    tq = min(512, S)
    kc = min(512, S)
    n_sub = math.gcd(4, S // tq)
    tqb = n_sub * tq
    nq = S // tqb
    n_buckets = rel_bias.shape[0]
    hw = 2 * hd
    bucket = jnp.asarray(_bucket_table(S, n_buckets, np.arange(2 * S) - S))
    bucket_t = jnp.asarray(_bucket_table(S, n_buckets, S - np.arange(2 * S)))
    body = functools.partial(_attn_body, S=S, tq=tq, n_sub=n_sub, kc=kc, hd=hd, vd=vd, lam_init=lam_init,
                             n_buckets=n_buckets)
    kblk = off_k // hw
    vblk = off_v // vd
    seg_q = np.kron(np.eye(2, dtype=np.float32), np.ones((hd, hd), np.float32))
    return pl.pallas_call(
        body,
        grid=(H, B, nq),
        in_specs=[pl.BlockSpec((1, 2 * S), lambda h, b, i: (0, 0)),
                  pl.BlockSpec((1, 2 * S), lambda h, b, i: (0, 0)),
                  pl.BlockSpec(memory_space=pltpu.SMEM),
                  pl.BlockSpec((tqb, hw), lambda h, b, i: (b * nq + i, h)),
                  pl.BlockSpec((S, hw), lambda h, b, i: (b, kblk + h)),
                  pl.BlockSpec((S, vd), lambda h, b, i: (b, vblk + h)),
                  pl.BlockSpec((1, hw), lambda h, b, i: (0, 0)),
                  pl.BlockSpec((1, hw), lambda h, b, i: (0, 0)),
                  pl.BlockSpec((4, hd), lambda h, b, i: (0, 0)),
                  pl.BlockSpec((1, vd), lambda h, b, i: (0, 0)),
                  pl.BlockSpec((hw, hw), lambda h, b, i: (0, 0)),
                  pl.BlockSpec((vd, vd), lambda h, b, i: (0, 0))],
        out_specs=pl.BlockSpec((tqb, vd), lambda h, b, i: (b * nq + i, h)),
        out_shape=jax.ShapeDtypeStruct((T, H * vd), BF16),
        scratch_shapes=[pltpu.VMEM((tq, 2 * S), F32),
                        pltpu.VMEM((tq, 2 * S), F32),
                        pltpu.SMEM((1,), jnp.int32),
                        pltpu.VMEM((S, hw), BF16),
                        pltpu.VMEM((S, 2 * vd), BF16),
                        pltpu.VMEM((vd + 16, S), BF16),
                        pltpu.VMEM((2, tq, 1), F32),
                        pltpu.VMEM((2, tq, 2 * vd), F32)],
        compiler_params=_params("arbitrary", "arbitrary", "arbitrary"),
        name="diff_attention",
    )(bucket, bucket_t, rel_bias.T, proj, proj, proj, jnp.tile(g_q, 2)[None], jnp.tile(g_k, 2)[None],
      lam_params, g_subln[None], jnp.asarray(seg_q, BF16), jnp.ones((vd, vd), BF16))


LRU_SEGMENTS = SUBLANES
LRU_PITCH_PAD = 8


def _sublane_scan(a, bb, reverse):
    ri = lax.broadcasted_iota(jnp.int32, a.shape, 0)
    for d in (1, 2, 4):
        if reverse:
            keep, sh = ri < SUBLANES - d, SUBLANES - d
        else:
            keep, sh = ri >= d, d
        bb = bb + a * jnp.where(keep, pltpu.roll(bb, sh, 0), 0.0)
        a = a * jnp.where(keep, pltpu.roll(a, sh, 0), 1.0)
    return bb


LRU_BLOCKS_PER_STEP = 2


def _lru_body(x_ref, y_ref, cw_ref, cb_ref, w_ref, lam_ref, o_ref,
              af_ref, bf_ref, ab_ref, bb_ref, hf_ref, pf_ref, hb_ref, pb_ref, hs_ref, *, S, W):
    nb = w_ref.shape[0]
    L = S // LRU_SEGMENTS
    pitch = L + LRU_PITCH_PAD
    x = x_ref[...].astype(F32)
    row = lax.broadcasted_iota(jnp.int32, x.shape, 0)
    cw = cw_ref[...]
    xc_all = (cw[0:1] * jnp.where(row >= 2, pltpu.roll(x, 2, 0), 0.0)
              + cw[1:2] * jnp.where(row >= 1, pltpu.roll(x, 1, 0), 0.0)
              + cw[2:3] * x
              + cw[3:4] * jnp.where(row < S - 1, pltpu.roll(x, S - 1, 0), 0.0)
              + cb_ref[...])
    lam = lam_ref[...]
    bias_lanes = jnp.where(lax.broadcasted_iota(jnp.int32, (S, W), 1) < 2, 1.0, 0.0).astype(BF16)
    for n in range(nb):
        xc = xc_all[:, n * W:(n + 1) * W]
        gneg = jnp.dot(jnp.concatenate([xc.astype(BF16), bias_lanes], axis=1), w_ref[n],
                       preferred_element_type=F32)
        for d, (a_ref, b_ref) in enumerate(((af_ref, bf_ref), (ab_ref, bb_ref))):
            r = 1.0 / (1.0 + jnp.exp2(gneg[:, (2 * d) * W:(2 * d + 1) * W]))
            ig = 1.0 / (1.0 + jnp.exp2(gneg[:, (2 * d + 1) * W:(2 * d + 2) * W]))
            nl = -lam[d:d + 1, n * W:(n + 1) * W]
            softplus = jnp.maximum(nl, 0.0) + jnp.log1p(jnp.exp(-jnp.abs(nl)))
            a = jnp.exp2((-LRU_C * LOG2E * softplus) * r)
            y2 = 1.0 - a * a
            bvals = jnp.where(y2 > 0.0, y2 * lax.rsqrt(y2), 0.0) * ig * xc
            for sg in range(LRU_SEGMENTS):
                a_ref[n, sg * pitch:sg * pitch + L, :] = a[sg * L:(sg + 1) * L]
                b_ref[n, sg * pitch:sg * pitch + L, :] = bvals[sg * L:(sg + 1) * L]

    def seg_rows(ref, n, j):
        return ref[n, pl.ds(j, LRU_SEGMENTS, stride=pitch), :]

    def step(j, carry):
        out = []
        jb = L - 1 - j
        for n in range(nb):
            hf, pf, hb, pb = carry[4 * n:4 * n + 4]
            a = seg_rows(af_ref, n, j)
            hf = a * hf + seg_rows(bf_ref, n, j)
            pf = a * pf
            hf_ref[n, j] = hf
            pf_ref[n, j] = pf
            a = seg_rows(ab_ref, n, jb)
            hb = a * hb + seg_rows(bb_ref, n, jb)
            pb = a * pb
            hb_ref[n, jb] = hb
            pb_ref[n, jb] = pb
            out += [hf, pf, hb, pb]
        return tuple(out)

    zero = jnp.zeros((LRU_SEGMENTS, W), F32)
    one = jnp.ones((LRU_SEGMENTS, W), F32)
    ends = lax.fori_loop(0, L, step, (zero, one, zero, one) * nb, unroll=8)

    si = lax.broadcasted_iota(jnp.int32, (LRU_SEGMENTS, W), 0)
    cfs, cbs = [], []
    for n in range(nb):
        hf, pf, hb, pb = ends[4 * n:4 * n + 4]
        cfs.append(jnp.where(si >= 1, pltpu.roll(_sublane_scan(pf, hf, False), 1, 0), 0.0))
        cbs.append(jnp.where(si < LRU_SEGMENTS - 1,
                             pltpu.roll(_sublane_scan(pb, hb, True), LRU_SEGMENTS - 1, 0), 0.0))

    def fix(j, carry):
        for n in range(nb):
            hs_ref[n, pl.ds(j, LRU_SEGMENTS, stride=pitch), :] = (
                hf_ref[n, j] + pf_ref[n, j] * cfs[n] + hb_ref[n, j] + pb_ref[n, j] * cbs[n])
        return carry

    lax.fori_loop(0, L, fix, 0, unroll=8)

    for n in range(nb):
        for sg in range(LRU_SEGMENTS):
            y = y_ref[sg * L:(sg + 1) * L, n * W:(n + 1) * W].astype(F32)
            o_ref[sg * L:(sg + 1) * L, n * W:(n + 1) * W] = (
                hs_ref[n, sg * pitch:sg * pitch + L, :] * jax.nn.gelu(y)).astype(o_ref.dtype)


def _rg_lru(proj, conv_w, conv_b, w_ext, lru_lambda, *, B, S, off_x, off_y):
    T = B * S
    NB, W2, _ = w_ext.shape
    W = W2 // 2
    nb = math.gcd(LRU_BLOCKS_PER_STEP, NB)
    wb = nb * W
    xblk = off_x // wb
    yblk = off_y // wb
    body = functools.partial(_lru_body, S=S, W=W)
    padded = LRU_SEGMENTS * (S // LRU_SEGMENTS + LRU_PITCH_PAD)
    return pl.pallas_call(
        body,
        grid=(B, NB // nb),
        in_specs=[pl.BlockSpec((S, wb), lambda b, n: (b, xblk + n)),
                  pl.BlockSpec((S, wb), lambda b, n: (b, yblk + n)),
                  pl.BlockSpec((conv_w.shape[0], wb), lambda b, n: (0, n)),
                  pl.BlockSpec((1, wb), lambda b, n: (0, n)),
                  pl.BlockSpec((nb, 2 * W, 4 * W), lambda b, n: (n, 0, 0)),
                  pl.BlockSpec((2, wb), lambda b, n: (0, n))],
        out_specs=pl.BlockSpec((S, wb), lambda b, n: (b, n)),
        out_shape=jax.ShapeDtypeStruct((T, NB * W), BF16),
        scratch_shapes=([pltpu.VMEM((nb, padded, W), F32)] * 4
                        + [pltpu.VMEM((nb, S // LRU_SEGMENTS, LRU_SEGMENTS, W), F32)] * 4
                        + [pltpu.VMEM((nb, padded, W), F32)]),
        compiler_params=_params("parallel", "parallel"),
        name="rg_lru",
    )(proj, proj, conv_w, conv_b[None], w_ext, lru_lambda)


def _merge_body(o_ref, r_ref, ga_ref, gr_ref, x_ref, wpa_ref, wpl_ref, wo_ref, g_ref, wr_ref,
                h_ref, hn_ref, aff_ref, *, E, n_sub):
    ts = o_ref.shape[0] // n_sub
    for t in range(n_sub):
        rows = slice(t * ts, (t + 1) * ts)
        ba = jnp.dot(o_ref[rows, :], wpa_ref[...], preferred_element_type=F32)
        br = jnp.dot(r_ref[rows, :], wpl_ref[...], preferred_element_type=F32)
        mixed = (jax.nn.sigmoid(ga_ref[rows, :].astype(F32)) * ba
                 + jax.nn.sigmoid(gr_ref[rows, :].astype(F32)) * br)
        h = x_ref[rows, :] + jnp.dot(mixed.astype(BF16), wo_ref[...], preferred_element_type=F32)
        h_ref[rows, :] = h
        hn = h * lax.rsqrt(jnp.mean(h * h, axis=-1, keepdims=True) + EPS) * g_ref[...]
        hn_hi = hn.astype(BF16)
        hn_ref[rows, :] = hn_hi
        hn_lo = (hn - hn_hi.astype(F32)).astype(BF16)
        lg = (jnp.dot(hn_hi, wr_ref[...], preferred_element_type=F32)
              + jnp.dot(hn_lo, wr_ref[...], preferred_element_type=F32))
        logits = lg[:, :E] + lg[:, E:2 * E]
        logits = logits - jnp.max(logits, axis=-1, keepdims=True)
        ex = jnp.exp(logits)
        aff_ref[rows, :] = ex / jnp.sum(ex, axis=-1, keepdims=True)


def _merge(o_attn, lru_out, proj, x2, wpa, wpl, wo, g_ffn, wr2, *, off_ga, off_gr, E):
    T, D = x2.shape
    tm = min(1024, T)
    n_sub = 4 if tm % 1024 == 0 else 1
    resident = dict(pipeline_mode=pl.Buffered(1))
    gab = off_ga // D
    grb = off_gr // D
    row = lambda i: (i, 0)
    const = lambda i: (0, 0)
    return pl.pallas_call(
        functools.partial(_merge_body, E=E, n_sub=n_sub),
        grid=(T // tm,),
        in_specs=[pl.BlockSpec((tm, D), row), pl.BlockSpec((tm, D), row),
                  pl.BlockSpec((tm, D), lambda i: (i, gab)), pl.BlockSpec((tm, D), lambda i: (i, grb)),
                  pl.BlockSpec((tm, D), row),
                  pl.BlockSpec((D, D), const, **resident), pl.BlockSpec((D, D), const, **resident),
                  pl.BlockSpec((D, D), const, **resident),
                  pl.BlockSpec((1, D), const), pl.BlockSpec((D, 2 * E), const)],
        out_specs=[pl.BlockSpec((tm, D), row), pl.BlockSpec((tm, D), row), pl.BlockSpec((tm, E), row)],
        out_shape=[jax.ShapeDtypeStruct((T, D), F32), jax.ShapeDtypeStruct((T, D), BF16),
                   jax.ShapeDtypeStruct((T, E), F32)],
        compiler_params=_params("parallel"),
        name="merge_router",
    )(o_attn, lru_out, proj, proj, x2, wpa, wpl, wo, g_ffn, wr2)


SELECT_SEQS_PER_STEP = 8


def _select_body(aff_ref, pos_ref, cnt_ref, *, S, C):
    rows = aff_ref.shape[0] * aff_ref.shape[1]
    bits = pltpu.bitcast(aff_ref[...].reshape(rows, S), jnp.int32)

    def refine(k, t):
        cand = t | jnp.left_shift(jnp.int32(1), 30 - k)
        cnt = jnp.sum(jnp.where(bits >= cand, 1.0, 0.0), axis=-1, keepdims=True)
        return jnp.where(cnt >= C, cand, t)

    t = lax.fori_loop(0, 31, refine, jnp.zeros((rows, 1), jnp.int32))
    gt = bits > t
    eq = bits == t
    need = C - jnp.sum(jnp.where(gt, 1.0, 0.0), axis=-1, keepdims=True).astype(jnp.int32)
    packed = jnp.where(gt, 1 << 16, 0) + jnp.where(eq, 1, 0)
    lane = lax.broadcasted_iota(jnp.int32, (rows, S), 1)
    incl = packed
    d = 1
    while d < S:
        incl = incl + jnp.where(lane >= d, pltpu.roll(incl, d, 1), 0)
        d *= 2
    excl = incl - packed
    n_gt = excl >> 16
    n_eq = excl & 0xFFFF
    sel = gt | (eq & (n_eq < need))
    before = n_gt + jnp.minimum(n_eq, need)
    cnt_ref[...] = before.reshape(cnt_ref.shape)
    pos_ref[...] = jnp.where(sel, before, -1).reshape(pos_ref.shape)


def _select(aff_t, C):
    B, E, S = aff_t.shape
    nb = math.gcd(SELECT_SEQS_PER_STEP, B)
    spec = pl.BlockSpec((nb, E, S), lambda b: (b, 0, 0))
    return pl.pallas_call(
        functools.partial(_select_body, S=S, C=C),
        grid=(B // nb,),
        in_specs=[spec],
        out_specs=[spec, spec],
        out_shape=[jax.ShapeDtypeStruct((B, E, S), jnp.int32)] * 2,
        compiler_params=_params("parallel"),
        name="topc_select",
    )(aff_t)


GATHER_WIN = 64
SLOT_ALIGN = 16


def _gather_body(lo_ref, pos_ref, hn_ref, xg_ref, *, E, C, S, ts):
    b = pl.program_id(0)
    n_tiles = S // ts
    xg_ref[...] = jnp.zeros(xg_ref.shape, xg_ref.dtype)
    sub = lax.broadcasted_iota(jnp.int32, (GATHER_WIN, ts), 0)

    def tile_body(k, carry):
        t0 = pl.multiple_of(k * ts, ts)
        nxt = jnp.minimum(k + 1, n_tiles - 1)
        starts = [(lo_ref[b, k, e] // SLOT_ALIGN) * SLOT_ALIGN for e in range(E)]
        ends = [jnp.where(k + 1 < n_tiles, lo_ref[b, nxt, e], C) for e in range(E)]
        hn_t = hn_ref[0, pl.ds(t0, ts), :]

        def one_pass(w, first):
            sel, row0s = [], []
            for e in range(E):
                start = starts[e] + w * GATHER_WIN
                row0 = jnp.minimum(start, C - GATHER_WIN)
                pe = pos_ref[0, e:e + 1, pl.ds(t0, ts)]
                if not first:
                    pe = jnp.where(pe >= start, pe, -1)
                sel.append(jnp.where(pe - row0 == sub, 1.0, 0.0).astype(BF16))
                row0s.append(pl.multiple_of(row0, SLOT_ALIGN))
            rows = jnp.dot(jnp.concatenate(sel, axis=0), hn_t,
                           preferred_element_type=F32).astype(xg_ref.dtype)
            for e in range(E):
                xg_ref[0, e, pl.ds(row0s[e], GATHER_WIN), :] += rows[e * GATHER_WIN:(e + 1) * GATHER_WIN]

        one_pass(0, True)
        n_pass = jnp.int32(1)
        for e in range(E):
            n_pass = jnp.maximum(n_pass, (ends[e] - starts[e] + GATHER_WIN - 1) // GATHER_WIN)

        def more(w, c):
            one_pass(w, False)
            return c

        lax.fori_loop(1, n_pass, more, 0)
        return carry

    lax.fori_loop(0, n_tiles, tile_body, 0)


def _gather(lo, pos_rows, hn3, C, ts):
    B, S, D = hn3.shape
    E = pos_rows.shape[1]
    assert C % GATHER_WIN == 0
    return pl.pallas_call(
        functools.partial(_gather_body, E=E, C=C, S=S, ts=ts),
        grid=(B,),
        in_specs=[pl.BlockSpec(memory_space=pltpu.SMEM),
                  pl.BlockSpec((1, E, S), lambda b: (b, 0, 0)),
                  pl.BlockSpec((1, S, D), lambda b: (b, 0, 0))],
        out_specs=pl.BlockSpec((1, E, C, D), lambda b: (b, 0, 0, 0)),
        out_shape=jax.ShapeDtypeStruct((B, E, C, D), BF16),
        compiler_params=_params("parallel"),
        name="moe_gather",
    )(lo, pos_rows, hn3)


def _ffn_body(pos_ref, aff_ref, xg_ref, wg_hbm, wu_hbm, wd_hbm, y_ref,
              wg_s, wu_s, wd_s, stg_g, stg_u, stg_d, sems, *, C, S, fc):
    e = pl.program_id(0)
    b = pl.program_id(1)
    n_experts = pl.num_programs(0)
    rows_in = stg_g.shape[0]
    rows_dn = stg_d.shape[0]
    n_chunks = wg_s.shape[1] // rows_in

    def chunk_copies(ee, k):
        r_in = pl.multiple_of(k * rows_in, rows_in)
        r_dn = pl.multiple_of(k * rows_dn, rows_dn)
        return (pltpu.make_async_copy(wg_hbm.at[ee, pl.ds(r_in, rows_in), :], stg_g, sems.at[0]),
                pltpu.make_async_copy(wu_hbm.at[ee, pl.ds(r_in, rows_in), :], stg_u, sems.at[1]),
                pltpu.make_async_copy(wd_hbm.at[ee, pl.ds(r_dn, rows_dn), :], stg_d, sems.at[2]))

    def cast_chunk(slot_, k):
        r_in = pl.multiple_of(k * rows_in, rows_in)
        r_dn = pl.multiple_of(k * rows_dn, rows_dn)
        wg_s[slot_, pl.ds(r_in, rows_in), :] = stg_g[...].astype(BF16)
        wu_s[slot_, pl.ds(r_in, rows_in), :] = stg_u[...].astype(BF16)
        wd_s[slot_, pl.ds(r_dn, rows_dn), :] = stg_d[...].astype(BF16)

    @pl.when((e == 0) & (b == 0))
    def _():
        def stage(k, carry):
            for cp in chunk_copies(0, k):
                cp.start()
            for cp in chunk_copies(0, k):
                cp.wait()
            cast_chunk(0, k)
            return carry

        lax.fori_loop(0, n_chunks, stage, 0)

        @pl.when(n_experts > 1)
        def _():
            for cp in chunk_copies(1, 0):
                cp.start()

    cur = e % 2
    nb = xg_ref.shape[0]
    slot = lax.broadcasted_iota(jnp.int32, (C, S), 0)
    vals = [jnp.sum(jnp.where(pos_ref[t, 0] == slot, aff_ref[t, 0], 0.0), axis=1, keepdims=True)
            for t in range(nb)]
    val = jnp.concatenate(vals, axis=0)
    xg = xg_ref[...].reshape(nb * C, xg_ref.shape[3])
    F = wg_s.shape[2]
    y = jnp.zeros(xg.shape, F32)
    for f0 in range(0, F, fc):
        g = jnp.dot(xg, wg_s[cur, :, f0:f0 + fc], preferred_element_type=F32)
        u = jnp.dot(xg, wu_s[cur, :, f0:f0 + fc], preferred_element_type=F32)
        hid = (g * jax.nn.sigmoid(g) * u).astype(BF16)
        y = y + jnp.dot(hid, wd_s[cur, f0:f0 + fc, :], preferred_element_type=F32)
    y_ref[...] = (y * val).astype(y_ref.dtype).reshape(y_ref.shape)

    @pl.when(e + 1 < n_experts)
    def _():
        for cp in chunk_copies(e + 1, b):
            cp.wait()
        cast_chunk(1 - cur, b)

    wrap = b + 1 == n_chunks
    e_nxt = jnp.where(wrap, e + 2, e + 1)
    k_nxt = jnp.where(wrap, 0, b + 1)

    @pl.when(e_nxt < n_experts)
    def _():
        for cp in chunk_copies(e_nxt, k_nxt):
            cp.start()


FFN_SEQS_PER_STEP = 4


def _expert_ffn(pos_rows, aff_rows, xg, wg, wu, wd):
    B, _, C, D = xg.shape
    S = pos_rows.shape[-1]
    E, _, F = wg.shape
    fc = min(1024, F)
    nb = math.gcd(FFN_SEQS_PER_STEP, B)
    steps = B // nb
    assert D % steps == 0 and F % steps == 0 and (D // steps) % 16 == 0 and (F // steps) % 16 == 0
    hbm = pl.BlockSpec(memory_space=pl.ANY)
    return pl.pallas_call(
        functools.partial(_ffn_body, C=C, S=S, fc=fc),
        grid=(E, steps),
        in_specs=[pl.BlockSpec((nb, 1, 1, S), lambda e, b: (b, e, 0, 0)),
                  pl.BlockSpec((nb, 1, 1, S), lambda e, b: (b, e, 0, 0)),
                  pl.BlockSpec((nb, 1, C, D), lambda e, b: (b, e, 0, 0)),
                  hbm, hbm, hbm],
        out_specs=pl.BlockSpec((nb, 1, C, D), lambda e, b: (b, e, 0, 0)),
        out_shape=jax.ShapeDtypeStruct((B, E, C, D), BF16),
        scratch_shapes=[pltpu.VMEM((2, D, F), BF16), pltpu.VMEM((2, D, F), BF16),
                        pltpu.VMEM((2, F, D), BF16),
                        pltpu.VMEM((D // steps, F), F32), pltpu.VMEM((D // steps, F), F32),
                        pltpu.VMEM((F // steps, D), F32),
                        pltpu.SemaphoreType.DMA((3,))],
        compiler_params=_params("arbitrary", "arbitrary"),
        name="expert_ffn",
    )(pos_rows, aff_rows, xg, wg, wu, wd)


COMBINE_WIN = 64


def _combine_body(lo_ref, pos_ref, y_ref, h_ref, o_ref, *, E, C, ts, n_tiles):
    b = pl.program_id(0)
    n_sub = pos_ref.shape[1] // ts
    lane = lax.broadcasted_iota(jnp.int32, (ts, LANES), 1)
    upper = lane >= COMBINE_WIN
    lane_in = jnp.where(upper, lane - COMBINE_WIN, lane)

    extra = []
    for t in range(n_sub):
        tile = pl.program_id(1) * n_sub + t
        rows = slice(t * ts, (t + 1) * ts)
        pc = pos_ref[0, rows, :]
        nxt = jnp.minimum(tile + 1, n_tiles - 1)
        starts = [(lo_ref[b, tile, e] // SLOT_ALIGN) * SLOT_ALIGN for e in range(E)]
        ends = [jnp.where(tile + 1 < n_tiles, lo_ref[b, nxt, e], C) for e in range(E)]

        def window_product(w, first, pc=pc, starts=starts):
            pieces, y_wins = [], []
            for p in range(E // 2):
                targets = []
                for e in (2 * p, 2 * p + 1):
                    start = starts[e] + w * COMBINE_WIN
                    row0 = jnp.minimum(start, C - COMBINE_WIN)
                    pe = pc[:, e:e + 1]
                    if not first:
                        pe = jnp.where(pe >= start, pe, -1)
                    targets.append(pe - row0)
                    y_wins.append(y_ref[0, e, pl.ds(pl.multiple_of(row0, SLOT_ALIGN), COMBINE_WIN), :])
                hit = jnp.where(upper, targets[1], targets[0]) == lane_in
                pieces.append(jnp.where(hit, 1.0, 0.0).astype(BF16))
            return jnp.dot(jnp.concatenate(pieces, axis=1), jnp.concatenate(y_wins, axis=0),
                           preferred_element_type=F32)

        o_ref[0, rows, :] = h_ref[0, rows, :] + window_product(0, True)

        n_pass = jnp.int32(1)
        for e in range(E):
            n_pass = jnp.maximum(n_pass, (ends[e] - starts[e] + COMBINE_WIN - 1) // COMBINE_WIN)
        extra.append((n_pass, rows, window_product))

    for n_pass, rows, window_product in extra:
        def more(w, carry, rows=rows, window_product=window_product):
            o_ref[0, rows, :] += window_product(w, False)
            return carry

        lax.fori_loop(1, n_pass, more, 0)


def _combine(lo, pos_cols, y, h3, ts):
    B, S, D = h3.shape
    _, E, C, _ = y.shape
    n_tiles = S // ts
    n_sub = math.gcd(4, n_tiles)
    tb = ts * n_sub
    assert E % 2 == 0 and C % COMBINE_WIN == 0 and 2 * COMBINE_WIN == LANES
    return pl.pallas_call(
        functools.partial(_combine_body, E=E, C=C, ts=ts, n_tiles=n_tiles),
        grid=(B, n_tiles // n_sub),
        in_specs=[pl.BlockSpec(memory_space=pltpu.SMEM),
                  pl.BlockSpec((1, tb, E), lambda b, i: (b, i, 0)),
                  pl.BlockSpec((1, E, C, D), lambda b, i: (b, 0, 0, 0)),
                  pl.BlockSpec((1, tb, D), lambda b, i: (b, i, 0))],
        out_specs=pl.BlockSpec((1, tb, D), lambda b, i: (b, i, 0)),
        out_shape=jax.ShapeDtypeStruct((B, S, D), F32),
        compiler_params=_params("parallel", "arbitrary"),
        name="moe_combine",
    )(lo, pos_cols, y, h3)


def kernel(x, g_mix, w_in, g_q, g_k, lam_q1, lam_k1, lam_q2, lam_k2, g_subln, rel_bias, conv_w, conv_b,
           gate_r_w, gate_r_b, gate_i_w, gate_i_b, lru_lambda, w_proj_attn, w_proj_lru, w_out, g_ffn,
           w_router, w_gate_e, w_up_e, w_down_e):
    B, S, D = x.shape
    depth = w_in.shape[0]
    H = rel_bias.shape[1]
    hd = g_q.shape[-1]
    vd = g_subln.shape[-1]
    qk_w = H * 2 * hd
    attn_w = H * vd
    lru_w = conv_w.shape[-1]
    NB, LB = gate_r_w.shape[2], gate_r_w.shape[3]
    E = w_router.shape[-1]
    C = EC_CAPACITY_FACTOR * S // E
    off_k = qk_w
    off_v = off_k + qk_w
    off_x = off_v + attn_w
    off_y = off_x + lru_w
    off_ga = off_y + lru_w
    off_gr = off_ga + D

    h2 = x.reshape(B * S, D)
    for layer in range(depth):
        lam_init = 0.8 - 0.6 * math.exp(-0.3 * layer)
        proj = _inproj(h2, g_mix[layer][None], w_in[layer].astype(BF16))

        lam_params = jnp.stack([lam_q1[layer], lam_k1[layer], lam_q2[layer], lam_k2[layer]])
        o_attn = _attention(proj, rel_bias, g_q[layer], g_k[layer], lam_params, g_subln[layer],
                            B=B, S=S, H=H, hd=hd, vd=vd, off_k=off_k, off_v=off_v, lam_init=lam_init)

        w_cat = -LOG2E * jnp.concatenate([gate_r_w[layer, 0], gate_i_w[layer, 0],
                                          gate_r_w[layer, 1], gate_i_w[layer, 1]], axis=-1)
        b_cat = -LOG2E * jnp.stack([gate_r_b[layer, 0].reshape(NB, LB), gate_i_b[layer, 0].reshape(NB, LB),
                                    gate_r_b[layer, 1].reshape(NB, LB), gate_i_b[layer, 1].reshape(NB, LB)],
                                   axis=1).reshape(NB, 1, 4 * LB)
        b_hi = b_cat.astype(BF16)
        b_lo = (b_cat - b_hi.astype(F32)).astype(BF16)
        w_ext = jnp.concatenate([w_cat.astype(BF16), b_hi, b_lo,
                                 jnp.zeros((NB, LB - 2, 4 * LB), BF16)], axis=1)
        lru_out = _rg_lru(proj, conv_w[layer], conv_b[layer], w_ext, lru_lambda[layer],
                          B=B, S=S, off_x=off_x, off_y=off_y)

        wr = w_router[layer]
        wr_hi = wr.astype(BF16)
        wr2 = jnp.concatenate([wr_hi, (wr - wr_hi.astype(F32)).astype(BF16)], axis=1)
        h2, hn, aff = _merge(o_attn, lru_out, proj, h2, w_proj_attn[layer].astype(BF16),
                             w_proj_lru[layer].astype(BF16), w_out[layer].astype(BF16),
                             g_ffn[layer][None], wr2, off_ga=off_ga, off_gr=off_gr, E=E)

        aff_rows = aff.reshape(B, S, E).transpose(0, 2, 1)
        pos_rows, cnt_rows = _select(aff_rows, C)
        ts = min(256, S)
        lo = cnt_rows[:, :, ::ts].transpose(0, 2, 1)
        xg = _gather(lo, pos_rows, hn.reshape(B, S, D), C, ts)
        y = _expert_ffn(pos_rows.reshape(B, E, 1, S), aff_rows.reshape(B, E, 1, S), xg,
                        w_gate_e[layer], w_up_e[layer], w_down_e[layer])
        h3 = _combine(lo, pos_rows.transpose(0, 2, 1), y, h2.reshape(B, S, D), ts)
        h2 = h3.reshape(B * S, D)
    return h2.reshape(B, S, D)
```

```python
import functools
import math

import jax
import jax.numpy as jnp
import numpy as np
from jax import lax
from jax.experimental import pallas as pl
from jax.experimental.pallas import tpu as pltpu

F32 = jnp.float32
BF16 = jnp.bfloat16
EPS = 1e-6
LANES = 128
SUBLANES = 8
VMEM_LIMIT = 56 * 1024 * 1024
LOG2E = 1.4426950408889634
NEG_BIG = -1e30
REL_MAX_DIST = 128
LRU_C = 8.0
EC_CAPACITY_FACTOR = 2


def _params(*sem):
    return pltpu.CompilerParams(dimension_semantics=sem, vmem_limit_bytes=VMEM_LIMIT)


def _inproj_body(x_ref, g_ref, w_ref, o_ref, xn_ref):
    @pl.when(pl.program_id(1) == 0)
    def _():
        x = x_ref[...]
        ms = jnp.mean(x * x, axis=-1, keepdims=True)
        xn_ref[...] = (x * lax.rsqrt(ms + EPS) * g_ref[...]).astype(BF16)

    o_ref[...] = jnp.dot(xn_ref[...], w_ref[...], preferred_element_type=F32).astype(o_ref.dtype)


def _inproj(x2, g, w_bf):
    T, D = x2.shape
    N = w_bf.shape[1]
    tm = min(1024, T)
    tn = N // 2 if N % (2 * LANES) == 0 else N
    return pl.pallas_call(
        _inproj_body,
        grid=(T // tm, N // tn),
        in_specs=[pl.BlockSpec((tm, D), lambda i, j: (i, 0)),
                  pl.BlockSpec((1, D), lambda i, j: (0, 0)),
                  pl.BlockSpec((D, tn), lambda i, j: (0, j))],
        out_specs=pl.BlockSpec((tm, tn), lambda i, j: (i, j)),
        out_shape=jax.ShapeDtypeStruct((T, N), BF16),
        scratch_shapes=[pltpu.VMEM((tm, D), BF16)],
        compiler_params=_params("parallel", "arbitrary"),
        name="in_proj",
    )(x2, g, w_bf)


def _bucket_table(S, n_buckets, rel):
    half = n_buckets // 2
    max_exact = half // 2
    ret = np.where(rel > 0, half, 0)
    n = np.abs(rel)
    nf = np.maximum(n, max_exact).astype(np.float64)
    large = max_exact + (np.log(nf / max_exact) / math.log(REL_MAX_DIST / max_exact)
                         * (half - max_exact)).astype(np.int32)
    large = np.minimum(large, half - 1)
    return (ret + np.where(n < max_exact, n, large)).astype(np.int32).reshape(1, 2 * S)


def _seg_sumsq(x, seg_ones, split=True):
    x2 = x * x
    hi = x2.astype(BF16)
    out = jnp.dot(hi, seg_ones, preferred_element_type=F32)
    if split:
        lo = (x2 - hi.astype(F32)).astype(BF16)
        out = out + jnp.dot(lo, seg_ones, preferred_element_type=F32)
    return out


NORM_SLACK = 1.01
SAFE_LOG2_SPAN = 100.0


def _attn_body(bucket_ref, bucket_t_ref, relb_ref, q_ref, k_ref, v_ref, gq_ref, gk_ref, lamp_ref, gs_ref,
               segq_ref, segv_ref, o_ref, e_ref, et_ref, fast_ref, kn_ref, va_ref, vat_ref, m_ref, acc_ref,
               *, S, tq, n_sub, kc, hd, vd, lam_init, n_buckets):
    h = pl.program_id(0)
    b = pl.program_id(1)
    i = pl.program_id(2)
    q_scale = hd ** -0.5 * LOG2E

    @pl.when((b == 0) & (i == 0))
    def _():
        def bias_row(bk):
            tab = jnp.zeros((1, 2 * S), F32)
            for n in range(n_buckets):
                tab = jnp.where(bk == n, relb_ref[h, n], tab)
            return tab * LOG2E

        tab = bias_row(bucket_ref[...])
        bmax = jnp.max(tab, axis=-1, keepdims=True)
        bmin = jnp.min(tab, axis=-1, keepdims=True)
        bound = (NORM_SLACK * hd * q_scale) * (jnp.max(jnp.abs(gq_ref[...]), axis=-1, keepdims=True)
                                               * jnp.max(jnp.abs(gk_ref[...]), axis=-1, keepdims=True))
        span = 2.0 * bound + (bmax - bmin)
        fast_ref[0] = (span[0, 0] <= SAFE_LOG2_SPAN).astype(jnp.int32)
        shift = bound + bmax
        e_ref[...] = pltpu.roll(jnp.broadcast_to(tab - shift, (tq, 2 * S)), 0, 1, stride=1, stride_axis=0)
        tab_t = bias_row(bucket_t_ref[...])
        et_ref[...] = pltpu.roll(jnp.broadcast_to(tab_t - shift, (tq, 2 * S)), 0, 1, stride=1, stride_axis=0)

    @pl.when(i == 0)
    def _():
        k = k_ref[...].astype(F32)
        ms = _seg_sumsq(k, segq_ref[...], split=False) * (1.0 / hd)
        kn_ref[...] = (k * lax.rsqrt(ms + EPS) * gk_ref[...]).astype(BF16)
        va_ref[:, :vd] = v_ref[...]
        va_ref[:, vd:] = jnp.ones((S, vd), BF16)
        vat_ref[:vd, :] = v_ref[...].astype(F32).T.astype(BF16)
        vat_ref[vd:, :] = jnp.ones((vat_ref.shape[0] - vd, S), BF16)

    nt_dims = (((1,), (1,)), ((), ()))
    lp = lamp_ref[...]
    lam = (jnp.exp(jnp.sum(lp[0:1] * lp[1:2], axis=-1, keepdims=True))
           - jnp.exp(jnp.sum(lp[2:3] * lp[3:4], axis=-1, keepdims=True)) + lam_init)

    def q_maps_of(t):
        q = q_ref[t * tq:(t + 1) * tq, :].astype(F32)
        lo = lax.broadcasted_iota(jnp.int32, q.shape, 1) < hd
        q2 = q * q
        s_lo = jnp.sum(jnp.where(lo, q2, 0.0), axis=-1, keepdims=True)
        s_hi = jnp.sum(jnp.where(lo, 0.0, q2), axis=-1, keepdims=True)
        ms = jnp.where(lo, s_lo, s_hi) * (1.0 / hd)
        qn = q * lax.rsqrt(ms + EPS) * (gq_ref[...] * q_scale)
        return jnp.where(lo, qn, 0.0).astype(BF16), jnp.where(lo, 0.0, qn).astype(BF16)

    def e_start(t, k0):
        return pl.multiple_of(S + k0 - (i * n_sub + t) * tq, LANES)

    def finish(t, a1, a2):
        o = a1[:, :vd] / a1[:, vd:] - lam * (a2[:, :vd] / a2[:, vd:])
        o = o * lax.rsqrt(_seg_sumsq(o, segv_ref[...]) * (1.0 / vd) + EPS)
        o_ref[t * tq:(t + 1) * tq, :] = (o * (gs_ref[...] * (1.0 - lam_init))).astype(o_ref.dtype)

    def finish_t(t, a1, a2):
        ot = a1[:vd] / a1[vd:vd + 1] - lam * (a2[:vd] / a2[vd:vd + 1])
        ot = ot * lax.rsqrt(jnp.mean(ot * ot, axis=0, keepdims=True) + EPS)
        o = ot.T
        o_ref[t * tq:(t + 1) * tq, :] = (o * (gs_ref[...] * (1.0 - lam_init))).astype(o_ref.dtype)

    @pl.when(fast_ref[0] == 1)
    def _():
        for t in range(n_sub):
            q_maps = q_maps_of(t)
            tile = i * n_sub + t
            chunks = ([], [])
            for c in range(S // tq):
                start = pl.multiple_of(S + (tile - c) * tq, LANES)
                bias_t = et_ref[:, pl.ds(start, tq)]
                for mi in range(2):
                    st = lax.dot_general(kn_ref[c * tq:(c + 1) * tq, :], q_maps[mi], nt_dims,
                                         preferred_element_type=F32)
                    chunks[mi].append(jnp.exp2(st + bias_t).astype(BF16))
            accs = [jnp.dot(vat_ref[...], jnp.concatenate(chunks[mi], axis=0), preferred_element_type=F32)
                    for mi in range(2)]
            finish_t(t, *accs)

    @pl.when(fast_ref[0] != 1)
    def _():
        for t in range(n_sub):
            q_maps = q_maps_of(t)
            m_ref[...] = jnp.full(m_ref.shape, NEG_BIG, F32)
            acc_ref[...] = jnp.zeros(acc_ref.shape, F32)

            def chunk(c, carry):
                k0 = pl.multiple_of(c * kc, kc)
                bias = e_ref[:, pl.ds(e_start(t, k0), kc)]
                for mi in range(2):
                    s = lax.dot_general(q_maps[mi], kn_ref[pl.ds(k0, kc), :], nt_dims,
                                        preferred_element_type=F32) + bias
                    m_old = m_ref[mi]
                    m_new = jnp.maximum(m_old, jnp.max(s, axis=-1, keepdims=True))
                    p = jnp.exp2(s - m_new).astype(BF16)
                    acc_ref[mi] = (jnp.exp2(m_old - m_new) * acc_ref[mi]
                                   + jnp.dot(p, va_ref[pl.ds(k0, kc), :], preferred_element_type=F32))
                    m_ref[mi] = m_new
                return carry

            lax.fori_loop(0, S // kc, chunk, 0)
            finish(t, acc_ref[0], acc_ref[1])


def _attention(proj, rel_bias, g_q, g_k, lam_params, g_subln, *, B, S, H, hd, vd, off_k, off_v, lam_init):
    T = B * S
    tq = min(512, S)
    kc = min(512, S)
    n_sub = math.gcd(4, S // tq)
    tqb = n_sub * tq
    nq = S // tqb
    n_buckets = rel_bias.shape[0]
    hw = 2 * hd
    bucket = jnp.asarray(_bucket_table(S, n_buckets, np.arange(2 * S) - S))
    bucket_t = jnp.asarray(_bucket_table(S, n_buckets, S - np.arange(2 * S)))
    body = functools.partial(_attn_body, S=S, tq=tq, n_sub=n_sub, kc=kc, hd=hd, vd=vd, lam_init=lam_init,
                             n_buckets=n_buckets)
    kblk = off_k // hw
    vblk = off_v // vd
    seg_q = np.kron(np.eye(2, dtype=np.float32), np.ones((hd, hd), np.float32))
    return pl.pallas_call(
        body,
        grid=(H, B, nq),
        in_specs=[pl.BlockSpec((1, 2 * S), lambda h, b, i: (0, 0)),
                  pl.BlockSpec((1, 2 * S), lambda h, b, i: (0, 0)),
                  pl.BlockSpec(memory_space=pltpu.SMEM),
                  pl.BlockSpec((tqb, hw), lambda h, b, i: (b * nq + i, h)),
                  pl.BlockSpec((S, hw), lambda h, b, i: (b, kblk + h)),
                  pl.BlockSpec((S, vd), lambda h, b, i: (b, vblk + h)),
                  pl.BlockSpec((1, hw), lambda h, b, i: (0, 0)),
                  pl.BlockSpec((1, hw), lambda h, b, i: (0, 0)),
                  pl.BlockSpec((4, hd), lambda h, b, i: (0, 0)),
                  pl.BlockSpec((1, vd), lambda h, b, i: (0, 0)),
                  pl.BlockSpec((hw, hw), lambda h, b, i: (0, 0)),
                  pl.BlockSpec((vd, vd), lambda h, b, i: (0, 0))],
        out_specs=pl.BlockSpec((tqb, vd), lambda h, b, i: (b * nq + i, h)),
        out_shape=jax.ShapeDtypeStruct((T, H * vd), BF16),
        scratch_shapes=[pltpu.VMEM((tq, 2 * S), F32),
                        pltpu.VMEM((tq, 2 * S), F32),
                        pltpu.SMEM((1,), jnp.int32),
                        pltpu.VMEM((S, hw), BF16),
                        pltpu.VMEM((S, 2 * vd), BF16),
                        pltpu.VMEM((vd + 16, S), BF16),
                        pltpu.VMEM((2, tq, 1), F32),
                        pltpu.VMEM((2, tq, 2 * vd), F32)],
        compiler_params=_params("arbitrary", "arbitrary", "arbitrary"),
        name="diff_attention",
    )(bucket, bucket_t, rel_bias.T, proj, proj, proj, jnp.tile(g_q, 2)[None], jnp.tile(g_k, 2)[None],
      lam_params, g_subln[None], jnp.asarray(seg_q, BF16), jnp.ones((vd, vd), BF16))


LRU_SEGMENTS = SUBLANES
LRU_PITCH_PAD = 8


def _sublane_scan(a, bb, reverse):
    ri = lax.broadcasted_iota(jnp.int32, a.shape, 0)
    for d in (1, 2, 4):
        if reverse:
            keep, sh = ri < SUBLANES - d, SUBLANES - d
        else:
            keep, sh = ri >= d, d
        bb = bb + a * jnp.where(keep, pltpu.roll(bb, sh, 0), 0.0)
        a = a * jnp.where(keep, pltpu.roll(a, sh, 0), 1.0)
    return bb


LRU_BLOCKS_PER_STEP = 2


def _lru_body(x_ref, y_ref, cw_ref, cb_ref, w_ref, lam_ref, o_ref,
              af_ref, bf_ref, ab_ref, bb_ref, hf_ref, pf_ref, hb_ref, pb_ref, hs_ref, *, S, W):
    nb = w_ref.shape[0]
    L = S // LRU_SEGMENTS
    pitch = L + LRU_PITCH_PAD
    x = x_ref[...].astype(F32)
    row = lax.broadcasted_iota(jnp.int32, x.shape, 0)
    cw = cw_ref[...]
    xc_all = (cw[0:1] * jnp.where(row >= 2, pltpu.roll(x, 2, 0), 0.0)
              + cw[1:2] * jnp.where(row >= 1, pltpu.roll(x, 1, 0), 0.0)
              + cw[2:3] * x
              + cw[3:4] * jnp.where(row < S - 1, pltpu.roll(x, S - 1, 0), 0.0)
              + cb_ref[...])
    lam = lam_ref[...]
    bias_lanes = jnp.where(lax.broadcasted_iota(jnp.int32, (S, W), 1) < 2, 1.0, 0.0).astype(BF16)
    for n in range(nb):
        xc = xc_all[:, n * W:(n + 1) * W]
        gneg = jnp.dot(jnp.concatenate([xc.astype(BF16), bias_lanes], axis=1), w_ref[n],
                       preferred_element_type=F32)
        for d, (a_ref, b_ref) in enumerate(((af_ref, bf_ref), (ab_ref, bb_ref))):
            r = 1.0 / (1.0 + jnp.exp2(gneg[:, (2 * d) * W:(2 * d + 1) * W]))
            ig = 1.0 / (1.0 + jnp.exp2(gneg[:, (2 * d + 1) * W:(2 * d + 2) * W]))
            nl = -lam[d:d + 1, n * W:(n + 1) * W]
            softplus = jnp.maximum(nl, 0.0) + jnp.log1p(jnp.exp(-jnp.abs(nl)))
            a = jnp.exp2((-LRU_C * LOG2E * softplus) * r)
            y2 = 1.0 - a * a
            bvals = jnp.where(y2 > 0.0, y2 * lax.rsqrt(y2), 0.0) * ig * xc
            for sg in range(LRU_SEGMENTS):
                a_ref[n, sg * pitch:sg * pitch + L, :] = a[sg * L:(sg + 1) * L]
                b_ref[n, sg * pitch:sg * pitch + L, :] = bvals[sg * L:(sg + 1) * L]

    def seg_rows(ref, n, j):
        return ref[n, pl.ds(j, LRU_SEGMENTS, stride=pitch), :]

    def step(j, carry):
        out = []
        jb = L - 1 - j
        for n in range(nb):
            hf, pf, hb, pb = carry[4 * n:4 * n + 4]
            a = seg_rows(af_ref, n, j)
            hf = a * hf + seg_rows(bf_ref, n, j)
            pf = a * pf
            hf_ref[n, j] = hf
            pf_ref[n, j] = pf
            a = seg_rows(ab_ref, n, jb)
            hb = a * hb + seg_rows(bb_ref, n, jb)
            pb = a * pb
            hb_ref[n, jb] = hb
            pb_ref[n, jb] = pb
            out += [hf, pf, hb, pb]
        return tuple(out)

    zero = jnp.zeros((LRU_SEGMENTS, W), F32)
    one = jnp.ones((LRU_SEGMENTS, W), F32)
    ends = lax.fori_loop(0, L, step, (zero, one, zero, one) * nb, unroll=8)

    si = lax.broadcasted_iota(jnp.int32, (LRU_SEGMENTS, W), 0)
    cfs, cbs = [], []
    for n in range(nb):
        hf, pf, hb, pb = ends[4 * n:4 * n + 4]
        cfs.append(jnp.where(si >= 1, pltpu.roll(_sublane_scan(pf, hf, False), 1, 0), 0.0))
        cbs.append(jnp.where(si < LRU_SEGMENTS - 1,
                             pltpu.roll(_sublane_scan(pb, hb, True), LRU_SEGMENTS - 1, 0), 0.0))

    def fix(j, carry):
        for n in range(nb):
            hs_ref[n, pl.ds(j, LRU_SEGMENTS, stride=pitch), :] = (
                hf_ref[n, j] + pf_ref[n, j] * cfs[n] + hb_ref[n, j] + pb_ref[n, j] * cbs[n])
        return carry

    lax.fori_loop(0, L, fix, 0, unroll=8)

    for n in range(nb):
        for sg in range(LRU_SEGMENTS):
            y = y_ref[sg * L:(sg + 1) * L, n * W:(n + 1) * W].astype(F32)
            o_ref[sg * L:(sg + 1) * L, n * W:(n + 1) * W] = (
                hs_ref[n, sg * pitch:sg * pitch + L, :] * jax.nn.gelu(y)).astype(o_ref.dtype)


def _rg_lru(proj, conv_w, conv_b, w_ext, lru_lambda, *, B, S, off_x, off_y):
    T = B * S
    NB, W2, _ = w_ext.shape
    W = W2 // 2
    nb = math.gcd(LRU_BLOCKS_PER_STEP, NB)
    wb = nb * W
    xblk = off_x // wb
    yblk = off_y // wb
    body = functools.partial(_lru_body, S=S, W=W)
    padded = LRU_SEGMENTS * (S // LRU_SEGMENTS + LRU_PITCH_PAD)
    return pl.pallas_call(
        body,
        grid=(B, NB // nb),
        in_specs=[pl.BlockSpec((S, wb), lambda b, n: (b, xblk + n)),
                  pl.BlockSpec((S, wb), lambda b, n: (b, yblk + n)),
                  pl.BlockSpec((conv_w.shape[0], wb), lambda b, n: (0, n)),
                  pl.BlockSpec((1, wb), lambda b, n: (0, n)),
                  pl.BlockSpec((nb, 2 * W, 4 * W), lambda b, n: (n, 0, 0)),
                  pl.BlockSpec((2, wb), lambda b, n: (0, n))],
        out_specs=pl.BlockSpec((S, wb), lambda b, n: (b, n)),
        out_shape=jax.ShapeDtypeStruct((T, NB * W), BF16),
        scratch_shapes=([pltpu.VMEM((nb, padded, W), F32)] * 4
                        + [pltpu.VMEM((nb, S // LRU_SEGMENTS, LRU_SEGMENTS, W), F32)] * 4
                        + [pltpu.VMEM((nb, padded, W), F32)]),
        compiler_params=_params("parallel", "parallel"),
        name="rg_lru",
    )(proj, proj, conv_w, conv_b[None], w_ext, lru_lambda)


def _merge_body(o_ref, r_ref, ga_ref, gr_ref, x_ref, wpa_ref, wpl_ref, wo_ref, g_ref, wr_ref,
                h_ref, hn_ref, aff_ref, *, E, n_sub):
    ts = o_ref.shape[0] // n_sub
    for t in range(n_sub):
        rows = slice(t * ts, (t + 1) * ts)
        ba = jnp.dot(o_ref[rows, :], wpa_ref[...], preferred_element_type=F32)
        br = jnp.dot(r_ref[rows, :], wpl_ref[...], preferred_element_type=F32)
        mixed = (jax.nn.sigmoid(ga_ref[rows, :].astype(F32)) * ba
                 + jax.nn.sigmoid(gr_ref[rows, :].astype(F32)) * br)
        h = x_ref[rows, :] + jnp.dot(mixed.astype(BF16), wo_ref[...], preferred_element_type=F32)
        h_ref[rows, :] = h
        hn = h * lax.rsqrt(jnp.mean(h * h, axis=-1, keepdims=True) + EPS) * g_ref[...]
        hn_hi = hn.astype(BF16)
        hn_ref[rows, :] = hn_hi
        hn_lo = (hn - hn_hi.astype(F32)).astype(BF16)
        lg = (jnp.dot(hn_hi, wr_ref[...], preferred_element_type=F32)
              + jnp.dot(hn_lo, wr_ref[...], preferred_element_type=F32))
        logits = lg[:, :E] + lg[:, E:2 * E]
        logits = logits - jnp.max(logits, axis=-1, keepdims=True)
        ex = jnp.exp(logits)
        aff_ref[rows, :] = ex / jnp.sum(ex, axis=-1, keepdims=True)


def _merge(o_attn, lru_out, proj, x2, wpa, wpl, wo, g_ffn, wr2, *, off_ga, off_gr, E):
    T, D = x2.shape
    tm = min(1024, T)
    n_sub = 4 if tm % 1024 == 0 else 1
    resident = dict(pipeline_mode=pl.Buffered(1))
    gab = off_ga // D
    grb = off_gr // D
    row = lambda i: (i, 0)
    const = lambda i: (0, 0)
    return pl.pallas_call(
        functools.partial(_merge_body, E=E, n_sub=n_sub),
        grid=(T // tm,),
        in_specs=[pl.BlockSpec((tm, D), row), pl.BlockSpec((tm, D), row),
                  pl.BlockSpec((tm, D), lambda i: (i, gab)), pl.BlockSpec((tm, D), lambda i: (i, grb)),
                  pl.BlockSpec((tm, D), row),
                  pl.BlockSpec((D, D), const, **resident), pl.BlockSpec((D, D), const, **resident),
                  pl.BlockSpec((D, D), const, **resident),
                  pl.BlockSpec((1, D), const), pl.BlockSpec((D, 2 * E), const)],
        out_specs=[pl.BlockSpec((tm, D), row), pl.BlockSpec((tm, D), row), pl.BlockSpec((tm, E), row)],
        out_shape=[jax.ShapeDtypeStruct((T, D), F32), jax.ShapeDtypeStruct((T, D), BF16),
                   jax.ShapeDtypeStruct((T, E), F32)],
        compiler_params=_params("parallel"),
        name="merge_router",
    )(o_attn, lru_out, proj, proj, x2, wpa, wpl, wo, g_ffn, wr2)


SELECT_SEQS_PER_STEP = 8


def _select_body(aff_ref, pos_ref, cnt_ref, *, S, C):
    rows = aff_ref.shape[0] * aff_ref.shape[1]
    bits = pltpu.bitcast(aff_ref[...].reshape(rows, S), jnp.int32)

    def refine(k, t):
        cand = t | jnp.left_shift(jnp.int32(1), 30 - k)
        cnt = jnp.sum(jnp.where(bits >= cand, 1.0, 0.0), axis=-1, keepdims=True)
        return jnp.where(cnt >= C, cand, t)

    t = lax.fori_loop(0, 31, refine, jnp.zeros((rows, 1), jnp.int32))
    gt = bits > t
    eq = bits == t
    need = C - jnp.sum(jnp.where(gt, 1.0, 0.0), axis=-1, keepdims=True).astype(jnp.int32)
    packed = jnp.where(gt, 1 << 16, 0) + jnp.where(eq, 1, 0)
    lane = lax.broadcasted_iota(jnp.int32, (rows, S), 1)
    incl = packed
    d = 1
    while d < S:
        incl = incl + jnp.where(lane >= d, pltpu.roll(incl, d, 1), 0)
        d *= 2
    excl = incl - packed
    n_gt = excl >> 16
    n_eq = excl & 0xFFFF
    sel = gt | (eq & (n_eq < need))
    before = n_gt + jnp.minimum(n_eq, need)
    cnt_ref[...] = before.reshape(cnt_ref.shape)
    pos_ref[...] = jnp.where(sel, before, -1).reshape(pos_ref.shape)


def _select(aff_t, C):
    B, E, S = aff_t.shape
    nb = math.gcd(SELECT_SEQS_PER_STEP, B)
    spec = pl.BlockSpec((nb, E, S), lambda b: (b, 0, 0))
    return pl.pallas_call(
        functools.partial(_select_body, S=S, C=C),
        grid=(B // nb,),
        in_specs=[spec],
        out_specs=[spec, spec],
        out_shape=[jax.ShapeDtypeStruct((B, E, S), jnp.int32)] * 2,
        compiler_params=_params("parallel"),
        name="topc_select",
    )(aff_t)


GATHER_WIN = 64
SLOT_ALIGN = 16


def _gather_body(lo_ref, pos_ref, hn_ref, xg_ref, *, E, C, S, ts):
    b = pl.program_id(0)
    n_tiles = S // ts
    xg_ref[...] = jnp.zeros(xg_ref.shape, xg_ref.dtype)
    sub = lax.broadcasted_iota(jnp.int32, (GATHER_WIN, ts), 0)

    def tile_body(k, carry):
        t0 = pl.multiple_of(k * ts, ts)
        nxt = jnp.minimum(k + 1, n_tiles - 1)
        starts = [(lo_ref[b, k, e] // SLOT_ALIGN) * SLOT_ALIGN for e in range(E)]
        ends = [jnp.where(k + 1 < n_tiles, lo_ref[b, nxt, e], C) for e in range(E)]
        hn_t = hn_ref[0, pl.ds(t0, ts), :]

        def one_pass(w, first):
            sel, row0s = [], []
            for e in range(E):
                start = starts[e] + w * GATHER_WIN
                row0 = jnp.minimum(start, C - GATHER_WIN)
                pe = pos_ref[0, e:e + 1, pl.ds(t0, ts)]
                if not first:
                    pe = jnp.where(pe >= start, pe, -1)
                sel.append(jnp.where(pe - row0 == sub, 1.0, 0.0).astype(BF16))
                row0s.append(pl.multiple_of(row0, SLOT_ALIGN))
            rows = jnp.dot(jnp.concatenate(sel, axis=0), hn_t,
                           preferred_element_type=F32).astype(xg_ref.dtype)
            for e in range(E):
                xg_ref[0, e, pl.ds(row0s[e], GATHER_WIN), :] += rows[e * GATHER_WIN:(e + 1) * GATHER_WIN]

        one_pass(0, True)
        n_pass = jnp.int32(1)
        for e in range(E):
            n_pass = jnp.maximum(n_pass, (ends[e] - starts[e] + GATHER_WIN - 1) // GATHER_WIN)

        def more(w, c):
            one_pass(w, False)
            return c

        lax.fori_loop(1, n_pass, more, 0)
        return carry

    lax.fori_loop(0, n_tiles, tile_body, 0)


def _gather(lo, pos_rows, hn3, C, ts):
    B, S, D = hn3.shape
    E = pos_rows.shape[1]
    assert C % GATHER_WIN == 0
    return pl.pallas_call(
        functools.partial(_gather_body, E=E, C=C, S=S, ts=ts),
        grid=(B,),
        in_specs=[pl.BlockSpec(memory_space=pltpu.SMEM),
                  pl.BlockSpec((1, E, S), lambda b: (b, 0, 0)),
                  pl.BlockSpec((1, S, D), lambda b: (b, 0, 0))],
        out_specs=pl.BlockSpec((1, E, C, D), lambda b: (b, 0, 0, 0)),
        out_shape=jax.ShapeDtypeStruct((B, E, C, D), BF16),
        compiler_params=_params("parallel"),
        name="moe_gather",
    )(lo, pos_rows, hn3)


def _ffn_body(pos_ref, aff_ref, xg_ref, wg_hbm, wu_hbm, wd_hbm, y_ref,
              wg_s, wu_s, wd_s, stg_g, stg_u, stg_d, sems, *, C, S, fc):
    e = pl.program_id(0)
    b = pl.program_id(1)
    n_experts = pl.num_programs(0)
    rows_in = stg_g.shape[0]
    rows_dn = stg_d.shape[0]
    n_chunks = wg_s.shape[1] // rows_in

    def chunk_copies(ee, k):
        r_in = pl.multiple_of(k * rows_in, rows_in)
        r_dn = pl.multiple_of(k * rows_dn, rows_dn)
        return (pltpu.make_async_copy(wg_hbm.at[ee, pl.ds(r_in, rows_in), :], stg_g, sems.at[0]),
                pltpu.make_async_copy(wu_hbm.at[ee, pl.ds(r_in, rows_in), :], stg_u, sems.at[1]),
                pltpu.make_async_copy(wd_hbm.at[ee, pl.ds(r_dn, rows_dn), :], stg_d, sems.at[2]))

    def cast_chunk(slot_, k):
        r_in = pl.multiple_of(k * rows_in, rows_in)
        r_dn = pl.multiple_of(k * rows_dn, rows_dn)
        wg_s[slot_, pl.ds(r_in, rows_in), :] = stg_g[...].astype(BF16)
        wu_s[slot_, pl.ds(r_in, rows_in), :] = stg_u[...].astype(BF16)
        wd_s[slot_, pl.ds(r_dn, rows_dn), :] = stg_d[...].astype(BF16)

    @pl.when((e == 0) & (b == 0))
    def _():
        def stage(k, carry):
            for cp in chunk_copies(0, k):
                cp.start()
            for cp in chunk_copies(0, k):
                cp.wait()
            cast_chunk(0, k)
            return carry

        lax.fori_loop(0, n_chunks, stage, 0)

        @pl.when(n_experts > 1)
        def _():
            for cp in chunk_copies(1, 0):
                cp.start()

    cur = e % 2
    nb = xg_ref.shape[0]
    slot = lax.broadcasted_iota(jnp.int32, (C, S), 0)
    vals = [jnp.sum(jnp.where(pos_ref[t, 0] == slot, aff_ref[t, 0], 0.0), axis=1, keepdims=True)
            for t in range(nb)]
    val = jnp.concatenate(vals, axis=0)
    xg = xg_ref[...].reshape(nb * C, xg_ref.shape[3])
    F = wg_s.shape[2]
    y = jnp.zeros(xg.shape, F32)
    for f0 in range(0, F, fc):
        g = jnp.dot(xg, wg_s[cur, :, f0:f0 + fc], preferred_element_type=F32)
        u = jnp.dot(xg, wu_s[cur, :, f0:f0 + fc], preferred_element_type=F32)
        hid = (g * jax.nn.sigmoid(g) * u).astype(BF16)
        y = y + jnp.dot(hid, wd_s[cur, f0:f0 + fc, :], preferred_element_type=F32)
    y_ref[...] = (y * val).astype(y_ref.dtype).reshape(y_ref.shape)

    @pl.when(e + 1 < n_experts)
    def _():
        for cp in chunk_copies(e + 1, b):
            cp.wait()
        cast_chunk(1 - cur, b)

    wrap = b + 1 == n_chunks
    e_nxt = jnp.where(wrap, e + 2, e + 1)
    k_nxt = jnp.where(wrap, 0, b + 1)

    @pl.when(e_nxt < n_experts)
    def _():
        for cp in chunk_copies(e_nxt, k_nxt):
            cp.start()


FFN_SEQS_PER_STEP = 4


def _expert_ffn(pos_rows, aff_rows, xg, wg, wu, wd):
    B, _, C, D = xg.shape
    S = pos_rows.shape[-1]
    E, _, F = wg.shape
    fc = min(1024, F)
    nb = math.gcd(FFN_SEQS_PER_STEP, B)
    steps = B // nb
    assert D % steps == 0 and F % steps == 0 and (D // steps) % 16 == 0 and (F // steps) % 16 == 0
    hbm = pl.BlockSpec(memory_space=pl.ANY)
    return pl.pallas_call(
        functools.partial(_ffn_body, C=C, S=S, fc=fc),
        grid=(E, steps),
        in_specs=[pl.BlockSpec((nb, 1, 1, S), lambda e, b: (b, e, 0, 0)),
                  pl.BlockSpec((nb, 1, 1, S), lambda e, b: (b, e, 0, 0)),
                  pl.BlockSpec((nb, 1, C, D), lambda e, b: (b, e, 0, 0)),
                  hbm, hbm, hbm],
        out_specs=pl.BlockSpec((nb, 1, C, D), lambda e, b: (b, e, 0, 0)),
        out_shape=jax.ShapeDtypeStruct((B, E, C, D), BF16),
        scratch_shapes=[pltpu.VMEM((2, D, F), BF16), pltpu.VMEM((2, D, F), BF16),
                        pltpu.VMEM((2, F, D), BF16),
                        pltpu.VMEM((D // steps, F), F32), pltpu.VMEM((D // steps, F), F32),
                        pltpu.VMEM((F // steps, D), F32),
                        pltpu.SemaphoreType.DMA((3,))],
        compiler_params=_params("arbitrary", "arbitrary"),
        name="expert_ffn",
    )(pos_rows, aff_rows, xg, wg, wu, wd)


COMBINE_WIN = 64


def _combine_body(lo_ref, pos_ref, y_ref, h_ref, o_ref, *, E, C, ts, n_tiles):
    b = pl.program_id(0)
    n_sub = pos_ref.shape[1] // ts
    lane = lax.broadcasted_iota(jnp.int32, (ts, LANES), 1)
    upper = lane >= COMBINE_WIN
    lane_in = jnp.where(upper, lane - COMBINE_WIN, lane)

    extra = []
    for t in range(n_sub):
        tile = pl.program_id(1) * n_sub + t
        rows = slice(t * ts, (t + 1) * ts)
        pc = pos_ref[0, rows, :]
        nxt = jnp.minimum(tile + 1, n_tiles - 1)
        starts = [(lo_ref[b, tile, e] // SLOT_ALIGN) * SLOT_ALIGN for e in range(E)]
        ends = [jnp.where(tile + 1 < n_tiles, lo_ref[b, nxt, e], C) for e in range(E)]

        def window_product(w, first, pc=pc, starts=starts):
            pieces, y_wins = [], []
            for p in range(E // 2):
                targets = []
                for e in (2 * p, 2 * p + 1):
                    start = starts[e] + w * COMBINE_WIN
                    row0 = jnp.minimum(start, C - COMBINE_WIN)
                    pe = pc[:, e:e + 1]
                    if not first:
                        pe = jnp.where(pe >= start, pe, -1)
                    targets.append(pe - row0)
                    y_wins.append(y_ref[0, e, pl.ds(pl.multiple_of(row0, SLOT_ALIGN), COMBINE_WIN), :])
                hit = jnp.where(upper, targets[1], targets[0]) == lane_in
                pieces.append(jnp.where(hit, 1.0, 0.0).astype(BF16))
            return jnp.dot(jnp.concatenate(pieces, axis=1), jnp.concatenate(y_wins, axis=0),
                           preferred_element_type=F32)

        o_ref[0, rows, :] = h_ref[0, rows, :] + window_product(0, True)

        n_pass = jnp.int32(1)
        for e in range(E):
            n_pass = jnp.maximum(n_pass, (ends[e] - starts[e] + COMBINE_WIN - 1) // COMBINE_WIN)
        extra.append((n_pass, rows, window_product))

    for n_pass, rows, window_product in extra:
        def more(w, carry, rows=rows, window_product=window_product):
            o_ref[0, rows, :] += window_product(w, False)
            return carry

        lax.fori_loop(1, n_pass, more, 0)


def _combine(lo, pos_cols, y, h3, ts):
    B, S, D = h3.shape
    _, E, C, _ = y.shape
    n_tiles = S // ts
    n_sub = math.gcd(4, n_tiles)
    tb = ts * n_sub
    assert E % 2 == 0 and C % COMBINE_WIN == 0 and 2 * COMBINE_WIN == LANES
    return pl.pallas_call(
        functools.partial(_combine_body, E=E, C=C, ts=ts, n_tiles=n_tiles),
        grid=(B, n_tiles // n_sub),
        in_specs=[pl.BlockSpec(memory_space=pltpu.SMEM),
                  pl.BlockSpec((1, tb, E), lambda b, i: (b, i, 0)),
                  pl.BlockSpec((1, E, C, D), lambda b, i: (b, 0, 0, 0)),
                  pl.BlockSpec((1, tb, D), lambda b, i: (b, i, 0))],
        out_specs=pl.BlockSpec((1, tb, D), lambda b, i: (b, i, 0)),
        out_shape=jax.ShapeDtypeStruct((B, S, D), F32),
        compiler_params=_params("parallel", "arbitrary"),
        name="moe_combine",
    )(lo, pos_cols, y, h3)


def kernel(x, g_mix, w_in, g_q, g_k, lam_q1, lam_k1, lam_q2, lam_k2, g_subln, rel_bias, conv_w, conv_b,
           gate_r_w, gate_r_b, gate_i_w, gate_i_b, lru_lambda, w_proj_attn, w_proj_lru, w_out, g_ffn,
           w_router, w_gate_e, w_up_e, w_down_e):
    B, S, D = x.shape
    depth = w_in.shape[0]
    H = rel_bias.shape[1]
    hd = g_q.shape[-1]
    vd = g_subln.shape[-1]
    qk_w = H * 2 * hd
    attn_w = H * vd
    lru_w = conv_w.shape[-1]
    NB, LB = gate_r_w.shape[2], gate_r_w.shape[3]
    E = w_router.shape[-1]
    C = EC_CAPACITY_FACTOR * S // E
    off_k = qk_w
    off_v = off_k + qk_w
    off_x = off_v + attn_w
    off_y = off_x + lru_w
    off_ga = off_y + lru_w
    off_gr = off_ga + D

    h2 = x.reshape(B * S, D)
    for layer in range(depth):
        lam_init = 0.8 - 0.6 * math.exp(-0.3 * layer)
        proj = _inproj(h2, g_mix[layer][None], w_in[layer].astype(BF16))

        lam_params = jnp.stack([lam_q1[layer], lam_k1[layer], lam_q2[layer], lam_k2[layer]])
        o_attn = _attention(proj, rel_bias, g_q[layer], g_k[layer], lam_params, g_subln[layer],
                            B=B, S=S, H=H, hd=hd, vd=vd, off_k=off_k, off_v=off_v, lam_init=lam_init)

        w_cat = -LOG2E * jnp.concatenate([gate_r_w[layer, 0], gate_i_w[layer, 0],
                                          gate_r_w[layer, 1], gate_i_w[layer, 1]], axis=-1)
        b_cat = -LOG2E * jnp.stack([gate_r_b[layer, 0].reshape(NB, LB), gate_i_b[layer, 0].reshape(NB, LB),
                                    gate_r_b[layer, 1].reshape(NB, LB), gate_i_b[layer, 1].reshape(NB, LB)],
                                   axis=1).reshape(NB, 1, 4 * LB)
        b_hi = b_cat.astype(BF16)
        b_lo = (b_cat - b_hi.astype(F32)).astype(BF16)
        w_ext = jnp.concatenate([w_cat.astype(BF16), b_hi, b_lo,
                                 jnp.zeros((NB, LB - 2, 4 * LB), BF16)], axis=1)
        lru_out = _rg_lru(proj, conv_w[layer], conv_b[layer], w_ext, lru_lambda[layer],
                          B=B, S=S, off_x=off_x, off_y=off_y)

        wr = w_router[layer]
        wr_hi = wr.astype(BF16)
        wr2 = jnp.concatenate([wr_hi, (wr - wr_hi.astype(F32)).astype(BF16)], axis=1)
        h2, hn, aff = _merge(o_attn, lru_out, proj, h2, w_proj_attn[layer].astype(BF16),
                             w_proj_lru[layer].astype(BF16), w_out[layer].astype(BF16),
                             g_ffn[layer][None], wr2, off_ga=off_ga, off_gr=off_gr, E=E)

        aff_rows = aff.reshape(B, S, E).transpose(0, 2, 1)
        pos_rows, cnt_rows = _select(aff_rows, C)
        ts = min(256, S)
        lo = cnt_rows[:, :, ::ts].transpose(0, 2, 1)
        xg = _gather(lo, pos_rows, hn.reshape(B, S, D), C, ts)
        y = _expert_ffn(pos_rows.reshape(B, E, 1, S), aff_rows.reshape(B, E, 1, S), xg,
                        w_gate_e[layer], w_up_e[layer], w_down_e[layer])
        h3 = _combine(lo, pos_rows.transpose(0, 2, 1), y, h2.reshape(B, S, D), ts)
        h2 = h3.reshape(B * S, D)
    return h2.reshape(B, S, D)
```

```python
import functools
import math

import jax
import jax.numpy as jnp
import numpy as np
from jax import lax
from jax.experimental import pallas as pl
from jax.experimental.pallas import tpu as pltpu

F32 = jnp.float32
BF16 = jnp.bfloat16
EPS = 1e-6
LANES = 128
SUBLANES = 8
VMEM_LIMIT = 56 * 1024 * 1024
LOG2E = 1.4426950408889634
NEG_BIG = -1e30
REL_MAX_DIST = 128
LRU_C = 8.0
EC_CAPACITY_FACTOR = 2


def _params(*sem):
    return pltpu.CompilerParams(dimension_semantics=sem, vmem_limit_bytes=VMEM_LIMIT)


def _inproj_body(x_ref, g_ref, w_ref, o_ref, xn_ref):
    @pl.when(pl.program_id(1) == 0)
    def _():
        x = x_ref[...]
        ms = jnp.mean(x * x, axis=-1, keepdims=True)
        xn_ref[...] = (x * lax.rsqrt(ms + EPS) * g_ref[...]).astype(BF16)

    o_ref[...] = jnp.dot(xn_ref[...], w_ref[...], preferred_element_type=F32).astype(o_ref.dtype)


def _inproj(x2, g, w_bf):
    T, D = x2.shape
    N = w_bf.shape[1]
    tm = min(1024, T)
    tn = N // 2 if N % (2 * LANES) == 0 else N
    return pl.pallas_call(
        _inproj_body,
        grid=(T // tm, N // tn),
        in_specs=[pl.BlockSpec((tm, D), lambda i, j: (i, 0)),
                  pl.BlockSpec((1, D), lambda i, j: (0, 0)),
                  pl.BlockSpec((D, tn), lambda i, j: (0, j))],
        out_specs=pl.BlockSpec((tm, tn), lambda i, j: (i, j)),
        out_shape=jax.ShapeDtypeStruct((T, N), BF16),
        scratch_shapes=[pltpu.VMEM((tm, D), BF16)],
        compiler_params=_params("parallel", "arbitrary"),
        name="in_proj",
    )(x2, g, w_bf)


def _bucket_table(S, n_buckets, rel):
    half = n_buckets // 2
    max_exact = half // 2
    ret = np.where(rel > 0, half, 0)
    n = np.abs(rel)
    nf = np.maximum(n, max_exact).astype(np.float64)
    large = max_exact + (np.log(nf / max_exact) / math.log(REL_MAX_DIST / max_exact)
                         * (half - max_exact)).astype(np.int32)
    large = np.minimum(large, half - 1)
    return (ret + np.where(n < max_exact, n, large)).astype(np.int32).reshape(1, 2 * S)


def _seg_sumsq(x, seg_ones, split=True):
    x2 = x * x
    hi = x2.astype(BF16)
    out = jnp.dot(hi, seg_ones, preferred_element_type=F32)
    if split:
        lo = (x2 - hi.astype(F32)).astype(BF16)
        out = out + jnp.dot(lo, seg_ones, preferred_element_type=F32)
    return out


NORM_SLACK = 1.01
SAFE_LOG2_SPAN = 100.0


def _attn_body(bucket_ref, bucket_t_ref, relb_ref, q_ref, k_ref, v_ref, gq_ref, gk_ref, lamp_ref, gs_ref,
               segq_ref, segv_ref, o_ref, e_ref, et_ref, fast_ref, kn_ref, va_ref, vat_ref, m_ref, acc_ref,
               *, S, tq, n_sub, kc, hd, vd, lam_init, n_buckets):
    h = pl.program_id(0)
    b = pl.program_id(1)
    i = pl.program_id(2)
    q_scale = hd ** -0.5 * LOG2E

    @pl.when((b == 0) & (i == 0))
    def _():
        def bias_row(bk):
            tab = jnp.zeros((1, 2 * S), F32)
            for n in range(n_buckets):
                tab = jnp.where(bk == n, relb_ref[h, n], tab)
            return tab * LOG2E

        tab = bias_row(bucket_ref[...])
        bmax = jnp.max(tab, axis=-1, keepdims=True)
        bmin = jnp.min(tab, axis=-1, keepdims=True)
        bound = (NORM_SLACK * hd * q_scale) * (jnp.max(jnp.abs(gq_ref[...]), axis=-1, keepdims=True)
                                               * jnp.max(jnp.abs(gk_ref[...]), axis=-1, keepdims=True))
        span = 2.0 * bound + (bmax - bmin)
        fast_ref[0] = (span[0, 0] <= SAFE_LOG2_SPAN).astype(jnp.int32)
        shift = bound + bmax
        e_ref[...] = pltpu.roll(jnp.broadcast_to(tab - shift, (tq, 2 * S)), 0, 1, stride=1, stride_axis=0)
        tab_t = bias_row(bucket_t_ref[...])
        et_ref[...] = pltpu.roll(jnp.broadcast_to(tab_t - shift, (tq, 2 * S)), 0, 1, stride=1, stride_axis=0)

    @pl.when(i == 0)
    def _():
        k = k_ref[...].astype(F32)
        ms = _seg_sumsq(k, segq_ref[...], split=False) * (1.0 / hd)
        kn_ref[...] = (k * lax.rsqrt(ms + EPS) * gk_ref[...]).astype(BF16)
        va_ref[:, :vd] = v_ref[...]
        va_ref[:, vd:] = jnp.ones((S, vd), BF16)
        vat_ref[:vd, :] = v_ref[...].astype(F32).T.astype(BF16)
        vat_ref[vd:, :] = jnp.ones((vat_ref.shape[0] - vd, S), BF16)

    nt_dims = (((1,), (1,)), ((), ()))
    lp = lamp_ref[...]
    lam = (jnp.exp(jnp.sum(lp[0:1] * lp[1:2], axis=-1, keepdims=True))
           - jnp.exp(jnp.sum(lp[2:3] * lp[3:4], axis=-1, keepdims=True)) + lam_init)

    def q_maps_of(t):
        q = q_ref[t * tq:(t + 1) * tq, :].astype(F32)
        lo = lax.broadcasted_iota(jnp.int32, q.shape, 1) < hd
        q2 = q * q
        s_lo = jnp.sum(jnp.where(lo, q2, 0.0), axis=-1, keepdims=True)
        s_hi = jnp.sum(jnp.where(lo, 0.0, q2), axis=-1, keepdims=True)
        ms = jnp.where(lo, s_lo, s_hi) * (1.0 / hd)
        qn = q * lax.rsqrt(ms + EPS) * (gq_ref[...] * q_scale)
        return jnp.where(lo, qn, 0.0).astype(BF16), jnp.where(lo, 0.0, qn).astype(BF16)

    def e_start(t, k0):
        return pl.multiple_of(S + k0 - (i * n_sub + t) * tq, LANES)

    def finish(t, a1, a2):
        o = a1[:, :vd] / a1[:, vd:] - lam * (a2[:, :vd] / a2[:, vd:])
        o = o * lax.rsqrt(_seg_sumsq(o, segv_ref[...]) * (1.0 / vd) + EPS)
        o_ref[t * tq:(t + 1) * tq, :] = (o * (gs_ref[...] * (1.0 - lam_init))).astype(o_ref.dtype)

    def finish_t(t, a1, a2):
        ot = a1[:vd] / a1[vd:vd + 1] - lam * (a2[:vd] / a2[vd:vd + 1])
        ot = ot * lax.rsqrt(jnp.mean(ot * ot, axis=0, keepdims=True) + EPS)
        o = ot.T
        o_ref[t * tq:(t + 1) * tq, :] = (o * (gs_ref[...] * (1.0 - lam_init))).astype(o_ref.dtype)

    @pl.when(fast_ref[0] == 1)
    def _():
        for t in range(n_sub):
            q_maps = q_maps_of(t)
            tile = i * n_sub + t
            chunks = ([], [])
            for c in range(S // tq):
                start = pl.multiple_of(S + (tile - c) * tq, LANES)
                bias_t = et_ref[:, pl.ds(start, tq)]
                for mi in range(2):
                    st = lax.dot_general(kn_ref[c * tq:(c + 1) * tq, :], q_maps[mi], nt_dims,
                                         preferred_element_type=F32)
                    chunks[mi].append(jnp.exp2(st + bias_t).astype(BF16))
            accs = [jnp.dot(vat_ref[...], jnp.concatenate(chunks[mi], axis=0), preferred_element_type=F32)
                    for mi in range(2)]
            finish_t(t, *accs)

    @pl.when(fast_ref[0] != 1)
    def _():
        for t in range(n_sub):
            q_maps = q_maps_of(t)
            m_ref[...] = jnp.full(m_ref.shape, NEG_BIG, F32)
            acc_ref[...] = jnp.zeros(acc_ref.shape, F32)

            def chunk(c, carry):
                k0 = pl.multiple_of(c * kc, kc)
                bias = e_ref[:, pl.ds(e_start(t, k0), kc)]
                for mi in range(2):
                    s = lax.dot_general(q_maps[mi], kn_ref[pl.ds(k0, kc), :], nt_dims,
                                        preferred_element_type=F32) + bias
                    m_old = m_ref[mi]
                    m_new = jnp.maximum(m_old, jnp.max(s, axis=-1, keepdims=True))
                    p = jnp.exp2(s - m_new).astype(BF16)
                    acc_ref[mi] = (jnp.exp2(m_old - m_new) * acc_ref[mi]
                                   + jnp.dot(p, va_ref[pl.ds(k0, kc), :], preferred_element_type=F32))
                    m_ref[mi] = m_new
                return carry

            lax.fori_loop(0, S // kc, chunk, 0)
            finish(t, acc_ref[0], acc_ref[1])


def _attention(proj, rel_bias, g_q, g_k, lam_params, g_subln, *, B, S, H, hd, vd, off_k, off_v, lam_init):
    T = B * S
    tq = min(512, S)
    kc = min(512, S)
    n_sub = math.gcd(4, S // tq)
    tqb = n_sub * tq
    nq = S // tqb
    n_buckets = rel_bias.shape[0]
    hw = 2 * hd
    bucket = jnp.asarray(_bucket_table(S, n_buckets, np.arange(2 * S) - S))
    bucket_t = jnp.asarray(_bucket_table(S, n_buckets, S - np.arange(2 * S)))
    body = functools.partial(_attn_body, S=S, tq=tq, n_sub=n_sub, kc=kc, hd=hd, vd=vd, lam_init=lam_init,
                             n_buckets=n_buckets)
    kblk = off_k // hw
    vblk = off_v // vd
    seg_q = np.kron(np.eye(2, dtype=np.float32), np.ones((hd, hd), np.float32))
    return pl.pallas_call(
        body,
        grid=(H, B, nq),
        in_specs=[pl.BlockSpec((1, 2 * S), lambda h, b, i: (0, 0)),
                  pl.BlockSpec((1, 2 * S), lambda h, b, i: (0, 0)),
                  pl.BlockSpec(memory_space=pltpu.SMEM),
                  pl.BlockSpec((tqb, hw), lambda h, b, i: (b * nq + i, h)),
                  pl.BlockSpec((S, hw), lambda h, b, i: (b, kblk + h)),
                  pl.BlockSpec((S, vd), lambda h, b, i: (b, vblk + h)),
                  pl.BlockSpec((1, hw), lambda h, b, i: (0, 0)),
                  pl.BlockSpec((1, hw), lambda h, b, i: (0, 0)),
                  pl.BlockSpec((4, hd), lambda h, b, i: (0, 0)),
                  pl.BlockSpec((1, vd), lambda h, b, i: (0, 0)),
                  pl.BlockSpec((hw, hw), lambda h, b, i: (0, 0)),
                  pl.BlockSpec((vd, vd), lambda h, b, i: (0, 0))],
        out_specs=pl.BlockSpec((tqb, vd), lambda h, b, i: (b * nq + i, h)),
        out_shape=jax.ShapeDtypeStruct((T, H * vd), BF16),
        scratch_shapes=[pltpu.VMEM((tq, 2 * S), F32),
                        pltpu.VMEM((tq, 2 * S), F32),
                        pltpu.SMEM((1,), jnp.int32),
                        pltpu.VMEM((S, hw), BF16),
                        pltpu.VMEM((S, 2 * vd), BF16),
                        pltpu.VMEM((vd + 16, S), BF16),
                        pltpu.VMEM((2, tq, 1), F32),
                        pltpu.VMEM((2, tq, 2 * vd), F32)],
        compiler_params=_params("arbitrary", "arbitrary", "arbitrary"),
        name="diff_attention",
    )(bucket, bucket_t, rel_bias.T, proj, proj, proj, jnp.tile(g_q, 2)[None], jnp.tile(g_k, 2)[None],
      lam_params, g_subln[None], jnp.asarray(seg_q, BF16), jnp.ones((vd, vd), BF16))


LRU_SEGMENTS = SUBLANES
LRU_PITCH_PAD = 8


def _sublane_scan(a, bb, reverse):
    ri = lax.broadcasted_iota(jnp.int32, a.shape, 0)
    for d in (1, 2, 4):
        if reverse:
            keep, sh = ri < SUBLANES - d, SUBLANES - d
        else:
            keep, sh = ri >= d, d
        bb = bb + a * jnp.where(keep, pltpu.roll(bb, sh, 0), 0.0)
        a = a * jnp.where(keep, pltpu.roll(a, sh, 0), 1.0)
    return bb


LRU_BLOCKS_PER_STEP = 4


def _lru_body(x_ref, y_ref, cw_ref, cb_ref, w_ref, lam_ref, o_ref,
              af_ref, bf_ref, ab_ref, bb_ref, hf_ref, pf_ref, hb_ref, pb_ref, hs_ref, *, S, W):
    nb = w_ref.shape[0]
    L = S // LRU_SEGMENTS
    pitch = L + LRU_PITCH_PAD
    x = x_ref[...].astype(F32)
    row = lax.broadcasted_iota(jnp.int32, x.shape, 0)
    cw = cw_ref[...]
    xc_all = (cw[0:1] * jnp.where(row >= 2, pltpu.roll(x, 2, 0), 0.0)
              + cw[1:2] * jnp.where(row >= 1, pltpu.roll(x, 1, 0), 0.0)
              + cw[2:3] * x
              + cw[3:4] * jnp.where(row < S - 1, pltpu.roll(x, S - 1, 0), 0.0)
              + cb_ref[...])
    lam = lam_ref[...]
    bias_lanes = jnp.where(lax.broadcasted_iota(jnp.int32, (S, W), 1) < 2, 1.0, 0.0).astype(BF16)
    for n in range(nb):
        xc = xc_all[:, n * W:(n + 1) * W]
        gneg = jnp.dot(jnp.concatenate([xc.astype(BF16), bias_lanes], axis=1), w_ref[n],
                       preferred_element_type=F32)
        for d, (a_ref, b_ref) in enumerate(((af_ref, bf_ref), (ab_ref, bb_ref))):
            r = 1.0 / (1.0 + jnp.exp2(gneg[:, (2 * d) * W:(2 * d + 1) * W]))
            ig = 1.0 / (1.0 + jnp.exp2(gneg[:, (2 * d + 1) * W:(2 * d + 2) * W]))
            nl = -lam[d:d + 1, n * W:(n + 1) * W]
            softplus = jnp.maximum(nl, 0.0) + jnp.log1p(jnp.exp(-jnp.abs(nl)))
            a = jnp.exp2((-LRU_C * LOG2E * softplus) * r)
            y2 = 1.0 - a * a
            bvals = jnp.where(y2 > 0.0, y2 * lax.rsqrt(y2), 0.0) * ig * xc
            for sg in range(LRU_SEGMENTS):
                a_ref[n, sg * pitch:sg * pitch + L, :] = a[sg * L:(sg + 1) * L]
                b_ref[n, sg * pitch:sg * pitch + L, :] = bvals[sg * L:(sg + 1) * L]

    def seg_rows(ref, n, j):
        return ref[n, pl.ds(j, LRU_SEGMENTS, stride=pitch), :]

    def step(j, carry):
        out = []
        jb = L - 1 - j
        for n in range(nb):
            hf, pf, hb, pb = carry[4 * n:4 * n + 4]
            a = seg_rows(af_ref, n, j)
            hf = a * hf + seg_rows(bf_ref, n, j)
            pf = a * pf
            hf_ref[n, j] = hf
            pf_ref[n, j] = pf
            a = seg_rows(ab_ref, n, jb)
            hb = a * hb + seg_rows(bb_ref, n, jb)
            pb = a * pb
            hb_ref[n, jb] = hb
            pb_ref[n, jb] = pb
            out += [hf, pf, hb, pb]
        return tuple(out)

    zero = jnp.zeros((LRU_SEGMENTS, W), F32)
    one = jnp.ones((LRU_SEGMENTS, W), F32)
    ends = lax.fori_loop(0, L, step, (zero, one, zero, one) * nb, unroll=8)

    si = lax.broadcasted_iota(jnp.int32, (LRU_SEGMENTS, W), 0)
    cfs, cbs = [], []
    for n in range(nb):
        hf, pf, hb, pb = ends[4 * n:4 * n + 4]
        cfs.append(jnp.where(si >= 1, pltpu.roll(_sublane_scan(pf, hf, False), 1, 0), 0.0))
        cbs.append(jnp.where(si < LRU_SEGMENTS - 1,
                             pltpu.roll(_sublane_scan(pb, hb, True), LRU_SEGMENTS - 1, 0), 0.0))

    def fix(j, carry):
        for n in range(nb):
            hs_ref[n, pl.ds(j, LRU_SEGMENTS, stride=pitch), :] = (
                hf_ref[n, j] + pf_ref[n, j] * cfs[n] + hb_ref[n, j] + pb_ref[n, j] * cbs[n])
        return carry

    lax.fori_loop(0, L, fix, 0, unroll=8)

    for n in range(nb):
        for sg in range(LRU_SEGMENTS):
            y = y_ref[sg * L:(sg + 1) * L, n * W:(n + 1) * W].astype(F32)
            o_ref[sg * L:(sg + 1) * L, n * W:(n + 1) * W] = (
                hs_ref[n, sg * pitch:sg * pitch + L, :] * jax.nn.gelu(y)).astype(o_ref.dtype)


def _rg_lru(proj, conv_w, conv_b, w_ext, lru_lambda, *, B, S, off_x, off_y):
    T = B * S
    NB, W2, _ = w_ext.shape
    W = W2 // 2
    nb = math.gcd(LRU_BLOCKS_PER_STEP, NB)
    wb = nb * W
    xblk = off_x // wb
    yblk = off_y // wb
    body = functools.partial(_lru_body, S=S, W=W)
    padded = LRU_SEGMENTS * (S // LRU_SEGMENTS + LRU_PITCH_PAD)
    return pl.pallas_call(
        body,
        grid=(B, NB // nb),
        in_specs=[pl.BlockSpec((S, wb), lambda b, n: (b, xblk + n)),
                  pl.BlockSpec((S, wb), lambda b, n: (b, yblk + n)),
                  pl.BlockSpec((conv_w.shape[0], wb), lambda b, n: (0, n)),
                  pl.BlockSpec((1, wb), lambda b, n: (0, n)),
                  pl.BlockSpec((nb, 2 * W, 4 * W), lambda b, n: (n, 0, 0)),
                  pl.BlockSpec((2, wb), lambda b, n: (0, n))],
        out_specs=pl.BlockSpec((S, wb), lambda b, n: (b, n)),
        out_shape=jax.ShapeDtypeStruct((T, NB * W), BF16),
        scratch_shapes=([pltpu.VMEM((nb, padded, W), F32)] * 4
                        + [pltpu.VMEM((nb, S // LRU_SEGMENTS, LRU_SEGMENTS, W), F32)] * 4
                        + [pltpu.VMEM((nb, padded, W), F32)]),
        compiler_params=_params("parallel", "parallel"),
        name="rg_lru",
    )(proj, proj, conv_w, conv_b[None], w_ext, lru_lambda)


def _merge_body(o_ref, r_ref, ga_ref, gr_ref, x_ref, wpa_ref, wpl_ref, wo_ref, g_ref, wr_ref,
                h_ref, hn_ref, aff_ref, *, E, n_sub):
    ts = o_ref.shape[0] // n_sub
    for t in range(n_sub):
        rows = slice(t * ts, (t + 1) * ts)
        ba = jnp.dot(o_ref[rows, :], wpa_ref[...], preferred_element_type=F32)
        br = jnp.dot(r_ref[rows, :], wpl_ref[...], preferred_element_type=F32)
        mixed = (jax.nn.sigmoid(ga_ref[rows, :].astype(F32)) * ba
                 + jax.nn.sigmoid(gr_ref[rows, :].astype(F32)) * br)
        h = x_ref[rows, :] + jnp.dot(mixed.astype(BF16), wo_ref[...], preferred_element_type=F32)
        h_ref[rows, :] = h
        hn = h * lax.rsqrt(jnp.mean(h * h, axis=-1, keepdims=True) + EPS) * g_ref[...]
        hn_hi = hn.astype(BF16)
        hn_ref[rows, :] = hn_hi
        hn_lo = (hn - hn_hi.astype(F32)).astype(BF16)
        lg = (jnp.dot(hn_hi, wr_ref[...], preferred_element_type=F32)
              + jnp.dot(hn_lo, wr_ref[...], preferred_element_type=F32))
        logits = lg[:, :E] + lg[:, E:2 * E]
        logits = logits - jnp.max(logits, axis=-1, keepdims=True)
        ex = jnp.exp(logits)
        aff_ref[rows, :] = ex / jnp.sum(ex, axis=-1, keepdims=True)


def _merge(o_attn, lru_out, proj, x2, wpa, wpl, wo, g_ffn, wr2, *, off_ga, off_gr, E):
    T, D = x2.shape
    tm = min(1024, T)
    n_sub = 4 if tm % 1024 == 0 else 1
    resident = dict(pipeline_mode=pl.Buffered(1))
    gab = off_ga // D
    grb = off_gr // D
    row = lambda i: (i, 0)
    const = lambda i: (0, 0)
    return pl.pallas_call(
        functools.partial(_merge_body, E=E, n_sub=n_sub),
        grid=(T // tm,),
        in_specs=[pl.BlockSpec((tm, D), row), pl.BlockSpec((tm, D), row),
                  pl.BlockSpec((tm, D), lambda i: (i, gab)), pl.BlockSpec((tm, D), lambda i: (i, grb)),
                  pl.BlockSpec((tm, D), row),
                  pl.BlockSpec((D, D), const, **resident), pl.BlockSpec((D, D), const, **resident),
                  pl.BlockSpec((D, D), const, **resident),
                  pl.BlockSpec((1, D), const), pl.BlockSpec((D, 2 * E), const)],
        out_specs=[pl.BlockSpec((tm, D), row), pl.BlockSpec((tm, D), row), pl.BlockSpec((tm, E), row)],
        out_shape=[jax.ShapeDtypeStruct((T, D), F32), jax.ShapeDtypeStruct((T, D), BF16),
                   jax.ShapeDtypeStruct((T, E), F32)],
        compiler_params=_params("parallel"),
        name="merge_router",
    )(o_attn, lru_out, proj, proj, x2, wpa, wpl, wo, g_ffn, wr2)


SELECT_SEQS_PER_STEP = 8


def _select_body(aff_ref, pos_ref, cnt_ref, *, S, C):
    rows = aff_ref.shape[0] * aff_ref.shape[1]
    bits = pltpu.bitcast(aff_ref[...].reshape(rows, S), jnp.int32)

    def refine(k, t):
        cand = t | jnp.left_shift(jnp.int32(1), 30 - k)
        cnt = jnp.sum(jnp.where(bits >= cand, 1.0, 0.0), axis=-1, keepdims=True)
        return jnp.where(cnt >= C, cand, t)

    t = lax.fori_loop(0, 31, refine, jnp.zeros((rows, 1), jnp.int32))
    gt = bits > t
    eq = bits == t
    need = C - jnp.sum(jnp.where(gt, 1.0, 0.0), axis=-1, keepdims=True).astype(jnp.int32)
    packed = jnp.where(gt, 1 << 16, 0) + jnp.where(eq, 1, 0)
    lane = lax.broadcasted_iota(jnp.int32, (rows, S), 1)
    incl = packed
    d = 1
    while d < S:
        incl = incl + jnp.where(lane >= d, pltpu.roll(incl, d, 1), 0)
        d *= 2
    excl = incl - packed
    n_gt = excl >> 16
    n_eq = excl & 0xFFFF
    sel = gt | (eq & (n_eq < need))
    before = n_gt + jnp.minimum(n_eq, need)
    cnt_ref[...] = before.reshape(cnt_ref.shape)
    pos_ref[...] = jnp.where(sel, before, -1).reshape(pos_ref.shape)


def _select(aff_t, C):
    B, E, S = aff_t.shape
    nb = math.gcd(SELECT_SEQS_PER_STEP, B)
    spec = pl.BlockSpec((nb, E, S), lambda b: (b, 0, 0))
    return pl.pallas_call(
        functools.partial(_select_body, S=S, C=C),
        grid=(B // nb,),
        in_specs=[spec],
        out_specs=[spec, spec],
        out_shape=[jax.ShapeDtypeStruct((B, E, S), jnp.int32)] * 2,
        compiler_params=_params("parallel"),
        name="topc_select",
    )(aff_t)


GATHER_WIN = 64
SLOT_ALIGN = 16


def _gather_body(lo_ref, pos_ref, hn_ref, xg_ref, *, E, C, S, ts):
    b = pl.program_id(0)
    n_tiles = S // ts
    xg_ref[...] = jnp.zeros(xg_ref.shape, xg_ref.dtype)
    sub = lax.broadcasted_iota(jnp.int32, (GATHER_WIN, ts), 0)

    def tile_body(k, carry):
        t0 = pl.multiple_of(k * ts, ts)
        nxt = jnp.minimum(k + 1, n_tiles - 1)
        starts = [(lo_ref[b, k, e] // SLOT_ALIGN) * SLOT_ALIGN for e in range(E)]
        ends = [jnp.where(k + 1 < n_tiles, lo_ref[b, nxt, e], C) for e in range(E)]
        hn_t = hn_ref[0, pl.ds(t0, ts), :]

        def one_pass(w, first):
            sel, row0s = [], []
            for e in range(E):
                start = starts[e] + w * GATHER_WIN
                row0 = jnp.minimum(start, C - GATHER_WIN)
                pe = pos_ref[0, e:e + 1, pl.ds(t0, ts)]
                if not first:
                    pe = jnp.where(pe >= start, pe, -1)
                sel.append(jnp.where(pe - row0 == sub, 1.0, 0.0).astype(BF16))
                row0s.append(pl.multiple_of(row0, SLOT_ALIGN))
            rows = jnp.dot(jnp.concatenate(sel, axis=0), hn_t,
                           preferred_element_type=F32).astype(xg_ref.dtype)
            for e in range(E):
                xg_ref[0, e, pl.ds(row0s[e], GATHER_WIN), :] += rows[e * GATHER_WIN:(e + 1) * GATHER_WIN]

        one_pass(0, True)
        n_pass = jnp.int32(1)
        for e in range(E):
            n_pass = jnp.maximum(n_pass, (ends[e] - starts[e] + GATHER_WIN - 1) // GATHER_WIN)

        def more(w, c):
            one_pass(w, False)
            return c

        lax.fori_loop(1, n_pass, more, 0)
        return carry

    lax.fori_loop(0, n_tiles, tile_body, 0)


def _gather(lo, pos_rows, hn3, C, ts):
    B, S, D = hn3.shape
    E = pos_rows.shape[1]
    assert C % GATHER_WIN == 0
    return pl.pallas_call(
        functools.partial(_gather_body, E=E, C=C, S=S, ts=ts),
        grid=(B,),
        in_specs=[pl.BlockSpec(memory_space=pltpu.SMEM),
                  pl.BlockSpec((1, E, S), lambda b: (b, 0, 0)),
                  pl.BlockSpec((1, S, D), lambda b: (b, 0, 0))],
        out_specs=pl.BlockSpec((1, E, C, D), lambda b: (b, 0, 0, 0)),
        out_shape=jax.ShapeDtypeStruct((B, E, C, D), BF16),
        compiler_params=_params("parallel"),
        name="moe_gather",
    )(lo, pos_rows, hn3)


def _ffn_body(pos_ref, aff_ref, xg_ref, wg_hbm, wu_hbm, wd_hbm, y_ref,
              wg_s, wu_s, wd_s, stg_g, stg_u, stg_d, sems, *, C, S, fc):
    e = pl.program_id(0)
    b = pl.program_id(1)
    n_experts = pl.num_programs(0)
    rows_in = stg_g.shape[0]
    rows_dn = stg_d.shape[0]
    n_chunks = wg_s.shape[1] // rows_in

    def chunk_copies(ee, k):
        r_in = pl.multiple_of(k * rows_in, rows_in)
        r_dn = pl.multiple_of(k * rows_dn, rows_dn)
        return (pltpu.make_async_copy(wg_hbm.at[ee, pl.ds(r_in, rows_in), :], stg_g, sems.at[0]),
                pltpu.make_async_copy(wu_hbm.at[ee, pl.ds(r_in, rows_in), :], stg_u, sems.at[1]),
                pltpu.make_async_copy(wd_hbm.at[ee, pl.ds(r_dn, rows_dn), :], stg_d, sems.at[2]))

    def cast_chunk(slot_, k):
        r_in = pl.multiple_of(k * rows_in, rows_in)
        r_dn = pl.multiple_of(k * rows_dn, rows_dn)
        wg_s[slot_, pl.ds(r_in, rows_in), :] = stg_g[...].astype(BF16)
        wu_s[slot_, pl.ds(r_in, rows_in), :] = stg_u[...].astype(BF16)
        wd_s[slot_, pl.ds(r_dn, rows_dn), :] = stg_d[...].astype(BF16)

    @pl.when((e == 0) & (b == 0))
    def _():
        def stage(k, carry):
            for cp in chunk_copies(0, k):
                cp.start()
            for cp in chunk_copies(0, k):
                cp.wait()
            cast_chunk(0, k)
            return carry

        lax.fori_loop(0, n_chunks, stage, 0)

        @pl.when(n_experts > 1)
        def _():
            for cp in chunk_copies(1, 0):
                cp.start()

    cur = e % 2
    nb = xg_ref.shape[0]
    slot = lax.broadcasted_iota(jnp.int32, (C, S), 0)
    vals = [jnp.sum(jnp.where(pos_ref[t, 0] == slot, aff_ref[t, 0], 0.0), axis=1, keepdims=True)
            for t in range(nb)]
    val = jnp.concatenate(vals, axis=0)
    xg = xg_ref[...].reshape(nb * C, xg_ref.shape[3])
    F = wg_s.shape[2]
    y = jnp.zeros(xg.shape, F32)
    for f0 in range(0, F, fc):
        g = jnp.dot(xg, wg_s[cur, :, f0:f0 + fc], preferred_element_type=F32)
        u = jnp.dot(xg, wu_s[cur, :, f0:f0 + fc], preferred_element_type=F32)
        hid = (g * jax.nn.sigmoid(g) * u).astype(BF16)
        y = y + jnp.dot(hid, wd_s[cur, f0:f0 + fc, :], preferred_element_type=F32)
    y_ref[...] = (y * val).astype(y_ref.dtype).reshape(y_ref.shape)

    @pl.when(e + 1 < n_experts)
    def _():
        for cp in chunk_copies(e + 1, b):
            cp.wait()
        cast_chunk(1 - cur, b)

    wrap = b + 1 == n_chunks
    e_nxt = jnp.where(wrap, e + 2, e + 1)
    k_nxt = jnp.where(wrap, 0, b + 1)

    @pl.when(e_nxt < n_experts)
    def _():
        for cp in chunk_copies(e_nxt, k_nxt):
            cp.start()


FFN_SEQS_PER_STEP = 4


def _expert_ffn(pos_rows, aff_rows, xg, wg, wu, wd):
    B, _, C, D = xg.shape
    S = pos_rows.shape[-1]
    E, _, F = wg.shape
    fc = min(1024, F)
    nb = math.gcd(FFN_SEQS_PER_STEP, B)
    steps = B // nb
    assert D % steps == 0 and F % steps == 0 and (D // steps) % 16 == 0 and (F // steps) % 16 == 0
    hbm = pl.BlockSpec(memory_space=pl.ANY)
    return pl.pallas_call(
        functools.partial(_ffn_body, C=C, S=S, fc=fc),
        grid=(E, steps),
        in_specs=[pl.BlockSpec((nb, 1, 1, S), lambda e, b: (b, e, 0, 0)),
                  pl.BlockSpec((nb, 1, 1, S), lambda e, b: (b, e, 0, 0)),
                  pl.BlockSpec((nb, 1, C, D), lambda e, b: (b, e, 0, 0)),
                  hbm, hbm, hbm],
        out_specs=pl.BlockSpec((nb, 1, C, D), lambda e, b: (b, e, 0, 0)),
        out_shape=jax.ShapeDtypeStruct((B, E, C, D), BF16),
        scratch_shapes=[pltpu.VMEM((2, D, F), BF16), pltpu.VMEM((2, D, F), BF16),
                        pltpu.VMEM((2, F, D), BF16),
                        pltpu.VMEM((D // steps, F), F32), pltpu.VMEM((D // steps, F), F32),
                        pltpu.VMEM((F // steps, D), F32),
                        pltpu.SemaphoreType.DMA((3,))],
        compiler_params=_params("arbitrary", "arbitrary"),
        name="expert_ffn",
    )(pos_rows, aff_rows, xg, wg, wu, wd)


COMBINE_WIN = 64


def _combine_body(lo_ref, pos_ref, y_ref, h_ref, o_ref, *, E, C, ts, n_tiles):
    b = pl.program_id(0)
    n_sub = pos_ref.shape[1] // ts
    lane = lax.broadcasted_iota(jnp.int32, (ts, LANES), 1)
    upper = lane >= COMBINE_WIN
    lane_in = jnp.where(upper, lane - COMBINE_WIN, lane)

    extra = []
    for t in range(n_sub):
        tile = pl.program_id(1) * n_sub + t
        rows = slice(t * ts, (t + 1) * ts)
        pc = pos_ref[0, rows, :]
        nxt = jnp.minimum(tile + 1, n_tiles - 1)
        starts = [(lo_ref[b, tile, e] // SLOT_ALIGN) * SLOT_ALIGN for e in range(E)]
        ends = [jnp.where(tile + 1 < n_tiles, lo_ref[b, nxt, e], C) for e in range(E)]

        def window_product(w, first, pc=pc, starts=starts):
            pieces, y_wins = [], []
            for p in range(E // 2):
                targets = []
                for e in (2 * p, 2 * p + 1):
                    start = starts[e] + w * COMBINE_WIN
                    row0 = jnp.minimum(start, C - COMBINE_WIN)
                    pe = pc[:, e:e + 1]
                    if not first:
                        pe = jnp.where(pe >= start, pe, -1)
                    targets.append(pe - row0)
                    y_wins.append(y_ref[0, e, pl.ds(pl.multiple_of(row0, SLOT_ALIGN), COMBINE_WIN), :])
                hit = jnp.where(upper, targets[1], targets[0]) == lane_in
                pieces.append(jnp.where(hit, 1.0, 0.0).astype(BF16))
            return jnp.dot(jnp.concatenate(pieces, axis=1), jnp.concatenate(y_wins, axis=0),
                           preferred_element_type=F32)

        o_ref[0, rows, :] = h_ref[0, rows, :] + window_product(0, True)

        n_pass = jnp.int32(1)
        for e in range(E):
            n_pass = jnp.maximum(n_pass, (ends[e] - starts[e] + COMBINE_WIN - 1) // COMBINE_WIN)
        extra.append((n_pass, rows, window_product))

    for n_pass, rows, window_product in extra:
        def more(w, carry, rows=rows, window_product=window_product):
            o_ref[0, rows, :] += window_product(w, False)
            return carry

        lax.fori_loop(1, n_pass, more, 0)


def _combine(lo, pos_cols, y, h3, ts):
    B, S, D = h3.shape
    _, E, C, _ = y.shape
    n_tiles = S // ts
    n_sub = math.gcd(4, n_tiles)
    tb = ts * n_sub
    assert E % 2 == 0 and C % COMBINE_WIN == 0 and 2 * COMBINE_WIN == LANES
    return pl.pallas_call(
        functools.partial(_combine_body, E=E, C=C, ts=ts, n_tiles=n_tiles),
        grid=(B, n_tiles // n_sub),
        in_specs=[pl.BlockSpec(memory_space=pltpu.SMEM),
                  pl.BlockSpec((1, tb, E), lambda b, i: (b, i, 0)),
                  pl.BlockSpec((1, E, C, D), lambda b, i: (b, 0, 0, 0)),
                  pl.BlockSpec((1, tb, D), lambda b, i: (b, i, 0))],
        out_specs=pl.BlockSpec((1, tb, D), lambda b, i: (b, i, 0)),
        out_shape=jax.ShapeDtypeStruct((B, S, D), F32),
        compiler_params=_params("parallel", "arbitrary"),
        name="moe_combine",
    )(lo, pos_cols, y, h3)


def kernel(x, g_mix, w_in, g_q, g_k, lam_q1, lam_k1, lam_q2, lam_k2, g_subln, rel_bias, conv_w, conv_b,
           gate_r_w, gate_r_b, gate_i_w, gate_i_b, lru_lambda, w_proj_attn, w_proj_lru, w_out, g_ffn,
           w_router, w_gate_e, w_up_e, w_down_e):
    B, S, D = x.shape
    depth = w_in.shape[0]
    H = rel_bias.shape[1]
    hd = g_q.shape[-1]
    vd = g_subln.shape[-1]
    qk_w = H * 2 * hd
    attn_w = H * vd
    lru_w = conv_w.shape[-1]
    NB, LB = gate_r_w.shape[2], gate_r_w.shape[3]
    E = w_router.shape[-1]
    C = EC_CAPACITY_FACTOR * S // E
    off_k = qk_w
    off_v = off_k + qk_w
    off_x = off_v + attn_w
    off_y = off_x + lru_w
    off_ga = off_y + lru_w
    off_gr = off_ga + D

    h2 = x.reshape(B * S, D)
    for layer in range(depth):
        lam_init = 0.8 - 0.6 * math.exp(-0.3 * layer)
        proj = _inproj(h2, g_mix[layer][None], w_in[layer].astype(BF16))

        lam_params = jnp.stack([lam_q1[layer], lam_k1[layer], lam_q2[layer], lam_k2[layer]])
        o_attn = _attention(proj, rel_bias, g_q[layer], g_k[layer], lam_params, g_subln[layer],
                            B=B, S=S, H=H, hd=hd, vd=vd, off_k=off_k, off_v=off_v, lam_init=lam_init)

        w_cat = -LOG2E * jnp.concatenate([gate_r_w[layer, 0], gate_i_w[layer, 0],
                                          gate_r_w[layer, 1], gate_i_w[layer, 1]], axis=-1)
        b_cat = -LOG2E * jnp.stack([gate_r_b[layer, 0].reshape(NB, LB), gate_i_b[layer, 0].reshape(NB, LB),
                                    gate_r_b[layer, 1].reshape(NB, LB), gate_i_b[layer, 1].reshape(NB, LB)],
                                   axis=1).reshape(NB, 1, 4 * LB)
        b_hi = b_cat.astype(BF16)
        b_lo = (b_cat - b_hi.astype(F32)).astype(BF16)
        w_ext = jnp.concatenate([w_cat.astype(BF16), b_hi, b_lo,
                                 jnp.zeros((NB, LB - 2, 4 * LB), BF16)], axis=1)
        lru_out = _rg_lru(proj, conv_w[layer], conv_b[layer], w_ext, lru_lambda[layer],
                          B=B, S=S, off_x=off_x, off_y=off_y)

        wr = w_router[layer]
        wr_hi = wr.astype(BF16)
        wr2 = jnp.concatenate([wr_hi, (wr - wr_hi.astype(F32)).astype(BF16)], axis=1)
        h2, hn, aff = _merge(o_attn, lru_out, proj, h2, w_proj_attn[layer].astype(BF16),
                             w_proj_lru[layer].astype(BF16), w_out[layer].astype(BF16),
                             g_ffn[layer][None], wr2, off_ga=off_ga, off_gr=off_gr, E=E)

        aff_rows = aff.reshape(B, S, E).transpose(0, 2, 1)
        pos_rows, cnt_rows = _select(aff_rows, C)
        ts = min(256, S)
        lo = cnt_rows[:, :, ::ts].transpose(0, 2, 1)
        xg = _gather(lo, pos_rows, hn.reshape(B, S, D), C, ts)
        y = _expert_ffn(pos_rows.reshape(B, E, 1, S), aff_rows.reshape(B, E, 1, S), xg,
                        w_gate_e[layer], w_up_e[layer], w_down_e[layer])
        h3 = _combine(lo, pos_rows.transpose(0, 2, 1), y, h2.reshape(B, S, D), ts)
        h2 = h3.reshape(B * S, D)
    return h2.reshape(B, S, D)
```

```python
import functools
import math

import jax
import jax.numpy as jnp
import numpy as np
from jax import lax
from jax.experimental import pallas as pl
from jax.experimental.pallas import tpu as pltpu

F32 = jnp.float32
BF16 = jnp.bfloat16
EPS = 1e-6
LANES = 128
SUBLANES = 8
VMEM_LIMIT = 56 * 1024 * 1024
LOG2E = 1.4426950408889634
NEG_BIG = -1e30
REL_MAX_DIST = 128
LRU_C = 8.0
EC_CAPACITY_FACTOR = 2


def _params(*sem):
    return pltpu.CompilerParams(dimension_semantics=sem, vmem_limit_bytes=VMEM_LIMIT)


def _inproj_body(x_ref, g_ref, w_ref, o_ref, xn_ref):
    @pl.when(pl.program_id(1) == 0)
    def _():
        x = x_ref[...]
        ms = jnp.mean(x * x, axis=-1, keepdims=True)
        xn_ref[...] = (x * lax.rsqrt(ms + EPS) * g_ref[...]).astype(BF16)

    o_ref[...] = jnp.dot(xn_ref[...], w_ref[...], preferred_element_type=F32).astype(o_ref.dtype)


def _inproj(x2, g, w_bf):
    T, D = x2.shape
    N = w_bf.shape[1]
    tm = min(1024, T)
    tn = N // 2 if N % (2 * LANES) == 0 else N
    return pl.pallas_call(
        _inproj_body,
        grid=(T // tm, N // tn),
        in_specs=[pl.BlockSpec((tm, D), lambda i, j: (i, 0)),
                  pl.BlockSpec((1, D), lambda i, j: (0, 0)),
                  pl.BlockSpec((D, tn), lambda i, j: (0, j))],
        out_specs=pl.BlockSpec((tm, tn), lambda i, j: (i, j)),
        out_shape=jax.ShapeDtypeStruct((T, N), BF16),
        scratch_shapes=[pltpu.VMEM((tm, D), BF16)],
        compiler_params=_params("parallel", "arbitrary"),
        name="in_proj",
    )(x2, g, w_bf)


def _bucket_table(S, n_buckets, rel):
    half = n_buckets // 2
    max_exact = half // 2
    ret = np.where(rel > 0, half, 0)
    n = np.abs(rel)
    nf = np.maximum(n, max_exact).astype(np.float64)
    large = max_exact + (np.log(nf / max_exact) / math.log(REL_MAX_DIST / max_exact)
                         * (half - max_exact)).astype(np.int32)
    large = np.minimum(large, half - 1)
    return (ret + np.where(n < max_exact, n, large)).astype(np.int32).reshape(1, 2 * S)


def _seg_sumsq(x, seg_ones, split=True):
    x2 = x * x
    hi = x2.astype(BF16)
    out = jnp.dot(hi, seg_ones, preferred_element_type=F32)
    if split:
        lo = (x2 - hi.astype(F32)).astype(BF16)
        out = out + jnp.dot(lo, seg_ones, preferred_element_type=F32)
    return out


NORM_SLACK = 1.01
SAFE_LOG2_SPAN = 100.0


def _attn_body(bucket_ref, bucket_t_ref, relb_ref, q_ref, k_ref, v_ref, gq_ref, gk_ref, lamp_ref, gs_ref,
               segq_ref, segv_ref, o_ref, e_ref, et_ref, fast_ref, kn_ref, va_ref, vat_ref, m_ref, acc_ref,
               *, S, tq, n_sub, kc, hd, vd, lam_init, n_buckets):
    h = pl.program_id(0)
    b = pl.program_id(1)
    i = pl.program_id(2)
    q_scale = hd ** -0.5 * LOG2E

    @pl.when((b == 0) & (i == 0))
    def _():
        def bias_row(bk):
            tab = jnp.zeros((1, 2 * S), F32)
            for n in range(n_buckets):
                tab = jnp.where(bk == n, relb_ref[h, n], tab)
            return tab * LOG2E

        tab = bias_row(bucket_ref[...])
        bmax = jnp.max(tab, axis=-1, keepdims=True)
        bmin = jnp.min(tab, axis=-1, keepdims=True)
        bound = (NORM_SLACK * hd * q_scale) * (jnp.max(jnp.abs(gq_ref[...]), axis=-1, keepdims=True)
                                               * jnp.max(jnp.abs(gk_ref[...]), axis=-1, keepdims=True))
        span = 2.0 * bound + (bmax - bmin)
        fast_ref[0] = (span[0, 0] <= SAFE_LOG2_SPAN).astype(jnp.int32)
        shift = bound + bmax
        e_ref[...] = pltpu.roll(jnp.broadcast_to(tab - shift, (tq, 2 * S)), 0, 1, stride=1, stride_axis=0)
        tab_t = bias_row(bucket_t_ref[...])
        et_ref[...] = pltpu.roll(jnp.broadcast_to(tab_t - shift, (tq, 2 * S)), 0, 1, stride=1, stride_axis=0)

    @pl.when(i == 0)
    def _():
        k = k_ref[...].astype(F32)
        ms = _seg_sumsq(k, segq_ref[...], split=False) * (1.0 / hd)
        kn_ref[...] = (k * lax.rsqrt(ms + EPS) * gk_ref[...]).astype(BF16)
        vat_ref[:vd, :] = v_ref[...].astype(F32).T.astype(BF16)
        vat_ref[vd:, :] = jnp.ones((vat_ref.shape[0] - vd, S), BF16)

    nt_dims = (((1,), (1,)), ((), ()))
    lp = lamp_ref[...]
    lam = (jnp.exp(jnp.sum(lp[0:1] * lp[1:2], axis=-1, keepdims=True))
           - jnp.exp(jnp.sum(lp[2:3] * lp[3:4], axis=-1, keepdims=True)) + lam_init)

    def q_maps_of(t):
        q = q_ref[t * tq:(t + 1) * tq, :].astype(F32)
        lo = lax.broadcasted_iota(jnp.int32, q.shape, 1) < hd
        q2 = q * q
        s_lo = jnp.sum(jnp.where(lo, q2, 0.0), axis=-1, keepdims=True)
        s_hi = jnp.sum(jnp.where(lo, 0.0, q2), axis=-1, keepdims=True)
        ms = jnp.where(lo, s_lo, s_hi) * (1.0 / hd)
        qn = q * lax.rsqrt(ms + EPS) * (gq_ref[...] * q_scale)
        return jnp.where(lo, qn, 0.0).astype(BF16), jnp.where(lo, 0.0, qn).astype(BF16)

    def e_start(t, k0):
        return pl.multiple_of(S + k0 - (i * n_sub + t) * tq, LANES)

    def finish(t, a1, a2):
        o = a1[:, :vd] / a1[:, vd:] - lam * (a2[:, :vd] / a2[:, vd:])
        o = o * lax.rsqrt(_seg_sumsq(o, segv_ref[...]) * (1.0 / vd) + EPS)
        o_ref[t * tq:(t + 1) * tq, :] = (o * (gs_ref[...] * (1.0 - lam_init))).astype(o_ref.dtype)

    def finish_t(t, a1, a2):
        ot = a1[:vd] / a1[vd:vd + 1] - lam * (a2[:vd] / a2[vd:vd + 1])
        ot = ot * lax.rsqrt(jnp.mean(ot * ot, axis=0, keepdims=True) + EPS)
        o = ot.T
        o_ref[t * tq:(t + 1) * tq, :] = (o * (gs_ref[...] * (1.0 - lam_init))).astype(o_ref.dtype)

    @pl.when(fast_ref[0] == 1)
    def _():
        for t in range(n_sub):
            q_maps = q_maps_of(t)
            tile = i * n_sub + t
            chunks = ([], [])
            for c in range(S // tq):
                start = pl.multiple_of(S + (tile - c) * tq, LANES)
                bias_t = et_ref[:, pl.ds(start, tq)]
                for mi in range(2):
                    st = lax.dot_general(kn_ref[c * tq:(c + 1) * tq, :], q_maps[mi], nt_dims,
                                         preferred_element_type=F32)
                    chunks[mi].append(jnp.exp2(st + bias_t).astype(BF16))
            accs = [jnp.dot(vat_ref[...], jnp.concatenate(chunks[mi], axis=0), preferred_element_type=F32)
                    for mi in range(2)]
            finish_t(t, *accs)

    @pl.when(fast_ref[0] != 1)
    def _():
        va_ref[:, :vd] = v_ref[...]
        va_ref[:, vd:] = jnp.ones((S, vd), BF16)
        for t in range(n_sub):
            q_maps = q_maps_of(t)
            m_ref[...] = jnp.full(m_ref.shape, NEG_BIG, F32)
            acc_ref[...] = jnp.zeros(acc_ref.shape, F32)

            def chunk(c, carry):
                k0 = pl.multiple_of(c * kc, kc)
                bias = e_ref[:, pl.ds(e_start(t, k0), kc)]
                for mi in range(2):
                    s = lax.dot_general(q_maps[mi], kn_ref[pl.ds(k0, kc), :], nt_dims,
                                        preferred_element_type=F32) + bias
                    m_old = m_ref[mi]
                    m_new = jnp.maximum(m_old, jnp.max(s, axis=-1, keepdims=True))
                    p = jnp.exp2(s - m_new).astype(BF16)
                    acc_ref[mi] = (jnp.exp2(m_old - m_new) * acc_ref[mi]
                                   + jnp.dot(p, va_ref[pl.ds(k0, kc), :], preferred_element_type=F32))
                    m_ref[mi] = m_new
                return carry

            lax.fori_loop(0, S // kc, chunk, 0)
            finish(t, acc_ref[0], acc_ref[1])


def _attention(proj, rel_bias, g_q, g_k, lam_params, g_subln, *, B, S, H, hd, vd, off_k, off_v, lam_init):
    T = B * S
    tq = min(512, S)
    kc = min(512, S)
    n_sub = math.gcd(4, S // tq)
    tqb = n_sub * tq
    nq = S // tqb
    n_buckets = rel_bias.shape[0]
    hw = 2 * hd
    bucket = jnp.asarray(_bucket_table(S, n_buckets, np.arange(2 * S) - S))
    bucket_t = jnp.asarray(_bucket_table(S, n_buckets, S - np.arange(2 * S)))
    body = functools.partial(_attn_body, S=S, tq=tq, n_sub=n_sub, kc=kc, hd=hd, vd=vd, lam_init=lam_init,
                             n_buckets=n_buckets)
    kblk = off_k // hw
    vblk = off_v // vd
    seg_q = np.kron(np.eye(2, dtype=np.float32), np.ones((hd, hd), np.float32))
    return pl.pallas_call(
        body,
        grid=(H, B, nq),
        in_specs=[pl.BlockSpec((1, 2 * S), lambda h, b, i: (0, 0)),
                  pl.BlockSpec((1, 2 * S), lambda h, b, i: (0, 0)),
                  pl.BlockSpec(memory_space=pltpu.SMEM),
                  pl.BlockSpec((tqb, hw), lambda h, b, i: (b * nq + i, h)),
                  pl.BlockSpec((S, hw), lambda h, b, i: (b, kblk + h)),
                  pl.BlockSpec((S, vd), lambda h, b, i: (b, vblk + h)),
                  pl.BlockSpec((1, hw), lambda h, b, i: (0, 0)),
                  pl.BlockSpec((1, hw), lambda h, b, i: (0, 0)),
                  pl.BlockSpec((4, hd), lambda h, b, i: (0, 0)),
                  pl.BlockSpec((1, vd), lambda h, b, i: (0, 0)),
                  pl.BlockSpec((hw, hw), lambda h, b, i: (0, 0)),
                  pl.BlockSpec((vd, vd), lambda h, b, i: (0, 0))],
        out_specs=pl.BlockSpec((tqb, vd), lambda h, b, i: (b * nq + i, h)),
        out_shape=jax.ShapeDtypeStruct((T, H * vd), BF16),
        scratch_shapes=[pltpu.VMEM((tq, 2 * S), F32),
                        pltpu.VMEM((tq, 2 * S), F32),
                        pltpu.SMEM((1,), jnp.int32),
                        pltpu.VMEM((S, hw), BF16),
                        pltpu.VMEM((S, 2 * vd), BF16),
                        pltpu.VMEM((vd + 16, S), BF16),
                        pltpu.VMEM((2, tq, 1), F32),
                        pltpu.VMEM((2, tq, 2 * vd), F32)],
        compiler_params=_params("arbitrary", "arbitrary", "arbitrary"),
        name="diff_attention",
    )(bucket, bucket_t, rel_bias.T, proj, proj, proj, jnp.tile(g_q, 2)[None], jnp.tile(g_k, 2)[None],
      lam_params, g_subln[None], jnp.asarray(seg_q, BF16), jnp.ones((vd, vd), BF16))


LRU_SEGMENTS = SUBLANES
LRU_PITCH_PAD = 8


def _sublane_scan(a, bb, reverse):
    ri = lax.broadcasted_iota(jnp.int32, a.shape, 0)
    for d in (1, 2, 4):
        if reverse:
            keep, sh = ri < SUBLANES - d, SUBLANES - d
        else:
            keep, sh = ri >= d, d
        bb = bb + a * jnp.where(keep, pltpu.roll(bb, sh, 0), 0.0)
        a = a * jnp.where(keep, pltpu.roll(a, sh, 0), 1.0)
    return bb


LRU_BLOCKS_PER_STEP = 4


def _lru_body(x_ref, y_ref, cw_ref, cb_ref, w_ref, lam_ref, o_ref,
              af_ref, bf_ref, ab_ref, bb_ref, hf_ref, pf_ref, hb_ref, pb_ref, hs_ref, *, S, W):
    nb = w_ref.shape[0]
    L = S // LRU_SEGMENTS
    pitch = L + LRU_PITCH_PAD
    x = x_ref[...].astype(F32)
    row = lax.broadcasted_iota(jnp.int32, x.shape, 0)
    cw = cw_ref[...]
    xc_all = (cw[0:1] * jnp.where(row >= 2, pltpu.roll(x, 2, 0), 0.0)
              + cw[1:2] * jnp.where(row >= 1, pltpu.roll(x, 1, 0), 0.0)
              + cw[2:3] * x
              + cw[3:4] * jnp.where(row < S - 1, pltpu.roll(x, S - 1, 0), 0.0)
              + cb_ref[...])
    lam = lam_ref[...]
    bias_lanes = jnp.where(lax.broadcasted_iota(jnp.int32, (S, W), 1) < 2, 1.0, 0.0).astype(BF16)
    for n in range(nb):
        xc = xc_all[:, n * W:(n + 1) * W]
        gneg = jnp.dot(jnp.concatenate([xc.astype(BF16), bias_lanes], axis=1), w_ref[n],
                       preferred_element_type=F32)
        for d, (a_ref, b_ref) in enumerate(((af_ref, bf_ref), (ab_ref, bb_ref))):
            r = 1.0 / (1.0 + jnp.exp2(gneg[:, (2 * d) * W:(2 * d + 1) * W]))
            ig = 1.0 / (1.0 + jnp.exp2(gneg[:, (2 * d + 1) * W:(2 * d + 2) * W]))
            nl = -lam[d:d + 1, n * W:(n + 1) * W]
            softplus = jnp.maximum(nl, 0.0) + jnp.log1p(jnp.exp(-jnp.abs(nl)))
            a = jnp.exp2((-LRU_C * LOG2E * softplus) * r)
            y2 = 1.0 - a * a
            bvals = jnp.where(y2 > 0.0, y2 * lax.rsqrt(y2), 0.0) * ig * xc
            for sg in range(LRU_SEGMENTS):
                a_ref[n, sg * pitch:sg * pitch + L, :] = a[sg * L:(sg + 1) * L]
                b_ref[n, sg * pitch:sg * pitch + L, :] = bvals[sg * L:(sg + 1) * L]

    def seg_rows(ref, n, j):
        return ref[n, pl.ds(j, LRU_SEGMENTS, stride=pitch), :]

    def step(j, carry):
        out = []
        jb = L - 1 - j
        for n in range(nb):
            hf, pf, hb, pb = carry[4 * n:4 * n + 4]
            a = seg_rows(af_ref, n, j)
            hf = a * hf + seg_rows(bf_ref, n, j)
            pf = a * pf
            hf_ref[n, j] = hf
            pf_ref[n, j] = pf
            a = seg_rows(ab_ref, n, jb)
            hb = a * hb + seg_rows(bb_ref, n, jb)
            pb = a * pb
            hb_ref[n, jb] = hb
            pb_ref[n, jb] = pb
            out += [hf, pf, hb, pb]
        return tuple(out)

    zero = jnp.zeros((LRU_SEGMENTS, W), F32)
    one = jnp.ones((LRU_SEGMENTS, W), F32)
    ends = lax.fori_loop(0, L, step, (zero, one, zero, one) * nb, unroll=8)

    si = lax.broadcasted_iota(jnp.int32, (LRU_SEGMENTS, W), 0)
    cfs, cbs = [], []
    for n in range(nb):
        hf, pf, hb, pb = ends[4 * n:4 * n + 4]
        cfs.append(jnp.where(si >= 1, pltpu.roll(_sublane_scan(pf, hf, False), 1, 0), 0.0))
        cbs.append(jnp.where(si < LRU_SEGMENTS - 1,
                             pltpu.roll(_sublane_scan(pb, hb, True), LRU_SEGMENTS - 1, 0), 0.0))

    def fix(j, carry):
        for n in range(nb):
            hs_ref[n, pl.ds(j, LRU_SEGMENTS, stride=pitch), :] = (
                hf_ref[n, j] + pf_ref[n, j] * cfs[n] + hb_ref[n, j] + pb_ref[n, j] * cbs[n])
        return carry

    lax.fori_loop(0, L, fix, 0, unroll=8)

    for n in range(nb):
        for sg in range(LRU_SEGMENTS):
            y = y_ref[sg * L:(sg + 1) * L, n * W:(n + 1) * W].astype(F32)
            o_ref[sg * L:(sg + 1) * L, n * W:(n + 1) * W] = (
                hs_ref[n, sg * pitch:sg * pitch + L, :] * jax.nn.gelu(y)).astype(o_ref.dtype)


def _rg_lru(proj, conv_w, conv_b, w_ext, lru_lambda, *, B, S, off_x, off_y):
    T = B * S
    NB, W2, _ = w_ext.shape
    W = W2 // 2
    nb = math.gcd(LRU_BLOCKS_PER_STEP, NB)
    wb = nb * W
    xblk = off_x // wb
    yblk = off_y // wb
    body = functools.partial(_lru_body, S=S, W=W)
    padded = LRU_SEGMENTS * (S // LRU_SEGMENTS + LRU_PITCH_PAD)
    return pl.pallas_call(
        body,
        grid=(B, NB // nb),
        in_specs=[pl.BlockSpec((S, wb), lambda b, n: (b, xblk + n)),
                  pl.BlockSpec((S, wb), lambda b, n: (b, yblk + n)),
                  pl.BlockSpec((conv_w.shape[0], wb), lambda b, n: (0, n)),
                  pl.BlockSpec((1, wb), lambda b, n: (0, n)),
                  pl.BlockSpec((nb, 2 * W, 4 * W), lambda b, n: (n, 0, 0)),
                  pl.BlockSpec((2, wb), lambda b, n: (0, n))],
        out_specs=pl.BlockSpec((S, wb), lambda b, n: (b, n)),
        out_shape=jax.ShapeDtypeStruct((T, NB * W), BF16),
        scratch_shapes=([pltpu.VMEM((nb, padded, W), F32)] * 4
                        + [pltpu.VMEM((nb, S // LRU_SEGMENTS, LRU_SEGMENTS, W), F32)] * 4
                        + [pltpu.VMEM((nb, padded, W), F32)]),
        compiler_params=_params("parallel", "parallel"),
        name="rg_lru",
    )(proj, proj, conv_w, conv_b[None], w_ext, lru_lambda)


def _merge_body(o_ref, r_ref, ga_ref, gr_ref, x_ref, wpa_ref, wpl_ref, wo_ref, g_ref, wr_ref,
                h_ref, hn_ref, aff_ref, *, E, n_sub):
    ts = o_ref.shape[0] // n_sub
    for t in range(n_sub):
        rows = slice(t * ts, (t + 1) * ts)
        ba = jnp.dot(o_ref[rows, :], wpa_ref[...], preferred_element_type=F32)
        br = jnp.dot(r_ref[rows, :], wpl_ref[...], preferred_element_type=F32)
        mixed = (jax.nn.sigmoid(ga_ref[rows, :].astype(F32)) * ba
                 + jax.nn.sigmoid(gr_ref[rows, :].astype(F32)) * br)
        h = x_ref[rows, :] + jnp.dot(mixed.astype(BF16), wo_ref[...], preferred_element_type=F32)
        h_ref[rows, :] = h
        hn = h * lax.rsqrt(jnp.mean(h * h, axis=-1, keepdims=True) + EPS) * g_ref[...]
        hn_hi = hn.astype(BF16)
        hn_ref[rows, :] = hn_hi
        hn_lo = (hn - hn_hi.astype(F32)).astype(BF16)
        lg = (jnp.dot(hn_hi, wr_ref[...], preferred_element_type=F32)
              + jnp.dot(hn_lo, wr_ref[...], preferred_element_type=F32))
        logits = lg[:, :E] + lg[:, E:2 * E]
        logits = logits - jnp.max(logits, axis=-1, keepdims=True)
        ex = jnp.exp(logits)
        aff_ref[rows, :] = ex / jnp.sum(ex, axis=-1, keepdims=True)


def _merge(o_attn, lru_out, proj, x2, wpa, wpl, wo, g_ffn, wr2, *, off_ga, off_gr, E):
    T, D = x2.shape
    tm = min(1024, T)
    n_sub = 4 if tm % 1024 == 0 else 1
    resident = dict(pipeline_mode=pl.Buffered(1))
    gab = off_ga // D
    grb = off_gr // D
    row = lambda i: (i, 0)
    const = lambda i: (0, 0)
    return pl.pallas_call(
        functools.partial(_merge_body, E=E, n_sub=n_sub),
        grid=(T // tm,),
        in_specs=[pl.BlockSpec((tm, D), row), pl.BlockSpec((tm, D), row),
                  pl.BlockSpec((tm, D), lambda i: (i, gab)), pl.BlockSpec((tm, D), lambda i: (i, grb)),
                  pl.BlockSpec((tm, D), row),
                  pl.BlockSpec((D, D), const, **resident), pl.BlockSpec((D, D), const, **resident),
                  pl.BlockSpec((D, D), const, **resident),
                  pl.BlockSpec((1, D), const), pl.BlockSpec((D, 2 * E), const)],
        out_specs=[pl.BlockSpec((tm, D), row), pl.BlockSpec((tm, D), row), pl.BlockSpec((tm, E), row)],
        out_shape=[jax.ShapeDtypeStruct((T, D), F32), jax.ShapeDtypeStruct((T, D), BF16),
                   jax.ShapeDtypeStruct((T, E), F32)],
        compiler_params=_params("parallel"),
        name="merge_router",
    )(o_attn, lru_out, proj, proj, x2, wpa, wpl, wo, g_ffn, wr2)


SELECT_SEQS_PER_STEP = 8


def _select_body(aff_ref, pos_ref, cnt_ref, *, S, C):
    rows = aff_ref.shape[0] * aff_ref.shape[1]
    bits = pltpu.bitcast(aff_ref[...].reshape(rows, S), jnp.int32)

    def refine(k, t):
        cand = t | jnp.left_shift(jnp.int32(1), 30 - k)
        cnt = jnp.sum(jnp.where(bits >= cand, 1.0, 0.0), axis=-1, keepdims=True)
        return jnp.where(cnt >= C, cand, t)

    t = lax.fori_loop(0, 31, refine, jnp.zeros((rows, 1), jnp.int32))
    gt = bits > t
    eq = bits == t
    need = C - jnp.sum(jnp.where(gt, 1.0, 0.0), axis=-1, keepdims=True).astype(jnp.int32)
    packed = jnp.where(gt, 1 << 16, 0) + jnp.where(eq, 1, 0)
    lane = lax.broadcasted_iota(jnp.int32, (rows, S), 1)
    incl = packed
    d = 1
    while d < S:
        incl = incl + jnp.where(lane >= d, pltpu.roll(incl, d, 1), 0)
        d *= 2
    excl = incl - packed
    n_gt = excl >> 16
    n_eq = excl & 0xFFFF
    sel = gt | (eq & (n_eq < need))
    before = n_gt + jnp.minimum(n_eq, need)
    cnt_ref[...] = before.reshape(cnt_ref.shape)
    pos_ref[...] = jnp.where(sel, before, -1).reshape(pos_ref.shape)


def _select(aff_t, C):
    B, E, S = aff_t.shape
    nb = math.gcd(SELECT_SEQS_PER_STEP, B)
    spec = pl.BlockSpec((nb, E, S), lambda b: (b, 0, 0))
    return pl.pallas_call(
        functools.partial(_select_body, S=S, C=C),
        grid=(B // nb,),
        in_specs=[spec],
        out_specs=[spec, spec],
        out_shape=[jax.ShapeDtypeStruct((B, E, S), jnp.int32)] * 2,
        compiler_params=_params("parallel"),
        name="topc_select",
    )(aff_t)


GATHER_WIN = 64
SLOT_ALIGN = 16


def _gather_body(lo_ref, pos_ref, hn_ref, xg_ref, *, E, C, S, ts):
    b = pl.program_id(0)
    n_tiles = S // ts
    xg_ref[...] = jnp.zeros(xg_ref.shape, xg_ref.dtype)
    sub = lax.broadcasted_iota(jnp.int32, (GATHER_WIN, ts), 0)

    def tile_body(k, carry):
        t0 = pl.multiple_of(k * ts, ts)
        nxt = jnp.minimum(k + 1, n_tiles - 1)
        starts = [(lo_ref[b, k, e] // SLOT_ALIGN) * SLOT_ALIGN for e in range(E)]
        ends = [jnp.where(k + 1 < n_tiles, lo_ref[b, nxt, e], C) for e in range(E)]
        hn_t = hn_ref[0, pl.ds(t0, ts), :]

        def one_pass(w, first):
            sel, row0s = [], []
            for e in range(E):
                start = starts[e] + w * GATHER_WIN
                row0 = jnp.minimum(start, C - GATHER_WIN)
                pe = pos_ref[0, e:e + 1, pl.ds(t0, ts)]
                if not first:
                    pe = jnp.where(pe >= start, pe, -1)
                sel.append(jnp.where(pe - row0 == sub, 1.0, 0.0).astype(BF16))
                row0s.append(pl.multiple_of(row0, SLOT_ALIGN))
            rows = jnp.dot(jnp.concatenate(sel, axis=0), hn_t,
                           preferred_element_type=F32).astype(xg_ref.dtype)
            for e in range(E):
                xg_ref[0, e, pl.ds(row0s[e], GATHER_WIN), :] += rows[e * GATHER_WIN:(e + 1) * GATHER_WIN]

        one_pass(0, True)
        n_pass = jnp.int32(1)
        for e in range(E):
            n_pass = jnp.maximum(n_pass, (ends[e] - starts[e] + GATHER_WIN - 1) // GATHER_WIN)

        def more(w, c):
            one_pass(w, False)
            return c

        lax.fori_loop(1, n_pass, more, 0)
        return carry

    lax.fori_loop(0, n_tiles, tile_body, 0)


def _gather(lo, pos_rows, hn3, C, ts):
    B, S, D = hn3.shape
    E = pos_rows.shape[1]
    assert C % GATHER_WIN == 0
    return pl.pallas_call(
        functools.partial(_gather_body, E=E, C=C, S=S, ts=ts),
        grid=(B,),
        in_specs=[pl.BlockSpec(memory_space=pltpu.SMEM),
                  pl.BlockSpec((1, E, S), lambda b: (b, 0, 0)),
                  pl.BlockSpec((1, S, D), lambda b: (b, 0, 0))],
        out_specs=pl.BlockSpec((1, E, C, D), lambda b: (b, 0, 0, 0)),
        out_shape=jax.ShapeDtypeStruct((B, E, C, D), BF16),
        compiler_params=_params("parallel"),
        name="moe_gather",
    )(lo, pos_rows, hn3)


def _ffn_body(pos_ref, aff_ref, xg_ref, wg_hbm, wu_hbm, wd_hbm, y_ref,
              wg_s, wu_s, wd_s, stg_g, stg_u, stg_d, sems, *, C, S, fc):
    e = pl.program_id(0)
    b = pl.program_id(1)
    n_experts = pl.num_programs(0)
    rows_in = stg_g.shape[0]
    rows_dn = stg_d.shape[0]
    n_chunks = wg_s.shape[1] // rows_in

    def chunk_copies(ee, k):
        r_in = pl.multiple_of(k * rows_in, rows_in)
        r_dn = pl.multiple_of(k * rows_dn, rows_dn)
        return (pltpu.make_async_copy(wg_hbm.at[ee, pl.ds(r_in, rows_in), :], stg_g, sems.at[0]),
                pltpu.make_async_copy(wu_hbm.at[ee, pl.ds(r_in, rows_in), :], stg_u, sems.at[1]),
                pltpu.make_async_copy(wd_hbm.at[ee, pl.ds(r_dn, rows_dn), :], stg_d, sems.at[2]))

    def cast_chunk(slot_, k):
        r_in = pl.multiple_of(k * rows_in, rows_in)
        r_dn = pl.multiple_of(k * rows_dn, rows_dn)
        wg_s[slot_, pl.ds(r_in, rows_in), :] = stg_g[...].astype(BF16)
        wu_s[slot_, pl.ds(r_in, rows_in), :] = stg_u[...].astype(BF16)
        wd_s[slot_, pl.ds(r_dn, rows_dn), :] = stg_d[...].astype(BF16)

    @pl.when((e == 0) & (b == 0))
    def _():
        def stage(k, carry):
            for cp in chunk_copies(0, k):
                cp.start()
            for cp in chunk_copies(0, k):
                cp.wait()
            cast_chunk(0, k)
            return carry

        lax.fori_loop(0, n_chunks, stage, 0)

        @pl.when(n_experts > 1)
        def _():
            for cp in chunk_copies(1, 0):
                cp.start()

    cur = e % 2
    nb = xg_ref.shape[0]
    slot = lax.broadcasted_iota(jnp.int32, (C, S), 0)
    vals = [jnp.sum(jnp.where(pos_ref[t, 0] == slot, aff_ref[t, 0], 0.0), axis=1, keepdims=True)
            for t in range(nb)]
    val = jnp.concatenate(vals, axis=0)
    xg = xg_ref[...].reshape(nb * C, xg_ref.shape[3])
    F = wg_s.shape[2]
    y = jnp.zeros(xg.shape, F32)
    for f0 in range(0, F, fc):
        g = jnp.dot(xg, wg_s[cur, :, f0:f0 + fc], preferred_element_type=F32)
        u = jnp.dot(xg, wu_s[cur, :, f0:f0 + fc], preferred_element_type=F32)
        hid = (g * jax.nn.sigmoid(g) * u).astype(BF16)
        y = y + jnp.dot(hid, wd_s[cur, f0:f0 + fc, :], preferred_element_type=F32)
    y_ref[...] = (y * val).astype(y_ref.dtype).reshape(y_ref.shape)

    @pl.when(e + 1 < n_experts)
    def _():
        for cp in chunk_copies(e + 1, b):
            cp.wait()
        cast_chunk(1 - cur, b)

    wrap = b + 1 == n_chunks
    e_nxt = jnp.where(wrap, e + 2, e + 1)
    k_nxt = jnp.where(wrap, 0, b + 1)

    @pl.when(e_nxt < n_experts)
    def _():
        for cp in chunk_copies(e_nxt, k_nxt):
            cp.start()


FFN_SEQS_PER_STEP = 4


def _expert_ffn(pos_rows, aff_rows, xg, wg, wu, wd):
    B, _, C, D = xg.shape
    S = pos_rows.shape[-1]
    E, _, F = wg.shape
    fc = min(1024, F)
    nb = math.gcd(FFN_SEQS_PER_STEP, B)
    steps = B // nb
    assert D % steps == 0 and F % steps == 0 and (D // steps) % 16 == 0 and (F // steps) % 16 == 0
    hbm = pl.BlockSpec(memory_space=pl.ANY)
    return pl.pallas_call(
        functools.partial(_ffn_body, C=C, S=S, fc=fc),
        grid=(E, steps),
        in_specs=[pl.BlockSpec((nb, 1, 1, S), lambda e, b: (b, e, 0, 0)),
                  pl.BlockSpec((nb, 1, 1, S), lambda e, b: (b, e, 0, 0)),
                  pl.BlockSpec((nb, 1, C, D), lambda e, b: (b, e, 0, 0)),
                  hbm, hbm, hbm],
        out_specs=pl.BlockSpec((nb, 1, C, D), lambda e, b: (b, e, 0, 0)),
        out_shape=jax.ShapeDtypeStruct((B, E, C, D), BF16),
        scratch_shapes=[pltpu.VMEM((2, D, F), BF16), pltpu.VMEM((2, D, F), BF16),
                        pltpu.VMEM((2, F, D), BF16),
                        pltpu.VMEM((D // steps, F), F32), pltpu.VMEM((D // steps, F), F32),
                        pltpu.VMEM((F // steps, D), F32),
                        pltpu.SemaphoreType.DMA((3,))],
        compiler_params=_params("arbitrary", "arbitrary"),
        name="expert_ffn",
    )(pos_rows, aff_rows, xg, wg, wu, wd)


COMBINE_WIN = 64


def _combine_body(lo_ref, pos_ref, y_ref, h_ref, o_ref, *, E, C, ts, n_tiles):
    b = pl.program_id(0)
    n_sub = pos_ref.shape[1] // ts
    lane = lax.broadcasted_iota(jnp.int32, (ts, LANES), 1)
    upper = lane >= COMBINE_WIN
    lane_in = jnp.where(upper, lane - COMBINE_WIN, lane)

    extra = []
    for t in range(n_sub):
        tile = pl.program_id(1) * n_sub + t
        rows = slice(t * ts, (t + 1) * ts)
        pc = pos_ref[0, rows, :]
        nxt = jnp.minimum(tile + 1, n_tiles - 1)
        starts = [(lo_ref[b, tile, e] // SLOT_ALIGN) * SLOT_ALIGN for e in range(E)]
        ends = [jnp.where(tile + 1 < n_tiles, lo_ref[b, nxt, e], C) for e in range(E)]

        def window_product(w, first, pc=pc, starts=starts):
            pieces, y_wins = [], []
            for p in range(E // 2):
                targets = []
                for e in (2 * p, 2 * p + 1):
                    start = starts[e] + w * COMBINE_WIN
                    row0 = jnp.minimum(start, C - COMBINE_WIN)
                    pe = pc[:, e:e + 1]
                    if not first:
                        pe = jnp.where(pe >= start, pe, -1)
                    targets.append(pe - row0)
                    y_wins.append(y_ref[0, e, pl.ds(pl.multiple_of(row0, SLOT_ALIGN), COMBINE_WIN), :])
                hit = jnp.where(upper, targets[1], targets[0]) == lane_in
                pieces.append(jnp.where(hit, 1.0, 0.0).astype(BF16))
            return jnp.dot(jnp.concatenate(pieces, axis=1), jnp.concatenate(y_wins, axis=0),
                           preferred_element_type=F32)

        o_ref[0, rows, :] = h_ref[0, rows, :] + window_product(0, True)

        n_pass = jnp.int32(1)
        for e in range(E):
            n_pass = jnp.maximum(n_pass, (ends[e] - starts[e] + COMBINE_WIN - 1) // COMBINE_WIN)
        extra.append((n_pass, rows, window_product))

    for n_pass, rows, window_product in extra:
        def more(w, carry, rows=rows, window_product=window_product):
            o_ref[0, rows, :] += window_product(w, False)
            return carry

        lax.fori_loop(1, n_pass, more, 0)


def _combine(lo, pos_cols, y, h3, ts):
    B, S, D = h3.shape
    _, E, C, _ = y.shape
    n_tiles = S // ts
    n_sub = math.gcd(4, n_tiles)
    tb = ts * n_sub
    assert E % 2 == 0 and C % COMBINE_WIN == 0 and 2 * COMBINE_WIN == LANES
    return pl.pallas_call(
        functools.partial(_combine_body, E=E, C=C, ts=ts, n_tiles=n_tiles),
        grid=(B, n_tiles // n_sub),
        in_specs=[pl.BlockSpec(memory_space=pltpu.SMEM),
                  pl.BlockSpec((1, tb, E), lambda b, i: (b, i, 0)),
                  pl.BlockSpec((1, E, C, D), lambda b, i: (b, 0, 0, 0)),
                  pl.BlockSpec((1, tb, D), lambda b, i: (b, i, 0))],
        out_specs=pl.BlockSpec((1, tb, D), lambda b, i: (b, i, 0)),
        out_shape=jax.ShapeDtypeStruct((B, S, D), F32),
        compiler_params=_params("parallel", "arbitrary"),
        name="moe_combine",
    )(lo, pos_cols, y, h3)


def kernel(x, g_mix, w_in, g_q, g_k, lam_q1, lam_k1, lam_q2, lam_k2, g_subln, rel_bias, conv_w, conv_b,
           gate_r_w, gate_r_b, gate_i_w, gate_i_b, lru_lambda, w_proj_attn, w_proj_lru, w_out, g_ffn,
           w_router, w_gate_e, w_up_e, w_down_e):
    B, S, D = x.shape
    depth = w_in.shape[0]
    H = rel_bias.shape[1]
    hd = g_q.shape[-1]
    vd = g_subln.shape[-1]
    qk_w = H * 2 * hd
    attn_w = H * vd
    lru_w = conv_w.shape[-1]
    NB, LB = gate_r_w.shape[2], gate_r_w.shape[3]
    E = w_router.shape[-1]
    C = EC_CAPACITY_FACTOR * S // E
    off_k = qk_w
    off_v = off_k + qk_w
    off_x = off_v + attn_w
    off_y = off_x + lru_w
    off_ga = off_y + lru_w
    off_gr = off_ga + D

    h2 = x.reshape(B * S, D)
    for layer in range(depth):
        lam_init = 0.8 - 0.6 * math.exp(-0.3 * layer)
        proj = _inproj(h2, g_mix[layer][None], w_in[layer].astype(BF16))

        lam_params = jnp.stack([lam_q1[layer], lam_k1[layer], lam_q2[layer], lam_k2[layer]])
        o_attn = _attention(proj, rel_bias, g_q[layer], g_k[layer], lam_params, g_subln[layer],
                            B=B, S=S, H=H, hd=hd, vd=vd, off_k=off_k, off_v=off_v, lam_init=lam_init)

        w_cat = -LOG2E * jnp.concatenate([gate_r_w[layer, 0], gate_i_w[layer, 0],
                                          gate_r_w[layer, 1], gate_i_w[layer, 1]], axis=-1)
        b_cat = -LOG2E * jnp.stack([gate_r_b[layer, 0].reshape(NB, LB), gate_i_b[layer, 0].reshape(NB, LB),
                                    gate_r_b[layer, 1].reshape(NB, LB), gate_i_b[layer, 1].reshape(NB, LB)],
                                   axis=1).reshape(NB, 1, 4 * LB)
        b_hi = b_cat.astype(BF16)
        b_lo = (b_cat - b_hi.astype(F32)).astype(BF16)
        w_ext = jnp.concatenate([w_cat.astype(BF16), b_hi, b_lo,
                                 jnp.zeros((NB, LB - 2, 4 * LB), BF16)], axis=1)
        lru_out = _rg_lru(proj, conv_w[layer], conv_b[layer], w_ext, lru_lambda[layer],
                          B=B, S=S, off_x=off_x, off_y=off_y)

        wr = w_router[layer]
        wr_hi = wr.astype(BF16)
        wr2 = jnp.concatenate([wr_hi, (wr - wr_hi.astype(F32)).astype(BF16)], axis=1)
        h2, hn, aff = _merge(o_attn, lru_out, proj, h2, w_proj_attn[layer].astype(BF16),
                             w_proj_lru[layer].astype(BF16), w_out[layer].astype(BF16),
                             g_ffn[layer][None], wr2, off_ga=off_ga, off_gr=off_gr, E=E)

        aff_rows = aff.reshape(B, S, E).transpose(0, 2, 1)
        pos_rows, cnt_rows = _select(aff_rows, C)
        ts = min(256, S)
        lo = cnt_rows[:, :, ::ts].transpose(0, 2, 1)
        xg = _gather(lo, pos_rows, hn.reshape(B, S, D), C, ts)
        y = _expert_ffn(pos_rows.reshape(B, E, 1, S), aff_rows.reshape(B, E, 1, S), xg,
                        w_gate_e[layer], w_up_e[layer], w_down_e[layer])
        h3 = _combine(lo, pos_rows.transpose(0, 2, 1), y, h2.reshape(B, S, D), ts)
        h2 = h3.reshape(B * S, D)
    return h2.reshape(B, S, D)
```
